```python
import math
import jax, jax.numpy as jnp
from jax import lax
import numpy as np

D_MODEL = 1024
BATCH = 8
SEQ = 2048
DEPTH = 1
DEC_BATCH = 128
DEC_SEQ = 4
PAST_LEN = 16384
PAGE_SIZE = 128

S5_GROUP = 16
S5_WIDTH = D_MODEL // 2
S5_GROUPS = S5_WIDTH // S5_GROUP
S5_STATE = 64
S5_DT_MIN = 0.001
S5_DT_MAX = 0.1
HG_HEADS = 8
HG_DK = D_MODEL // HG_HEADS
HG_DV = D_MODEL // HG_HEADS
HG_KEY_WIDTH = HG_HEADS * HG_DK
HG_VAL_WIDTH = HG_HEADS * HG_DV
HG_CHUNK = 32
D_FF = 4 * D_MODEL
IN_COLS = S5_WIDTH + 2 * HG_KEY_WIDTH + 2 * HG_VAL_WIDTH + 2 * D_MODEL
SPLIT_POINTS = [S5_WIDTH,
                S5_WIDTH + HG_KEY_WIDTH,
                S5_WIDTH + 2 * HG_KEY_WIDTH,
                S5_WIDTH + 2 * HG_KEY_WIDTH + HG_VAL_WIDTH,
                S5_WIDTH + 2 * HG_KEY_WIDTH + 2 * HG_VAL_WIDTH,
                S5_WIDTH + 2 * HG_KEY_WIDTH + 2 * HG_VAL_WIDTH + D_MODEL]
NORM_EPS = 1e-6

kernel_name = "s5_hgrn2_gated_parallel_decoder_step"

f32 = jnp.float32


def rms_norm(x, g):
    xf = x.astype(f32)
    y = xf * lax.rsqrt(jnp.mean(xf * xf, axis=-1, keepdims=True) + NORM_EPS)
    return (y * g.astype(f32)).astype(x.dtype)


def s5_discretise(a_re, a_im, log_dt, b_re, b_im):
    dt = jnp.exp(log_dt)[:, None]
    mag = jnp.exp(dt * a_re)
    ab_re = mag * jnp.cos(dt * a_im)
    ab_im = mag * jnp.sin(dt * a_im)
    den = a_re * a_re + a_im * a_im
    nr = ab_re - 1.0
    ni = ab_im
    coef_re = (nr * a_re + ni * a_im) / den
    coef_im = (ni * a_re - nr * a_im) / den
    bb_re = coef_re[..., None] * b_re - coef_im[..., None] * b_im
    bb_im = coef_re[..., None] * b_im + coef_im[..., None] * b_re
    return ab_re, ab_im, bb_re, bb_im


def s5_combine(e1, e2):
    a1r, a1i, b1r, b1i = e1
    a2r, a2i, b2r, b2i = e2
    ar = a1r * a2r - a1i * a2i
    ai = a1r * a2i + a1i * a2r
    br = a2r * b1r - a2i * b1i + b2r
    bi = a2r * b1i + a2i * b1r + b2i
    return (ar, ai, br, bi)


def s5_mixer(u, h0_re, h0_im, a_re, a_im, log_dt, b_re, b_im, c_re, c_im, d, w_glu, b_glu):
    bsz, L, _ = u.shape
    uf = u.astype(f32)
    ug = uf.reshape(bsz, L, S5_GROUPS, S5_GROUP)
    ab_re, ab_im, bb_re, bb_im = s5_discretise(a_re.astype(f32), a_im.astype(f32), log_dt.astype(f32),
                                               b_re.astype(f32), b_im.astype(f32))
    bu_re = jnp.einsum('gnc,blgc->blgn', bb_re, ug)
    bu_im = jnp.einsum('gnc,blgc->blgn', bb_im, ug)
    shp = (bsz, L, S5_GROUPS, S5_STATE)
    elems = (jnp.broadcast_to(ab_re, shp), jnp.broadcast_to(ab_im, shp), bu_re, bu_im)
    pr, pi, sr, si = lax.associative_scan(s5_combine, elems, axis=1)
    h0r = h0_re.astype(f32)[:, None]
    h0i = h0_im.astype(f32)[:, None]
    h_re = pr * h0r - pi * h0i + sr
    h_im = pr * h0i + pi * h0r + si
    y = (jnp.einsum('gcn,blgn->blgc', c_re.astype(f32), h_re)
         - jnp.einsum('gcn,blgn->blgc', c_im.astype(f32), h_im))
    y = y.reshape(bsz, L, S5_WIDTH) + d.astype(f32) * uf
    y = jax.nn.gelu(y)
    y = y * jax.nn.sigmoid(y @ w_glu.astype(f32) + b_glu.astype(f32))
    return y.astype(u.dtype), h_re[:, -1], h_im[:, -1]


def hgrn2_mixer(q_raw, f_raw, i_raw, g_raw, s0, lb, norm_g):
    bsz, L, _ = q_raw.shape
    C = math.gcd(L, HG_CHUNK)
    nc = L // C
    q = jax.nn.silu(q_raw.astype(f32))
    f = lb.astype(f32) + (1.0 - lb.astype(f32)) * jax.nn.sigmoid(f_raw.astype(f32))
    log_f = jnp.log(f)
    k = 1.0 - f
    v = i_raw.astype(f32)

    def to_chunks(t, width):
        return jnp.moveaxis(t.reshape(bsz, nc, C, HG_HEADS, width), 1, 0)

    xs = (to_chunks(q, HG_DK), to_chunks(k, HG_DK), to_chunks(v, HG_DV), to_chunks(log_f, HG_DK))
    causal = jnp.tril(jnp.ones((C, C), dtype=bool))

    def chunk_step(S, chunk):
        qc, kc, vc, lfc = chunk
        b = jnp.cumsum(lfc, axis=1)
        o_inter = jnp.einsum('bthk,bhkv->bthv', qc * jnp.exp(b), S)
        diff = b[:, :, None] - b[:, None, :]
        decay = jnp.exp(jnp.where(causal[None, :, :, None, None], diff, -jnp.inf))
        scores = jnp.einsum('bthk,bshk,btshk->btsh', qc, kc, decay)
        o_intra = jnp.einsum('btsh,bshv->bthv', scores, vc)
        b_last = b[:, -1]
        S_new = (jnp.exp(b_last)[..., None] * S
                 + jnp.einsum('bshk,bshv->bhkv', kc * jnp.exp(b_last[:, None] - b), vc))
        return S_new, o_inter + o_intra

    S_fin, o = lax.scan(chunk_step, s0.astype(f32), xs)
    o = jnp.moveaxis(o, 0, 1).reshape(bsz, L, HG_HEADS, HG_DV)
    o = o * lax.rsqrt(jnp.mean(o * o, axis=-1, keepdims=True) + NORM_EPS) * norm_g.astype(f32)
    o = o * jax.nn.silu(g_raw.astype(f32).reshape(bsz, L, HG_HEADS, HG_DV))
    return o.reshape(bsz, L, HG_VAL_WIDTH).astype(q_raw.dtype), S_fin


def trunk_layer(x, s5_h0_re, s5_h0_im, hg_s0, lb,
                norm_mix_pre, norm_mix_post, norm_mlp_pre, norm_mlp_post, w_in, b_in,
                s5_a_re, s5_a_im, s5_log_dt, s5_b_re, s5_b_im, s5_c_re, s5_c_im, s5_d, s5_w_glu, s5_b_glu,
                hg_norm, w_br_s5, w_br_hg, w_out, w_up, w_down):
    h = rms_norm(x, norm_mix_pre)
    proj = h @ w_in + b_in
    u, q, f, i, g, gate_s5, gate_hg = jnp.split(proj, SPLIT_POINTS, axis=-1)
    y_s5, s5_re, s5_im = s5_mixer(u, s5_h0_re, s5_h0_im, s5_a_re, s5_a_im, s5_log_dt,
                                  s5_b_re, s5_b_im, s5_c_re, s5_c_im, s5_d, s5_w_glu, s5_b_glu)
    y_hg, hg_s = hgrn2_mixer(q, f, i, g, hg_s0, lb, hg_norm)
    merged = jax.nn.sigmoid(gate_s5) * (y_s5 @ w_br_s5) + jax.nn.sigmoid(gate_hg) * (y_hg @ w_br_hg)
    x = x + rms_norm(merged @ w_out, norm_mix_post)
    h2 = rms_norm(x, norm_mlp_pre)
    m = jnp.square(jax.nn.relu(h2 @ w_up)) @ w_down
    x = x + rms_norm(m, norm_mlp_post)
    return x, s5_re, s5_im, hg_s


def setup_inputs(seed: int = 0) -> dict:
    key = jax.random.key(seed)
    ks = jax.random.split(key, 32)
    nrm = lambda k, shp, s: jax.random.normal(k, shp, f32) * s
    n_idx = jnp.arange(S5_STATE, dtype=f32)
    a_re = -0.5 + nrm(ks[10], (DEPTH, S5_GROUPS, S5_STATE), 0.01)
    a_im = math.pi * n_idx[None, None, :] + nrm(ks[11], (DEPTH, S5_GROUPS, S5_STATE), 0.01)
    log_dt = jax.random.uniform(ks[12], (DEPTH, S5_GROUPS), f32,
                                math.log(S5_DT_MIN), math.log(S5_DT_MAX))
    return {
        "x_prompt": nrm(ks[0], (BATCH, SEQ, D_MODEL), 1.0),
        "x_sample": nrm(ks[1], (DEC_BATCH, DEC_SEQ, D_MODEL), 1.0),
        "state_s5_re": nrm(ks[2], (DEPTH, DEC_BATCH, S5_GROUPS, S5_STATE), 0.3),
        "state_s5_im": nrm(ks[3], (DEPTH, DEC_BATCH, S5_GROUPS, S5_STATE), 0.3),
        "state_hg": nrm(ks[4], (DEPTH, DEC_BATCH, HG_HEADS, HG_DK, HG_DV), 0.3),
        "norm_mix_pre": 1.0 + nrm(ks[5], (DEPTH, D_MODEL), 0.02),
        "norm_mix_post": 1.0 + nrm(ks[6], (DEPTH, D_MODEL), 0.02),
        "norm_mlp_pre": 1.0 + nrm(ks[7], (DEPTH, D_MODEL), 0.02),
        "norm_mlp_post": 1.0 + nrm(ks[8], (DEPTH, D_MODEL), 0.02),
        "w_in": nrm(ks[9], (DEPTH, D_MODEL, IN_COLS), D_MODEL ** -0.5),
        "b_in": nrm(ks[13], (DEPTH, IN_COLS), 0.01),
        "s5_a_re": a_re,
        "s5_a_im": a_im,
        "s5_log_dt": log_dt,
        "s5_b_re": nrm(ks[14], (DEPTH, S5_GROUPS, S5_STATE, S5_GROUP), S5_GROUP ** -0.5),
        "s5_b_im": nrm(ks[15], (DEPTH, S5_GROUPS, S5_STATE, S5_GROUP), S5_GROUP ** -0.5),
        "s5_c_re": nrm(ks[16], (DEPTH, S5_GROUPS, S5_GROUP, S5_STATE), S5_STATE ** -0.5),
        "s5_c_im": nrm(ks[17], (DEPTH, S5_GROUPS, S5_GROUP, S5_STATE), S5_STATE ** -0.5),
        "s5_d": nrm(ks[18], (DEPTH, S5_WIDTH), 0.5),
        "s5_w_glu": nrm(ks[19], (DEPTH, S5_WIDTH, S5_WIDTH), S5_WIDTH ** -0.5),
        "s5_b_glu": nrm(ks[20], (DEPTH, S5_WIDTH), 0.01),
        "hg_lb_logits": nrm(ks[21], (DEPTH + 1, HG_KEY_WIDTH), 0.1),
        "hg_norm": 1.0 + nrm(ks[22], (DEPTH, HG_DV), 0.02),
        "w_br_s5": nrm(ks[23], (DEPTH, S5_WIDTH, D_MODEL), S5_WIDTH ** -0.5),
        "w_br_hg": nrm(ks[24], (DEPTH, HG_VAL_WIDTH, D_MODEL), HG_VAL_WIDTH ** -0.5),
        "w_out": nrm(ks[25], (DEPTH, D_MODEL, D_MODEL), D_MODEL ** -0.5),
        "w_up": nrm(ks[26], (DEPTH, D_MODEL, D_FF), D_MODEL ** -0.5),
        "w_down": nrm(ks[27], (DEPTH, D_FF, D_MODEL), D_FF ** -0.5),
    }


def reference(x_prompt, x_sample, state_s5_re, state_s5_im, state_hg,
              norm_mix_pre, norm_mix_post, norm_mlp_pre, norm_mlp_post, w_in, b_in,
              s5_a_re, s5_a_im, s5_log_dt, s5_b_re, s5_b_im, s5_c_re, s5_c_im, s5_d, s5_w_glu, s5_b_glu,
              hg_lb_logits, hg_norm, w_br_s5, w_br_hg, w_out, w_up, w_down):
    lb_all = jnp.cumsum(jax.nn.softmax(hg_lb_logits.astype(f32), axis=0), axis=0)
    xp, xs = x_prompt, x_sample
    p_re, p_im, p_hg, s_re, s_im, s_hg = [], [], [], [], [], []
    for l in range(DEPTH):
        w = (norm_mix_pre[l], norm_mix_post[l], norm_mlp_pre[l], norm_mlp_post[l], w_in[l], b_in[l],
             s5_a_re[l], s5_a_im[l], s5_log_dt[l], s5_b_re[l], s5_b_im[l], s5_c_re[l], s5_c_im[l],
             s5_d[l], s5_w_glu[l], s5_b_glu[l], hg_norm[l], w_br_s5[l], w_br_hg[l], w_out[l],
             w_up[l], w_down[l])
        bsz = xp.shape[0]
        xp, r, im, hg = trunk_layer(xp,
                                    jnp.zeros((bsz, S5_GROUPS, S5_STATE), f32),
                                    jnp.zeros((bsz, S5_GROUPS, S5_STATE), f32),
                                    jnp.zeros((bsz, HG_HEADS, HG_DK, HG_DV), f32),
                                    lb_all[l], *w)
        p_re.append(r); p_im.append(im); p_hg.append(hg)
        xs, r, im, hg = trunk_layer(xs, state_s5_re[l], state_s5_im[l], state_hg[l], lb_all[l], *w)
        s_re.append(r); s_im.append(im); s_hg.append(hg)
    return (xp, xs, jnp.stack(p_re), jnp.stack(p_im), jnp.stack(p_hg),
            jnp.stack(s_re), jnp.stack(s_im), jnp.stack(s_hg))
```

```python
import functools
import math

import jax
import jax.numpy as jnp
from jax import lax
from jax.experimental import pallas as pl
from jax.experimental.pallas import tpu as pltpu

f32 = jnp.float32
bf16 = jnp.bfloat16

NORM_EPS = 1e-6
S5_GROUP = 16
S5_STATE = 64
S5_CHUNK = 8
HG_HEADS = 8
HG_DK = 128
HG_CHUNK = 64
LANES = 128
VMEM_LIMIT = 56 * 1024 * 1024
PROMPT_TILE = 512
MLP_TILE = 512
MLP_FCHUNK = 1024
SAMPLE_SEQ_BLOCK = 8


def _sigmoid(x):
    return 0.5 * (jnp.tanh(0.5 * x) + 1.0)


def _gelu_tanh(x):
    c = math.sqrt(2.0 / math.pi)
    return 0.5 * x * (1.0 + jnp.tanh(c * (x + 0.044715 * (x * x * x))))


def _rms(x):
    return x * lax.rsqrt(jnp.mean(x * x, axis=-1, keepdims=True) + NORM_EPS)


def _dot(a, b):
    return jnp.dot(a, b, preferred_element_type=f32)


def _dot_nt(a, b):
    return lax.dot_general(a, b, (((1,), (1,)), ((), ())), preferred_element_type=f32)


def _dot_tn(a, b):
    return lax.dot_general(a, b, (((0,), (0,)), ((), ())), preferred_element_type=f32)


def _block_transpose8(xs, blk):
    xs = list(xs)
    for d in (4, 2, 1):
        keep = (blk & d) == 0
        for i in range(8):
            if i & d == 0:
                a, b = xs[i], xs[i + d]
                xs[i] = jnp.where(keep, a, pltpu.roll(b, d * S5_GROUP, 1))
                xs[i + d] = jnp.where(keep, pltpu.roll(a, LANES - d * S5_GROUP, 1), b)
    return xs


def _mixer_kernel(x_ref, gpre_ref, wu_ref, bu_ref, w1_ref, w2_ref, apr_ref, api_ref, d_ref,
                  wglu_ref, bglu_ref, wh_ref, bh_ref, lb_ref, hgn_ref, wg_ref, bg_ref,
                  wbs_ref, wbh_ref, wout_ref, gpost_ref,
                  y_ref, s5r_ref, s5i_ref, hgs_ref,
                  hb_scr, u_scr, ys_scr, o_scr, yloc_scr, er_scr, ei_scr, cr_scr, ci_scr,
                  st_scr, qfig_scr, yhg_scr, m_scr):
    t_idx = pl.program_id(1)
    n_t = pl.num_programs(1)
    tile = x_ref.shape[1]
    rows = tile // S5_CHUNK
    n_pairs = w1_ref.shape[0]
    n_cols = u_scr.shape[0]

    @pl.when(t_idx == 0)
    def _():
        cr_scr[...] = jnp.zeros_like(cr_scr)
        ci_scr[...] = jnp.zeros_like(ci_scr)
        st_scr[...] = jnp.zeros_like(st_scr)

    x = x_ref[0]
    hb_scr[...] = (_rms(x) * gpre_ref[...]).astype(bf16)

    u = _dot(hb_scr[...], wu_ref[...]) + bu_ref[...]
    for j in range(n_cols):
        u_scr[j] = u[:, j * LANES:(j + 1) * LANES]

    blk = lax.broadcasted_iota(jnp.int32, (rows, LANES), 1) // S5_GROUP
    for j in range(n_cols):
        xs = [u_scr[j, pl.ds(s, rows, stride=S5_CHUNK), :] for s in range(S5_CHUNK)]
        ys = _block_transpose8(xs, blk)
        for q in range(4):
            o_scr[j * 4 + q] = jnp.concatenate([ys[2 * q], ys[2 * q + 1]], axis=1).astype(bf16)

    for p in range(n_pairs):
        r1 = _dot(o_scr[p], w1_ref[p])
        yloc_scr[p] = r1[:, :2 * LANES]
        er_scr[:, p * LANES:(p + 1) * LANES] = r1[:, 2 * LANES:3 * LANES]
        ei_scr[:, p * LANES:(p + 1) * LANES] = r1[:, 3 * LANES:]

    er = er_scr[...]
    ei = ei_scr[...]
    row = lax.broadcasted_iota(jnp.int32, (rows, 1), 0)
    first = row == 0
    cr = cr_scr[...]
    ci = ci_scr[...]
    a_r = apr_ref[0:1, :]
    a_i = api_ref[0:1, :]
    er = er + jnp.where(first, a_r * cr - a_i * ci, 0.0)
    ei = ei + jnp.where(first, a_r * ci + a_i * cr, 0.0)
    for k in range(int(math.log2(rows))):
        d = 1 << k
        p_r = apr_ref[k:k + 1, :]
        p_i = api_ref[k:k + 1, :]
        s_r = pltpu.roll(er, d, 0)
        s_i = pltpu.roll(ei, d, 0)
        valid = row >= d
        er, ei = (er + jnp.where(valid, p_r * s_r - p_i * s_i, 0.0),
                  ei + jnp.where(valid, p_r * s_i + p_i * s_r, 0.0))
    hs_r = jnp.where(first, cr, pltpu.roll(er, 1, 0))
    hs_i = jnp.where(first, ci, pltpu.roll(ei, 1, 0))
    cr_scr[...] = er[rows - 1:rows, :]
    ci_scr[...] = ei[rows - 1:rows, :]

    for j in range(n_cols):
        halves = []
        for q in range(4):
            p = j * 4 + q
            hp = jnp.concatenate([hs_r[:, p * LANES:(p + 1) * LANES],
                                  hs_i[:, p * LANES:(p + 1) * LANES]], axis=1).astype(bf16)
            yp = yloc_scr[p] + _dot(hp, w2_ref[p])
            halves += [yp[:, :LANES], yp[:, LANES:]]
        zs = _block_transpose8(halves, blk)
        for s in range(S5_CHUNK):
            ys_scr[j, pl.ds(s, rows, stride=S5_CHUNK), :] = zs[s]

    y = jnp.concatenate([ys_scr[j] for j in range(n_cols)], axis=1)
    u = jnp.concatenate([u_scr[j] for j in range(n_cols)], axis=1)
    y = _gelu_tanh(y + d_ref[...] * u)
    z = _dot(y.astype(bf16), wglu_ref[...]) + bglu_ref[...]
    ys5 = (y * _sigmoid(z)).astype(bf16)
    dm = wbs_ref.shape[1]
    gate_s5 = _sigmoid(_dot(hb_scr[...], wg_ref[:, :dm]) + bg_ref[:, :dm])
    m_scr[...] = gate_s5 * _dot(ys5, wbs_ref[...])

    ch = HG_CHUNK
    ri = lax.broadcasted_iota(jnp.int32, (ch, ch), 0)
    ci_ = lax.broadcasted_iota(jnp.int32, (ch, ch), 1)
    causal = ci_ <= ri
    tril = causal.astype(bf16)
    for h in range(HG_HEADS):
        qfig_scr[...] = _dot(hb_scr[...], wh_ref[h]) + bh_ref[h]
        lb = lb_ref[h]
        hgn = hgn_ref[...]

        def chunk_body(c, carry, h=h, lb=lb, hgn=hgn):
            r0 = pl.multiple_of(c * ch, ch)
            blk4 = qfig_scr[pl.ds(r0, ch), :]
            q_raw = blk4[:, :HG_DK]
            f_raw = blk4[:, HG_DK:2 * HG_DK]
            v = blk4[:, 2 * HG_DK:3 * HG_DK]
            g_raw = blk4[:, 3 * HG_DK:]
            q = q_raw * _sigmoid(q_raw)
            f = lb + (1.0 - lb) * _sigmoid(f_raw)
            lf = jnp.log(f)
            kk = 1.0 - f
            hi = lf.astype(bf16)
            lo = (lf - hi.astype(f32)).astype(bf16)
            cs = _dot(tril, jnp.concatenate([hi, lo], axis=1))
            b = cs[:, :HG_DK] + cs[:, HG_DK:]
            mid = b[ch // 2 - 1:ch // 2, :]
            b_last = b[ch - 1:ch, :]
            qh = q * jnp.exp(b - mid)
            kh = kk * jnp.exp(mid - b)
            sc = _dot_nt(qh.astype(bf16), kh.astype(bf16))
            sc = jnp.where(causal, sc, 0.0)
            vb = v.astype(bf16)
            o = _dot(sc.astype(bf16), vb)
            st = st_scr[h]
            o = o + _dot_nt((qh * jnp.exp(mid)).astype(bf16), st.astype(bf16))
            kd = (kh * jnp.exp(b_last - mid)).astype(bf16)
            st_scr[h] = st * jnp.exp(b_last) + _dot_tn(vb, kd)
            on = _rms(o) * hgn * (g_raw * _sigmoid(g_raw))
            yhg_scr[pl.ds(r0, ch), h * HG_DK:(h + 1) * HG_DK] = on.astype(bf16)
            return carry

        lax.fori_loop(0, tile // ch, chunk_body, 0)

    gate_hg = _sigmoid(_dot(hb_scr[...], wg_ref[:, dm:]) + bg_ref[:, dm:])
    merged = m_scr[...] + gate_hg * _dot(yhg_scr[...], wbh_ref[...])
    mo = _dot(merged.astype(bf16), wout_ref[...])
    y_ref[0] = x_ref[0] + _rms(mo) * gpost_ref[...]

    @pl.when(t_idx == n_t - 1)
    def _():
        s5r_ref[0] = cr_scr[...]
        s5i_ref[0] = ci_scr[...]
        for h in range(HG_HEADS):
            hgs_ref[0, h] = st_scr[h].T


def _const_spec(shape):
    nd = len(shape)
    return pl.BlockSpec(shape, lambda *_: (0,) * nd, pipeline_mode=pl.Buffered(1))


def _mixer_call(x, w):
    bsz, seq, dm = x.shape
    tile = min(PROMPT_TILE, seq)
    assert seq % tile == 0 and tile % HG_CHUNK == 0
    rows = tile // S5_CHUNK
    assert rows & (rows - 1) == 0 and w["apr"].shape[0] >= int(math.log2(rows))
    s5w = w["wu"].shape[1]
    n_cols = s5w // LANES
    n_pairs = w["w1"].shape[0]
    sdim = w["apr"].shape[1]
    consts = [w[k] for k in ("gpre", "wu", "bu", "w1", "w2", "apr", "api", "d", "wglu", "bglu",
                             "wh", "bh", "lb", "hgn", "wg", "bg", "wbs", "wbh", "wout", "gpost")]
    in_specs = [pl.BlockSpec((1, tile, dm), lambda b, t: (b, t, 0))]
    in_specs += [_const_spec(c.shape) for c in consts]
    out_shape = (jax.ShapeDtypeStruct((bsz, seq, dm), f32),
                 jax.ShapeDtypeStruct((bsz, 1, sdim), f32),
                 jax.ShapeDtypeStruct((bsz, 1, sdim), f32),
                 jax.ShapeDtypeStruct((bsz, HG_HEADS, HG_DK, HG_DK), f32))
    out_specs = (pl.BlockSpec((1, tile, dm), lambda b, t: (b, t, 0)),
                 pl.BlockSpec((1, 1, sdim), lambda b, t: (b, 0, 0)),
                 pl.BlockSpec((1, 1, sdim), lambda b, t: (b, 0, 0)),
                 pl.BlockSpec((1, HG_HEADS, HG_DK, HG_DK), lambda b, t: (b, 0, 0, 0)))
    scratch = [
        pltpu.VMEM((tile, dm), bf16),
        pltpu.VMEM((n_cols, tile, LANES), f32),
        pltpu.VMEM((n_cols, tile, LANES), f32),
        pltpu.VMEM((n_pairs, rows, 2 * LANES), bf16),
        pltpu.VMEM((n_pairs, rows, 2 * LANES), f32),
        pltpu.VMEM((rows, sdim), f32),
        pltpu.VMEM((rows, sdim), f32),
        pltpu.VMEM((1, sdim), f32),
        pltpu.VMEM((1, sdim), f32),
        pltpu.VMEM((HG_HEADS, HG_DK, HG_DK), f32),
        pltpu.VMEM((tile, 4 * HG_DK), f32),
        pltpu.VMEM((tile, HG_HEADS * HG_DK), bf16),
        pltpu.VMEM((tile, dm), f32),
    ]
    return pl.pallas_call(
        _mixer_kernel,
        grid=(bsz, seq // tile),
        in_specs=in_specs,
        out_specs=out_specs,
        out_shape=out_shape,
        scratch_shapes=scratch,
        compiler_params=pltpu.CompilerParams(
            dimension_semantics=("arbitrary", "arbitrary"), vmem_limit_bytes=VMEM_LIMIT),
        name="prompt_mixer",
    )(x, *consts)


def _mlp_body(x, gpre_ref, wup_ref, wdn_ref, gpost_ref):
    h2 = (_rms(x) * gpre_ref[...]).astype(bf16)
    dff = wup_ref.shape[1]
    acc = None
    for c in range(dff // MLP_FCHUNK):
        a = _dot(h2, wup_ref[:, c * MLP_FCHUNK:(c + 1) * MLP_FCHUNK])
        a = jnp.maximum(a, 0.0)
        part = _dot((a * a).astype(bf16), wdn_ref[c * MLP_FCHUNK:(c + 1) * MLP_FCHUNK, :])
        acc = part if acc is None else acc + part
    return x + _rms(acc) * gpost_ref[...]


def _mlp_kernel(x_ref, gpre_ref, wup_ref, wdn_ref, gpost_ref, y_ref):
    y_ref[...] = _mlp_body(x_ref[...], gpre_ref, wup_ref, wdn_ref, gpost_ref)


def _mlp_call(x2d, w):
    n, dm = x2d.shape
    tile = min(MLP_TILE, n)
    assert n % tile == 0
    consts = [w[k] for k in ("g2pre", "wup", "wdn", "g2post")]
    return pl.pallas_call(
        _mlp_kernel,
        grid=(n // tile,),
        in_specs=[pl.BlockSpec((tile, dm), lambda i: (i, 0))] + [_const_spec(c.shape) for c in consts],
        out_specs=pl.BlockSpec((tile, dm), lambda i: (i, 0)),
        out_shape=jax.ShapeDtypeStruct((n, dm), f32),
        compiler_params=pltpu.CompilerParams(
            dimension_semantics=("arbitrary",), vmem_limit_bytes=VMEM_LIMIT),
        name="mlp",
    )(x2d, *consts)


def _sample_pre_kernel(x_ref, h0r_ref, h0i_ref, gpre_ref, wu_ref, bu_ref, bblk_ref, cblk_ref,
                       abr_ref, abi_ref, d_ref, wglu_ref, bglu_ref, whq_ref, bhq_ref, lb_ref, hgn_ref,
                       wg_ref, bg_ref, wbs_ref,
                       m5_ref, ghg_ref, gn_ref, qt_ref, kd_ref, v_ref, oin_ref, e3_ref,
                       s5r_ref, s5i_ref):
    n_seq = h0r_ref.shape[0]
    steps = x_ref.shape[0] // n_seq
    dm = x_ref.shape[1]
    sdim = h0r_ref.shape[1]
    hw = HG_HEADS * HG_DK
    x = x_ref[...]
    hb = (_rms(x) * gpre_ref[...]).astype(bf16)

    u = _dot(hb, wu_ref[...]) + bu_ref[...]
    bu = _dot(u.astype(bf16), bblk_ref[...])
    hr = h0r_ref[...]
    hi = h0i_ref[...]
    a_r = abr_ref[...]
    a_i = abi_ref[...]
    ys = []
    for t in range(steps):
        sl = slice(t * n_seq, (t + 1) * n_seq)
        hr, hi = (a_r * hr - a_i * hi + bu[sl, :sdim], a_r * hi + a_i * hr + bu[sl, sdim:])
        ys.append(_dot(jnp.concatenate([hr, hi], axis=1).astype(bf16), cblk_ref[...]))
    s5r_ref[...] = hr
    s5i_ref[...] = hi
    y = jnp.concatenate(ys, axis=0)
    y = _gelu_tanh(y + d_ref[...] * u)
    z = _dot(y.astype(bf16), wglu_ref[...]) + bglu_ref[...]
    ys5 = (y * _sigmoid(z)).astype(bf16)
    gates = _dot(hb, wg_ref[...]) + bg_ref[...]
    m5_ref[...] = _sigmoid(gates[:, :dm]) * _dot(ys5, wbs_ref[...])
    ghg_ref[...] = _sigmoid(gates[:, dm:])

    pq = _dot(hb, whq_ref[...]) + bhq_ref[...]
    q_raw = pq[:, :hw]
    f_raw = pq[:, hw:2 * hw]
    v = pq[:, 2 * hw:3 * hw]
    g_raw = pq[:, 3 * hw:]
    q = q_raw * _sigmoid(q_raw)
    lb = lb_ref[...]
    f = lb + (1.0 - lb) * _sigmoid(f_raw)
    lf = jnp.log(f)
    kk = 1.0 - f
    gn_ref[...] = g_raw * _sigmoid(g_raw) * hgn_ref[...]
    v_ref[...] = v
    bs = []
    acc = None
    for t in range(steps):
        sl = slice(t * n_seq, (t + 1) * n_seq)
        acc = lf[sl] if acc is None else acc + lf[sl]
        bs.append(acc)
    b_last = bs[-1]
    e3_ref[...] = jnp.exp(b_last)
    for t in range(steps):
        sl = slice(t * n_seq, (t + 1) * n_seq)
        qt_ref[sl, :] = q[sl] * jnp.exp(bs[t])
        kd_ref[sl, :] = kk[sl] * jnp.exp(b_last - bs[t])
        o_t = None
        for s in range(t + 1):
            sls = slice(s * n_seq, (s + 1) * n_seq)
            prod = q[sl] * kk[sls] * jnp.exp(bs[t] - bs[s])
            parts = []
            for h in range(HG_HEADS):
                hs = slice(h * HG_DK, (h + 1) * HG_DK)
                wgt = jnp.sum(prod[:, hs], axis=1, keepdims=True)
                parts.append(wgt * v[sls, hs])
            term = jnp.concatenate(parts, axis=1)
            o_t = term if o_t is None else o_t + term
        oin_ref[sl, :] = o_t


def _sample_pre_call(x_tm, h0r, h0i, w):
    rows, dm = x_tm.shape
    n_seq, sdim = h0r.shape
    hw = HG_HEADS * HG_DK
    ins = [x_tm, h0r, h0i] + [w[k] for k in ("gpre", "wu", "bu", "bblk", "cblk", "abr", "abi", "d",
                                              "wglu", "bglu", "whq", "bhq", "lbrow", "hgnrow",
                                              "wg", "bg", "wbs")]
    out_shape = (jax.ShapeDtypeStruct((rows, dm), f32),
                 jax.ShapeDtypeStruct((rows, dm), f32),
                 jax.ShapeDtypeStruct((rows, hw), f32),
                 jax.ShapeDtypeStruct((rows, hw), f32),
                 jax.ShapeDtypeStruct((rows, hw), f32),
                 jax.ShapeDtypeStruct((rows, hw), f32),
                 jax.ShapeDtypeStruct((rows, hw), f32),
                 jax.ShapeDtypeStruct((n_seq, hw), f32),
                 jax.ShapeDtypeStruct((n_seq, sdim), f32),
                 jax.ShapeDtypeStruct((n_seq, sdim), f32))
    return pl.pallas_call(
        _sample_pre_kernel,
        out_shape=out_shape,
        compiler_params=pltpu.CompilerParams(vmem_limit_bytes=VMEM_LIMIT),
        name="sample_pre",
    )(*ins)


def _sample_state_kernel(s0_ref, qt_ref, v_ref, cols_ref, snew_ref, oint_ref):
    nb = s0_ref.shape[0]
    tpad = qt_ref.shape[1]
    ncol = cols_ref.shape[1]
    for i in range(nb):
        xt = jnp.concatenate([cols_ref[i], jnp.zeros((LANES - ncol, HG_DK), f32)], axis=0).T
        qt = qt_ref[i]
        vv = v_ref[i]
        outs = []
        for h in range(HG_HEADS):
            hs = slice(h * HG_DK, (h + 1) * HG_DK)
            s0 = s0_ref[i, h]
            outs.append(_dot(qt[:, hs].astype(bf16), s0.astype(bf16)))
            base = h * 8
            sn = xt[:, base + 4:base + 5] * s0
            for t in range(4):
                sn = sn + xt[:, base + t:base + t + 1] * vv[t:t + 1, hs]
            snew_ref[i, h] = sn
        oint_ref[i] = jnp.concatenate(outs, axis=1)


def _sample_state_call(s0, qt_bt, v_bt, cols):
    n_seq = s0.shape[0]
    nb = SAMPLE_SEQ_BLOCK
    tpad = qt_bt.shape[1]
    hw = HG_HEADS * HG_DK
    return pl.pallas_call(
        _sample_state_kernel,
        grid=(n_seq // nb,),
        in_specs=[pl.BlockSpec((nb, HG_HEADS, HG_DK, HG_DK), lambda i: (i, 0, 0, 0)),
                  pl.BlockSpec((nb, tpad, hw), lambda i: (i, 0, 0)),
                  pl.BlockSpec((nb, tpad, hw), lambda i: (i, 0, 0)),
                  pl.BlockSpec((nb,) + cols.shape[1:], lambda i: (i, 0, 0))],
        out_specs=(pl.BlockSpec((nb, HG_HEADS, HG_DK, HG_DK), lambda i: (i, 0, 0, 0)),
                   pl.BlockSpec((nb, tpad, hw), lambda i: (i, 0, 0))),
        out_shape=(jax.ShapeDtypeStruct(s0.shape, f32),
                   jax.ShapeDtypeStruct((n_seq, tpad, hw), f32)),
        compiler_params=pltpu.CompilerParams(
            dimension_semantics=("arbitrary",), vmem_limit_bytes=VMEM_LIMIT),
        name="sample_state",
    )(s0, qt_bt, v_bt, cols)


def _sample_post_kernel(x_ref, oint_ref, oin_ref, gn_ref, m5_ref, ghg_ref, wbh_ref, wout_ref, gpost_ref,
                        g2pre_ref, wup_ref, wdn_ref, g2post_ref, y_ref):
    o = oint_ref[...] + oin_ref[...]
    parts = []
    for h in range(HG_HEADS):
        hs = slice(h * HG_DK, (h + 1) * HG_DK)
        parts.append(_rms(o[:, hs]))
    yhg = (jnp.concatenate(parts, axis=1) * gn_ref[...]).astype(bf16)
    merged = m5_ref[...] + ghg_ref[...] * _dot(yhg, wbh_ref[...])
    mo = _dot(merged.astype(bf16), wout_ref[...])
    x1 = x_ref[...] + _rms(mo) * gpost_ref[...]
    y_ref[...] = _mlp_body(x1, g2pre_ref, wup_ref, wdn_ref, g2post_ref)


def _sample_post_call(x_tm, oint_tm, oin, gn, m5, ghg, w):
    ins = [x_tm, oint_tm, oin, gn, m5, ghg] + [w[k] for k in ("wbh", "wout", "gpost", "g2pre", "wup",
                                                              "wdn", "g2post")]
    return pl.pallas_call(
        _sample_post_kernel,
        out_shape=jax.ShapeDtypeStruct(x_tm.shape, f32),
        compiler_params=pltpu.CompilerParams(vmem_limit_bytes=VMEM_LIMIT),
        name="sample_post",
    )(*ins)


def _prep_weights(l, norm_mix_pre, norm_mix_post, norm_mlp_pre, norm_mlp_post, w_in, b_in,
                  s5_a_re, s5_a_im, s5_log_dt, s5_b_re, s5_b_im, s5_c_re, s5_c_im, s5_d, s5_w_glu,
                  s5_b_glu, hg_lb_logits, hg_norm, w_br_s5, w_br_hg, w_out, w_up, w_down, n_pow):
    hp = lax.Precision.HIGHEST
    dm = w_in.shape[1]
    s5w = s5_d.shape[1]
    kw = HG_HEADS * HG_DK
    n_groups = s5w // S5_GROUP
    n_pairs = n_groups // 2
    w = {}
    row = lambda a: a.astype(f32).reshape(1, -1)
    w["gpre"], w["gpost"] = row(norm_mix_pre[l]), row(norm_mix_post[l])
    w["g2pre"], w["g2post"] = row(norm_mlp_pre[l]), row(norm_mlp_post[l])
    wi, bi = w_in[l], b_in[l].astype(f32)
    w["wu"], w["bu"] = wi[:, :s5w].astype(bf16), row(bi[:s5w])
    seg = lambda a, i: a[..., s5w + i * kw: s5w + (i + 1) * kw]
    wq = jnp.stack([seg(wi, i).reshape(dm, HG_HEADS, HG_DK) for i in range(4)], axis=2)
    w["wh"] = jnp.transpose(wq, (1, 0, 2, 3)).reshape(HG_HEADS, dm, 4 * HG_DK).astype(bf16)
    bq = jnp.stack([seg(bi, i).reshape(HG_HEADS, HG_DK) for i in range(4)], axis=1)
    w["bh"] = bq.reshape(HG_HEADS, 1, 4 * HG_DK)
    w["whq"] = wi[:, s5w:s5w + 4 * kw].astype(bf16)
    w["bhq"] = row(bi[s5w:s5w + 4 * kw])
    w["wg"] = wi[:, s5w + 4 * kw:].astype(bf16)
    w["bg"] = row(bi[s5w + 4 * kw:])
    lb_all = jnp.cumsum(jax.nn.softmax(hg_lb_logits.astype(f32), axis=0), axis=0)
    w["lb"] = lb_all[l].reshape(HG_HEADS, 1, HG_DK)
    w["lbrow"] = row(lb_all[l])
    w["hgn"] = row(hg_norm[l])
    w["hgnrow"] = row(jnp.tile(hg_norm[l].astype(f32), HG_HEADS))
    w["d"] = row(s5_d[l])
    w["wglu"], w["bglu"] = s5_w_glu[l].astype(bf16), row(s5_b_glu[l])
    w["wbs"], w["wbh"], w["wout"] = w_br_s5[l].astype(bf16), w_br_hg[l].astype(bf16), w_out[l].astype(bf16)
    w["wup"], w["wdn"] = w_up[l].astype(bf16), w_down[l].astype(bf16)

    a_re, a_im = s5_a_re[l].astype(f32), s5_a_im[l].astype(f32)
    dt = jnp.exp(s5_log_dt[l].astype(f32))[:, None]
    mag = jnp.exp(dt * a_re)
    abr, abi = mag * jnp.cos(dt * a_im), mag * jnp.sin(dt * a_im)
    den = a_re * a_re + a_im * a_im
    nr, ni = abr - 1.0, abi
    cfr, cfi = (nr * a_re + ni * a_im) / den, (ni * a_re - nr * a_im) / den
    b_re, b_im = s5_b_re[l].astype(f32), s5_b_im[l].astype(f32)
    bbr = cfr[..., None] * b_re - cfi[..., None] * b_im
    bbi = cfr[..., None] * b_im + cfi[..., None] * b_re
    c_re, c_im = s5_c_re[l].astype(f32), s5_c_im[l].astype(f32)
    w["abr"], w["abi"] = abr.reshape(1, -1), abi.reshape(1, -1)

    pr, pi = [jnp.ones_like(abr)], [jnp.zeros_like(abr)]
    for _ in range(S5_CHUNK):
        pr, pi = pr + [pr[-1] * abr - pi[-1] * abi], pi + [pr[-1] * abi + pi[-1] * abr]
    pr, pi = jnp.stack(pr), jnp.stack(pi)
    cpr = c_re[None] * pr[:, :, None, :] - c_im[None] * pi[:, :, None, :]
    cpi = c_re[None] * pi[:, :, None, :] + c_im[None] * pr[:, :, None, :]
    taps = (jnp.einsum("jgon,gni->jgoi", cpr[:S5_CHUNK], bbr, precision=hp)
            - jnp.einsum("jgon,gni->jgoi", cpi[:S5_CHUNK], bbi, precision=hp))
    s_idx = jnp.arange(S5_CHUNK)[:, None]
    t_idx = jnp.arange(S5_CHUNK)[None, :]
    lag = t_idx - s_idx
    toe = jnp.where((lag >= 0)[:, :, None, None, None],
                    taps[jnp.clip(lag, 0, S5_CHUNK - 1)], 0.0)
    toe = jnp.transpose(toe, (2, 0, 4, 1, 3)).reshape(n_groups, LANES, LANES)
    prs, pis = pr[S5_CHUNK - 1::-1][:S5_CHUNK], pi[S5_CHUNK - 1::-1][:S5_CHUNK]
    ber = prs[..., None] * bbr[None] - pis[..., None] * bbi[None]
    bei = prs[..., None] * bbi[None] + pis[..., None] * bbr[None]
    ber = jnp.transpose(ber, (1, 0, 3, 2)).reshape(n_groups, LANES, S5_STATE)
    bei = jnp.transpose(bei, (1, 0, 3, 2)).reshape(n_groups, LANES, S5_STATE)
    car = jnp.transpose(cpr[1:], (1, 3, 0, 2)).reshape(n_groups, S5_STATE, LANES)
    cai = jnp.transpose(cpi[1:], (1, 3, 0, 2)).reshape(n_groups, S5_STATE, LANES)

    def pair_diag(m):
        g, r, c = m.shape
        m = m.reshape(g // 2, 2, r, c)
        z = jnp.zeros_like(m[:, 0])
        return jnp.concatenate([jnp.concatenate([m[:, 0], z], axis=2),
                                jnp.concatenate([z, m[:, 1]], axis=2)], axis=1)

    w["w1"] = jnp.concatenate([pair_diag(toe), pair_diag(ber), pair_diag(bei)], axis=2).astype(bf16)
    w["w2"] = jnp.concatenate([pair_diag(car), pair_diag(-cai)], axis=1).astype(bf16)
    qr, qi = pr[S5_CHUNK], pi[S5_CHUNK]
    aps_r, aps_i = [], []
    for _ in range(n_pow):
        aps_r.append(qr.reshape(-1))
        aps_i.append(qi.reshape(-1))
        qr, qi = qr * qr - qi * qi, 2.0 * qr * qi
    w["apr"], w["api"] = jnp.stack(aps_r), jnp.stack(aps_i)

    eye = jnp.eye(n_groups, dtype=f32)
    def bdiag(m):
        g, r, c = m.shape
        return (eye[:, None, :, None] * m[:, :, None, :]).reshape(g * r, g * c)
    bt_r, bt_i = jnp.transpose(bbr, (0, 2, 1)), jnp.transpose(bbi, (0, 2, 1))
    w["bblk"] = jnp.concatenate([bdiag(bt_r), bdiag(bt_i)], axis=1).astype(bf16)
    ct_r, ct_i = jnp.transpose(c_re, (0, 2, 1)), jnp.transpose(c_im, (0, 2, 1))
    w["cblk"] = jnp.concatenate([bdiag(ct_r), bdiag(-ct_i)], axis=0).astype(bf16)
    return w


def kernel(x_prompt, x_sample, state_s5_re, state_s5_im, state_hg, norm_mix_pre, norm_mix_post,
           norm_mlp_pre, norm_mlp_post, w_in, b_in, s5_a_re, s5_a_im, s5_log_dt, s5_b_re, s5_b_im,
           s5_c_re, s5_c_im, s5_d, s5_w_glu, s5_b_glu, hg_lb_logits, hg_norm, w_br_s5, w_br_hg,
           w_out, w_up, w_down):
    depth = w_in.shape[0]
    bsz, seq, dm = x_prompt.shape
    n_seq, steps, _ = x_sample.shape
    n_groups, n_state = s5_a_re.shape[1], s5_a_re.shape[2]
    hw = HG_HEADS * HG_DK
    assert depth == 1 and steps == 4 and n_state == S5_STATE
    n_pow = max(1, int(math.log2(min(PROMPT_TILE, seq) // S5_CHUNK)))
    xp, xs = x_prompt, x_sample
    outs = [[] for _ in range(6)]
    for l in range(depth):
        w = _prep_weights(l, norm_mix_pre, norm_mix_post, norm_mlp_pre, norm_mlp_post, w_in, b_in,
                          s5_a_re, s5_a_im, s5_log_dt, s5_b_re, s5_b_im, s5_c_re, s5_c_im, s5_d,
                          s5_w_glu, s5_b_glu, hg_lb_logits, hg_norm, w_br_s5, w_br_hg, w_out, w_up,
                          w_down, n_pow)
        x1, p_re, p_im, p_hg = _mixer_call(xp, w)
        xp = _mlp_call(x1.reshape(bsz * seq, dm), w).reshape(bsz, seq, dm)
        outs[0].append(p_re.reshape(bsz, n_groups, n_state))
        outs[1].append(p_im.reshape(bsz, n_groups, n_state))
        outs[2].append(p_hg)
        x_tm = jnp.transpose(xs, (1, 0, 2)).reshape(steps * n_seq, dm)
        h0r = state_s5_re[l].reshape(n_seq, n_groups * n_state)
        h0i = state_s5_im[l].reshape(n_seq, n_groups * n_state)
        m5, ghg, gn, qt, kd, v, oin, e3, s_re, s_im = _sample_pre_call(x_tm, h0r, h0i, w)
        tpad = 8
        def to_bt(a):
            a = jnp.transpose(a.reshape(steps, n_seq, -1), (1, 0, 2))
            return jnp.pad(a, ((0, 0), (0, tpad - steps), (0, 0)))
        kd4 = jnp.transpose(kd.reshape(steps, n_seq, HG_HEADS, HG_DK), (1, 2, 0, 3))
        cols = jnp.concatenate([kd4, e3.reshape(n_seq, HG_HEADS, 1, HG_DK),
                                jnp.zeros((n_seq, HG_HEADS, 3, HG_DK), f32)], axis=2)
        cols = cols.reshape(n_seq, HG_HEADS * 8, HG_DK)
        s_hg, oint_bt = _sample_state_call(state_hg[l], to_bt(qt), to_bt(v), cols)
        oint_tm = jnp.transpose(oint_bt[:, :steps], (1, 0, 2)).reshape(steps * n_seq, hw)
        y_tm = _sample_post_call(x_tm, oint_tm, oin, gn, m5, ghg, w)
        xs = jnp.transpose(y_tm.reshape(steps, n_seq, dm), (1, 0, 2))
        outs[3].append(s_re.reshape(n_seq, n_groups, n_state))
        outs[4].append(s_im.reshape(n_seq, n_groups, n_state))
        outs[5].append(s_hg)
    return (xp, xs) + tuple(jnp.stack(o) for o in outs)
```

```python
import functools
import math

import jax
import jax.numpy as jnp
from jax import lax
from jax.experimental import pallas as pl
from jax.experimental.pallas import tpu as pltpu

f32 = jnp.float32
bf16 = jnp.bfloat16

NORM_EPS = 1e-6
S5_GROUP = 16
S5_STATE = 64
S5_CHUNK = 8
HG_HEADS = 8
HG_DK = 128
HG_CHUNK = 64
LANES = 128
VMEM_LIMIT = 56 * 1024 * 1024
PROMPT_TILE = 512
MLP_TILE = 512
MLP_FCHUNK = 1024
SAMPLE_SEQ_BLOCK = 8


def _sigmoid(x):
    return 0.5 * (jnp.tanh(0.5 * x) + 1.0)


def _gelu_tanh(x):
    c = math.sqrt(2.0 / math.pi)
    return 0.5 * x * (1.0 + jnp.tanh(c * (x + 0.044715 * (x * x * x))))


def _rms(x):
    return x * lax.rsqrt(jnp.mean(x * x, axis=-1, keepdims=True) + NORM_EPS)


def _dot(a, b):
    return jnp.dot(a, b, preferred_element_type=f32)


def _dot_nt(a, b):
    return lax.dot_general(a, b, (((1,), (1,)), ((), ())), preferred_element_type=f32)


def _dot_tn(a, b):
    return lax.dot_general(a, b, (((0,), (0,)), ((), ())), preferred_element_type=f32)


def _block_transpose8(xs, blk):
    xs = list(xs)
    for d in (4, 2, 1):
        keep = (blk & d) == 0
        for i in range(8):
            if i & d == 0:
                a, b = xs[i], xs[i + d]
                xs[i] = jnp.where(keep, a, pltpu.roll(b, d * S5_GROUP, 1))
                xs[i + d] = jnp.where(keep, pltpu.roll(a, LANES - d * S5_GROUP, 1), b)
    return xs


def _mixer_kernel(x_ref, gpre_ref, wu_ref, bu_ref, w1_ref, w2_ref, apr_ref, api_ref, d_ref,
                  wglu_ref, bglu_ref, wh_ref, bh_ref, lb_ref, hgn_ref, wg_ref, bg_ref,
                  wbs_ref, wbh_ref, wout_ref, gpost_ref,
                  y_ref, s5r_ref, s5i_ref, hgs_ref,
                  hb_scr, u_scr, ys_scr, o_scr, yloc_scr, er_scr, ei_scr, cr_scr, ci_scr,
                  st_scr, qfig_scr, yhg_scr, m_scr):
    t_idx = pl.program_id(1)
    n_t = pl.num_programs(1)
    tile = x_ref.shape[1]
    rows = tile // S5_CHUNK
    n_pairs = w1_ref.shape[0]
    n_cols = u_scr.shape[0]

    @pl.when(t_idx == 0)
    def _():
        cr_scr[...] = jnp.zeros_like(cr_scr)
        ci_scr[...] = jnp.zeros_like(ci_scr)
        st_scr[...] = jnp.zeros_like(st_scr)

    x = x_ref[0]
    hb_scr[...] = (_rms(x) * gpre_ref[...]).astype(bf16)

    u = _dot(hb_scr[...], wu_ref[...]) + bu_ref[...]
    for j in range(n_cols):
        u_scr[j] = u[:, j * LANES:(j + 1) * LANES]

    blk = lax.broadcasted_iota(jnp.int32, (rows, LANES), 1) // S5_GROUP
    for j in range(n_cols):
        xs = [u_scr[j, pl.ds(s, rows, stride=S5_CHUNK), :] for s in range(S5_CHUNK)]
        ys = _block_transpose8(xs, blk)
        for q in range(4):
            o_scr[j * 4 + q] = jnp.concatenate([ys[2 * q], ys[2 * q + 1]], axis=1).astype(bf16)

    for p in range(n_pairs):
        r1 = _dot(o_scr[p], w1_ref[p])
        yloc_scr[p] = r1[:, :2 * LANES]
        er_scr[:, p * LANES:(p + 1) * LANES] = r1[:, 2 * LANES:3 * LANES]
        ei_scr[:, p * LANES:(p + 1) * LANES] = r1[:, 3 * LANES:]

    er = er_scr[...]
    ei = ei_scr[...]
    row = lax.broadcasted_iota(jnp.int32, (rows, 1), 0)
    first = row == 0
    cr = cr_scr[...]
    ci = ci_scr[...]
    a_r = apr_ref[0:1, :]
    a_i = api_ref[0:1, :]
    er = er + jnp.where(first, a_r * cr - a_i * ci, 0.0)
    ei = ei + jnp.where(first, a_r * ci + a_i * cr, 0.0)
    for k in range(int(math.log2(rows))):
        d = 1 << k
        p_r = apr_ref[k:k + 1, :]
        p_i = api_ref[k:k + 1, :]
        s_r = pltpu.roll(er, d, 0)
        s_i = pltpu.roll(ei, d, 0)
        valid = row >= d
        er, ei = (er + jnp.where(valid, p_r * s_r - p_i * s_i, 0.0),
                  ei + jnp.where(valid, p_r * s_i + p_i * s_r, 0.0))
    hs_r = jnp.where(first, cr, pltpu.roll(er, 1, 0))
    hs_i = jnp.where(first, ci, pltpu.roll(ei, 1, 0))
    cr_scr[...] = er[rows - 1:rows, :]
    ci_scr[...] = ei[rows - 1:rows, :]

    for j in range(n_cols):
        halves = []
        for q in range(4):
            p = j * 4 + q
            hp = jnp.concatenate([hs_r[:, p * LANES:(p + 1) * LANES],
                                  hs_i[:, p * LANES:(p + 1) * LANES]], axis=1).astype(bf16)
            yp = yloc_scr[p] + _dot(hp, w2_ref[p])
            halves += [yp[:, :LANES], yp[:, LANES:]]
        zs = _block_transpose8(halves, blk)
        for s in range(S5_CHUNK):
            ys_scr[j, pl.ds(s, rows, stride=S5_CHUNK), :] = zs[s]

    y = jnp.concatenate([ys_scr[j] for j in range(n_cols)], axis=1)
    u = jnp.concatenate([u_scr[j] for j in range(n_cols)], axis=1)
    y = _gelu_tanh(y + d_ref[...] * u)
    z = _dot(y.astype(bf16), wglu_ref[...]) + bglu_ref[...]
    ys5 = (y * _sigmoid(z)).astype(bf16)
    dm = wbs_ref.shape[1]
    gate_s5 = _sigmoid(_dot(hb_scr[...], wg_ref[:, :dm]) + bg_ref[:, :dm])
    m_scr[...] = gate_s5 * _dot(ys5, wbs_ref[...])

    ch = HG_CHUNK
    ri = lax.broadcasted_iota(jnp.int32, (ch, ch), 0)
    ci_ = lax.broadcasted_iota(jnp.int32, (ch, ch), 1)
    causal = ci_ <= ri
    tril = causal.astype(bf16)
    hw = HG_HEADS * HG_DK
    qfig_scr[...] = _dot(hb_scr[...], wh_ref[...]) + bh_ref[...]
    lb = lb_ref[...]
    hgn = hgn_ref[...]

    def chunk_body(c, carry):
        r0 = pl.multiple_of(c * ch, ch)
        q_raw = qfig_scr[pl.ds(r0, ch), 0:hw]
        f_raw = qfig_scr[pl.ds(r0, ch), hw:2 * hw]
        q = q_raw * _sigmoid(q_raw)
        f = lb + (1.0 - lb) * _sigmoid(f_raw)
        lf = jnp.log(f)
        kk = 1.0 - f
        hi = lf.astype(bf16)
        lo = (lf - hi.astype(f32)).astype(bf16)
        cs = _dot(tril, jnp.concatenate([hi, lo], axis=1))
        b = cs[:, :hw] + cs[:, hw:]
        mid = b[ch // 2 - 1:ch // 2, :]
        b_last = b[ch - 1:ch, :]
        qh = q * jnp.exp(b - mid)
        kh = kk * jnp.exp(mid - b)
        qhb = qh.astype(bf16)
        khb = kh.astype(bf16)
        qtb = (qh * jnp.exp(mid)).astype(bf16)
        kdb = (kh * jnp.exp(b_last - mid)).astype(bf16)
        e_last = jnp.exp(b_last)
        vb = qfig_scr[pl.ds(r0, ch), 2 * hw:3 * hw].astype(bf16)
        heads = [slice(h * HG_DK, (h + 1) * HG_DK) for h in range(HG_HEADS)]
        scs = [jnp.where(causal, _dot_nt(qhb[:, hs], khb[:, hs]), 0.0).astype(bf16) for hs in heads]
        outs = []
        for h, hs in enumerate(heads):
            st = st_scr[h]
            o = _dot(scs[h], vb[:, hs]) + _dot_nt(qtb[:, hs], st.astype(bf16))
            st_scr[h] = st * e_last[:, hs] + _dot_tn(vb[:, hs], kdb[:, hs])
            outs.append(_rms(o))
        g_raw = qfig_scr[pl.ds(r0, ch), 3 * hw:4 * hw]
        on = jnp.concatenate(outs, axis=1) * hgn * (g_raw * _sigmoid(g_raw))
        yhg_scr[pl.ds(r0, ch), :] = on.astype(bf16)
        return carry

    lax.fori_loop(0, tile // ch, chunk_body, 0)

    gate_hg = _sigmoid(_dot(hb_scr[...], wg_ref[:, dm:]) + bg_ref[:, dm:])
    merged = m_scr[...] + gate_hg * _dot(yhg_scr[...], wbh_ref[...])
    mo = _dot(merged.astype(bf16), wout_ref[...])
    y_ref[0] = x_ref[0] + _rms(mo) * gpost_ref[...]

    @pl.when(t_idx == n_t - 1)
    def _():
        s5r_ref[0] = cr_scr[...]
        s5i_ref[0] = ci_scr[...]
        for h in range(HG_HEADS):
            hgs_ref[0, h] = st_scr[h].T


def _const_spec(shape):
    nd = len(shape)
    return pl.BlockSpec(shape, lambda *_: (0,) * nd, pipeline_mode=pl.Buffered(1))


def _mixer_call(x, w):
    bsz, seq, dm = x.shape
    tile = min(PROMPT_TILE, seq)
    assert seq % tile == 0 and tile % HG_CHUNK == 0
    rows = tile // S5_CHUNK
    assert rows & (rows - 1) == 0 and w["apr"].shape[0] >= int(math.log2(rows))
    s5w = w["wu"].shape[1]
    n_cols = s5w // LANES
    n_pairs = w["w1"].shape[0]
    sdim = w["apr"].shape[1]
    consts = [w[k] for k in ("gpre", "wu", "bu", "w1", "w2", "apr", "api", "d", "wglu", "bglu",
                             "whq", "bhq", "lbrow", "hgnrow", "wg", "bg", "wbs", "wbh", "wout", "gpost")]
    in_specs = [pl.BlockSpec((1, tile, dm), lambda b, t: (b, t, 0))]
    in_specs += [_const_spec(c.shape) for c in consts]
    out_shape = (jax.ShapeDtypeStruct((bsz, seq, dm), f32),
                 jax.ShapeDtypeStruct((bsz, 1, sdim), f32),
                 jax.ShapeDtypeStruct((bsz, 1, sdim), f32),
                 jax.ShapeDtypeStruct((bsz, HG_HEADS, HG_DK, HG_DK), f32))
    out_specs = (pl.BlockSpec((1, tile, dm), lambda b, t: (b, t, 0)),
                 pl.BlockSpec((1, 1, sdim), lambda b, t: (b, 0, 0)),
                 pl.BlockSpec((1, 1, sdim), lambda b, t: (b, 0, 0)),
                 pl.BlockSpec((1, HG_HEADS, HG_DK, HG_DK), lambda b, t: (b, 0, 0, 0)))
    scratch = [
        pltpu.VMEM((tile, dm), bf16),
        pltpu.VMEM((n_cols, tile, LANES), f32),
        pltpu.VMEM((n_cols, tile, LANES), f32),
        pltpu.VMEM((n_pairs, rows, 2 * LANES), bf16),
        pltpu.VMEM((n_pairs, rows, 2 * LANES), f32),
        pltpu.VMEM((rows, sdim), f32),
        pltpu.VMEM((rows, sdim), f32),
        pltpu.VMEM((1, sdim), f32),
        pltpu.VMEM((1, sdim), f32),
        pltpu.VMEM((HG_HEADS, HG_DK, HG_DK), f32),
        pltpu.VMEM((tile, HG_HEADS * 4 * HG_DK), f32),
        pltpu.VMEM((tile, HG_HEADS * HG_DK), bf16),
        pltpu.VMEM((tile, dm), f32),
    ]
    return pl.pallas_call(
        _mixer_kernel,
        grid=(bsz, seq // tile),
        in_specs=in_specs,
        out_specs=out_specs,
        out_shape=out_shape,
        scratch_shapes=scratch,
        compiler_params=pltpu.CompilerParams(
            dimension_semantics=("arbitrary", "arbitrary"), vmem_limit_bytes=VMEM_LIMIT),
        name="prompt_mixer",
    )(x, *consts)


def _mlp_body(x, gpre_ref, wup_ref, wdn_ref, gpost_ref):
    h2 = (_rms(x) * gpre_ref[...]).astype(bf16)
    dff = wup_ref.shape[1]
    acc = None
    for c in range(dff // MLP_FCHUNK):
        a = _dot(h2, wup_ref[:, c * MLP_FCHUNK:(c + 1) * MLP_FCHUNK])
        a = jnp.maximum(a, 0.0)
        part = _dot((a * a).astype(bf16), wdn_ref[c * MLP_FCHUNK:(c + 1) * MLP_FCHUNK, :])
        acc = part if acc is None else acc + part
    return x + _rms(acc) * gpost_ref[...]


def _mlp_kernel(x_ref, gpre_ref, wup_ref, wdn_ref, gpost_ref, y_ref):
    y_ref[...] = _mlp_body(x_ref[...], gpre_ref, wup_ref, wdn_ref, gpost_ref)


def _mlp_call(x2d, w):
    n, dm = x2d.shape
    tile = min(MLP_TILE, n)
    assert n % tile == 0
    consts = [w[k] for k in ("g2pre", "wup", "wdn", "g2post")]
    return pl.pallas_call(
        _mlp_kernel,
        grid=(n // tile,),
        in_specs=[pl.BlockSpec((tile, dm), lambda i: (i, 0))] + [_const_spec(c.shape) for c in consts],
        out_specs=pl.BlockSpec((tile, dm), lambda i: (i, 0)),
        out_shape=jax.ShapeDtypeStruct((n, dm), f32),
        compiler_params=pltpu.CompilerParams(
            dimension_semantics=("arbitrary",), vmem_limit_bytes=VMEM_LIMIT),
        name="mlp",
    )(x2d, *consts)


def _sample_pre_kernel(x_ref, h0r_ref, h0i_ref, gpre_ref, wu_ref, bu_ref, bblk_ref, cblk_ref,
                       abr_ref, abi_ref, d_ref, wglu_ref, bglu_ref, whq_ref, bhq_ref, lb_ref, hgn_ref,
                       wg_ref, bg_ref, wbs_ref,
                       m5_ref, ghg_ref, gn_ref, qt_ref, kd_ref, v_ref, oin_ref, e3_ref,
                       s5r_ref, s5i_ref):
    n_seq = h0r_ref.shape[0]
    steps = x_ref.shape[0] // n_seq
    dm = x_ref.shape[1]
    sdim = h0r_ref.shape[1]
    hw = HG_HEADS * HG_DK
    x = x_ref[...]
    hb = (_rms(x) * gpre_ref[...]).astype(bf16)

    u = _dot(hb, wu_ref[...]) + bu_ref[...]
    bu = _dot(u.astype(bf16), bblk_ref[...])
    hr = h0r_ref[...]
    hi = h0i_ref[...]
    a_r = abr_ref[...]
    a_i = abi_ref[...]
    ys = []
    for t in range(steps):
        sl = slice(t * n_seq, (t + 1) * n_seq)
        hr, hi = (a_r * hr - a_i * hi + bu[sl, :sdim], a_r * hi + a_i * hr + bu[sl, sdim:])
        ys.append(_dot(jnp.concatenate([hr, hi], axis=1).astype(bf16), cblk_ref[...]))
    s5r_ref[...] = hr
    s5i_ref[...] = hi
    y = jnp.concatenate(ys, axis=0)
    y = _gelu_tanh(y + d_ref[...] * u)
    z = _dot(y.astype(bf16), wglu_ref[...]) + bglu_ref[...]
    ys5 = (y * _sigmoid(z)).astype(bf16)
    gates = _dot(hb, wg_ref[...]) + bg_ref[...]
    m5_ref[...] = _sigmoid(gates[:, :dm]) * _dot(ys5, wbs_ref[...])
    ghg_ref[...] = _sigmoid(gates[:, dm:])

    pq = _dot(hb, whq_ref[...]) + bhq_ref[...]
    q_raw = pq[:, :hw]
    f_raw = pq[:, hw:2 * hw]
    v = pq[:, 2 * hw:3 * hw]
    g_raw = pq[:, 3 * hw:]
    q = q_raw * _sigmoid(q_raw)
    lb = lb_ref[...]
    f = lb + (1.0 - lb) * _sigmoid(f_raw)
    lf = jnp.log(f)
    kk = 1.0 - f
    gn_ref[...] = g_raw * _sigmoid(g_raw) * hgn_ref[...]
    v_ref[...] = v
    bs = []
    acc = None
    for t in range(steps):
        sl = slice(t * n_seq, (t + 1) * n_seq)
        acc = lf[sl] if acc is None else acc + lf[sl]
        bs.append(acc)
    b_last = bs[-1]
    e3_ref[...] = jnp.exp(b_last)
    for t in range(steps):
        sl = slice(t * n_seq, (t + 1) * n_seq)
        qt_ref[sl, :] = q[sl] * jnp.exp(bs[t])
        kd_ref[sl, :] = kk[sl] * jnp.exp(b_last - bs[t])
        o_t = None
        for s in range(t + 1):
            sls = slice(s * n_seq, (s + 1) * n_seq)
            prod = q[sl] * kk[sls] * jnp.exp(bs[t] - bs[s])
            parts = []
            for h in range(HG_HEADS):
                hs = slice(h * HG_DK, (h + 1) * HG_DK)
                wgt = jnp.sum(prod[:, hs], axis=1, keepdims=True)
                parts.append(wgt * v[sls, hs])
            term = jnp.concatenate(parts, axis=1)
            o_t = term if o_t is None else o_t + term
        oin_ref[sl, :] = o_t


def _sample_pre_call(x_tm, h0r, h0i, w):
    rows, dm = x_tm.shape
    n_seq, sdim = h0r.shape
    hw = HG_HEADS * HG_DK
    ins = [x_tm, h0r, h0i] + [w[k] for k in ("gpre", "wu", "bu", "bblk", "cblk", "abr", "abi", "d",
                                              "wglu", "bglu", "whq", "bhq", "lbrow", "hgnrow",
                                              "wg", "bg", "wbs")]
    out_shape = (jax.ShapeDtypeStruct((rows, dm), f32),
                 jax.ShapeDtypeStruct((rows, dm), f32),
                 jax.ShapeDtypeStruct((rows, hw), f32),
                 jax.ShapeDtypeStruct((rows, hw), f32),
                 jax.ShapeDtypeStruct((rows, hw), f32),
                 jax.ShapeDtypeStruct((rows, hw), f32),
                 jax.ShapeDtypeStruct((rows, hw), f32),
                 jax.ShapeDtypeStruct((n_seq, hw), f32),
                 jax.ShapeDtypeStruct((n_seq, sdim), f32),
                 jax.ShapeDtypeStruct((n_seq, sdim), f32))
    return pl.pallas_call(
        _sample_pre_kernel,
        out_shape=out_shape,
        compiler_params=pltpu.CompilerParams(vmem_limit_bytes=VMEM_LIMIT),
        name="sample_pre",
    )(*ins)


def _sample_state_kernel(s0_ref, qt_ref, v_ref, cols_ref, snew_ref, oint_ref):
    nb = s0_ref.shape[0]
    tpad = qt_ref.shape[1]
    ncol = cols_ref.shape[1]
    for i in range(nb):
        xt = jnp.concatenate([cols_ref[i], jnp.zeros((LANES - ncol, HG_DK), f32)], axis=0).T
        qt = qt_ref[i]
        vv = v_ref[i]
        outs = []
        for h in range(HG_HEADS):
            hs = slice(h * HG_DK, (h + 1) * HG_DK)
            s0 = s0_ref[i, h]
            outs.append(_dot(qt[:, hs].astype(bf16), s0.astype(bf16)))
            base = h * 8
            sn = xt[:, base + 4:base + 5] * s0
            for t in range(4):
                sn = sn + xt[:, base + t:base + t + 1] * vv[t:t + 1, hs]
            snew_ref[i, h] = sn
        oint_ref[i] = jnp.concatenate(outs, axis=1)


def _sample_state_call(s0, qt_bt, v_bt, cols):
    n_seq = s0.shape[0]
    nb = SAMPLE_SEQ_BLOCK
    tpad = qt_bt.shape[1]
    hw = HG_HEADS * HG_DK
    return pl.pallas_call(
        _sample_state_kernel,
        grid=(n_seq // nb,),
        in_specs=[pl.BlockSpec((nb, HG_HEADS, HG_DK, HG_DK), lambda i: (i, 0, 0, 0)),
                  pl.BlockSpec((nb, tpad, hw), lambda i: (i, 0, 0)),
                  pl.BlockSpec((nb, tpad, hw), lambda i: (i, 0, 0)),
                  pl.BlockSpec((nb,) + cols.shape[1:], lambda i: (i, 0, 0))],
        out_specs=(pl.BlockSpec((nb, HG_HEADS, HG_DK, HG_DK), lambda i: (i, 0, 0, 0)),
                   pl.BlockSpec((nb, tpad, hw), lambda i: (i, 0, 0))),
        out_shape=(jax.ShapeDtypeStruct(s0.shape, f32),
                   jax.ShapeDtypeStruct((n_seq, tpad, hw), f32)),
        compiler_params=pltpu.CompilerParams(
            dimension_semantics=("arbitrary",), vmem_limit_bytes=VMEM_LIMIT),
        name="sample_state",
    )(s0, qt_bt, v_bt, cols)


def _sample_post_kernel(x_ref, oint_ref, oin_ref, gn_ref, m5_ref, ghg_ref, wbh_ref, wout_ref, gpost_ref,
                        g2pre_ref, wup_ref, wdn_ref, g2post_ref, y_ref):
    o = oint_ref[...] + oin_ref[...]
    parts = []
    for h in range(HG_HEADS):
        hs = slice(h * HG_DK, (h + 1) * HG_DK)
        parts.append(_rms(o[:, hs]))
    yhg = (jnp.concatenate(parts, axis=1) * gn_ref[...]).astype(bf16)
    merged = m5_ref[...] + ghg_ref[...] * _dot(yhg, wbh_ref[...])
    mo = _dot(merged.astype(bf16), wout_ref[...])
    x1 = x_ref[...] + _rms(mo) * gpost_ref[...]
    y_ref[...] = _mlp_body(x1, g2pre_ref, wup_ref, wdn_ref, g2post_ref)


def _sample_post_call(x_tm, oint_tm, oin, gn, m5, ghg, w):
    ins = [x_tm, oint_tm, oin, gn, m5, ghg] + [w[k] for k in ("wbh", "wout", "gpost", "g2pre", "wup",
                                                              "wdn", "g2post")]
    return pl.pallas_call(
        _sample_post_kernel,
        out_shape=jax.ShapeDtypeStruct(x_tm.shape, f32),
        compiler_params=pltpu.CompilerParams(vmem_limit_bytes=VMEM_LIMIT),
        name="sample_post",
    )(*ins)


def _prep_weights(l, norm_mix_pre, norm_mix_post, norm_mlp_pre, norm_mlp_post, w_in, b_in,
                  s5_a_re, s5_a_im, s5_log_dt, s5_b_re, s5_b_im, s5_c_re, s5_c_im, s5_d, s5_w_glu,
                  s5_b_glu, hg_lb_logits, hg_norm, w_br_s5, w_br_hg, w_out, w_up, w_down, n_pow):
    hp = lax.Precision.HIGHEST
    dm = w_in.shape[1]
    s5w = s5_d.shape[1]
    kw = HG_HEADS * HG_DK
    n_groups = s5w // S5_GROUP
    n_pairs = n_groups // 2
    w = {}
    row = lambda a: a.astype(f32).reshape(1, -1)
    w["gpre"], w["gpost"] = row(norm_mix_pre[l]), row(norm_mix_post[l])
    w["g2pre"], w["g2post"] = row(norm_mlp_pre[l]), row(norm_mlp_post[l])
    wi, bi = w_in[l], b_in[l].astype(f32)
    w["wu"], w["bu"] = wi[:, :s5w].astype(bf16), row(bi[:s5w])
    w["whq"] = wi[:, s5w:s5w + 4 * kw].astype(bf16)
    w["bhq"] = row(bi[s5w:s5w + 4 * kw])
    w["wg"] = wi[:, s5w + 4 * kw:].astype(bf16)
    w["bg"] = row(bi[s5w + 4 * kw:])
    lb_all = jnp.cumsum(jax.nn.softmax(hg_lb_logits.astype(f32), axis=0), axis=0)
    w["lbrow"] = row(lb_all[l])
    w["hgnrow"] = row(jnp.tile(hg_norm[l].astype(f32), HG_HEADS))
    w["d"] = row(s5_d[l])
    w["wglu"], w["bglu"] = s5_w_glu[l].astype(bf16), row(s5_b_glu[l])
    w["wbs"], w["wbh"], w["wout"] = w_br_s5[l].astype(bf16), w_br_hg[l].astype(bf16), w_out[l].astype(bf16)
    w["wup"], w["wdn"] = w_up[l].astype(bf16), w_down[l].astype(bf16)

    a_re, a_im = s5_a_re[l].astype(f32), s5_a_im[l].astype(f32)
    dt = jnp.exp(s5_log_dt[l].astype(f32))[:, None]
    mag = jnp.exp(dt * a_re)
    abr, abi = mag * jnp.cos(dt * a_im), mag * jnp.sin(dt * a_im)
    den = a_re * a_re + a_im * a_im
    nr, ni = abr - 1.0, abi
    cfr, cfi = (nr * a_re + ni * a_im) / den, (ni * a_re - nr * a_im) / den
    b_re, b_im = s5_b_re[l].astype(f32), s5_b_im[l].astype(f32)
    bbr = cfr[..., None] * b_re - cfi[..., None] * b_im
    bbi = cfr[..., None] * b_im + cfi[..., None] * b_re
    c_re, c_im = s5_c_re[l].astype(f32), s5_c_im[l].astype(f32)
    w["abr"], w["abi"] = abr.reshape(1, -1), abi.reshape(1, -1)

    pr, pi = [jnp.ones_like(abr)], [jnp.zeros_like(abr)]
    for _ in range(S5_CHUNK):
        pr, pi = pr + [pr[-1] * abr - pi[-1] * abi], pi + [pr[-1] * abi + pi[-1] * abr]
    pr, pi = jnp.stack(pr), jnp.stack(pi)
    cpr = c_re[None] * pr[:, :, None, :] - c_im[None] * pi[:, :, None, :]
    cpi = c_re[None] * pi[:, :, None, :] + c_im[None] * pr[:, :, None, :]
    taps = (jnp.einsum("jgon,gni->jgoi", cpr[:S5_CHUNK], bbr, precision=hp)
            - jnp.einsum("jgon,gni->jgoi", cpi[:S5_CHUNK], bbi, precision=hp))
    s_idx = jnp.arange(S5_CHUNK)[:, None]
    t_idx = jnp.arange(S5_CHUNK)[None, :]
    lag = t_idx - s_idx
    toe = jnp.where((lag >= 0)[:, :, None, None, None],
                    taps[jnp.clip(lag, 0, S5_CHUNK - 1)], 0.0)
    toe = jnp.transpose(toe, (2, 0, 4, 1, 3)).reshape(n_groups, LANES, LANES)
    prs, pis = pr[S5_CHUNK - 1::-1][:S5_CHUNK], pi[S5_CHUNK - 1::-1][:S5_CHUNK]
    ber = prs[..., None] * bbr[None] - pis[..., None] * bbi[None]
    bei = prs[..., None] * bbi[None] + pis[..., None] * bbr[None]
    ber = jnp.transpose(ber, (1, 0, 3, 2)).reshape(n_groups, LANES, S5_STATE)
    bei = jnp.transpose(bei, (1, 0, 3, 2)).reshape(n_groups, LANES, S5_STATE)
    car = jnp.transpose(cpr[1:], (1, 3, 0, 2)).reshape(n_groups, S5_STATE, LANES)
    cai = jnp.transpose(cpi[1:], (1, 3, 0, 2)).reshape(n_groups, S5_STATE, LANES)

    def pair_diag(m):
        g, r, c = m.shape
        m = m.reshape(g // 2, 2, r, c)
        z = jnp.zeros_like(m[:, 0])
        return jnp.concatenate([jnp.concatenate([m[:, 0], z], axis=2),
                                jnp.concatenate([z, m[:, 1]], axis=2)], axis=1)

    w["w1"] = jnp.concatenate([pair_diag(toe), pair_diag(ber), pair_diag(bei)], axis=2).astype(bf16)
    w["w2"] = jnp.concatenate([pair_diag(car), pair_diag(-cai)], axis=1).astype(bf16)
    qr, qi = pr[S5_CHUNK], pi[S5_CHUNK]
    aps_r, aps_i = [], []
    for _ in range(n_pow):
        aps_r.append(qr.reshape(-1))
        aps_i.append(qi.reshape(-1))
        qr, qi = qr * qr - qi * qi, 2.0 * qr * qi
    w["apr"], w["api"] = jnp.stack(aps_r), jnp.stack(aps_i)

    eye = jnp.eye(n_groups, dtype=f32)
    def bdiag(m):
        g, r, c = m.shape
        return (eye[:, None, :, None] * m[:, :, None, :]).reshape(g * r, g * c)
    bt_r, bt_i = jnp.transpose(bbr, (0, 2, 1)), jnp.transpose(bbi, (0, 2, 1))
    w["bblk"] = jnp.concatenate([bdiag(bt_r), bdiag(bt_i)], axis=1).astype(bf16)
    ct_r, ct_i = jnp.transpose(c_re, (0, 2, 1)), jnp.transpose(c_im, (0, 2, 1))
    w["cblk"] = jnp.concatenate([bdiag(ct_r), bdiag(-ct_i)], axis=0).astype(bf16)
    return w


def kernel(x_prompt, x_sample, state_s5_re, state_s5_im, state_hg, norm_mix_pre, norm_mix_post,
           norm_mlp_pre, norm_mlp_post, w_in, b_in, s5_a_re, s5_a_im, s5_log_dt, s5_b_re, s5_b_im,
           s5_c_re, s5_c_im, s5_d, s5_w_glu, s5_b_glu, hg_lb_logits, hg_norm, w_br_s5, w_br_hg,
           w_out, w_up, w_down):
    depth = w_in.shape[0]
    bsz, seq, dm = x_prompt.shape
    n_seq, steps, _ = x_sample.shape
    n_groups, n_state = s5_a_re.shape[1], s5_a_re.shape[2]
    hw = HG_HEADS * HG_DK
    assert depth == 1 and steps == 4 and n_state == S5_STATE
    n_pow = max(1, int(math.log2(min(PROMPT_TILE, seq) // S5_CHUNK)))
    xp, xs = x_prompt, x_sample
    outs = [[] for _ in range(6)]
    for l in range(depth):
        w = _prep_weights(l, norm_mix_pre, norm_mix_post, norm_mlp_pre, norm_mlp_post, w_in, b_in,
                          s5_a_re, s5_a_im, s5_log_dt, s5_b_re, s5_b_im, s5_c_re, s5_c_im, s5_d,
                          s5_w_glu, s5_b_glu, hg_lb_logits, hg_norm, w_br_s5, w_br_hg, w_out, w_up,
                          w_down, n_pow)
        x1, p_re, p_im, p_hg = _mixer_call(xp, w)
        xp = _mlp_call(x1.reshape(bsz * seq, dm), w).reshape(bsz, seq, dm)
        outs[0].append(p_re.reshape(bsz, n_groups, n_state))
        outs[1].append(p_im.reshape(bsz, n_groups, n_state))
        outs[2].append(p_hg)
        x_tm = jnp.transpose(xs, (1, 0, 2)).reshape(steps * n_seq, dm)
        h0r = state_s5_re[l].reshape(n_seq, n_groups * n_state)
        h0i = state_s5_im[l].reshape(n_seq, n_groups * n_state)
        m5, ghg, gn, qt, kd, v, oin, e3, s_re, s_im = _sample_pre_call(x_tm, h0r, h0i, w)
        tpad = 8
        def to_bt(a):
            a = jnp.transpose(a.reshape(steps, n_seq, -1), (1, 0, 2))
            return jnp.pad(a, ((0, 0), (0, tpad - steps), (0, 0)))
        kd4 = jnp.transpose(kd.reshape(steps, n_seq, HG_HEADS, HG_DK), (1, 2, 0, 3))
        cols = jnp.concatenate([kd4, e3.reshape(n_seq, HG_HEADS, 1, HG_DK),
                                jnp.zeros((n_seq, HG_HEADS, 3, HG_DK), f32)], axis=2)
        cols = cols.reshape(n_seq, HG_HEADS * 8, HG_DK)
        s_hg, oint_bt = _sample_state_call(state_hg[l], to_bt(qt), to_bt(v), cols)
        oint_tm = jnp.transpose(oint_bt[:, :steps], (1, 0, 2)).reshape(steps * n_seq, hw)
        y_tm = _sample_post_call(x_tm, oint_tm, oin, gn, m5, ghg, w)
        xs = jnp.transpose(y_tm.reshape(steps, n_seq, dm), (1, 0, 2))
        outs[3].append(s_re.reshape(n_seq, n_groups, n_state))
        outs[4].append(s_im.reshape(n_seq, n_groups, n_state))
        outs[5].append(s_hg)
    return (xp, xs) + tuple(jnp.stack(o) for o in outs)
```

```python
import functools
import math

import jax
import jax.numpy as jnp
from jax import lax
from jax.experimental import pallas as pl
from jax.experimental.pallas import tpu as pltpu

f32 = jnp.float32
bf16 = jnp.bfloat16

NORM_EPS = 1e-6
S5_GROUP = 16
S5_STATE = 64
S5_CHUNK = 8
HG_HEADS = 8
HG_DK = 128
HG_CHUNK = 64
LANES = 128
VMEM_LIMIT = 56 * 1024 * 1024
PROMPT_TILE = 512
MLP_TILE = 512
MLP_FCHUNK = 1024
SAMPLE_SEQ_BLOCK = 8
SAMPLE_PRE_BLOCK = 32


def _sigmoid(x):
    return 0.5 * (jnp.tanh(0.5 * x) + 1.0)


def _gelu_tanh(x):
    c = math.sqrt(2.0 / math.pi)
    return 0.5 * x * (1.0 + jnp.tanh(c * (x + 0.044715 * (x * x * x))))


def _rms(x):
    return x * lax.rsqrt(jnp.mean(x * x, axis=-1, keepdims=True) + NORM_EPS)


def _dot(a, b):
    return jnp.dot(a, b, preferred_element_type=f32)


def _dot_nt(a, b):
    return lax.dot_general(a, b, (((1,), (1,)), ((), ())), preferred_element_type=f32)


def _dot_tn(a, b):
    return lax.dot_general(a, b, (((0,), (0,)), ((), ())), preferred_element_type=f32)


def _block_transpose8(xs, blk):
    xs = list(xs)
    for d in (4, 2, 1):
        keep = (blk & d) == 0
        for i in range(8):
            if i & d == 0:
                a, b = xs[i], xs[i + d]
                xs[i] = jnp.where(keep, a, pltpu.roll(b, d * S5_GROUP, 1))
                xs[i + d] = jnp.where(keep, pltpu.roll(a, LANES - d * S5_GROUP, 1), b)
    return xs


def _mixer_kernel(x_ref, gpre_ref, win_ref, bin_ref, w1_ref, w2_ref, apr_ref, api_ref, d_ref,
                  wglu_ref, bglu_ref, lb_ref, hgn_ref, wbs_ref, wbh_ref, wout_ref, gpost_ref,
                  y_ref, s5r_ref, s5i_ref, hgs_ref,
                  hb_scr, u_scr, ys_scr, o_scr, yloc_scr, er_scr, ei_scr, cr_scr, ci_scr,
                  st_scr, qfig_scr, yhg_scr, m_scr):
    t_idx = pl.program_id(1)
    n_t = pl.num_programs(1)
    tile = x_ref.shape[1]
    rows = tile // S5_CHUNK
    n_pairs = w1_ref.shape[0]
    n_cols = u_scr.shape[0]
    hw = HG_HEADS * HG_DK
    s5w = n_cols * LANES
    g0 = s5w + 4 * hw

    @pl.when(t_idx == 0)
    def _():
        cr_scr[...] = jnp.zeros_like(cr_scr)
        ci_scr[...] = jnp.zeros_like(ci_scr)
        st_scr[...] = jnp.zeros_like(st_scr)

    x = x_ref[0]
    hb_scr[...] = (_rms(x) * gpre_ref[...]).astype(bf16)

    u = _dot(hb_scr[...], win_ref[:, :s5w]) + bin_ref[:, :s5w]
    for j in range(n_cols):
        u_scr[j] = u[:, j * LANES:(j + 1) * LANES]

    blk = lax.broadcasted_iota(jnp.int32, (rows, LANES), 1) // S5_GROUP
    for j in range(n_cols):
        xs = [u_scr[j, pl.ds(s, rows, stride=S5_CHUNK), :] for s in range(S5_CHUNK)]
        ys = _block_transpose8(xs, blk)
        for q in range(4):
            o_scr[j * 4 + q] = jnp.concatenate([ys[2 * q], ys[2 * q + 1]], axis=1).astype(bf16)

    for p in range(n_pairs):
        r1 = _dot(o_scr[p], w1_ref[p])
        yloc_scr[p] = r1[:, :2 * LANES]
        er_scr[:, p * LANES:(p + 1) * LANES] = r1[:, 2 * LANES:3 * LANES]
        ei_scr[:, p * LANES:(p + 1) * LANES] = r1[:, 3 * LANES:]

    er = er_scr[...]
    ei = ei_scr[...]
    row = lax.broadcasted_iota(jnp.int32, (rows, 1), 0)
    first = row == 0
    cr = cr_scr[...]
    ci = ci_scr[...]
    a_r = apr_ref[0:1, :]
    a_i = api_ref[0:1, :]
    er = er + jnp.where(first, a_r * cr - a_i * ci, 0.0)
    ei = ei + jnp.where(first, a_r * ci + a_i * cr, 0.0)
    for k in range(int(math.log2(rows))):
        d = 1 << k
        p_r = apr_ref[k:k + 1, :]
        p_i = api_ref[k:k + 1, :]
        s_r = pltpu.roll(er, d, 0)
        s_i = pltpu.roll(ei, d, 0)
        valid = row >= d
        er, ei = (er + jnp.where(valid, p_r * s_r - p_i * s_i, 0.0),
                  ei + jnp.where(valid, p_r * s_i + p_i * s_r, 0.0))
    hs_r = jnp.where(first, cr, pltpu.roll(er, 1, 0))
    hs_i = jnp.where(first, ci, pltpu.roll(ei, 1, 0))
    cr_scr[...] = er[rows - 1:rows, :]
    ci_scr[...] = ei[rows - 1:rows, :]

    for j in range(n_cols):
        halves = []
        for q in range(4):
            p = j * 4 + q
            hp = jnp.concatenate([hs_r[:, p * LANES:(p + 1) * LANES],
                                  hs_i[:, p * LANES:(p + 1) * LANES]], axis=1).astype(bf16)
            yp = yloc_scr[p] + _dot(hp, w2_ref[p])
            halves += [yp[:, :LANES], yp[:, LANES:]]
        zs = _block_transpose8(halves, blk)
        for s in range(S5_CHUNK):
            ys_scr[j, pl.ds(s, rows, stride=S5_CHUNK), :] = zs[s]

    y = jnp.concatenate([ys_scr[j] for j in range(n_cols)], axis=1)
    u = jnp.concatenate([u_scr[j] for j in range(n_cols)], axis=1)
    y = _gelu_tanh(y + d_ref[...] * u)
    z = _dot(y.astype(bf16), wglu_ref[...]) + bglu_ref[...]
    ys5 = (y * _sigmoid(z)).astype(bf16)
    dm = wbs_ref.shape[1]
    gate_s5 = _sigmoid(_dot(hb_scr[...], win_ref[:, g0:g0 + dm]) + bin_ref[:, g0:g0 + dm])
    m_scr[...] = gate_s5 * _dot(ys5, wbs_ref[...])

    ch = HG_CHUNK
    ri = lax.broadcasted_iota(jnp.int32, (ch, ch), 0)
    ci_ = lax.broadcasted_iota(jnp.int32, (ch, ch), 1)
    causal = ci_ <= ri
    tril = causal.astype(bf16)
    qfig_scr[...] = _dot(hb_scr[...], win_ref[:, s5w:g0]) + bin_ref[:, s5w:g0]
    lb = lb_ref[...]
    hgn = hgn_ref[...]

    def chunk_body(c, carry):
        r0 = pl.multiple_of(c * ch, ch)
        q_raw = qfig_scr[pl.ds(r0, ch), 0:hw]
        f_raw = qfig_scr[pl.ds(r0, ch), hw:2 * hw]
        q = q_raw * _sigmoid(q_raw)
        f = lb + (1.0 - lb) * _sigmoid(f_raw)
        lf = jnp.log(f)
        kk = 1.0 - f
        hi = lf.astype(bf16)
        lo = (lf - hi.astype(f32)).astype(bf16)
        cs = _dot(tril, jnp.concatenate([hi, lo], axis=1))
        b = cs[:, :hw] + cs[:, hw:]
        mid = b[ch // 2 - 1:ch // 2, :]
        b_last = b[ch - 1:ch, :]
        qh = q * jnp.exp(b - mid)
        kh = kk * jnp.exp(mid - b)
        qhb = qh.astype(bf16)
        khb = kh.astype(bf16)
        qtb = (qh * jnp.exp(mid)).astype(bf16)
        kdb = (kh * jnp.exp(b_last - mid)).astype(bf16)
        e_last = jnp.exp(b_last)
        vb = qfig_scr[pl.ds(r0, ch), 2 * hw:3 * hw].astype(bf16)
        heads = [slice(h * HG_DK, (h + 1) * HG_DK) for h in range(HG_HEADS)]
        scs = [jnp.where(causal, _dot_nt(qhb[:, hs], khb[:, hs]), 0.0).astype(bf16) for hs in heads]
        outs = []
        for h, hs in enumerate(heads):
            st = st_scr[h]
            o = _dot(scs[h], vb[:, hs]) + _dot_nt(qtb[:, hs], st.astype(bf16))
            st_scr[h] = st * e_last[:, hs] + _dot_tn(vb[:, hs], kdb[:, hs])
            outs.append(_rms(o))
        g_raw = qfig_scr[pl.ds(r0, ch), 3 * hw:4 * hw]
        on = jnp.concatenate(outs, axis=1) * hgn * (g_raw * _sigmoid(g_raw))
        yhg_scr[pl.ds(r0, ch), :] = on.astype(bf16)
        return carry

    lax.fori_loop(0, tile // ch, chunk_body, 0)

    gate_hg = _sigmoid(_dot(hb_scr[...], win_ref[:, g0 + dm:]) + bin_ref[:, g0 + dm:])
    merged = m_scr[...] + gate_hg * _dot(yhg_scr[...], wbh_ref[...])
    mo = _dot(merged.astype(bf16), wout_ref[...])
    y_ref[0] = x_ref[0] + _rms(mo) * gpost_ref[...]

    @pl.when(t_idx == n_t - 1)
    def _():
        s5r_ref[0] = cr_scr[...]
        s5i_ref[0] = ci_scr[...]
        for h in range(HG_HEADS):
            hgs_ref[0, h] = st_scr[h].T


def _const_spec(shape):
    nd = len(shape)
    return pl.BlockSpec(shape, lambda *_: (0,) * nd, pipeline_mode=pl.Buffered(1))


def _mixer_call(x, w):
    bsz, seq, dm = x.shape
    tile = min(PROMPT_TILE, seq)
    assert seq % tile == 0 and tile % HG_CHUNK == 0
    rows = tile // S5_CHUNK
    assert rows & (rows - 1) == 0 and w["apr"].shape[0] >= int(math.log2(rows))
    s5w = w["d"].shape[1]
    n_cols = s5w // LANES
    n_pairs = w["w1"].shape[0]
    sdim = w["apr"].shape[1]
    consts = [w[k] for k in ("gpre", "win", "bin", "w1", "w2", "apr", "api", "d", "wglu", "bglu",
                             "lbrow", "hgnrow", "wbs", "wbh", "wout", "gpost")]
    in_specs = [pl.BlockSpec((1, tile, dm), lambda b, t: (b, t, 0))]
    in_specs += [_const_spec(c.shape) for c in consts]
    out_shape = (jax.ShapeDtypeStruct((bsz, seq, dm), f32),
                 jax.ShapeDtypeStruct((bsz, 1, sdim), f32),
                 jax.ShapeDtypeStruct((bsz, 1, sdim), f32),
                 jax.ShapeDtypeStruct((bsz, HG_HEADS, HG_DK, HG_DK), f32))
    out_specs = (pl.BlockSpec((1, tile, dm), lambda b, t: (b, t, 0)),
                 pl.BlockSpec((1, 1, sdim), lambda b, t: (b, 0, 0)),
                 pl.BlockSpec((1, 1, sdim), lambda b, t: (b, 0, 0)),
                 pl.BlockSpec((1, HG_HEADS, HG_DK, HG_DK), lambda b, t: (b, 0, 0, 0)))
    scratch = [
        pltpu.VMEM((tile, dm), bf16),
        pltpu.VMEM((n_cols, tile, LANES), f32),
        pltpu.VMEM((n_cols, tile, LANES), f32),
        pltpu.VMEM((n_pairs, rows, 2 * LANES), bf16),
        pltpu.VMEM((n_pairs, rows, 2 * LANES), f32),
        pltpu.VMEM((rows, sdim), f32),
        pltpu.VMEM((rows, sdim), f32),
        pltpu.VMEM((1, sdim), f32),
        pltpu.VMEM((1, sdim), f32),
        pltpu.VMEM((HG_HEADS, HG_DK, HG_DK), f32),
        pltpu.VMEM((tile, HG_HEADS * 4 * HG_DK), f32),
        pltpu.VMEM((tile, HG_HEADS * HG_DK), bf16),
        pltpu.VMEM((tile, dm), f32),
    ]
    return pl.pallas_call(
        _mixer_kernel,
        grid=(bsz, seq // tile),
        in_specs=in_specs,
        out_specs=out_specs,
        out_shape=out_shape,
        scratch_shapes=scratch,
        compiler_params=pltpu.CompilerParams(
            dimension_semantics=("arbitrary", "arbitrary"), vmem_limit_bytes=VMEM_LIMIT),
        name="prompt_mixer",
    )(x, *consts)


def _mlp_body(x, gpre_ref, wup_ref, wdn_ref, gpost_ref):
    h2 = (_rms(x) * gpre_ref[...]).astype(bf16)
    dff = wup_ref.shape[1]
    acc = None
    for c in range(dff // MLP_FCHUNK):
        a = _dot(h2, wup_ref[:, c * MLP_FCHUNK:(c + 1) * MLP_FCHUNK])
        a = jnp.maximum(a, 0.0)
        part = _dot((a * a).astype(bf16), wdn_ref[c * MLP_FCHUNK:(c + 1) * MLP_FCHUNK, :])
        acc = part if acc is None else acc + part
    return x + _rms(acc) * gpost_ref[...]


def _mlp_kernel(x_ref, gpre_ref, wup_ref, wdn_ref, gpost_ref, y_ref):
    y_ref[...] = _mlp_body(x_ref[...], gpre_ref, wup_ref, wdn_ref, gpost_ref)


def _mlp_call(x2d, w):
    n, dm = x2d.shape
    tile = min(MLP_TILE, n)
    assert n % tile == 0
    consts = [w[k] for k in ("g2pre", "wup", "wdn", "g2post")]
    return pl.pallas_call(
        _mlp_kernel,
        grid=(n // tile,),
        in_specs=[pl.BlockSpec((tile, dm), lambda i: (i, 0))] + [_const_spec(c.shape) for c in consts],
        out_specs=pl.BlockSpec((tile, dm), lambda i: (i, 0)),
        out_shape=jax.ShapeDtypeStruct((n, dm), f32),
        compiler_params=pltpu.CompilerParams(
            dimension_semantics=("arbitrary",), vmem_limit_bytes=VMEM_LIMIT),
        name="mlp",
    )(x2d, *consts)


def _sample_pre_kernel(x_ref, h0r_ref, h0i_ref, gpre_ref, win_ref, bin_ref, bblk_ref, cblk_ref,
                       abr_ref, abi_ref, d_ref, wglu_ref, bglu_ref, lb_ref, hgn_ref, wbs_ref,
                       m5_ref, ghg_ref, gn_ref, qt_ref, cols_ref, v_ref, oin_ref,
                       s5r_ref, s5i_ref):
    n_seq = h0r_ref.shape[0]
    dm = gpre_ref.shape[1]
    steps = x_ref.shape[1] // dm
    sdim = h0r_ref.shape[1]
    hw = HG_HEADS * HG_DK
    s5w = d_ref.shape[1]
    x = jnp.concatenate([x_ref[:, t * dm:(t + 1) * dm] for t in range(steps)], axis=0)
    hb = (_rms(x) * gpre_ref[...]).astype(bf16)

    u = _dot(hb, win_ref[:, :s5w]) + bin_ref[:, :s5w]
    ub = u.astype(bf16)
    hr = h0r_ref[...]
    hi = h0i_ref[...]
    a_r = abr_ref[...]
    a_i = abi_ref[...]
    ys = []
    for t in range(steps):
        bu = _dot(ub[t * n_seq:(t + 1) * n_seq, :], bblk_ref[...])
        hr, hi = (a_r * hr - a_i * hi + bu[:, :sdim], a_r * hi + a_i * hr + bu[:, sdim:])
        ys.append(_dot(jnp.concatenate([hr, hi], axis=1).astype(bf16), cblk_ref[...]))
    s5r_ref[...] = hr
    s5i_ref[...] = hi
    y = jnp.concatenate(ys, axis=0)
    y = _gelu_tanh(y + d_ref[...] * u)
    z = _dot(y.astype(bf16), wglu_ref[...]) + bglu_ref[...]
    ys5 = (y * _sigmoid(z)).astype(bf16)
    g0 = s5w + 4 * hw
    def put_steps(ref, val):
        wcol = val.shape[1]
        for t in range(steps):
            ref[:, t * wcol:(t + 1) * wcol] = val[t * n_seq:(t + 1) * n_seq, :]

    gates = _dot(hb, win_ref[:, g0:]) + bin_ref[:, g0:]
    put_steps(m5_ref, _sigmoid(gates[:, :dm]) * _dot(ys5, wbs_ref[...]))
    put_steps(ghg_ref, _sigmoid(gates[:, dm:]))

    def proj(i):
        c0 = s5w + i * hw
        return _dot(hb, win_ref[:, c0:c0 + hw]) + bin_ref[:, c0:c0 + hw]

    q_raw = proj(0)
    f_raw = proj(1)
    v = proj(2)
    g_raw = proj(3)
    q = q_raw * _sigmoid(q_raw)
    lb = lb_ref[...]
    f = lb + (1.0 - lb) * _sigmoid(f_raw)
    lf = jnp.log(f)
    kk = 1.0 - f
    put_steps(gn_ref, g_raw * _sigmoid(g_raw) * hgn_ref[...])
    bs = []
    acc = None
    for t in range(steps):
        sl = slice(t * n_seq, (t + 1) * n_seq)
        acc = lf[sl] if acc is None else acc + lf[sl]
        bs.append(acc)
    b_last = bs[-1]

    def put(ref, slot, val):
        for h in range(HG_HEADS):
            c0 = (h * 8 + slot) * HG_DK
            ref[:, c0:c0 + HG_DK] = val[:, h * HG_DK:(h + 1) * HG_DK]

    zero = jnp.zeros((n_seq, hw), f32)
    put(cols_ref, steps, jnp.exp(b_last))
    for slot in range(steps, 8):
        put(qt_ref, slot, zero)
        put(v_ref, slot, zero)
        if slot > steps:
            put(cols_ref, slot, zero)
    for t in range(steps):
        sl = slice(t * n_seq, (t + 1) * n_seq)
        put(qt_ref, t, q[sl] * jnp.exp(bs[t]))
        put(cols_ref, t, kk[sl] * jnp.exp(b_last - bs[t]))
        put(v_ref, t, v[sl])
        o_t = None
        for s in range(t + 1):
            sls = slice(s * n_seq, (s + 1) * n_seq)
            prod = q[sl] * kk[sls] * jnp.exp(bs[t] - bs[s])
            parts = []
            for h in range(HG_HEADS):
                hs = slice(h * HG_DK, (h + 1) * HG_DK)
                wgt = jnp.sum(prod[:, hs], axis=1, keepdims=True)
                parts.append(wgt * v[sls, hs])
            term = jnp.concatenate(parts, axis=1)
            o_t = term if o_t is None else o_t + term
        oin_ref[:, t * hw:(t + 1) * hw] = o_t


def _sample_pre_call(x_sm, h0r, h0i, w):
    n_seq, sdim = h0r.shape
    dm = w["gpre"].shape[1]
    steps = x_sm.shape[1] // dm
    hw = HG_HEADS * HG_DK
    nb = SAMPLE_PRE_BLOCK
    assert n_seq % nb == 0
    consts = [w[k] for k in ("gpre", "win", "bin", "bblk", "cblk", "abr", "abi", "d",
                             "wglu", "bglu", "lbrow", "hgnrow", "wbs")]
    widths = (steps * dm,
              steps * dm,
              steps * hw,
              8 * hw,
              8 * hw,
              8 * hw,
              steps * hw,
              sdim, sdim)
    seq_spec = lambda width: pl.BlockSpec((nb, width), lambda i: (i, 0))
    return pl.pallas_call(
        _sample_pre_kernel,
        grid=(n_seq // nb,),
        in_specs=[seq_spec(steps * dm), seq_spec(sdim), seq_spec(sdim)] + [_const_spec(c.shape) for c in consts],
        out_specs=tuple(seq_spec(wd) for wd in widths),
        out_shape=tuple(jax.ShapeDtypeStruct((n_seq, wd), f32) for wd in widths),
        compiler_params=pltpu.CompilerParams(
            dimension_semantics=("arbitrary",), vmem_limit_bytes=VMEM_LIMIT),
        name="sample_pre",
    )(x_sm, h0r, h0i, *consts)


def _sample_state_kernel(s0_ref, qt_ref, v_ref, cols_ref, snew_ref, oint_ref):
    nb = s0_ref.shape[0]
    nrow = cols_ref.shape[1]
    hw = HG_HEADS * HG_DK
    row_head = lax.broadcasted_iota(jnp.int32, (nrow, hw), 0) // 8
    col_head = lax.broadcasted_iota(jnp.int32, (nrow, hw), 1) // HG_DK
    diag = row_head == col_head
    pad = jnp.zeros((LANES - nrow, HG_DK), f32)
    for i in range(nb):
        xt = jnp.concatenate([cols_ref[i], pad], axis=0).T
        vbd = jnp.where(diag, jnp.concatenate([v_ref[i]] * HG_HEADS, axis=1), 0.0)
        vbd = jnp.concatenate([vbd, jnp.zeros((LANES - nrow, hw), f32)], axis=0).astype(bf16)
        ds = _dot(xt.astype(bf16), vbd)
        for h in range(HG_HEADS):
            s0 = s0_ref[i, h]
            oint_ref[i, h] = _dot(qt_ref[i, h].astype(bf16), s0.astype(bf16))
            snew_ref[i, h] = xt[:, h * 8 + 4:h * 8 + 5] * s0 + ds[:, h * HG_DK:(h + 1) * HG_DK]


def _sample_state_call(s0, qt8, v8, cols):
    n_seq = s0.shape[0]
    nb = SAMPLE_SEQ_BLOCK
    blk4 = (nb, HG_HEADS, 8, HG_DK)
    return pl.pallas_call(
        _sample_state_kernel,
        grid=(n_seq // nb,),
        in_specs=[pl.BlockSpec((nb, HG_HEADS, HG_DK, HG_DK), lambda i: (i, 0, 0, 0)),
                  pl.BlockSpec(blk4, lambda i: (i, 0, 0, 0)),
                  pl.BlockSpec((nb, HG_HEADS * 8, HG_DK), lambda i: (i, 0, 0)),
                  pl.BlockSpec((nb, HG_HEADS * 8, HG_DK), lambda i: (i, 0, 0))],
        out_specs=(pl.BlockSpec((nb, HG_HEADS, HG_DK, HG_DK), lambda i: (i, 0, 0, 0)),
                   pl.BlockSpec(blk4, lambda i: (i, 0, 0, 0))),
        out_shape=(jax.ShapeDtypeStruct(s0.shape, f32),
                   jax.ShapeDtypeStruct((n_seq, HG_HEADS, 8, HG_DK), f32)),
        compiler_params=pltpu.CompilerParams(
            dimension_semantics=("arbitrary",), vmem_limit_bytes=VMEM_LIMIT),
        name="sample_state",
    )(s0, qt8.reshape((n_seq,) + blk4[1:]), v8.reshape(n_seq, HG_HEADS * 8, HG_DK),
      cols.reshape(n_seq, HG_HEADS * 8, HG_DK))


def _sample_post_kernel(x_ref, oint_ref, oin_ref, gn_ref, m5_ref, ghg_ref, wbh_ref, wout_ref, gpost_ref,
                        g2pre_ref, wup_ref, wdn_ref, g2post_ref, y_ref):
    n_seq = x_ref.shape[0]
    dm = gpost_ref.shape[1]
    steps = x_ref.shape[1] // dm
    hw = HG_HEADS * HG_DK

    def steps_major(ref, wcol):
        return jnp.concatenate([ref[:, t * wcol:(t + 1) * wcol] for t in range(steps)], axis=0)

    x = steps_major(x_ref, dm)
    oint = jnp.concatenate(
        [jnp.concatenate([oint_ref[:, (h * 8 + t) * HG_DK:(h * 8 + t + 1) * HG_DK] for h in range(HG_HEADS)],
                         axis=1) for t in range(steps)], axis=0)
    o = oint + steps_major(oin_ref, hw)
    parts = []
    for h in range(HG_HEADS):
        hs = slice(h * HG_DK, (h + 1) * HG_DK)
        parts.append(_rms(o[:, hs]))
    yhg = (jnp.concatenate(parts, axis=1) * steps_major(gn_ref, hw)).astype(bf16)
    merged = steps_major(m5_ref, dm) + steps_major(ghg_ref, dm) * _dot(yhg, wbh_ref[...])
    mo = _dot(merged.astype(bf16), wout_ref[...])
    x1 = x + _rms(mo) * gpost_ref[...]
    y = _mlp_body(x1, g2pre_ref, wup_ref, wdn_ref, g2post_ref)
    for t in range(steps):
        y_ref[:, t * dm:(t + 1) * dm] = y[t * n_seq:(t + 1) * n_seq, :]


def _sample_post_call(x_sm, oint, oin, gn, m5, ghg, w):
    ins = [x_sm, oint, oin, gn, m5, ghg] + [w[k] for k in ("wbh", "wout", "gpost", "g2pre", "wup",
                                                           "wdn", "g2post")]
    return pl.pallas_call(
        _sample_post_kernel,
        out_shape=jax.ShapeDtypeStruct(x_sm.shape, f32),
        compiler_params=pltpu.CompilerParams(vmem_limit_bytes=VMEM_LIMIT),
        name="sample_post",
    )(*ins)


def _prep_weights(l, norm_mix_pre, norm_mix_post, norm_mlp_pre, norm_mlp_post, w_in, b_in,
                  s5_a_re, s5_a_im, s5_log_dt, s5_b_re, s5_b_im, s5_c_re, s5_c_im, s5_d, s5_w_glu,
                  s5_b_glu, hg_lb_logits, hg_norm, w_br_s5, w_br_hg, w_out, w_up, w_down, n_pow):
    hp = lax.Precision.HIGHEST
    dm = w_in.shape[1]
    s5w = s5_d.shape[1]
    kw = HG_HEADS * HG_DK
    n_groups = s5w // S5_GROUP
    n_pairs = n_groups // 2
    w = {}
    row = lambda a: a.astype(f32).reshape(1, -1)
    w["gpre"], w["gpost"] = row(norm_mix_pre[l]), row(norm_mix_post[l])
    w["g2pre"], w["g2post"] = row(norm_mlp_pre[l]), row(norm_mlp_post[l])
    w["win"], w["bin"] = w_in[l].astype(bf16), row(b_in[l])
    lb_all = jnp.cumsum(jax.nn.softmax(hg_lb_logits.astype(f32), axis=0), axis=0)
    w["lbrow"] = row(lb_all[l])
    w["hgnrow"] = row(jnp.tile(hg_norm[l].astype(f32), HG_HEADS))
    w["d"] = row(s5_d[l])
    w["wglu"], w["bglu"] = s5_w_glu[l].astype(bf16), row(s5_b_glu[l])
    w["wbs"], w["wbh"], w["wout"] = w_br_s5[l].astype(bf16), w_br_hg[l].astype(bf16), w_out[l].astype(bf16)
    w["wup"], w["wdn"] = w_up[l].astype(bf16), w_down[l].astype(bf16)

    a_re, a_im = s5_a_re[l].astype(f32), s5_a_im[l].astype(f32)
    dt = jnp.exp(s5_log_dt[l].astype(f32))[:, None]
    mag = jnp.exp(dt * a_re)
    abr, abi = mag * jnp.cos(dt * a_im), mag * jnp.sin(dt * a_im)
    den = a_re * a_re + a_im * a_im
    nr, ni = abr - 1.0, abi
    cfr, cfi = (nr * a_re + ni * a_im) / den, (ni * a_re - nr * a_im) / den
    b_re, b_im = s5_b_re[l].astype(f32), s5_b_im[l].astype(f32)
    bbr = cfr[..., None] * b_re - cfi[..., None] * b_im
    bbi = cfr[..., None] * b_im + cfi[..., None] * b_re
    c_re, c_im = s5_c_re[l].astype(f32), s5_c_im[l].astype(f32)
    w["abr"], w["abi"] = abr.reshape(1, -1), abi.reshape(1, -1)

    jj = jnp.arange(S5_CHUNK + 1, dtype=f32)[:, None, None]
    pmag = jnp.exp(jj * (dt * a_re)[None])
    pr, pi = pmag * jnp.cos(jj * (dt * a_im)[None]), pmag * jnp.sin(jj * (dt * a_im)[None])
    cpr = c_re[None] * pr[:, :, None, :] - c_im[None] * pi[:, :, None, :]
    cpi = c_re[None] * pi[:, :, None, :] + c_im[None] * pr[:, :, None, :]
    taps = (jnp.einsum("jgon,gni->jgoi", cpr[:S5_CHUNK], bbr, precision=hp)
            - jnp.einsum("jgon,gni->jgoi", cpi[:S5_CHUNK], bbi, precision=hp))
    s_idx = jnp.arange(S5_CHUNK)[:, None]
    t_idx = jnp.arange(S5_CHUNK)[None, :]
    lag = t_idx - s_idx
    toe = jnp.where((lag >= 0)[:, :, None, None, None],
                    taps[jnp.clip(lag, 0, S5_CHUNK - 1)], 0.0)
    toe = jnp.transpose(toe, (2, 0, 4, 1, 3)).reshape(n_groups, LANES, LANES)
    prs, pis = pr[S5_CHUNK - 1::-1][:S5_CHUNK], pi[S5_CHUNK - 1::-1][:S5_CHUNK]
    ber = prs[..., None] * bbr[None] - pis[..., None] * bbi[None]
    bei = prs[..., None] * bbi[None] + pis[..., None] * bbr[None]
    ber = jnp.transpose(ber, (1, 0, 3, 2)).reshape(n_groups, LANES, S5_STATE)
    bei = jnp.transpose(bei, (1, 0, 3, 2)).reshape(n_groups, LANES, S5_STATE)
    car = jnp.transpose(cpr[1:], (1, 3, 0, 2)).reshape(n_groups, S5_STATE, LANES)
    cai = jnp.transpose(cpi[1:], (1, 3, 0, 2)).reshape(n_groups, S5_STATE, LANES)

    def pair_diag(m):
        g, r, c = m.shape
        m = m.reshape(g // 2, 2, r, c)
        z = jnp.zeros_like(m[:, 0])
        return jnp.concatenate([jnp.concatenate([m[:, 0], z], axis=2),
                                jnp.concatenate([z, m[:, 1]], axis=2)], axis=1)

    w["w1"] = jnp.concatenate([pair_diag(toe), pair_diag(ber), pair_diag(bei)], axis=2).astype(bf16)
    w["w2"] = jnp.concatenate([pair_diag(car), pair_diag(-cai)], axis=1).astype(bf16)
    qr, qi = pr[S5_CHUNK], pi[S5_CHUNK]
    aps_r, aps_i = [], []
    for _ in range(n_pow):
        aps_r.append(qr.reshape(-1))
        aps_i.append(qi.reshape(-1))
        qr, qi = qr * qr - qi * qi, 2.0 * qr * qi
    w["apr"], w["api"] = jnp.stack(aps_r), jnp.stack(aps_i)

    eye = jnp.eye(n_groups, dtype=f32)
    def bdiag(m):
        g, r, c = m.shape
        return (eye[:, None, :, None] * m[:, :, None, :]).reshape(g * r, g * c)
    bt_r, bt_i = jnp.transpose(bbr, (0, 2, 1)), jnp.transpose(bbi, (0, 2, 1))
    w["bblk"] = jnp.concatenate([bdiag(bt_r), bdiag(bt_i)], axis=1).astype(bf16)
    ct_r, ct_i = jnp.transpose(c_re, (0, 2, 1)), jnp.transpose(c_im, (0, 2, 1))
    w["cblk"] = jnp.concatenate([bdiag(ct_r), bdiag(-ct_i)], axis=0).astype(bf16)
    return w


def kernel(x_prompt, x_sample, state_s5_re, state_s5_im, state_hg, norm_mix_pre, norm_mix_post,
           norm_mlp_pre, norm_mlp_post, w_in, b_in, s5_a_re, s5_a_im, s5_log_dt, s5_b_re, s5_b_im,
           s5_c_re, s5_c_im, s5_d, s5_w_glu, s5_b_glu, hg_lb_logits, hg_norm, w_br_s5, w_br_hg,
           w_out, w_up, w_down):
    depth = w_in.shape[0]
    bsz, seq, dm = x_prompt.shape
    n_seq, steps, _ = x_sample.shape
    n_groups, n_state = s5_a_re.shape[1], s5_a_re.shape[2]
    hw = HG_HEADS * HG_DK
    assert depth == 1 and steps == 4 and n_state == S5_STATE
    n_pow = max(1, int(math.log2(min(PROMPT_TILE, seq) // S5_CHUNK)))
    xp, xs = x_prompt, x_sample
    outs = [[] for _ in range(6)]
    for l in range(depth):
        w = _prep_weights(l, norm_mix_pre, norm_mix_post, norm_mlp_pre, norm_mlp_post, w_in, b_in,
                          s5_a_re, s5_a_im, s5_log_dt, s5_b_re, s5_b_im, s5_c_re, s5_c_im, s5_d,
                          s5_w_glu, s5_b_glu, hg_lb_logits, hg_norm, w_br_s5, w_br_hg, w_out, w_up,
                          w_down, n_pow)
        x1, p_re, p_im, p_hg = _mixer_call(xp, w)
        xp = _mlp_call(x1.reshape(bsz * seq, dm), w).reshape(bsz, seq, dm)
        outs[0].append(p_re.reshape(bsz, n_groups, n_state))
        outs[1].append(p_im.reshape(bsz, n_groups, n_state))
        outs[2].append(p_hg)
        x_sm = xs.reshape(n_seq, steps * dm)
        h0r = state_s5_re[l].reshape(n_seq, n_groups * n_state)
        h0i = state_s5_im[l].reshape(n_seq, n_groups * n_state)
        m5, ghg, gn, qt8, cols, v8, oin, s_re, s_im = _sample_pre_call(x_sm, h0r, h0i, w)
        s_hg, oint = _sample_state_call(state_hg[l], qt8, v8, cols)
        y_sm = _sample_post_call(x_sm, oint.reshape(n_seq, 8 * hw), oin, gn, m5, ghg, w)
        xs = y_sm.reshape(n_seq, steps, dm)
        outs[3].append(s_re.reshape(n_seq, n_groups, n_state))
        outs[4].append(s_im.reshape(n_seq, n_groups, n_state))
        outs[5].append(s_hg)
    return (xp, xs) + tuple(o[0][None] for o in outs)
```

```python
import functools
import math

import jax
import jax.numpy as jnp
import numpy as np
from jax import lax
from jax.experimental import pallas as pl
from jax.experimental.pallas import tpu as pltpu

f32 = jnp.float32
bf16 = jnp.bfloat16

NORM_EPS = 1e-6
S5_GROUP = 16
S5_STATE = 64
S5_CHUNK = 8
HG_HEADS = 8
HG_DK = 128
HG_CHUNK = 64
LANES = 128
VMEM_LIMIT = 56 * 1024 * 1024
PROMPT_TILE = 512
MLP_TILE = 512
MLP_FCHUNK = 1024
SAMPLE_SEQ_BLOCK = 8
SAMPLE_PRE_BLOCK = 32


def _sigmoid(x):
    return 0.5 * (jnp.tanh(0.5 * x) + 1.0)


def _gelu_tanh(x):
    c = math.sqrt(2.0 / math.pi)
    return 0.5 * x * (1.0 + jnp.tanh(c * (x + 0.044715 * (x * x * x))))


def _rms(x):
    return x * lax.rsqrt(jnp.mean(x * x, axis=-1, keepdims=True) + NORM_EPS)


def _dot(a, b):
    return jnp.dot(a, b, preferred_element_type=f32)


def _dot_nt(a, b):
    return lax.dot_general(a, b, (((1,), (1,)), ((), ())), preferred_element_type=f32)


def _dot_tn(a, b):
    return lax.dot_general(a, b, (((0,), (0,)), ((), ())), preferred_element_type=f32)


def _block_transpose8(xs, blk):
    xs = list(xs)
    for d in (4, 2, 1):
        keep = (blk & d) == 0
        for i in range(8):
            if i & d == 0:
                a, b = xs[i], xs[i + d]
                xs[i] = jnp.where(keep, a, pltpu.roll(b, d * S5_GROUP, 1))
                xs[i + d] = jnp.where(keep, pltpu.roll(a, LANES - d * S5_GROUP, 1), b)
    return xs


def _mixer_kernel(x_ref, gpre_ref, win_ref, bin_ref, w1_ref, w2_ref, apr_ref, api_ref, d_ref,
                  wglu_ref, bglu_ref, lb_ref, hgn_ref, wbs_ref, wbh_ref, wout_ref, gpost_ref,
                  y_ref, s5r_ref, s5i_ref, hgs_ref,
                  hb_scr, u_scr, ys_scr, o_scr, yloc_scr, er_scr, ei_scr, cr_scr, ci_scr,
                  st_scr, qfig_scr, yhg_scr, m_scr, fac_scr, el_scr):
    t_idx = pl.program_id(1)
    n_t = pl.num_programs(1)
    tile = x_ref.shape[1]
    rows = tile // S5_CHUNK
    n_pairs = w1_ref.shape[0]
    n_cols = u_scr.shape[0]
    hw = HG_HEADS * HG_DK
    s5w = n_cols * LANES
    g0 = s5w + 4 * hw

    @pl.when(t_idx == 0)
    def _():
        cr_scr[...] = jnp.zeros_like(cr_scr)
        ci_scr[...] = jnp.zeros_like(ci_scr)
        st_scr[...] = jnp.zeros_like(st_scr)

    x = x_ref[0]
    hb_scr[...] = (_rms(x) * gpre_ref[...]).astype(bf16)
    dm = wbs_ref.shape[1]
    half = tile // 2

    def project(r0, r1, c0, c1):
        qfig_scr[r0:r1, c0:c1] = (_dot(hb_scr[r0:r1, :], win_ref[:, s5w + c0:s5w + c1])
                                  + bin_ref[:, s5w + c0:s5w + c1])


    u = _dot(hb_scr[...], win_ref[:, :s5w]) + bin_ref[:, :s5w]
    for j in range(n_cols):
        u_scr[j] = u[:, j * LANES:(j + 1) * LANES]

    project(0, half, 0, hw)
    blk = lax.broadcasted_iota(jnp.int32, (rows, LANES), 1) // S5_GROUP
    for j in range(n_cols):
        xs = [u_scr[j, pl.ds(s, rows, stride=S5_CHUNK), :] for s in range(S5_CHUNK)]
        ys = _block_transpose8(xs, blk)
        for q in range(4):
            o_scr[j * 4 + q] = jnp.concatenate([ys[2 * q], ys[2 * q + 1]], axis=1).astype(bf16)

    project(0, half, hw, 2 * hw)
    for p in range(n_pairs):
        r1 = _dot(o_scr[p], w1_ref[p])
        yloc_scr[p] = r1[:, :2 * LANES]
        er_scr[:, p * LANES:(p + 1) * LANES] = r1[:, 2 * LANES:3 * LANES]
        ei_scr[:, p * LANES:(p + 1) * LANES] = r1[:, 3 * LANES:]

    er = er_scr[...]
    ei = ei_scr[...]
    row = lax.broadcasted_iota(jnp.int32, (rows, 1), 0)
    first = row == 0
    cr = cr_scr[...]
    ci = ci_scr[...]
    a_r = apr_ref[0:1, :]
    a_i = api_ref[0:1, :]
    project(0, half, 2 * hw, 3 * hw)
    er = er + jnp.where(first, a_r * cr - a_i * ci, 0.0)
    ei = ei + jnp.where(first, a_r * ci + a_i * cr, 0.0)
    for k in range(int(math.log2(rows))):
        d = 1 << k
        p_r = apr_ref[k:k + 1, :]
        p_i = api_ref[k:k + 1, :]
        s_r = pltpu.roll(er, d, 0)
        s_i = pltpu.roll(ei, d, 0)
        valid = row >= d
        er, ei = (er + jnp.where(valid, p_r * s_r - p_i * s_i, 0.0),
                  ei + jnp.where(valid, p_r * s_i + p_i * s_r, 0.0))
    hs_r = jnp.where(first, cr, pltpu.roll(er, 1, 0))
    hs_i = jnp.where(first, ci, pltpu.roll(ei, 1, 0))
    cr_scr[...] = er[rows - 1:rows, :]
    ci_scr[...] = ei[rows - 1:rows, :]

    project(0, half, 3 * hw, 4 * hw)
    for j in range(n_cols):
        halves = []
        for q in range(4):
            p = j * 4 + q
            hp = jnp.concatenate([hs_r[:, p * LANES:(p + 1) * LANES],
                                  hs_i[:, p * LANES:(p + 1) * LANES]], axis=1).astype(bf16)
            yp = yloc_scr[p] + _dot(hp, w2_ref[p])
            halves += [yp[:, :LANES], yp[:, LANES:]]
        zs = _block_transpose8(halves, blk)
        for s in range(S5_CHUNK):
            ys_scr[j, pl.ds(s, rows, stride=S5_CHUNK), :] = zs[s]

    y = jnp.concatenate([ys_scr[j] for j in range(n_cols)], axis=1)
    u = jnp.concatenate([u_scr[j] for j in range(n_cols)], axis=1)
    y = _gelu_tanh(y + d_ref[...] * u)
    z = _dot(y.astype(bf16), wglu_ref[...]) + bglu_ref[...]
    ys5 = (y * _sigmoid(z)).astype(bf16)
    gate_s5 = _sigmoid(_dot(hb_scr[...], win_ref[:, g0:g0 + dm]) + bin_ref[:, g0:g0 + dm])
    m_scr[...] = gate_s5 * _dot(ys5, wbs_ref[...])

    ch = HG_CHUNK
    ri = lax.broadcasted_iota(jnp.int32, (ch, ch), 0)
    ci_ = lax.broadcasted_iota(jnp.int32, (ch, ch), 1)
    causal = ci_ <= ri
    tril = causal.astype(bf16)
    lb = lb_ref[...]
    hgn = hgn_ref[...]
    n_chunks = tile // ch
    n_side = n_chunks // 2
    assert n_side == 4 and dm % n_side == 0

    def merge_hg(r0, r1, c0, c1):
        gate = _sigmoid(_dot(hb_scr[r0:r1, :], win_ref[:, g0 + dm + c0:g0 + dm + c1])
                        + bin_ref[:, g0 + dm + c0:g0 + dm + c1])
        m_scr[r0:r1, c0:c1] = m_scr[r0:r1, c0:c1] + gate * _dot(yhg_scr[r0:r1, :], wbh_ref[:, c0:c1])

    def chunk_factors(c):
        r0 = c * ch
        slot = c % 2
        q_raw = qfig_scr[pl.ds(r0, ch), 0:hw]
        f_raw = qfig_scr[pl.ds(r0, ch), hw:2 * hw]
        q = q_raw * _sigmoid(q_raw)
        f = lb + (1.0 - lb) * _sigmoid(f_raw)
        lf = jnp.log(f)
        kk = 1.0 - f
        hi = lf.astype(bf16)
        lo = (lf - hi.astype(f32)).astype(bf16)
        cs = _dot(tril, jnp.concatenate([hi, lo], axis=1))
        b = cs[:, :hw] + cs[:, hw:]
        mid = b[ch // 2 - 1:ch // 2, :]
        b_last = b[ch - 1:ch, :]
        qh = q * jnp.exp(b - mid)
        kh = kk * jnp.exp(mid - b)
        fac_scr[slot, 0] = qh.astype(bf16)
        fac_scr[slot, 1] = kh.astype(bf16)
        fac_scr[slot, 2] = (qh * jnp.exp(mid)).astype(bf16)
        fac_scr[slot, 3] = (kh * jnp.exp(b_last - mid)).astype(bf16)
        fac_scr[slot, 4] = qfig_scr[pl.ds(r0, ch), 2 * hw:3 * hw].astype(bf16)
        el_scr[slot] = jnp.exp(b_last)

    def chunk_matmuls(c):
        r0 = c * ch
        slot = c % 2
        heads = [slice(h * HG_DK, (h + 1) * HG_DK) for h in range(HG_HEADS)]
        scs = [jnp.where(causal, _dot_nt(fac_scr[slot, 0, :, hs], fac_scr[slot, 1, :, hs]), 0.0).astype(bf16)
               for hs in heads]
        e_last = el_scr[slot]
        outs = []
        for h, hs in enumerate(heads):
            st = st_scr[h]
            vb = fac_scr[slot, 4, :, hs]
            o = _dot(scs[h], vb) + _dot_nt(fac_scr[slot, 2, :, hs], st.astype(bf16))
            st_scr[h] = st * e_last[:, hs] + _dot_tn(vb, fac_scr[slot, 3, :, hs])
            outs.append(_rms(o))
        g_raw = qfig_scr[pl.ds(r0, ch), 3 * hw:4 * hw]
        on = jnp.concatenate(outs, axis=1) * hgn * (g_raw * _sigmoid(g_raw))
        yhg_scr[pl.ds(r0, ch), :] = on.astype(bf16)

    chunk_factors(0)
    for c in range(n_chunks):
        if c + 1 < n_chunks:
            chunk_factors(c + 1)
        chunk_matmuls(c)
        if c < n_side:
            project(half, tile, c * hw, (c + 1) * hw)
        else:
            j = c - n_side
            merge_hg(0, half, j * (dm // n_side), (j + 1) * (dm // n_side))
    merge_hg(half, tile, 0, dm)
    mo = _dot(m_scr[...].astype(bf16), wout_ref[...])
    y_ref[0] = x_ref[0] + _rms(mo) * gpost_ref[...]

    @pl.when(t_idx == n_t - 1)
    def _():
        s5r_ref[0] = cr_scr[...]
        s5i_ref[0] = ci_scr[...]
        for h in range(HG_HEADS):
            hgs_ref[0, h] = st_scr[h].T


def _const_spec(shape):
    nd = len(shape)
    return pl.BlockSpec(shape, lambda *_: (0,) * nd, pipeline_mode=pl.Buffered(1))


def _mixer_call(x, w):
    bsz, seq, dm = x.shape
    tile = min(PROMPT_TILE, seq)
    assert seq % tile == 0 and tile % HG_CHUNK == 0
    rows = tile // S5_CHUNK
    assert rows & (rows - 1) == 0 and w["apr"].shape[0] >= int(math.log2(rows))
    s5w = w["d"].shape[1]
    n_cols = s5w // LANES
    n_pairs = w["w1"].shape[0]
    sdim = w["apr"].shape[1]
    consts = [w[k] for k in ("gpre", "win", "bin", "w1", "w2", "apr", "api", "d", "wglu", "bglu",
                             "lbrow", "hgnrow", "wbs", "wbh", "wout", "gpost")]
    in_specs = [pl.BlockSpec((1, tile, dm), lambda b, t: (b, t, 0))]
    in_specs += [_const_spec(c.shape) for c in consts]
    out_shape = (jax.ShapeDtypeStruct((bsz, seq, dm), f32),
                 jax.ShapeDtypeStruct((bsz, 1, sdim), f32),
                 jax.ShapeDtypeStruct((bsz, 1, sdim), f32),
                 jax.ShapeDtypeStruct((bsz, HG_HEADS, HG_DK, HG_DK), f32))
    out_specs = (pl.BlockSpec((1, tile, dm), lambda b, t: (b, t, 0)),
                 pl.BlockSpec((1, 1, sdim), lambda b, t: (b, 0, 0)),
                 pl.BlockSpec((1, 1, sdim), lambda b, t: (b, 0, 0)),
                 pl.BlockSpec((1, HG_HEADS, HG_DK, HG_DK), lambda b, t: (b, 0, 0, 0)))
    scratch = [
        pltpu.VMEM((tile, dm), bf16),
        pltpu.VMEM((n_cols, tile, LANES), f32),
        pltpu.VMEM((n_cols, tile, LANES), f32),
        pltpu.VMEM((n_pairs, rows, 2 * LANES), bf16),
        pltpu.VMEM((n_pairs, rows, 2 * LANES), f32),
        pltpu.VMEM((rows, sdim), f32),
        pltpu.VMEM((rows, sdim), f32),
        pltpu.VMEM((1, sdim), f32),
        pltpu.VMEM((1, sdim), f32),
        pltpu.VMEM((HG_HEADS, HG_DK, HG_DK), f32),
        pltpu.VMEM((tile, HG_HEADS * 4 * HG_DK), f32),
        pltpu.VMEM((tile, HG_HEADS * HG_DK), bf16),
        pltpu.VMEM((tile, dm), f32),
        pltpu.VMEM((2, 5, HG_CHUNK, HG_HEADS * HG_DK), bf16),
        pltpu.VMEM((2, 1, HG_HEADS * HG_DK), f32),
    ]
    return pl.pallas_call(
        _mixer_kernel,
        grid=(bsz, seq // tile),
        in_specs=in_specs,
        out_specs=out_specs,
        out_shape=out_shape,
        scratch_shapes=scratch,
        compiler_params=pltpu.CompilerParams(
            dimension_semantics=("arbitrary", "arbitrary"), vmem_limit_bytes=VMEM_LIMIT),
        name="prompt_mixer",
    )(x, *consts)


def _mlp_body(x, gpre_ref, wup_ref, wdn_ref, gpost_ref):
    h2 = (_rms(x) * gpre_ref[...]).astype(bf16)
    dff = wup_ref.shape[1]
    acc = None
    for c in range(dff // MLP_FCHUNK):
        a = _dot(h2, wup_ref[:, c * MLP_FCHUNK:(c + 1) * MLP_FCHUNK])
        a = jnp.maximum(a, 0.0)
        part = _dot((a * a).astype(bf16), wdn_ref[c * MLP_FCHUNK:(c + 1) * MLP_FCHUNK, :])
        acc = part if acc is None else acc + part
    return x + _rms(acc) * gpost_ref[...]


def _mlp_kernel(x_ref, gpre_ref, wup_ref, wdn_ref, gpost_ref, y_ref):
    y_ref[...] = _mlp_body(x_ref[...], gpre_ref, wup_ref, wdn_ref, gpost_ref)


def _mlp_call(x2d, w):
    n, dm = x2d.shape
    tile = min(MLP_TILE, n)
    assert n % tile == 0
    consts = [w[k] for k in ("g2pre", "wup", "wdn", "g2post")]
    return pl.pallas_call(
        _mlp_kernel,
        grid=(n // tile,),
        in_specs=[pl.BlockSpec((tile, dm), lambda i: (i, 0))] + [_const_spec(c.shape) for c in consts],
        out_specs=pl.BlockSpec((tile, dm), lambda i: (i, 0)),
        out_shape=jax.ShapeDtypeStruct((n, dm), f32),
        compiler_params=pltpu.CompilerParams(
            dimension_semantics=("arbitrary",), vmem_limit_bytes=VMEM_LIMIT),
        name="mlp",
    )(x2d, *consts)


def _sample_pre_kernel(x_ref, h0r_ref, h0i_ref, gpre_ref, win_ref, bin_ref, bblk_ref, cblkt_ref,
                       abr_ref, abi_ref, d_ref, wglu_ref, bglu_ref, lb_ref, hgn_ref, wbs_ref,
                       m5_ref, ghg_ref, gn_ref, qt_ref, cols_ref, v_ref, oin_ref,
                       s5r_ref, s5i_ref):
    n_seq = h0r_ref.shape[0]
    dm = gpre_ref.shape[1]
    steps = x_ref.shape[1] // dm
    sdim = h0r_ref.shape[1]
    hw = HG_HEADS * HG_DK
    s5w = d_ref.shape[1]
    x = jnp.concatenate([x_ref[:, t * dm:(t + 1) * dm] for t in range(steps)], axis=0)
    hb = (_rms(x) * gpre_ref[...]).astype(bf16)

    u = _dot(hb, win_ref[:, :s5w]) + bin_ref[:, :s5w]
    ub = u.astype(bf16)
    hr = h0r_ref[...]
    hi = h0i_ref[...]
    a_r = abr_ref[...]
    a_i = abi_ref[...]
    ys = []
    for t in range(steps):
        bu = _dot(ub[t * n_seq:(t + 1) * n_seq, :], bblk_ref[...])
        hr, hi = (a_r * hr - a_i * hi + bu[:, :sdim], a_r * hi + a_i * hr + bu[:, sdim:])
        ys.append(_dot_nt(jnp.concatenate([hr, hi], axis=1).astype(bf16), cblkt_ref[...]))
    s5r_ref[...] = hr
    s5i_ref[...] = hi
    y = jnp.concatenate(ys, axis=0)
    y = _gelu_tanh(y + d_ref[...] * u)
    z = _dot(y.astype(bf16), wglu_ref[...]) + bglu_ref[...]
    ys5 = (y * _sigmoid(z)).astype(bf16)
    g0 = s5w + 4 * hw
    def put_steps(ref, val):
        wcol = val.shape[1]
        for t in range(steps):
            ref[:, t * wcol:(t + 1) * wcol] = val[t * n_seq:(t + 1) * n_seq, :]

    gates = _dot(hb, win_ref[:, g0:]) + bin_ref[:, g0:]
    put_steps(m5_ref, _sigmoid(gates[:, :dm]) * _dot(ys5, wbs_ref[...]))
    put_steps(ghg_ref, _sigmoid(gates[:, dm:]))

    def proj(i):
        c0 = s5w + i * hw
        return _dot(hb, win_ref[:, c0:c0 + hw]) + bin_ref[:, c0:c0 + hw]

    q_raw = proj(0)
    f_raw = proj(1)
    v = proj(2)
    g_raw = proj(3)
    q = q_raw * _sigmoid(q_raw)
    lb = lb_ref[...]
    f = lb + (1.0 - lb) * _sigmoid(f_raw)
    lf = jnp.log(f)
    kk = 1.0 - f
    put_steps(gn_ref, g_raw * _sigmoid(g_raw) * hgn_ref[...])
    bs = []
    acc = None
    for t in range(steps):
        sl = slice(t * n_seq, (t + 1) * n_seq)
        acc = lf[sl] if acc is None else acc + lf[sl]
        bs.append(acc)
    b_last = bs[-1]

    def put(ref, slot, val):
        for h in range(HG_HEADS):
            c0 = (h * 8 + slot) * HG_DK
            ref[:, c0:c0 + HG_DK] = val[:, h * HG_DK:(h + 1) * HG_DK]

    zero = jnp.zeros((n_seq, hw), f32)
    put(cols_ref, steps, jnp.exp(b_last))
    for slot in range(steps, 8):
        put(qt_ref, slot, zero)
        put(v_ref, slot, zero)
        if slot > steps:
            put(cols_ref, slot, zero)
    for t in range(steps):
        sl = slice(t * n_seq, (t + 1) * n_seq)
        put(qt_ref, t, q[sl] * jnp.exp(bs[t]))
        put(cols_ref, t, kk[sl] * jnp.exp(b_last - bs[t]))
        put(v_ref, t, v[sl])
        o_t = None
        for s in range(t + 1):
            sls = slice(s * n_seq, (s + 1) * n_seq)
            prod = q[sl] * kk[sls] * jnp.exp(bs[t] - bs[s])
            parts = []
            for h in range(HG_HEADS):
                hs = slice(h * HG_DK, (h + 1) * HG_DK)
                wgt = jnp.sum(prod[:, hs], axis=1, keepdims=True)
                parts.append(wgt * v[sls, hs])
            term = jnp.concatenate(parts, axis=1)
            o_t = term if o_t is None else o_t + term
        oin_ref[:, t * hw:(t + 1) * hw] = o_t


def _sample_pre_call(x_sm, h0r, h0i, w):
    n_seq, sdim = h0r.shape
    dm = w["gpre"].shape[1]
    steps = x_sm.shape[1] // dm
    hw = HG_HEADS * HG_DK
    nb = SAMPLE_PRE_BLOCK
    assert n_seq % nb == 0
    consts = [w[k] for k in ("gpre", "win", "bin", "bblk", "cblkt", "abr", "abi", "d",
                             "wglu", "bglu", "lbrow", "hgnrow", "wbs")]
    widths = (steps * dm,
              steps * dm,
              steps * hw,
              8 * hw,
              8 * hw,
              8 * hw,
              steps * hw,
              sdim, sdim)
    seq_spec = lambda width: pl.BlockSpec((nb, width), lambda i: (i, 0))
    return pl.pallas_call(
        _sample_pre_kernel,
        grid=(n_seq // nb,),
        in_specs=[seq_spec(steps * dm), seq_spec(sdim), seq_spec(sdim)] + [_const_spec(c.shape) for c in consts],
        out_specs=tuple(seq_spec(wd) for wd in widths),
        out_shape=tuple(jax.ShapeDtypeStruct((n_seq, wd), f32) for wd in widths),
        compiler_params=pltpu.CompilerParams(
            dimension_semantics=("arbitrary",), vmem_limit_bytes=VMEM_LIMIT),
        name="sample_pre",
    )(x_sm, h0r, h0i, *consts)


def _sample_state_kernel(s0_ref, qt_ref, v_ref, cols_ref, snew_ref, oint_ref):
    nb = s0_ref.shape[0]
    nrow = cols_ref.shape[1]
    hw = HG_HEADS * HG_DK
    row_head = lax.broadcasted_iota(jnp.int32, (nrow, hw), 0) // 8
    col_head = lax.broadcasted_iota(jnp.int32, (nrow, hw), 1) // HG_DK
    diag = row_head == col_head
    pad = jnp.zeros((LANES - nrow, HG_DK), f32)
    for i in range(nb):
        xt = jnp.concatenate([cols_ref[i], pad], axis=0).T
        vbd = jnp.where(diag, jnp.concatenate([v_ref[i]] * HG_HEADS, axis=1), 0.0)
        vbd = jnp.concatenate([vbd, jnp.zeros((LANES - nrow, hw), f32)], axis=0).astype(bf16)
        ds = _dot(xt.astype(bf16), vbd)
        for h in range(HG_HEADS):
            s0 = s0_ref[i, h]
            oint_ref[i, h] = _dot(qt_ref[i, h].astype(bf16), s0.astype(bf16))
            snew_ref[i, h] = xt[:, h * 8 + 4:h * 8 + 5] * s0 + ds[:, h * HG_DK:(h + 1) * HG_DK]


def _sample_state_call(s0, qt8, v8, cols):
    n_seq = s0.shape[0]
    nb = SAMPLE_SEQ_BLOCK
    blk4 = (nb, HG_HEADS, 8, HG_DK)
    return pl.pallas_call(
        _sample_state_kernel,
        grid=(n_seq // nb,),
        in_specs=[pl.BlockSpec((nb, HG_HEADS, HG_DK, HG_DK), lambda i: (i, 0, 0, 0)),
                  pl.BlockSpec(blk4, lambda i: (i, 0, 0, 0)),
                  pl.BlockSpec((nb, HG_HEADS * 8, HG_DK), lambda i: (i, 0, 0)),
                  pl.BlockSpec((nb, HG_HEADS * 8, HG_DK), lambda i: (i, 0, 0))],
        out_specs=(pl.BlockSpec((nb, HG_HEADS, HG_DK, HG_DK), lambda i: (i, 0, 0, 0)),
                   pl.BlockSpec(blk4, lambda i: (i, 0, 0, 0))),
        out_shape=(jax.ShapeDtypeStruct(s0.shape, f32),
                   jax.ShapeDtypeStruct((n_seq, HG_HEADS, 8, HG_DK), f32)),
        compiler_params=pltpu.CompilerParams(
            dimension_semantics=("arbitrary",), vmem_limit_bytes=VMEM_LIMIT),
        name="sample_state",
    )(s0, qt8.reshape((n_seq,) + blk4[1:]), v8.reshape(n_seq, HG_HEADS * 8, HG_DK),
      cols.reshape(n_seq, HG_HEADS * 8, HG_DK))


def _sample_post_kernel(x_ref, oint_ref, oin_ref, gn_ref, m5_ref, ghg_ref, wbh_ref, wout_ref, gpost_ref,
                        g2pre_ref, wup_ref, wdn_ref, g2post_ref, y_ref):
    n_seq = x_ref.shape[0]
    dm = gpost_ref.shape[1]
    steps = x_ref.shape[1] // dm
    hw = HG_HEADS * HG_DK

    def steps_major(ref, wcol):
        return jnp.concatenate([ref[:, t * wcol:(t + 1) * wcol] for t in range(steps)], axis=0)

    x = steps_major(x_ref, dm)
    oint = jnp.concatenate(
        [jnp.concatenate([oint_ref[:, (h * 8 + t) * HG_DK:(h * 8 + t + 1) * HG_DK] for h in range(HG_HEADS)],
                         axis=1) for t in range(steps)], axis=0)
    o = oint + steps_major(oin_ref, hw)
    parts = []
    for h in range(HG_HEADS):
        hs = slice(h * HG_DK, (h + 1) * HG_DK)
        parts.append(_rms(o[:, hs]))
    yhg = (jnp.concatenate(parts, axis=1) * steps_major(gn_ref, hw)).astype(bf16)
    merged = steps_major(m5_ref, dm) + steps_major(ghg_ref, dm) * _dot(yhg, wbh_ref[...])
    mo = _dot(merged.astype(bf16), wout_ref[...])
    x1 = x + _rms(mo) * gpost_ref[...]
    y = _mlp_body(x1, g2pre_ref, wup_ref, wdn_ref, g2post_ref)
    for t in range(steps):
        y_ref[:, t * dm:(t + 1) * dm] = y[t * n_seq:(t + 1) * n_seq, :]


def _sample_post_call(x_sm, oint, oin, gn, m5, ghg, w):
    ins = [x_sm, oint, oin, gn, m5, ghg] + [w[k] for k in ("wbh", "wout", "gpost", "g2pre", "wup",
                                                           "wdn", "g2post")]
    return pl.pallas_call(
        _sample_post_kernel,
        out_shape=jax.ShapeDtypeStruct(x_sm.shape, f32),
        compiler_params=pltpu.CompilerParams(vmem_limit_bytes=VMEM_LIMIT),
        name="sample_post",
    )(*ins)


def _prep_weights(l, norm_mix_pre, norm_mix_post, norm_mlp_pre, norm_mlp_post, w_in, b_in,
                  s5_a_re, s5_a_im, s5_log_dt, s5_b_re, s5_b_im, s5_c_re, s5_c_im, s5_d, s5_w_glu,
                  s5_b_glu, hg_lb_logits, hg_norm, w_br_s5, w_br_hg, w_out, w_up, w_down, n_pow):
    hp = lax.Precision.HIGHEST
    dm = w_in.shape[1]
    s5w = s5_d.shape[1]
    kw = HG_HEADS * HG_DK
    n_groups = s5w // S5_GROUP
    n_pairs = n_groups // 2
    w = {}
    row = lambda a: a.astype(f32).reshape(1, -1)
    w["gpre"], w["gpost"] = row(norm_mix_pre[l]), row(norm_mix_post[l])
    w["g2pre"], w["g2post"] = row(norm_mlp_pre[l]), row(norm_mlp_post[l])
    w["win"], w["bin"] = w_in[l].astype(bf16), row(b_in[l])
    lb_all = jnp.cumsum(jax.nn.softmax(hg_lb_logits.astype(f32), axis=0), axis=0)
    w["lbrow"] = row(lb_all[l])
    w["hgnrow"] = row(jnp.tile(hg_norm[l].astype(f32), HG_HEADS))
    w["d"] = row(s5_d[l])
    w["wglu"], w["bglu"] = s5_w_glu[l].astype(bf16), row(s5_b_glu[l])
    w["wbs"], w["wbh"], w["wout"] = w_br_s5[l].astype(bf16), w_br_hg[l].astype(bf16), w_out[l].astype(bf16)
    w["wup"], w["wdn"] = w_up[l].astype(bf16), w_down[l].astype(bf16)

    a_re, a_im = s5_a_re[l].astype(f32), s5_a_im[l].astype(f32)
    dt = jnp.exp(s5_log_dt[l].astype(f32))[:, None]
    mag = jnp.exp(dt * a_re)
    abr, abi = mag * jnp.cos(dt * a_im), mag * jnp.sin(dt * a_im)
    den = a_re * a_re + a_im * a_im
    nr, ni = abr - 1.0, abi
    cfr, cfi = (nr * a_re + ni * a_im) / den, (ni * a_re - nr * a_im) / den
    b_re, b_im = s5_b_re[l].astype(f32), s5_b_im[l].astype(f32)
    bbr = cfr[..., None] * b_re - cfi[..., None] * b_im
    bbi = cfr[..., None] * b_im + cfi[..., None] * b_re
    c_re, c_im = s5_c_re[l].astype(f32), s5_c_im[l].astype(f32)
    w["abr"], w["abi"] = abr.reshape(1, -1), abi.reshape(1, -1)

    lam_r, lam_i = dt * a_re, dt * a_im
    grp = S5_GROUP

    def powers(expo, axis):
        e = jnp.asarray(expo, f32)
        e = e[None, :, None] if axis == 1 else e[None, None, :]
        lr = lam_r[:, None, :] if axis == 1 else lam_r[:, :, None]
        li = lam_i[:, None, :] if axis == 1 else lam_i[:, :, None]
        pmag = jnp.exp(lr * e)
        return pmag * jnp.cos(li * e), pmag * jnp.sin(li * e)

    bt_r, bt_i = jnp.transpose(bbr, (0, 2, 1)), jnp.transpose(bbi, (0, 2, 1))
    ct_r, ct_i = jnp.transpose(c_re, (0, 2, 1)), jnp.transpose(c_im, (0, 2, 1))
    p_r, p_i = powers(np.repeat(np.arange(1, S5_CHUNK + 1), grp), 2)
    ctr8, cti8 = jnp.tile(ct_r, (1, 1, S5_CHUNK)), jnp.tile(ct_i, (1, 1, S5_CHUNK))
    car = ctr8 * p_r - cti8 * p_i
    cai = ctr8 * p_i + cti8 * p_r
    p_r, p_i = powers(np.repeat(np.arange(S5_CHUNK - 1, -1, -1), grp), 1)
    btr8, bti8 = jnp.tile(bt_r, (1, S5_CHUNK, 1)), jnp.tile(bt_i, (1, S5_CHUNK, 1))
    ber = p_r * btr8 - p_i * bti8
    bei = p_r * bti8 + p_i * btr8
    lag = np.repeat(np.arange(-(S5_CHUNK - 1), S5_CHUNK), grp)
    p_r, p_i = powers(np.maximum(lag, 0), 2)
    live = jnp.asarray(lag >= 0, f32)[None, None, :]
    n_blk = 2 * S5_CHUNK - 1
    ctr15, cti15 = jnp.tile(ct_r, (1, 1, n_blk)), jnp.tile(ct_i, (1, 1, n_blk))
    cp_all = jnp.concatenate([(ctr15 * p_r - cti15 * p_i) * live,
                              -(ctr15 * p_i + cti15 * p_r) * live], axis=1)
    rpad = jnp.einsum("gcn,gnl->gcl", jnp.concatenate([bt_r, bt_i], axis=2), cp_all, precision=hp)
    toe = jnp.stack([rpad[:, :, (S5_CHUNK - 1 - s) * grp:(S5_CHUNK - 1 - s) * grp + LANES]
                     for s in range(S5_CHUNK)], axis=1).reshape(n_groups, LANES, LANES)

    def pair_diag(m):
        g, r, c = m.shape
        m = m.reshape(g // 2, 2, r, c)
        z = jnp.zeros_like(m[:, 0])
        return jnp.concatenate([jnp.concatenate([m[:, 0], z], axis=2),
                                jnp.concatenate([z, m[:, 1]], axis=2)], axis=1)

    w["w1"] = jnp.concatenate([pair_diag(toe), pair_diag(ber), pair_diag(bei)], axis=2).astype(bf16)
    w["w2"] = jnp.concatenate([pair_diag(car), pair_diag(-cai)], axis=1).astype(bf16)
    pmag = jnp.exp(S5_CHUNK * lam_r)
    qr, qi = pmag * jnp.cos(S5_CHUNK * lam_i), pmag * jnp.sin(S5_CHUNK * lam_i)
    aps_r, aps_i = [], []
    for _ in range(n_pow):
        aps_r.append(qr.reshape(-1))
        aps_i.append(qi.reshape(-1))
        qr, qi = qr * qr - qi * qi, 2.0 * qr * qi
    w["apr"], w["api"] = jnp.stack(aps_r), jnp.stack(aps_i)

    n_state = a_re.shape[1]
    same_group = jnp.asarray(np.arange(s5w)[:, None] // grp == np.arange(n_groups * n_state)[None, :] // n_state)
    def bdiag(m):
        flat = m.reshape(s5w, n_state)
        return jnp.where(same_group, jnp.tile(flat, (1, n_groups)), 0.0)
    w["bblk"] = jnp.concatenate([bdiag(bt_r), bdiag(bt_i)], axis=1).astype(bf16)
    w["cblkt"] = jnp.concatenate([bdiag(c_re), bdiag(-c_im)], axis=1).astype(bf16)
    return w


def kernel(x_prompt, x_sample, state_s5_re, state_s5_im, state_hg, norm_mix_pre, norm_mix_post,
           norm_mlp_pre, norm_mlp_post, w_in, b_in, s5_a_re, s5_a_im, s5_log_dt, s5_b_re, s5_b_im,
           s5_c_re, s5_c_im, s5_d, s5_w_glu, s5_b_glu, hg_lb_logits, hg_norm, w_br_s5, w_br_hg,
           w_out, w_up, w_down):
    depth = w_in.shape[0]
    bsz, seq, dm = x_prompt.shape
    n_seq, steps, _ = x_sample.shape
    n_groups, n_state = s5_a_re.shape[1], s5_a_re.shape[2]
    hw = HG_HEADS * HG_DK
    assert depth == 1 and steps == 4 and n_state == S5_STATE
    n_pow = max(1, int(math.log2(min(PROMPT_TILE, seq) // S5_CHUNK)))
    xp, xs = x_prompt, x_sample
    outs = [[] for _ in range(6)]
    for l in range(depth):
        w = _prep_weights(l, norm_mix_pre, norm_mix_post, norm_mlp_pre, norm_mlp_post, w_in, b_in,
                          s5_a_re, s5_a_im, s5_log_dt, s5_b_re, s5_b_im, s5_c_re, s5_c_im, s5_d,
                          s5_w_glu, s5_b_glu, hg_lb_logits, hg_norm, w_br_s5, w_br_hg, w_out, w_up,
                          w_down, n_pow)
        x1, p_re, p_im, p_hg = _mixer_call(xp, w)
        xp = _mlp_call(x1.reshape(bsz * seq, dm), w).reshape(bsz, seq, dm)
        outs[0].append(p_re.reshape(bsz, n_groups, n_state))
        outs[1].append(p_im.reshape(bsz, n_groups, n_state))
        outs[2].append(p_hg)
        x_sm = xs.reshape(n_seq, steps * dm)
        h0r = state_s5_re[l].reshape(n_seq, n_groups * n_state)
        h0i = state_s5_im[l].reshape(n_seq, n_groups * n_state)
        m5, ghg, gn, qt8, cols, v8, oin, s_re, s_im = _sample_pre_call(x_sm, h0r, h0i, w)
        s_hg, oint = _sample_state_call(state_hg[l], qt8, v8, cols)
        y_sm = _sample_post_call(x_sm, oint.reshape(n_seq, 8 * hw), oin, gn, m5, ghg, w)
        xs = y_sm.reshape(n_seq, steps, dm)
        outs[3].append(s_re.reshape(n_seq, n_groups, n_state))
        outs[4].append(s_im.reshape(n_seq, n_groups, n_state))
        outs[5].append(s_hg)
    return (xp, xs) + tuple(o[0][None] for o in outs)
```

```python
import functools
import math

import jax
import jax.numpy as jnp
import numpy as np
from jax import lax
from jax.experimental import pallas as pl
from jax.experimental.pallas import tpu as pltpu

f32 = jnp.float32
bf16 = jnp.bfloat16

NORM_EPS = 1e-6
S5_GROUP = 16
S5_STATE = 64
S5_CHUNK = 8
HG_HEADS = 8
HG_DK = 128
HG_CHUNK = 64
LANES = 128
VMEM_LIMIT = 56 * 1024 * 1024
PROMPT_TILE = 512
MLP_TILE = 1024
MLP_FCHUNK = 1024
SAMPLE_SEQ_BLOCK = 8
SAMPLE_PRE_BLOCK = 32


def _sigmoid(x):
    return 0.5 * (jnp.tanh(0.5 * x) + 1.0)


def _gelu_tanh(x):
    c = math.sqrt(2.0 / math.pi)
    return 0.5 * x * (1.0 + jnp.tanh(c * (x + 0.044715 * (x * x * x))))


def _rms(x):
    return x * lax.rsqrt(jnp.mean(x * x, axis=-1, keepdims=True) + NORM_EPS)


def _dot(a, b):
    return jnp.dot(a, b, preferred_element_type=f32)


def _dot_nt(a, b):
    return lax.dot_general(a, b, (((1,), (1,)), ((), ())), preferred_element_type=f32)


def _dot_tn(a, b):
    return lax.dot_general(a, b, (((0,), (0,)), ((), ())), preferred_element_type=f32)


def _block_transpose8(xs, blk):
    xs = list(xs)
    for d in (4, 2, 1):
        keep = (blk & d) == 0
        for i in range(8):
            if i & d == 0:
                a, b = xs[i], xs[i + d]
                xs[i] = jnp.where(keep, a, pltpu.roll(b, d * S5_GROUP, 1))
                xs[i + d] = jnp.where(keep, pltpu.roll(a, LANES - d * S5_GROUP, 1), b)
    return xs


def _mixer_kernel(x_ref, gpre_ref, win_ref, bin_ref, w1_ref, w2_ref, apr_ref, api_ref, d_ref,
                  wglu_ref, bglu_ref, lb_ref, hgn_ref, wbs_ref, wbh_ref, wout_ref, gpost_ref,
                  y_ref, s5r_ref, s5i_ref, hgs_ref,
                  hb_scr, u_scr, ys_scr, o_scr, yloc_scr, er_scr, ei_scr, cr_scr, ci_scr,
                  st_scr, qfig_scr, yhg_scr, m_scr, fac_scr, el_scr):
    t_idx = pl.program_id(1)
    n_t = pl.num_programs(1)
    tile = x_ref.shape[1]
    rows = tile // S5_CHUNK
    n_pairs = w1_ref.shape[0]
    n_cols = u_scr.shape[0]
    hw = HG_HEADS * HG_DK
    s5w = n_cols * LANES
    g0 = s5w + 4 * hw

    @pl.when(t_idx == 0)
    def _():
        cr_scr[...] = jnp.zeros_like(cr_scr)
        ci_scr[...] = jnp.zeros_like(ci_scr)
        st_scr[...] = jnp.zeros_like(st_scr)

    x = x_ref[0]
    hb_scr[...] = (_rms(x) * gpre_ref[...]).astype(bf16)
    dm = wbs_ref.shape[1]
    half = tile // 2

    def project(r0, r1, c0, c1):
        qfig_scr[r0:r1, c0:c1] = (_dot(hb_scr[r0:r1, :], win_ref[:, s5w + c0:s5w + c1])
                                  + bin_ref[:, s5w + c0:s5w + c1])


    u = _dot(hb_scr[...], win_ref[:, :s5w]) + bin_ref[:, :s5w]
    for j in range(n_cols):
        u_scr[j] = u[:, j * LANES:(j + 1) * LANES]

    project(0, half, 0, hw)
    blk = lax.broadcasted_iota(jnp.int32, (rows, LANES), 1) // S5_GROUP
    for j in range(n_cols):
        xs = [u_scr[j, pl.ds(s, rows, stride=S5_CHUNK), :] for s in range(S5_CHUNK)]
        ys = _block_transpose8(xs, blk)
        for q in range(4):
            o_scr[j * 4 + q] = jnp.concatenate([ys[2 * q], ys[2 * q + 1]], axis=1).astype(bf16)

    project(0, half, hw, 2 * hw)
    for p in range(n_pairs):
        r1 = _dot(o_scr[p], w1_ref[p])
        yloc_scr[p] = r1[:, :2 * LANES]
        er_scr[:, p * LANES:(p + 1) * LANES] = r1[:, 2 * LANES:3 * LANES]
        ei_scr[:, p * LANES:(p + 1) * LANES] = r1[:, 3 * LANES:]

    er = er_scr[...]
    ei = ei_scr[...]
    row = lax.broadcasted_iota(jnp.int32, (rows, 1), 0)
    first = row == 0
    cr = cr_scr[...]
    ci = ci_scr[...]
    a_r = apr_ref[0:1, :]
    a_i = api_ref[0:1, :]
    project(0, half, 2 * hw, 3 * hw)
    er = er + jnp.where(first, a_r * cr - a_i * ci, 0.0)
    ei = ei + jnp.where(first, a_r * ci + a_i * cr, 0.0)
    for k in range(int(math.log2(rows))):
        d = 1 << k
        p_r = apr_ref[k:k + 1, :]
        p_i = api_ref[k:k + 1, :]
        s_r = pltpu.roll(er, d, 0)
        s_i = pltpu.roll(ei, d, 0)
        valid = row >= d
        er, ei = (er + jnp.where(valid, p_r * s_r - p_i * s_i, 0.0),
                  ei + jnp.where(valid, p_r * s_i + p_i * s_r, 0.0))
    hs_r = jnp.where(first, cr, pltpu.roll(er, 1, 0))
    hs_i = jnp.where(first, ci, pltpu.roll(ei, 1, 0))
    cr_scr[...] = er[rows - 1:rows, :]
    ci_scr[...] = ei[rows - 1:rows, :]

    project(0, half, 3 * hw, 4 * hw)
    for j in range(n_cols):
        halves = []
        for q in range(4):
            p = j * 4 + q
            hp = jnp.concatenate([hs_r[:, p * LANES:(p + 1) * LANES],
                                  hs_i[:, p * LANES:(p + 1) * LANES]], axis=1).astype(bf16)
            yp = yloc_scr[p] + _dot(hp, w2_ref[p])
            halves += [yp[:, :LANES], yp[:, LANES:]]
        zs = _block_transpose8(halves, blk)
        for s in range(S5_CHUNK):
            ys_scr[j, pl.ds(s, rows, stride=S5_CHUNK), :] = zs[s]

    y = jnp.concatenate([ys_scr[j] for j in range(n_cols)], axis=1)
    u = jnp.concatenate([u_scr[j] for j in range(n_cols)], axis=1)
    y = _gelu_tanh(y + d_ref[...] * u)
    z = _dot(y.astype(bf16), wglu_ref[...]) + bglu_ref[...]
    ys5 = (y * _sigmoid(z)).astype(bf16)
    gate_s5 = _sigmoid(_dot(hb_scr[...], win_ref[:, g0:g0 + dm]) + bin_ref[:, g0:g0 + dm])
    m_scr[...] = gate_s5 * _dot(ys5, wbs_ref[...])

    ch = HG_CHUNK
    ri = lax.broadcasted_iota(jnp.int32, (ch, ch), 0)
    ci_ = lax.broadcasted_iota(jnp.int32, (ch, ch), 1)
    causal = ci_ <= ri
    tril = causal.astype(bf16)
    lb = lb_ref[...]
    hgn = hgn_ref[...]
    n_chunks = tile // ch
    n_side = n_chunks // 2
    assert n_side == 4 and dm % n_side == 0

    def merge_hg(r0, r1, c0, c1):
        gate = _sigmoid(_dot(hb_scr[r0:r1, :], win_ref[:, g0 + dm + c0:g0 + dm + c1])
                        + bin_ref[:, g0 + dm + c0:g0 + dm + c1])
        m_scr[r0:r1, c0:c1] = m_scr[r0:r1, c0:c1] + gate * _dot(yhg_scr[r0:r1, :], wbh_ref[:, c0:c1])

    def chunk_factors(c):
        r0 = c * ch
        slot = c % 2
        q_raw = qfig_scr[pl.ds(r0, ch), 0:hw]
        f_raw = qfig_scr[pl.ds(r0, ch), hw:2 * hw]
        q = q_raw * _sigmoid(q_raw)
        f = lb + (1.0 - lb) * _sigmoid(f_raw)
        lf = jnp.log(f)
        kk = 1.0 - f
        hi = lf.astype(bf16)
        lo = (lf - hi.astype(f32)).astype(bf16)
        cs = _dot(tril, jnp.concatenate([hi, lo], axis=1))
        b = cs[:, :hw] + cs[:, hw:]
        mid = b[ch // 2 - 1:ch // 2, :]
        b_last = b[ch - 1:ch, :]
        qh = q * jnp.exp(b - mid)
        kh = kk * jnp.exp(mid - b)
        fac_scr[slot, 0] = qh.astype(bf16)
        fac_scr[slot, 1] = kh.astype(bf16)
        fac_scr[slot, 2] = (qh * jnp.exp(mid)).astype(bf16)
        fac_scr[slot, 3] = (kh * jnp.exp(b_last - mid)).astype(bf16)
        fac_scr[slot, 4] = qfig_scr[pl.ds(r0, ch), 2 * hw:3 * hw].astype(bf16)
        el_scr[slot] = jnp.exp(b_last)

    def chunk_matmuls(c):
        r0 = c * ch
        slot = c % 2
        heads = [slice(h * HG_DK, (h + 1) * HG_DK) for h in range(HG_HEADS)]
        scs = [jnp.where(causal, _dot_nt(fac_scr[slot, 0, :, hs], fac_scr[slot, 1, :, hs]), 0.0).astype(bf16)
               for hs in heads]
        e_last = el_scr[slot]
        outs = []
        for h, hs in enumerate(heads):
            st = st_scr[h]
            vb = fac_scr[slot, 4, :, hs]
            o = _dot(scs[h], vb) + _dot_nt(fac_scr[slot, 2, :, hs], st.astype(bf16))
            st_scr[h] = st * e_last[:, hs] + _dot_tn(vb, fac_scr[slot, 3, :, hs])
            outs.append(_rms(o))
        g_raw = qfig_scr[pl.ds(r0, ch), 3 * hw:4 * hw]
        on = jnp.concatenate(outs, axis=1) * hgn * (g_raw * _sigmoid(g_raw))
        yhg_scr[pl.ds(r0, ch), :] = on.astype(bf16)

    chunk_factors(0)
    for c in range(n_chunks):
        if c + 1 < n_chunks:
            chunk_factors(c + 1)
        chunk_matmuls(c)
        if c < n_side:
            project(half, tile, c * hw, (c + 1) * hw)
        else:
            j = c - n_side
            merge_hg(0, half, j * (dm // n_side), (j + 1) * (dm // n_side))
    for r0 in (0, half):
        if r0:
            merge_hg(half, tile, 0, dm)
        mo = _dot(m_scr[r0:r0 + half, :].astype(bf16), wout_ref[...])
        y_ref[0, r0:r0 + half, :] = x_ref[0, r0:r0 + half, :] + _rms(mo) * gpost_ref[...]

    @pl.when(t_idx == n_t - 1)
    def _():
        s5r_ref[0] = cr_scr[...]
        s5i_ref[0] = ci_scr[...]
        for h in range(HG_HEADS):
            hgs_ref[0, h] = st_scr[h].T


def _const_spec(shape):
    nd = len(shape)
    return pl.BlockSpec(shape, lambda *_: (0,) * nd, pipeline_mode=pl.Buffered(1))


def _mixer_call(x, w):
    bsz, seq, dm = x.shape
    tile = min(PROMPT_TILE, seq)
    assert seq % tile == 0 and tile % HG_CHUNK == 0
    rows = tile // S5_CHUNK
    assert rows & (rows - 1) == 0 and w["apr"].shape[0] >= int(math.log2(rows))
    s5w = w["d"].shape[1]
    n_cols = s5w // LANES
    n_pairs = w["w1"].shape[0]
    sdim = w["apr"].shape[1]
    consts = [w[k] for k in ("gpre", "win", "bin", "w1", "w2", "apr", "api", "d", "wglu", "bglu",
                             "lbrow", "hgnrow", "wbs", "wbh", "wout", "gpost")]
    in_specs = [pl.BlockSpec((1, tile, dm), lambda b, t: (b, t, 0))]
    in_specs += [_const_spec(c.shape) for c in consts]
    out_shape = (jax.ShapeDtypeStruct((bsz, seq, dm), f32),
                 jax.ShapeDtypeStruct((bsz, 1, sdim), f32),
                 jax.ShapeDtypeStruct((bsz, 1, sdim), f32),
                 jax.ShapeDtypeStruct((bsz, HG_HEADS, HG_DK, HG_DK), f32))
    out_specs = (pl.BlockSpec((1, tile, dm), lambda b, t: (b, t, 0)),
                 pl.BlockSpec((1, 1, sdim), lambda b, t: (b, 0, 0)),
                 pl.BlockSpec((1, 1, sdim), lambda b, t: (b, 0, 0)),
                 pl.BlockSpec((1, HG_HEADS, HG_DK, HG_DK), lambda b, t: (b, 0, 0, 0)))
    scratch = [
        pltpu.VMEM((tile, dm), bf16),
        pltpu.VMEM((n_cols, tile, LANES), f32),
        pltpu.VMEM((n_cols, tile, LANES), f32),
        pltpu.VMEM((n_pairs, rows, 2 * LANES), bf16),
        pltpu.VMEM((n_pairs, rows, 2 * LANES), f32),
        pltpu.VMEM((rows, sdim), f32),
        pltpu.VMEM((rows, sdim), f32),
        pltpu.VMEM((1, sdim), f32),
        pltpu.VMEM((1, sdim), f32),
        pltpu.VMEM((HG_HEADS, HG_DK, HG_DK), f32),
        pltpu.VMEM((tile, HG_HEADS * 4 * HG_DK), f32),
        pltpu.VMEM((tile, HG_HEADS * HG_DK), bf16),
        pltpu.VMEM((tile, dm), f32),
        pltpu.VMEM((2, 5, HG_CHUNK, HG_HEADS * HG_DK), bf16),
        pltpu.VMEM((2, 1, HG_HEADS * HG_DK), f32),
    ]
    return pl.pallas_call(
        _mixer_kernel,
        grid=(bsz, seq // tile),
        in_specs=in_specs,
        out_specs=out_specs,
        out_shape=out_shape,
        scratch_shapes=scratch,
        compiler_params=pltpu.CompilerParams(
            dimension_semantics=("arbitrary", "arbitrary"), vmem_limit_bytes=VMEM_LIMIT),
        name="prompt_mixer",
    )(x, *consts)


def _mlp_body(x, gpre_ref, wup_ref, wdn_ref, gpost_ref):
    h2 = (_rms(x) * gpre_ref[...]).astype(bf16)
    dff = wup_ref.shape[1]
    acc = None
    for c in range(dff // MLP_FCHUNK):
        a = _dot(h2, wup_ref[:, c * MLP_FCHUNK:(c + 1) * MLP_FCHUNK])
        a = jnp.maximum(a, 0.0)
        part = _dot((a * a).astype(bf16), wdn_ref[c * MLP_FCHUNK:(c + 1) * MLP_FCHUNK, :])
        acc = part if acc is None else acc + part
    return x + _rms(acc) * gpost_ref[...]


def _mlp_kernel(x_ref, gpre_ref, wup_ref, wdn_ref, gpost_ref, y_ref):
    half = x_ref.shape[0] // 2
    for r0 in (0, half):
        y_ref[r0:r0 + half, :] = _mlp_body(x_ref[r0:r0 + half, :], gpre_ref, wup_ref, wdn_ref, gpost_ref)


def _mlp_call(x2d, w):
    n, dm = x2d.shape
    tile = min(MLP_TILE, n)
    assert n % tile == 0
    consts = [w[k] for k in ("g2pre", "wup", "wdn", "g2post")]
    return pl.pallas_call(
        _mlp_kernel,
        grid=(n // tile,),
        in_specs=[pl.BlockSpec((tile, dm), lambda i: (i, 0))] + [_const_spec(c.shape) for c in consts],
        out_specs=pl.BlockSpec((tile, dm), lambda i: (i, 0)),
        out_shape=jax.ShapeDtypeStruct((n, dm), f32),
        compiler_params=pltpu.CompilerParams(
            dimension_semantics=("arbitrary",), vmem_limit_bytes=VMEM_LIMIT),
        name="mlp",
    )(x2d, *consts)


def _sample_pre_kernel(x_ref, h0r_ref, h0i_ref, gpre_ref, win_ref, bin_ref, bblk_ref, cblkt_ref,
                       abr_ref, abi_ref, d_ref, wglu_ref, bglu_ref, lb_ref, hgn_ref, wbs_ref,
                       m5_ref, ghg_ref, gn_ref, qt_ref, cols_ref, v_ref, oin_ref,
                       s5r_ref, s5i_ref):
    n_seq = h0r_ref.shape[0]
    dm = gpre_ref.shape[1]
    steps = x_ref.shape[1] // dm
    sdim = h0r_ref.shape[1]
    hw = HG_HEADS * HG_DK
    s5w = d_ref.shape[1]
    x = jnp.concatenate([x_ref[:, t * dm:(t + 1) * dm] for t in range(steps)], axis=0)
    hb = (_rms(x) * gpre_ref[...]).astype(bf16)

    u = _dot(hb, win_ref[:, :s5w]) + bin_ref[:, :s5w]
    bu = _dot(u.astype(bf16), bblk_ref[...])
    hr = h0r_ref[...]
    hi = h0i_ref[...]
    a_r = abr_ref[...]
    a_i = abi_ref[...]
    hs = []
    for t in range(steps):
        sl = slice(t * n_seq, (t + 1) * n_seq)
        hr, hi = (a_r * hr - a_i * hi + bu[sl, :sdim], a_r * hi + a_i * hr + bu[sl, sdim:])
        hs.append(jnp.concatenate([hr, hi], axis=1).astype(bf16))
    s5r_ref[...] = hr
    s5i_ref[...] = hi
    y = _dot_nt(jnp.concatenate(hs, axis=0), cblkt_ref[...])
    y = _gelu_tanh(y + d_ref[...] * u)
    z = _dot(y.astype(bf16), wglu_ref[...]) + bglu_ref[...]
    ys5 = (y * _sigmoid(z)).astype(bf16)
    g0 = s5w + 4 * hw
    def put_steps(ref, val):
        wcol = val.shape[1]
        for t in range(steps):
            ref[:, t * wcol:(t + 1) * wcol] = val[t * n_seq:(t + 1) * n_seq, :]

    gates = _dot(hb, win_ref[:, g0:]) + bin_ref[:, g0:]
    put_steps(m5_ref, _sigmoid(gates[:, :dm]) * _dot(ys5, wbs_ref[...]))
    put_steps(ghg_ref, _sigmoid(gates[:, dm:]))

    def proj(i):
        c0 = s5w + i * hw
        return _dot(hb, win_ref[:, c0:c0 + hw]) + bin_ref[:, c0:c0 + hw]

    q_raw = proj(0)
    f_raw = proj(1)
    v = proj(2)
    g_raw = proj(3)
    q = q_raw * _sigmoid(q_raw)
    lb = lb_ref[...]
    f = lb + (1.0 - lb) * _sigmoid(f_raw)
    lf = jnp.log(f)
    kk = 1.0 - f
    put_steps(gn_ref, g_raw * _sigmoid(g_raw) * hgn_ref[...])
    bs = []
    acc = None
    for t in range(steps):
        sl = slice(t * n_seq, (t + 1) * n_seq)
        acc = lf[sl] if acc is None else acc + lf[sl]
        bs.append(acc)
    b_last = bs[-1]

    def put(ref, slot, val):
        for h in range(HG_HEADS):
            c0 = (h * 8 + slot) * HG_DK
            ref[:, c0:c0 + HG_DK] = val[:, h * HG_DK:(h + 1) * HG_DK]

    zero = jnp.zeros((n_seq, hw), f32)
    put(cols_ref, steps, jnp.exp(b_last))
    for slot in range(steps, 8):
        put(qt_ref, slot, zero)
        put(v_ref, slot, zero)
        if slot > steps:
            put(cols_ref, slot, zero)
    for t in range(steps):
        sl = slice(t * n_seq, (t + 1) * n_seq)
        put(qt_ref, t, q[sl] * jnp.exp(bs[t]))
        put(cols_ref, t, kk[sl] * jnp.exp(b_last - bs[t]))
        put(v_ref, t, v[sl])
        o_t = None
        for s in range(t + 1):
            sls = slice(s * n_seq, (s + 1) * n_seq)
            prod = q[sl] * kk[sls] * jnp.exp(bs[t] - bs[s])
            parts = []
            for h in range(HG_HEADS):
                hs = slice(h * HG_DK, (h + 1) * HG_DK)
                wgt = jnp.sum(prod[:, hs], axis=1, keepdims=True)
                parts.append(wgt * v[sls, hs])
            term = jnp.concatenate(parts, axis=1)
            o_t = term if o_t is None else o_t + term
        oin_ref[:, t * hw:(t + 1) * hw] = o_t


def _sample_pre_call(x_sm, h0r, h0i, w):
    n_seq, sdim = h0r.shape
    dm = w["gpre"].shape[1]
    steps = x_sm.shape[1] // dm
    hw = HG_HEADS * HG_DK
    nb = SAMPLE_PRE_BLOCK
    assert n_seq % nb == 0
    consts = [w[k] for k in ("gpre", "win", "bin", "bblk", "cblkt", "abr", "abi", "d",
                             "wglu", "bglu", "lbrow", "hgnrow", "wbs")]
    widths = (steps * dm,
              steps * dm,
              steps * hw,
              8 * hw,
              8 * hw,
              8 * hw,
              steps * hw,
              sdim, sdim)
    seq_spec = lambda width: pl.BlockSpec((nb, width), lambda i: (i, 0))
    return pl.pallas_call(
        _sample_pre_kernel,
        grid=(n_seq // nb,),
        in_specs=[seq_spec(steps * dm), seq_spec(sdim), seq_spec(sdim)] + [_const_spec(c.shape) for c in consts],
        out_specs=tuple(seq_spec(wd) for wd in widths),
        out_shape=tuple(jax.ShapeDtypeStruct((n_seq, wd), f32) for wd in widths),
        compiler_params=pltpu.CompilerParams(
            dimension_semantics=("arbitrary",), vmem_limit_bytes=VMEM_LIMIT),
        name="sample_pre",
    )(x_sm, h0r, h0i, *consts)


def _sample_state_kernel(s0_ref, qt_ref, v_ref, cols_ref, snew_ref, oint_ref):
    nb = s0_ref.shape[0]
    nrow = cols_ref.shape[1]
    hw = HG_HEADS * HG_DK
    row_head = lax.broadcasted_iota(jnp.int32, (nrow, hw), 0) // 8
    col_head = lax.broadcasted_iota(jnp.int32, (nrow, hw), 1) // HG_DK
    diag = row_head == col_head
    pad = jnp.zeros((LANES - nrow, HG_DK), f32)
    for i in range(nb):
        xt = jnp.concatenate([cols_ref[i], pad], axis=0).T
        vbd = jnp.where(diag, jnp.concatenate([v_ref[i]] * HG_HEADS, axis=1), 0.0)
        vbd = jnp.concatenate([vbd, jnp.zeros((LANES - nrow, hw), f32)], axis=0).astype(bf16)
        ds = _dot(xt.astype(bf16), vbd)
        for h in range(HG_HEADS):
            s0 = s0_ref[i, h]
            oint_ref[i, h] = _dot(qt_ref[i, h].astype(bf16), s0.astype(bf16))
            snew_ref[i, h] = xt[:, h * 8 + 4:h * 8 + 5] * s0 + ds[:, h * HG_DK:(h + 1) * HG_DK]


def _sample_state_call(s0, qt8, v8, cols):
    n_seq = s0.shape[0]
    nb = SAMPLE_SEQ_BLOCK
    blk4 = (nb, HG_HEADS, 8, HG_DK)
    return pl.pallas_call(
        _sample_state_kernel,
        grid=(n_seq // nb,),
        in_specs=[pl.BlockSpec((nb, HG_HEADS, HG_DK, HG_DK), lambda i: (i, 0, 0, 0)),
                  pl.BlockSpec(blk4, lambda i: (i, 0, 0, 0)),
                  pl.BlockSpec((nb, HG_HEADS * 8, HG_DK), lambda i: (i, 0, 0)),
                  pl.BlockSpec((nb, HG_HEADS * 8, HG_DK), lambda i: (i, 0, 0))],
        out_specs=(pl.BlockSpec((nb, HG_HEADS, HG_DK, HG_DK), lambda i: (i, 0, 0, 0)),
                   pl.BlockSpec(blk4, lambda i: (i, 0, 0, 0))),
        out_shape=(jax.ShapeDtypeStruct(s0.shape, f32),
                   jax.ShapeDtypeStruct((n_seq, HG_HEADS, 8, HG_DK), f32)),
        compiler_params=pltpu.CompilerParams(
            dimension_semantics=("arbitrary",), vmem_limit_bytes=VMEM_LIMIT),
        name="sample_state",
    )(s0, qt8.reshape((n_seq,) + blk4[1:]), v8.reshape(n_seq, HG_HEADS * 8, HG_DK),
      cols.reshape(n_seq, HG_HEADS * 8, HG_DK))


def _sample_post_kernel(x_ref, oint_ref, oin_ref, gn_ref, m5_ref, ghg_ref, wbh_ref, wout_ref, gpost_ref,
                        g2pre_ref, wup_ref, wdn_ref, g2post_ref, y_ref):
    n_seq = x_ref.shape[0]
    dm = gpost_ref.shape[1]
    steps = x_ref.shape[1] // dm
    hw = HG_HEADS * HG_DK

    def steps_major(ref, wcol):
        return jnp.concatenate([ref[:, t * wcol:(t + 1) * wcol] for t in range(steps)], axis=0)

    x = steps_major(x_ref, dm)
    oint = jnp.concatenate(
        [jnp.concatenate([oint_ref[:, (h * 8 + t) * HG_DK:(h * 8 + t + 1) * HG_DK] for h in range(HG_HEADS)],
                         axis=1) for t in range(steps)], axis=0)
    o = oint + steps_major(oin_ref, hw)
    parts = []
    for h in range(HG_HEADS):
        hs = slice(h * HG_DK, (h + 1) * HG_DK)
        parts.append(_rms(o[:, hs]))
    yhg = (jnp.concatenate(parts, axis=1) * steps_major(gn_ref, hw)).astype(bf16)
    merged = steps_major(m5_ref, dm) + steps_major(ghg_ref, dm) * _dot(yhg, wbh_ref[...])
    mo = _dot(merged.astype(bf16), wout_ref[...])
    x1 = x + _rms(mo) * gpost_ref[...]
    y = _mlp_body(x1, g2pre_ref, wup_ref, wdn_ref, g2post_ref)
    for t in range(steps):
        y_ref[:, t * dm:(t + 1) * dm] = y[t * n_seq:(t + 1) * n_seq, :]


def _sample_post_call(x_sm, oint, oin, gn, m5, ghg, w):
    ins = [x_sm, oint, oin, gn, m5, ghg] + [w[k] for k in ("wbh", "wout", "gpost", "g2pre", "wup",
                                                           "wdn", "g2post")]
    return pl.pallas_call(
        _sample_post_kernel,
        out_shape=jax.ShapeDtypeStruct(x_sm.shape, f32),
        compiler_params=pltpu.CompilerParams(vmem_limit_bytes=VMEM_LIMIT),
        name="sample_post",
    )(*ins)


def _prep_weights(l, norm_mix_pre, norm_mix_post, norm_mlp_pre, norm_mlp_post, w_in, b_in,
                  s5_a_re, s5_a_im, s5_log_dt, s5_b_re, s5_b_im, s5_c_re, s5_c_im, s5_d, s5_w_glu,
                  s5_b_glu, hg_lb_logits, hg_norm, w_br_s5, w_br_hg, w_out, w_up, w_down, n_pow):
    hp = lax.Precision.HIGHEST
    dm = w_in.shape[1]
    s5w = s5_d.shape[1]
    kw = HG_HEADS * HG_DK
    n_groups = s5w // S5_GROUP
    n_pairs = n_groups // 2
    w = {}
    row = lambda a: a.astype(f32).reshape(1, -1)
    w["gpre"], w["gpost"] = row(norm_mix_pre[l]), row(norm_mix_post[l])
    w["g2pre"], w["g2post"] = row(norm_mlp_pre[l]), row(norm_mlp_post[l])
    w["win"], w["bin"] = w_in[l].astype(bf16), row(b_in[l])
    lb_all = jnp.cumsum(jax.nn.softmax(hg_lb_logits.astype(f32), axis=0), axis=0)
    w["lbrow"] = row(lb_all[l])
    w["hgnrow"] = row(jnp.tile(hg_norm[l].astype(f32), HG_HEADS))
    w["d"] = row(s5_d[l])
    w["wglu"], w["bglu"] = s5_w_glu[l].astype(bf16), row(s5_b_glu[l])
    w["wbs"], w["wbh"], w["wout"] = w_br_s5[l].astype(bf16), w_br_hg[l].astype(bf16), w_out[l].astype(bf16)
    w["wup"], w["wdn"] = w_up[l].astype(bf16), w_down[l].astype(bf16)

    a_re, a_im = s5_a_re[l].astype(f32), s5_a_im[l].astype(f32)
    dt = jnp.exp(s5_log_dt[l].astype(f32))[:, None]
    mag = jnp.exp(dt * a_re)
    abr, abi = mag * jnp.cos(dt * a_im), mag * jnp.sin(dt * a_im)
    den = a_re * a_re + a_im * a_im
    nr, ni = abr - 1.0, abi
    cfr, cfi = (nr * a_re + ni * a_im) / den, (ni * a_re - nr * a_im) / den
    b_re, b_im = s5_b_re[l].astype(f32), s5_b_im[l].astype(f32)
    bbr = cfr[..., None] * b_re - cfi[..., None] * b_im
    bbi = cfr[..., None] * b_im + cfi[..., None] * b_re
    c_re, c_im = s5_c_re[l].astype(f32), s5_c_im[l].astype(f32)
    w["abr"], w["abi"] = abr.reshape(1, -1), abi.reshape(1, -1)

    lam_r, lam_i = dt * a_re, dt * a_im
    grp = S5_GROUP
    n_state = a_re.shape[1]
    jj = jnp.arange(S5_CHUNK + 1, dtype=f32)[:, None, None]
    pmag = jnp.exp(jj * lam_r[None])
    pw_r, pw_i = pmag * jnp.cos(jj * lam_i[None]), pmag * jnp.sin(jj * lam_i[None])

    def spread_lanes(expo):
        sel = np.zeros((S5_CHUNK + 1, len(expo)), np.float32)
        keep = expo >= 0
        sel[expo[keep], np.nonzero(keep)[0]] = 1.0
        tab = jnp.stack([pw_r, pw_i]).reshape(2, S5_CHUNK + 1, n_groups * n_state)
        out = jnp.einsum("zjx,jl->zxl", tab, jnp.asarray(sel), precision=hp)
        return out.reshape(2, n_groups, n_state, len(expo))

    bt_r, bt_i = jnp.transpose(bbr, (0, 2, 1)), jnp.transpose(bbi, (0, 2, 1))
    ct_r, ct_i = jnp.transpose(c_re, (0, 2, 1)), jnp.transpose(c_im, (0, 2, 1))
    p_r, p_i = spread_lanes(np.repeat(np.arange(1, S5_CHUNK + 1), grp))
    ctr8, cti8 = jnp.tile(ct_r, (1, 1, S5_CHUNK)), jnp.tile(ct_i, (1, 1, S5_CHUNK))
    car = ctr8 * p_r - cti8 * p_i
    cai = ctr8 * p_i + cti8 * p_r
    rev = lambda t: jnp.repeat(jnp.transpose(t[S5_CHUNK - 1::-1], (1, 0, 2)), grp, axis=1)
    p_r, p_i = rev(pw_r), rev(pw_i)
    btr8, bti8 = jnp.tile(bt_r, (1, S5_CHUNK, 1)), jnp.tile(bt_i, (1, S5_CHUNK, 1))
    ber = p_r * btr8 - p_i * bti8
    bei = p_r * bti8 + p_i * btr8
    lag = np.repeat(np.arange(-(S5_CHUNK - 1), S5_CHUNK), grp)
    p_r, p_i = spread_lanes(lag)
    n_blk = 2 * S5_CHUNK - 1
    ctr15, cti15 = jnp.tile(ct_r, (1, 1, n_blk)), jnp.tile(ct_i, (1, 1, n_blk))
    cp_all = jnp.concatenate([ctr15 * p_r - cti15 * p_i, -(ctr15 * p_i + cti15 * p_r)], axis=1)
    rpad = jnp.einsum("gcn,gnl->gcl", jnp.concatenate([bt_r, bt_i], axis=2), cp_all, precision=hp)
    toe = jnp.stack([rpad[:, :, (S5_CHUNK - 1 - s) * grp:(S5_CHUNK - 1 - s) * grp + LANES]
                     for s in range(S5_CHUNK)], axis=1).reshape(n_groups, LANES, LANES)

    def pair_diag(m):
        g, r, c = m.shape
        m = m.reshape(g // 2, 2, r, c)
        z = jnp.zeros_like(m[:, 0])
        return jnp.concatenate([jnp.concatenate([m[:, 0], z], axis=2),
                                jnp.concatenate([z, m[:, 1]], axis=2)], axis=1)

    w["w1"] = jnp.concatenate([pair_diag(toe), pair_diag(ber), pair_diag(bei)], axis=2).astype(bf16)
    w["w2"] = jnp.concatenate([pair_diag(car), pair_diag(-cai)], axis=1).astype(bf16)
    qr, qi = pw_r[S5_CHUNK], pw_i[S5_CHUNK]
    aps_r, aps_i = [], []
    for _ in range(n_pow):
        aps_r.append(qr.reshape(-1))
        aps_i.append(qi.reshape(-1))
        qr, qi = qr * qr - qi * qi, 2.0 * qr * qi
    w["apr"], w["api"] = jnp.stack(aps_r), jnp.stack(aps_i)

    same_group = np.arange(s5w)[:, None] // grp == np.arange(n_groups * n_state)[None, :] // n_state
    lane_tile = np.tile(np.eye(n_state, dtype=np.float32), (1, n_groups))
    def bdiag(m):
        tiled = jnp.dot(m.reshape(s5w, n_state), jnp.asarray(lane_tile), precision=hp)
        return tiled * jnp.asarray(same_group, f32)
    w["bblk"] = jnp.concatenate([bdiag(bt_r), bdiag(bt_i)], axis=1).astype(bf16)
    w["cblkt"] = jnp.concatenate([bdiag(c_re), bdiag(-c_im)], axis=1).astype(bf16)
    return w


def kernel(x_prompt, x_sample, state_s5_re, state_s5_im, state_hg, norm_mix_pre, norm_mix_post,
           norm_mlp_pre, norm_mlp_post, w_in, b_in, s5_a_re, s5_a_im, s5_log_dt, s5_b_re, s5_b_im,
           s5_c_re, s5_c_im, s5_d, s5_w_glu, s5_b_glu, hg_lb_logits, hg_norm, w_br_s5, w_br_hg,
           w_out, w_up, w_down):
    depth = w_in.shape[0]
    bsz, seq, dm = x_prompt.shape
    n_seq, steps, _ = x_sample.shape
    n_groups, n_state = s5_a_re.shape[1], s5_a_re.shape[2]
    hw = HG_HEADS * HG_DK
    assert depth == 1 and steps == 4 and n_state == S5_STATE
    n_pow = max(1, int(math.log2(min(PROMPT_TILE, seq) // S5_CHUNK)))
    xp, xs = x_prompt, x_sample
    outs = [[] for _ in range(6)]
    for l in range(depth):
        w = _prep_weights(l, norm_mix_pre, norm_mix_post, norm_mlp_pre, norm_mlp_post, w_in, b_in,
                          s5_a_re, s5_a_im, s5_log_dt, s5_b_re, s5_b_im, s5_c_re, s5_c_im, s5_d,
                          s5_w_glu, s5_b_glu, hg_lb_logits, hg_norm, w_br_s5, w_br_hg, w_out, w_up,
                          w_down, n_pow)
        x1, p_re, p_im, p_hg = _mixer_call(xp, w)
        xp = _mlp_call(x1.reshape(bsz * seq, dm), w).reshape(bsz, seq, dm)
        outs[0].append(p_re.reshape(bsz, n_groups, n_state))
        outs[1].append(p_im.reshape(bsz, n_groups, n_state))
        outs[2].append(p_hg)
        x_sm = xs.reshape(n_seq, steps * dm)
        h0r = state_s5_re[l].reshape(n_seq, n_groups * n_state)
        h0i = state_s5_im[l].reshape(n_seq, n_groups * n_state)
        m5, ghg, gn, qt8, cols, v8, oin, s_re, s_im = _sample_pre_call(x_sm, h0r, h0i, w)
        s_hg, oint = _sample_state_call(state_hg[l], qt8, v8, cols)
        y_sm = _sample_post_call(x_sm, oint.reshape(n_seq, 8 * hw), oin, gn, m5, ghg, w)
        xs = y_sm.reshape(n_seq, steps, dm)
        outs[3].append(s_re.reshape(n_seq, n_groups, n_state))
        outs[4].append(s_im.reshape(n_seq, n_groups, n_state))
        outs[5].append(s_hg)
    return (xp, xs) + tuple(o[0][None] for o in outs)
```

```python
import functools
import math

import jax
import jax.numpy as jnp
import numpy as np
from jax import lax
from jax.experimental import pallas as pl
from jax.experimental.pallas import tpu as pltpu

f32 = jnp.float32
bf16 = jnp.bfloat16

NORM_EPS = 1e-6
S5_GROUP = 16
S5_STATE = 64
S5_CHUNK = 8
HG_HEADS = 8
HG_DK = 128
HG_CHUNK = 64
LANES = 128
VMEM_LIMIT = 56 * 1024 * 1024
PROMPT_TILE = 512
MLP_TILE = 1024
MLP_FCHUNK = 1024
SAMPLE_SEQ_BLOCK = 8
SAMPLE_PRE_BLOCK = 32


def _sigmoid(x):
    return 0.5 * (jnp.tanh(0.5 * x) + 1.0)


def _gelu_tanh(x):
    c = math.sqrt(2.0 / math.pi)
    return 0.5 * x * (1.0 + jnp.tanh(c * (x + 0.044715 * (x * x * x))))


def _rms(x):
    return x * lax.rsqrt(jnp.mean(x * x, axis=-1, keepdims=True) + NORM_EPS)


def _dot(a, b):
    return jnp.dot(a, b, preferred_element_type=f32)


def _dot_nt(a, b):
    return lax.dot_general(a, b, (((1,), (1,)), ((), ())), preferred_element_type=f32)


def _dot_tn(a, b):
    return lax.dot_general(a, b, (((0,), (0,)), ((), ())), preferred_element_type=f32)


def _block_transpose8(xs, blk):
    xs = list(xs)
    for d in (4, 2, 1):
        keep = (blk & d) == 0
        for i in range(8):
            if i & d == 0:
                a, b = xs[i], xs[i + d]
                xs[i] = jnp.where(keep, a, pltpu.roll(b, d * S5_GROUP, 1))
                xs[i + d] = jnp.where(keep, pltpu.roll(a, LANES - d * S5_GROUP, 1), b)
    return xs


def _mixer_kernel(x_ref, gpre_ref, win_ref, bin_ref, w1_ref, w2_ref, apr_ref, api_ref, d_ref,
                  wglu_ref, bglu_ref, lb_ref, hgn_ref, wbs_ref, wbh_ref, wout_ref, gpost_ref,
                  y_ref, s5r_ref, s5i_ref, hgs_ref,
                  hb_scr, u_scr, ys_scr, o_scr, yloc_scr, er_scr, ei_scr, cr_scr, ci_scr,
                  st_scr, qfig_scr, yhg_scr, m_scr, fac_scr, el_scr, ghg_scr):
    t_idx = pl.program_id(1)
    n_t = pl.num_programs(1)
    tile = x_ref.shape[1]
    rows = tile // S5_CHUNK
    n_pairs = w1_ref.shape[0]
    n_cols = u_scr.shape[0]
    hw = HG_HEADS * HG_DK
    s5w = n_cols * LANES
    g0 = s5w + 4 * hw

    @pl.when(t_idx == 0)
    def _():
        cr_scr[...] = jnp.zeros_like(cr_scr)
        ci_scr[...] = jnp.zeros_like(ci_scr)
        st_scr[...] = jnp.zeros_like(st_scr)

    x = x_ref[0]
    hb_scr[...] = (_rms(x) * gpre_ref[...]).astype(bf16)
    dm = wbs_ref.shape[1]
    half = tile // 2

    def project(r0, r1, c0, c1):
        qfig_scr[r0:r1, c0:c1] = (_dot(hb_scr[r0:r1, :], win_ref[:, s5w + c0:s5w + c1])
                                  + bin_ref[:, s5w + c0:s5w + c1])

    def gate_hg(j):
        c0, c1 = j * (dm // 4), (j + 1) * (dm // 4)
        ghg_scr[:, c0:c1] = _sigmoid(_dot(hb_scr[...], win_ref[:, g0 + dm + c0:g0 + dm + c1])
                                     + bin_ref[:, g0 + dm + c0:g0 + dm + c1])


    u = _dot(hb_scr[...], win_ref[:, :s5w]) + bin_ref[:, :s5w]
    for j in range(n_cols):
        u_scr[j] = u[:, j * LANES:(j + 1) * LANES]

    project(0, half, 0, hw)
    gate_hg(0)
    blk = lax.broadcasted_iota(jnp.int32, (rows, LANES), 1) // S5_GROUP
    for j in range(n_cols):
        xs = [u_scr[j, pl.ds(s, rows, stride=S5_CHUNK), :] for s in range(S5_CHUNK)]
        ys = _block_transpose8(xs, blk)
        for q in range(4):
            o_scr[j * 4 + q] = jnp.concatenate([ys[2 * q], ys[2 * q + 1]], axis=1).astype(bf16)

    project(0, half, hw, 2 * hw)
    gate_hg(1)
    for p in range(n_pairs):
        r1 = _dot(o_scr[p], w1_ref[p])
        yloc_scr[p] = r1[:, :2 * LANES]
        er_scr[:, p * LANES:(p + 1) * LANES] = r1[:, 2 * LANES:3 * LANES]
        ei_scr[:, p * LANES:(p + 1) * LANES] = r1[:, 3 * LANES:]

    er = er_scr[...]
    ei = ei_scr[...]
    row = lax.broadcasted_iota(jnp.int32, (rows, 1), 0)
    first = row == 0
    cr = cr_scr[...]
    ci = ci_scr[...]
    a_r = apr_ref[0:1, :]
    a_i = api_ref[0:1, :]
    project(0, half, 2 * hw, 3 * hw)
    gate_hg(2)
    er = er + jnp.where(first, a_r * cr - a_i * ci, 0.0)
    ei = ei + jnp.where(first, a_r * ci + a_i * cr, 0.0)
    for k in range(int(math.log2(rows))):
        d = 1 << k
        p_r = apr_ref[k:k + 1, :]
        p_i = api_ref[k:k + 1, :]
        s_r = pltpu.roll(er, d, 0)
        s_i = pltpu.roll(ei, d, 0)
        valid = row >= d
        er, ei = (er + jnp.where(valid, p_r * s_r - p_i * s_i, 0.0),
                  ei + jnp.where(valid, p_r * s_i + p_i * s_r, 0.0))
    hs_r = jnp.where(first, cr, pltpu.roll(er, 1, 0))
    hs_i = jnp.where(first, ci, pltpu.roll(ei, 1, 0))
    cr_scr[...] = er[rows - 1:rows, :]
    ci_scr[...] = ei[rows - 1:rows, :]

    project(0, half, 3 * hw, 4 * hw)
    gate_hg(3)
    for j in range(n_cols):
        halves = []
        for q in range(4):
            p = j * 4 + q
            hp = jnp.concatenate([hs_r[:, p * LANES:(p + 1) * LANES],
                                  hs_i[:, p * LANES:(p + 1) * LANES]], axis=1).astype(bf16)
            yp = yloc_scr[p] + _dot(hp, w2_ref[p])
            halves += [yp[:, :LANES], yp[:, LANES:]]
        zs = _block_transpose8(halves, blk)
        for s in range(S5_CHUNK):
            ys_scr[j, pl.ds(s, rows, stride=S5_CHUNK), :] = zs[s]

    y = jnp.concatenate([ys_scr[j] for j in range(n_cols)], axis=1)
    u = jnp.concatenate([u_scr[j] for j in range(n_cols)], axis=1)
    y = _gelu_tanh(y + d_ref[...] * u)
    z = _dot(y.astype(bf16), wglu_ref[...]) + bglu_ref[...]
    ys5 = (y * _sigmoid(z)).astype(bf16)
    gate_s5 = _sigmoid(_dot(hb_scr[...], win_ref[:, g0:g0 + dm]) + bin_ref[:, g0:g0 + dm])
    m_scr[...] = gate_s5 * _dot(ys5, wbs_ref[...])

    ch = HG_CHUNK
    ri = lax.broadcasted_iota(jnp.int32, (ch, ch), 0)
    ci_ = lax.broadcasted_iota(jnp.int32, (ch, ch), 1)
    causal = ci_ <= ri
    tril = causal.astype(bf16)
    lb = lb_ref[...]
    hgn = hgn_ref[...]
    n_chunks = tile // ch
    n_side = n_chunks // 2
    assert n_side == 4 and dm % n_side == 0

    def merge_hg(r0, r1, c0, c1):
        m_scr[r0:r1, c0:c1] = (m_scr[r0:r1, c0:c1]
                               + ghg_scr[r0:r1, c0:c1] * _dot(yhg_scr[r0:r1, :], wbh_ref[:, c0:c1]))

    def chunk_factors(c):
        r0 = c * ch
        slot = c % 2
        q_raw = qfig_scr[pl.ds(r0, ch), 0:hw]
        f_raw = qfig_scr[pl.ds(r0, ch), hw:2 * hw]
        q = q_raw * _sigmoid(q_raw)
        f = lb + (1.0 - lb) * _sigmoid(f_raw)
        lf = jnp.log(f)
        kk = 1.0 - f
        hi = lf.astype(bf16)
        lo = (lf - hi.astype(f32)).astype(bf16)
        cs = _dot(tril, jnp.concatenate([hi, lo], axis=1))
        b = cs[:, :hw] + cs[:, hw:]
        mid = b[ch // 2 - 1:ch // 2, :]
        b_last = b[ch - 1:ch, :]
        qh = q * jnp.exp(b - mid)
        kh = kk * jnp.exp(mid - b)
        fac_scr[slot, 0] = qh.astype(bf16)
        fac_scr[slot, 1] = kh.astype(bf16)
        fac_scr[slot, 2] = (qh * jnp.exp(mid)).astype(bf16)
        fac_scr[slot, 3] = (kh * jnp.exp(b_last - mid)).astype(bf16)
        fac_scr[slot, 4] = qfig_scr[pl.ds(r0, ch), 2 * hw:3 * hw].astype(bf16)
        el_scr[slot] = jnp.exp(b_last)

    def chunk_matmuls(c):
        r0 = c * ch
        slot = c % 2
        heads = [slice(h * HG_DK, (h + 1) * HG_DK) for h in range(HG_HEADS)]
        scs = [jnp.where(causal, _dot_nt(fac_scr[slot, 0, :, hs], fac_scr[slot, 1, :, hs]), 0.0).astype(bf16)
               for hs in heads]
        e_last = el_scr[slot]
        outs = []
        for h, hs in enumerate(heads):
            st = st_scr[h]
            vb = fac_scr[slot, 4, :, hs]
            o = _dot(scs[h], vb) + _dot_nt(fac_scr[slot, 2, :, hs], st.astype(bf16))
            st_scr[h] = st * e_last[:, hs] + _dot_tn(vb, fac_scr[slot, 3, :, hs])
            outs.append(_rms(o))
        g_raw = qfig_scr[pl.ds(r0, ch), 3 * hw:4 * hw]
        on = jnp.concatenate(outs, axis=1) * hgn * (g_raw * _sigmoid(g_raw))
        yhg_scr[pl.ds(r0, ch), :] = on.astype(bf16)

    chunk_factors(0)
    for c in range(n_chunks):
        if c + 1 < n_chunks:
            chunk_factors(c + 1)
        chunk_matmuls(c)
        if c < n_side:
            project(half, tile, c * hw, (c + 1) * hw)
        else:
            j = c - n_side
            merge_hg(0, half, j * (dm // n_side), (j + 1) * (dm // n_side))
    for r0 in (0, half):
        if r0:
            merge_hg(half, tile, 0, dm)
        mo = _dot(m_scr[r0:r0 + half, :].astype(bf16), wout_ref[...])
        y_ref[0, r0:r0 + half, :] = x_ref[0, r0:r0 + half, :] + _rms(mo) * gpost_ref[...]

    @pl.when(t_idx == n_t - 1)
    def _():
        s5r_ref[0] = cr_scr[...]
        s5i_ref[0] = ci_scr[...]
        for h in range(HG_HEADS):
            hgs_ref[0, h] = st_scr[h].T


def _const_spec(shape):
    nd = len(shape)
    return pl.BlockSpec(shape, lambda *_: (0,) * nd, pipeline_mode=pl.Buffered(1))


def _mixer_call(x, w):
    bsz, seq, dm = x.shape
    tile = min(PROMPT_TILE, seq)
    assert seq % tile == 0 and tile % HG_CHUNK == 0
    rows = tile // S5_CHUNK
    assert rows & (rows - 1) == 0 and w["apr"].shape[0] >= int(math.log2(rows))
    s5w = w["d"].shape[1]
    n_cols = s5w // LANES
    n_pairs = w["w1"].shape[0]
    sdim = w["apr"].shape[1]
    consts = [w[k] for k in ("gpre", "win", "bin", "w1", "w2", "apr", "api", "d", "wglu", "bglu",
                             "lbrow", "hgnrow", "wbs", "wbh", "wout", "gpost")]
    in_specs = [pl.BlockSpec((1, tile, dm), lambda b, t: (b, t, 0))]
    in_specs += [_const_spec(c.shape) for c in consts]
    out_shape = (jax.ShapeDtypeStruct((bsz, seq, dm), f32),
                 jax.ShapeDtypeStruct((bsz, 1, sdim), f32),
                 jax.ShapeDtypeStruct((bsz, 1, sdim), f32),
                 jax.ShapeDtypeStruct((bsz, HG_HEADS, HG_DK, HG_DK), f32))
    out_specs = (pl.BlockSpec((1, tile, dm), lambda b, t: (b, t, 0)),
                 pl.BlockSpec((1, 1, sdim), lambda b, t: (b, 0, 0)),
                 pl.BlockSpec((1, 1, sdim), lambda b, t: (b, 0, 0)),
                 pl.BlockSpec((1, HG_HEADS, HG_DK, HG_DK), lambda b, t: (b, 0, 0, 0)))
    scratch = [
        pltpu.VMEM((tile, dm), bf16),
        pltpu.VMEM((n_cols, tile, LANES), f32),
        pltpu.VMEM((n_cols, tile, LANES), f32),
        pltpu.VMEM((n_pairs, rows, 2 * LANES), bf16),
        pltpu.VMEM((n_pairs, rows, 2 * LANES), f32),
        pltpu.VMEM((rows, sdim), f32),
        pltpu.VMEM((rows, sdim), f32),
        pltpu.VMEM((1, sdim), f32),
        pltpu.VMEM((1, sdim), f32),
        pltpu.VMEM((HG_HEADS, HG_DK, HG_DK), f32),
        pltpu.VMEM((tile, HG_HEADS * 4 * HG_DK), f32),
        pltpu.VMEM((tile, HG_HEADS * HG_DK), bf16),
        pltpu.VMEM((tile, dm), f32),
        pltpu.VMEM((2, 5, HG_CHUNK, HG_HEADS * HG_DK), bf16),
        pltpu.VMEM((2, 1, HG_HEADS * HG_DK), f32),
        pltpu.VMEM((tile, dm), f32),
    ]
    return pl.pallas_call(
        _mixer_kernel,
        grid=(bsz, seq // tile),
        in_specs=in_specs,
        out_specs=out_specs,
        out_shape=out_shape,
        scratch_shapes=scratch,
        compiler_params=pltpu.CompilerParams(
            dimension_semantics=("arbitrary", "arbitrary"), vmem_limit_bytes=VMEM_LIMIT),
        name="prompt_mixer",
    )(x, *consts)


def _mlp_body(x, gpre_ref, wup_ref, wdn_ref, gpost_ref):
    h2 = (_rms(x) * gpre_ref[...]).astype(bf16)
    dff = wup_ref.shape[1]
    acc = None
    for c in range(dff // MLP_FCHUNK):
        a = _dot(h2, wup_ref[:, c * MLP_FCHUNK:(c + 1) * MLP_FCHUNK])
        a = jnp.maximum(a, 0.0)
        part = _dot((a * a).astype(bf16), wdn_ref[c * MLP_FCHUNK:(c + 1) * MLP_FCHUNK, :])
        acc = part if acc is None else acc + part
    return x + _rms(acc) * gpost_ref[...]


def _mlp_kernel(x_ref, gpre_ref, wup_ref, wdn_ref, gpost_ref, y_ref):
    half = x_ref.shape[0] // 2
    for r0 in (0, half):
        y_ref[r0:r0 + half, :] = _mlp_body(x_ref[r0:r0 + half, :], gpre_ref, wup_ref, wdn_ref, gpost_ref)


def _mlp_call(x2d, w):
    n, dm = x2d.shape
    tile = min(MLP_TILE, n)
    assert n % tile == 0
    consts = [w[k] for k in ("g2pre", "wup", "wdn", "g2post")]
    return pl.pallas_call(
        _mlp_kernel,
        grid=(n // tile,),
        in_specs=[pl.BlockSpec((tile, dm), lambda i: (i, 0))] + [_const_spec(c.shape) for c in consts],
        out_specs=pl.BlockSpec((tile, dm), lambda i: (i, 0)),
        out_shape=jax.ShapeDtypeStruct((n, dm), f32),
        compiler_params=pltpu.CompilerParams(
            dimension_semantics=("arbitrary",), vmem_limit_bytes=VMEM_LIMIT),
        name="mlp",
    )(x2d, *consts)


def _sample_pre_kernel(x_ref, h0r_ref, h0i_ref, gpre_ref, win_ref, bin_ref, bblk_ref, cblkt_ref,
                       abr_ref, abi_ref, d_ref, wglu_ref, bglu_ref, lb_ref, hgn_ref, wbs_ref,
                       m5_ref, ghg_ref, gn_ref, qt_ref, cols_ref, v_ref, oin_ref,
                       s5r_ref, s5i_ref):
    n_seq = h0r_ref.shape[0]
    dm = gpre_ref.shape[1]
    steps = x_ref.shape[1] // dm
    sdim = h0r_ref.shape[1]
    hw = HG_HEADS * HG_DK
    s5w = d_ref.shape[1]
    x = jnp.concatenate([x_ref[:, t * dm:(t + 1) * dm] for t in range(steps)], axis=0)
    hb = (_rms(x) * gpre_ref[...]).astype(bf16)

    u = _dot(hb, win_ref[:, :s5w]) + bin_ref[:, :s5w]
    bu = _dot(u.astype(bf16), bblk_ref[...])
    hr = h0r_ref[...]
    hi = h0i_ref[...]
    a_r = abr_ref[...]
    a_i = abi_ref[...]
    hs = []
    for t in range(steps):
        sl = slice(t * n_seq, (t + 1) * n_seq)
        hr, hi = (a_r * hr - a_i * hi + bu[sl, :sdim], a_r * hi + a_i * hr + bu[sl, sdim:])
        hs.append(jnp.concatenate([hr, hi], axis=1).astype(bf16))
    s5r_ref[...] = hr
    s5i_ref[...] = hi
    y = _dot_nt(jnp.concatenate(hs, axis=0), cblkt_ref[...])
    y = _gelu_tanh(y + d_ref[...] * u)
    z = _dot(y.astype(bf16), wglu_ref[...]) + bglu_ref[...]
    ys5 = (y * _sigmoid(z)).astype(bf16)
    g0 = s5w + 4 * hw
    def put_steps(ref, val):
        wcol = val.shape[1]
        for t in range(steps):
            ref[:, t * wcol:(t + 1) * wcol] = val[t * n_seq:(t + 1) * n_seq, :]

    gates = _dot(hb, win_ref[:, g0:]) + bin_ref[:, g0:]
    put_steps(m5_ref, _sigmoid(gates[:, :dm]) * _dot(ys5, wbs_ref[...]))
    put_steps(ghg_ref, _sigmoid(gates[:, dm:]))

    def proj(i):
        c0 = s5w + i * hw
        return _dot(hb, win_ref[:, c0:c0 + hw]) + bin_ref[:, c0:c0 + hw]

    q_raw = proj(0)
    f_raw = proj(1)
    v = proj(2)
    g_raw = proj(3)
    q = q_raw * _sigmoid(q_raw)
    lb = lb_ref[...]
    f = lb + (1.0 - lb) * _sigmoid(f_raw)
    lf = jnp.log(f)
    kk = 1.0 - f
    put_steps(gn_ref, g_raw * _sigmoid(g_raw) * hgn_ref[...])
    bs = []
    acc = None
    for t in range(steps):
        sl = slice(t * n_seq, (t + 1) * n_seq)
        acc = lf[sl] if acc is None else acc + lf[sl]
        bs.append(acc)
    b_last = bs[-1]

    def put(ref, slot, val):
        for h in range(HG_HEADS):
            c0 = (h * 8 + slot) * HG_DK
            ref[:, c0:c0 + HG_DK] = val[:, h * HG_DK:(h + 1) * HG_DK]

    zero = jnp.zeros((n_seq, hw), f32)
    put(cols_ref, steps, jnp.exp(b_last))
    for slot in range(steps, 8):
        put(qt_ref, slot, zero)
        put(v_ref, slot, zero)
        if slot > steps:
            put(cols_ref, slot, zero)
    for t in range(steps):
        sl = slice(t * n_seq, (t + 1) * n_seq)
        put(qt_ref, t, q[sl] * jnp.exp(bs[t]))
        put(cols_ref, t, kk[sl] * jnp.exp(b_last - bs[t]))
        put(v_ref, t, v[sl])
        o_t = None
        for s in range(t + 1):
            sls = slice(s * n_seq, (s + 1) * n_seq)
            prod = q[sl] * kk[sls] * jnp.exp(bs[t] - bs[s])
            parts = []
            for h in range(HG_HEADS):
                hs = slice(h * HG_DK, (h + 1) * HG_DK)
                wgt = jnp.sum(prod[:, hs], axis=1, keepdims=True)
                parts.append(wgt * v[sls, hs])
            term = jnp.concatenate(parts, axis=1)
            o_t = term if o_t is None else o_t + term
        oin_ref[:, t * hw:(t + 1) * hw] = o_t


def _sample_pre_call(x_sm, h0r, h0i, w):
    n_seq, sdim = h0r.shape
    dm = w["gpre"].shape[1]
    steps = x_sm.shape[1] // dm
    hw = HG_HEADS * HG_DK
    nb = SAMPLE_PRE_BLOCK
    assert n_seq % nb == 0
    consts = [w[k] for k in ("gpre", "win", "bin", "bblk", "cblkt", "abr", "abi", "d",
                             "wglu", "bglu", "lbrow", "hgnrow", "wbs")]
    widths = (steps * dm,
              steps * dm,
              steps * hw,
              8 * hw,
              8 * hw,
              8 * hw,
              steps * hw,
              sdim, sdim)
    seq_spec = lambda width: pl.BlockSpec((nb, width), lambda i: (i, 0))
    return pl.pallas_call(
        _sample_pre_kernel,
        grid=(n_seq // nb,),
        in_specs=[seq_spec(steps * dm), seq_spec(sdim), seq_spec(sdim)] + [_const_spec(c.shape) for c in consts],
        out_specs=tuple(seq_spec(wd) for wd in widths),
        out_shape=tuple(jax.ShapeDtypeStruct((n_seq, wd), f32) for wd in widths),
        compiler_params=pltpu.CompilerParams(
            dimension_semantics=("arbitrary",), vmem_limit_bytes=VMEM_LIMIT),
        name="sample_pre",
    )(x_sm, h0r, h0i, *consts)


def _sample_state_kernel(s0_ref, qt_ref, v_ref, cols_ref, snew_ref, oint_ref):
    nb = s0_ref.shape[0]
    nrow = cols_ref.shape[1]
    hw = HG_HEADS * HG_DK
    row_head = lax.broadcasted_iota(jnp.int32, (nrow, hw), 0) // 8
    col_head = lax.broadcasted_iota(jnp.int32, (nrow, hw), 1) // HG_DK
    diag = row_head == col_head
    pad = jnp.zeros((LANES - nrow, HG_DK), f32)
    for i in range(nb):
        xt = jnp.concatenate([cols_ref[i], pad], axis=0).T
        vbd = jnp.where(diag, jnp.concatenate([v_ref[i]] * HG_HEADS, axis=1), 0.0)
        vbd = jnp.concatenate([vbd, jnp.zeros((LANES - nrow, hw), f32)], axis=0).astype(bf16)
        ds = _dot(xt.astype(bf16), vbd)
        for h in range(HG_HEADS):
            s0 = s0_ref[i, h]
            oint_ref[i, h] = _dot(qt_ref[i, h].astype(bf16), s0.astype(bf16))
            snew_ref[i, h] = xt[:, h * 8 + 4:h * 8 + 5] * s0 + ds[:, h * HG_DK:(h + 1) * HG_DK]


def _sample_state_call(s0, qt8, v8, cols):
    n_seq = s0.shape[0]
    nb = SAMPLE_SEQ_BLOCK
    blk4 = (nb, HG_HEADS, 8, HG_DK)
    return pl.pallas_call(
        _sample_state_kernel,
        grid=(n_seq // nb,),
        in_specs=[pl.BlockSpec((nb, HG_HEADS, HG_DK, HG_DK), lambda i: (i, 0, 0, 0)),
                  pl.BlockSpec(blk4, lambda i: (i, 0, 0, 0)),
                  pl.BlockSpec((nb, HG_HEADS * 8, HG_DK), lambda i: (i, 0, 0)),
                  pl.BlockSpec((nb, HG_HEADS * 8, HG_DK), lambda i: (i, 0, 0))],
        out_specs=(pl.BlockSpec((nb, HG_HEADS, HG_DK, HG_DK), lambda i: (i, 0, 0, 0)),
                   pl.BlockSpec(blk4, lambda i: (i, 0, 0, 0))),
        out_shape=(jax.ShapeDtypeStruct(s0.shape, f32),
                   jax.ShapeDtypeStruct((n_seq, HG_HEADS, 8, HG_DK), f32)),
        compiler_params=pltpu.CompilerParams(
            dimension_semantics=("arbitrary",), vmem_limit_bytes=VMEM_LIMIT),
        name="sample_state",
    )(s0, qt8.reshape((n_seq,) + blk4[1:]), v8.reshape(n_seq, HG_HEADS * 8, HG_DK),
      cols.reshape(n_seq, HG_HEADS * 8, HG_DK))


def _sample_post_kernel(x_ref, oint_ref, oin_ref, gn_ref, m5_ref, ghg_ref, wbh_ref, wout_ref, gpost_ref,
                        g2pre_ref, wup_ref, wdn_ref, g2post_ref, y_ref):
    n_seq = x_ref.shape[0]
    dm = gpost_ref.shape[1]
    steps = x_ref.shape[1] // dm
    hw = HG_HEADS * HG_DK

    def steps_major(ref, wcol):
        return jnp.concatenate([ref[:, t * wcol:(t + 1) * wcol] for t in range(steps)], axis=0)

    x = steps_major(x_ref, dm)
    oint = jnp.concatenate(
        [jnp.concatenate([oint_ref[:, (h * 8 + t) * HG_DK:(h * 8 + t + 1) * HG_DK] for h in range(HG_HEADS)],
                         axis=1) for t in range(steps)], axis=0)
    o = oint + steps_major(oin_ref, hw)
    parts = []
    for h in range(HG_HEADS):
        hs = slice(h * HG_DK, (h + 1) * HG_DK)
        parts.append(_rms(o[:, hs]))
    yhg = (jnp.concatenate(parts, axis=1) * steps_major(gn_ref, hw)).astype(bf16)
    merged = steps_major(m5_ref, dm) + steps_major(ghg_ref, dm) * _dot(yhg, wbh_ref[...])
    mo = _dot(merged.astype(bf16), wout_ref[...])
    x1 = x + _rms(mo) * gpost_ref[...]
    y = _mlp_body(x1, g2pre_ref, wup_ref, wdn_ref, g2post_ref)
    for t in range(steps):
        y_ref[:, t * dm:(t + 1) * dm] = y[t * n_seq:(t + 1) * n_seq, :]


def _sample_post_call(x_sm, oint, oin, gn, m5, ghg, w):
    ins = [x_sm, oint, oin, gn, m5, ghg] + [w[k] for k in ("wbh", "wout", "gpost", "g2pre", "wup",
                                                           "wdn", "g2post")]
    return pl.pallas_call(
        _sample_post_kernel,
        out_shape=jax.ShapeDtypeStruct(x_sm.shape, f32),
        compiler_params=pltpu.CompilerParams(vmem_limit_bytes=VMEM_LIMIT),
        name="sample_post",
    )(*ins)


def _prep_weights(l, norm_mix_pre, norm_mix_post, norm_mlp_pre, norm_mlp_post, w_in, b_in,
                  s5_a_re, s5_a_im, s5_log_dt, s5_b_re, s5_b_im, s5_c_re, s5_c_im, s5_d, s5_w_glu,
                  s5_b_glu, hg_lb_logits, hg_norm, w_br_s5, w_br_hg, w_out, w_up, w_down, n_pow):
    hp = lax.Precision.HIGHEST
    dm = w_in.shape[1]
    s5w = s5_d.shape[1]
    kw = HG_HEADS * HG_DK
    n_groups = s5w // S5_GROUP
    n_pairs = n_groups // 2
    w = {}
    row = lambda a: a.astype(f32).reshape(1, -1)
    w["gpre"], w["gpost"] = row(norm_mix_pre[l]), row(norm_mix_post[l])
    w["g2pre"], w["g2post"] = row(norm_mlp_pre[l]), row(norm_mlp_post[l])
    w["win"], w["bin"] = w_in[l].astype(bf16), row(b_in[l])
    lb_all = jnp.cumsum(jax.nn.softmax(hg_lb_logits.astype(f32), axis=0), axis=0)
    w["lbrow"] = row(lb_all[l])
    w["hgnrow"] = row(jnp.tile(hg_norm[l].astype(f32), HG_HEADS))
    w["d"] = row(s5_d[l])
    w["wglu"], w["bglu"] = s5_w_glu[l].astype(bf16), row(s5_b_glu[l])
    w["wbs"], w["wbh"], w["wout"] = w_br_s5[l].astype(bf16), w_br_hg[l].astype(bf16), w_out[l].astype(bf16)
    w["wup"], w["wdn"] = w_up[l].astype(bf16), w_down[l].astype(bf16)

    a_re, a_im = s5_a_re[l].astype(f32), s5_a_im[l].astype(f32)
    dt = jnp.exp(s5_log_dt[l].astype(f32))[:, None]
    mag = jnp.exp(dt * a_re)
    abr, abi = mag * jnp.cos(dt * a_im), mag * jnp.sin(dt * a_im)
    den = a_re * a_re + a_im * a_im
    nr, ni = abr - 1.0, abi
    cfr, cfi = (nr * a_re + ni * a_im) / den, (ni * a_re - nr * a_im) / den
    b_re, b_im = s5_b_re[l].astype(f32), s5_b_im[l].astype(f32)
    bbr = cfr[..., None] * b_re - cfi[..., None] * b_im
    bbi = cfr[..., None] * b_im + cfi[..., None] * b_re
    c_re, c_im = s5_c_re[l].astype(f32), s5_c_im[l].astype(f32)
    w["abr"], w["abi"] = abr.reshape(1, -1), abi.reshape(1, -1)

    lam_r, lam_i = dt * a_re, dt * a_im
    grp = S5_GROUP
    n_state = a_re.shape[1]
    jj = jnp.arange(S5_CHUNK + 1, dtype=f32)[None, :, None]
    pmag = jnp.exp(jj * lam_r[:, None, :])
    pw_r, pw_i = pmag * jnp.cos(jj * lam_i[:, None, :]), pmag * jnp.sin(jj * lam_i[:, None, :])
    bt = jnp.transpose(jnp.stack([bbr, bbi]), (0, 1, 3, 2))
    bt_r, bt_i = bt[0], bt[1]
    c4r, c4i = c_re[:, None], c_im[:, None]

    def cmul(pr_, pi_, xr, xi):
        pr_, pi_ = pr_[:, :, None, :], pi_[:, :, None, :]
        return pr_ * xr - pi_ * xi, pr_ * xi + pi_ * xr

    rows128 = lambda t: t.reshape(n_groups, LANES, t.shape[-1])
    car_t, cai_t = cmul(pw_r[:, 1:], pw_i[:, 1:], c4r, c4i)
    ca = jnp.transpose(jnp.stack([rows128(car_t), -rows128(cai_t)]), (0, 1, 3, 2))
    ber, bei = cmul(pw_r[:, S5_CHUNK - 1::-1], pw_i[:, S5_CHUNK - 1::-1], bt_r[:, None], bt_i[:, None])
    ber, bei = rows128(ber), rows128(bei)
    zpad = jnp.zeros((n_groups, S5_CHUNK - 1, n_state), f32)
    lag_r = jnp.concatenate([zpad, pw_r[:, :S5_CHUNK]], axis=1)
    lag_i = jnp.concatenate([zpad, pw_i[:, :S5_CHUNK]], axis=1)
    cp_r, cp_i = cmul(lag_r, lag_i, c4r, c4i)
    n_blk = 2 * S5_CHUNK - 1
    cp_all = jnp.concatenate([cp_r, -cp_i], axis=3).reshape(n_groups, n_blk * grp, 2 * n_state)
    rpad_t = jnp.einsum("gmn,gnc->gmc", cp_all, jnp.concatenate([bbr, bbi], axis=1), precision=hp)
    toe_t = jnp.concatenate([rpad_t[:, (S5_CHUNK - 1 - s) * grp:(S5_CHUNK - 1 - s) * grp + LANES, :]
                             for s in range(S5_CHUNK)], axis=2)
    toe = jnp.transpose(toe_t, (0, 2, 1))

    def pair_diag(m):
        g, r, c = m.shape
        m = m.reshape(g // 2, 2, r, c)
        z = jnp.zeros_like(m[:, 0])
        return jnp.concatenate([jnp.concatenate([m[:, 0], z], axis=2),
                                jnp.concatenate([z, m[:, 1]], axis=2)], axis=1)

    w["w1"] = jnp.concatenate([pair_diag(toe), pair_diag(ber), pair_diag(bei)], axis=2).astype(bf16)
    w["w2"] = jnp.concatenate([pair_diag(ca[0]), pair_diag(ca[1])], axis=1).astype(bf16)
    qr, qi = pw_r[:, S5_CHUNK], pw_i[:, S5_CHUNK]
    aps_r, aps_i = [], []
    for _ in range(n_pow):
        aps_r.append(qr.reshape(-1))
        aps_i.append(qi.reshape(-1))
        qr, qi = qr * qr - qi * qi, 2.0 * qr * qi
    w["apr"], w["api"] = jnp.stack(aps_r), jnp.stack(aps_i)

    same_group = np.arange(s5w)[:, None] // grp == np.arange(n_groups * n_state)[None, :] // n_state
    lane_tile = np.tile(np.eye(n_state, dtype=np.float32), (1, n_groups))
    src = jnp.stack([bt_r, bt_i, c_re, -c_im]).reshape(4, s5w, n_state)
    tiled = jnp.einsum("zxn,nl->zxl", src, jnp.asarray(lane_tile), precision=hp) * jnp.asarray(same_group, f32)
    w["bblk"] = jnp.concatenate([tiled[0], tiled[1]], axis=1).astype(bf16)
    w["cblkt"] = jnp.concatenate([tiled[2], tiled[3]], axis=1).astype(bf16)
    return w


def kernel(x_prompt, x_sample, state_s5_re, state_s5_im, state_hg, norm_mix_pre, norm_mix_post,
           norm_mlp_pre, norm_mlp_post, w_in, b_in, s5_a_re, s5_a_im, s5_log_dt, s5_b_re, s5_b_im,
           s5_c_re, s5_c_im, s5_d, s5_w_glu, s5_b_glu, hg_lb_logits, hg_norm, w_br_s5, w_br_hg,
           w_out, w_up, w_down):
    depth = w_in.shape[0]
    bsz, seq, dm = x_prompt.shape
    n_seq, steps, _ = x_sample.shape
    n_groups, n_state = s5_a_re.shape[1], s5_a_re.shape[2]
    hw = HG_HEADS * HG_DK
    assert depth == 1 and steps == 4 and n_state == S5_STATE
    n_pow = max(1, int(math.log2(min(PROMPT_TILE, seq) // S5_CHUNK)))
    xp, xs = x_prompt, x_sample
    outs = [[] for _ in range(6)]
    for l in range(depth):
        w = _prep_weights(l, norm_mix_pre, norm_mix_post, norm_mlp_pre, norm_mlp_post, w_in, b_in,
                          s5_a_re, s5_a_im, s5_log_dt, s5_b_re, s5_b_im, s5_c_re, s5_c_im, s5_d,
                          s5_w_glu, s5_b_glu, hg_lb_logits, hg_norm, w_br_s5, w_br_hg, w_out, w_up,
                          w_down, n_pow)
        x1, p_re, p_im, p_hg = _mixer_call(xp, w)
        xp = _mlp_call(x1.reshape(bsz * seq, dm), w).reshape(bsz, seq, dm)
        outs[0].append(p_re.reshape(bsz, n_groups, n_state))
        outs[1].append(p_im.reshape(bsz, n_groups, n_state))
        outs[2].append(p_hg)
        x_sm = xs.reshape(n_seq, steps * dm)
        h0r = state_s5_re[l].reshape(n_seq, n_groups * n_state)
        h0i = state_s5_im[l].reshape(n_seq, n_groups * n_state)
        m5, ghg, gn, qt8, cols, v8, oin, s_re, s_im = _sample_pre_call(x_sm, h0r, h0i, w)
        s_hg, oint = _sample_state_call(state_hg[l], qt8, v8, cols)
        y_sm = _sample_post_call(x_sm, oint.reshape(n_seq, 8 * hw), oin, gn, m5, ghg, w)
        xs = y_sm.reshape(n_seq, steps, dm)
        outs[3].append(s_re.reshape(n_seq, n_groups, n_state))
        outs[4].append(s_im.reshape(n_seq, n_groups, n_state))
        outs[5].append(s_hg)
    return (xp, xs) + tuple(o[0][None] for o in outs)
```

```python
import functools
import math

import jax
import jax.numpy as jnp
import numpy as np
from jax import lax
from jax.experimental import pallas as pl
from jax.experimental.pallas import tpu as pltpu

f32 = jnp.float32
bf16 = jnp.bfloat16

NORM_EPS = 1e-6
S5_GROUP = 16
S5_STATE = 64
S5_CHUNK = 8
HG_HEADS = 8
HG_DK = 128
HG_CHUNK = 64
LANES = 128
VMEM_LIMIT = 56 * 1024 * 1024
PROMPT_TILE = 512
MLP_TILE = 1024
MLP_FCHUNK = 512
SAMPLE_SEQ_BLOCK = 8
SAMPLE_PRE_BLOCK = 32


def _sigmoid(x):
    return 0.5 * (jnp.tanh(0.5 * x) + 1.0)


def _gelu_tanh(x):
    c = math.sqrt(2.0 / math.pi)
    return 0.5 * x * (1.0 + jnp.tanh(c * (x + 0.044715 * (x * x * x))))


def _rms(x):
    return x * lax.rsqrt(jnp.mean(x * x, axis=-1, keepdims=True) + NORM_EPS)


def _dot(a, b):
    return jnp.dot(a, b, preferred_element_type=f32)


def _dot_nt(a, b):
    return lax.dot_general(a, b, (((1,), (1,)), ((), ())), preferred_element_type=f32)


def _dot_tn(a, b):
    return lax.dot_general(a, b, (((0,), (0,)), ((), ())), preferred_element_type=f32)


def _block_transpose8(xs, blk):
    xs = list(xs)
    for d in (4, 2, 1):
        keep = (blk & d) == 0
        for i in range(8):
            if i & d == 0:
                a, b = xs[i], xs[i + d]
                xs[i] = jnp.where(keep, a, pltpu.roll(b, d * S5_GROUP, 1))
                xs[i + d] = jnp.where(keep, pltpu.roll(a, LANES - d * S5_GROUP, 1), b)
    return xs


def _mixer_kernel(x_ref, gpre_ref, win_ref, bin_ref, w1_ref, w2_ref, apr_ref, api_ref, d_ref,
                  wglu_ref, bglu_ref, lb_ref, hgn_ref, wbs_ref, wbh_ref, wout_ref, gpost_ref,
                  y_ref, s5r_ref, s5i_ref, hgs_ref,
                  hb_scr, u_scr, ys_scr, o_scr, yloc_scr, er_scr, ei_scr, cr_scr, ci_scr,
                  st_scr, qfig_scr, yhg_scr, m_scr, fac_scr, el_scr, ghg_scr):
    t_idx = pl.program_id(1)
    n_t = pl.num_programs(1)
    tile = x_ref.shape[1]
    rows = tile // S5_CHUNK
    n_pairs = w1_ref.shape[0]
    n_cols = u_scr.shape[0]
    hw = HG_HEADS * HG_DK
    s5w = n_cols * LANES
    g0 = s5w + 4 * hw

    @pl.when(t_idx == 0)
    def _():
        cr_scr[...] = jnp.zeros_like(cr_scr)
        ci_scr[...] = jnp.zeros_like(ci_scr)
        st_scr[...] = jnp.zeros_like(st_scr)

    x = x_ref[0]
    hb_scr[...] = (_rms(x) * gpre_ref[...]).astype(bf16)
    dm = wbs_ref.shape[1]
    half = tile // 2

    def project(r0, r1, c0, c1):
        qfig_scr[r0:r1, c0:c1] = (_dot(hb_scr[r0:r1, :], win_ref[:, s5w + c0:s5w + c1])
                                  + bin_ref[:, s5w + c0:s5w + c1])

    def gate_hg(j):
        c0, c1 = j * (dm // 4), (j + 1) * (dm // 4)
        ghg_scr[:, c0:c1] = _sigmoid(_dot(hb_scr[...], win_ref[:, g0 + dm + c0:g0 + dm + c1])
                                     + bin_ref[:, g0 + dm + c0:g0 + dm + c1])


    u = _dot(hb_scr[...], win_ref[:, :s5w]) + bin_ref[:, :s5w]
    for j in range(n_cols):
        u_scr[j] = u[:, j * LANES:(j + 1) * LANES]

    project(0, half, 0, hw)
    gate_hg(0)
    blk = lax.broadcasted_iota(jnp.int32, (rows, LANES), 1) // S5_GROUP
    for j in range(n_cols):
        xs = [u_scr[j, pl.ds(s, rows, stride=S5_CHUNK), :] for s in range(S5_CHUNK)]
        ys = _block_transpose8(xs, blk)
        for q in range(4):
            o_scr[j * 4 + q] = jnp.concatenate([ys[2 * q], ys[2 * q + 1]], axis=1).astype(bf16)

    project(0, half, hw, 2 * hw)
    gate_hg(1)
    for p in range(n_pairs):
        r1 = _dot(o_scr[p], w1_ref[p])
        yloc_scr[p] = r1[:, :2 * LANES]
        er_scr[:, p * LANES:(p + 1) * LANES] = r1[:, 2 * LANES:3 * LANES]
        ei_scr[:, p * LANES:(p + 1) * LANES] = r1[:, 3 * LANES:]

    er = er_scr[...]
    ei = ei_scr[...]
    row = lax.broadcasted_iota(jnp.int32, (rows, 1), 0)
    first = row == 0
    cr = cr_scr[...]
    ci = ci_scr[...]
    a_r = apr_ref[0:1, :]
    a_i = api_ref[0:1, :]
    project(0, half, 2 * hw, 3 * hw)
    gate_hg(2)
    er = er + jnp.where(first, a_r * cr - a_i * ci, 0.0)
    ei = ei + jnp.where(first, a_r * ci + a_i * cr, 0.0)
    for k in range(int(math.log2(rows))):
        d = 1 << k
        p_r = apr_ref[k:k + 1, :]
        p_i = api_ref[k:k + 1, :]
        s_r = pltpu.roll(er, d, 0)
        s_i = pltpu.roll(ei, d, 0)
        valid = row >= d
        er, ei = (er + jnp.where(valid, p_r * s_r - p_i * s_i, 0.0),
                  ei + jnp.where(valid, p_r * s_i + p_i * s_r, 0.0))
    hs_r = jnp.where(first, cr, pltpu.roll(er, 1, 0))
    hs_i = jnp.where(first, ci, pltpu.roll(ei, 1, 0))
    cr_scr[...] = er[rows - 1:rows, :]
    ci_scr[...] = ei[rows - 1:rows, :]

    project(0, half, 3 * hw, 4 * hw)
    gate_hg(3)
    for j in range(n_cols):
        halves = []
        for q in range(4):
            p = j * 4 + q
            hp = jnp.concatenate([hs_r[:, p * LANES:(p + 1) * LANES],
                                  hs_i[:, p * LANES:(p + 1) * LANES]], axis=1).astype(bf16)
            yp = yloc_scr[p] + _dot(hp, w2_ref[p])
            halves += [yp[:, :LANES], yp[:, LANES:]]
        zs = _block_transpose8(halves, blk)
        for s in range(S5_CHUNK):
            ys_scr[j, pl.ds(s, rows, stride=S5_CHUNK), :] = zs[s]

    y = jnp.concatenate([ys_scr[j] for j in range(n_cols)], axis=1)
    u = jnp.concatenate([u_scr[j] for j in range(n_cols)], axis=1)
    y = _gelu_tanh(y + d_ref[...] * u)
    z = _dot(y.astype(bf16), wglu_ref[...]) + bglu_ref[...]
    ys5 = (y * _sigmoid(z)).astype(bf16)
    gate_s5 = _sigmoid(_dot(hb_scr[...], win_ref[:, g0:g0 + dm]) + bin_ref[:, g0:g0 + dm])
    m_scr[...] = gate_s5 * _dot(ys5, wbs_ref[...])

    ch = HG_CHUNK
    ri = lax.broadcasted_iota(jnp.int32, (ch, ch), 0)
    ci_ = lax.broadcasted_iota(jnp.int32, (ch, ch), 1)
    causal = ci_ <= ri
    tril = causal.astype(bf16)
    lb = lb_ref[...]
    hgn = hgn_ref[...]
    n_chunks = tile // ch
    n_side = n_chunks // 2
    assert n_side == 4 and dm % n_side == 0

    def merge_hg(r0, r1, c0, c1):
        m_scr[r0:r1, c0:c1] = (m_scr[r0:r1, c0:c1]
                               + ghg_scr[r0:r1, c0:c1] * _dot(yhg_scr[r0:r1, :], wbh_ref[:, c0:c1]))

    def chunk_factors(c):
        r0 = c * ch
        slot = c % 2
        q_raw = qfig_scr[pl.ds(r0, ch), 0:hw]
        f_raw = qfig_scr[pl.ds(r0, ch), hw:2 * hw]
        q = q_raw * _sigmoid(q_raw)
        f = lb + (1.0 - lb) * _sigmoid(f_raw)
        lf = jnp.log(f)
        kk = 1.0 - f
        hi = lf.astype(bf16)
        lo = (lf - hi.astype(f32)).astype(bf16)
        cs = _dot(tril, jnp.concatenate([hi, lo], axis=1))
        b = cs[:, :hw] + cs[:, hw:]
        mid = b[ch // 2 - 1:ch // 2, :]
        b_last = b[ch - 1:ch, :]
        qh = q * jnp.exp(b - mid)
        kh = kk * jnp.exp(mid - b)
        fac_scr[slot, 0] = qh.astype(bf16)
        fac_scr[slot, 1] = kh.astype(bf16)
        fac_scr[slot, 2] = (qh * jnp.exp(mid)).astype(bf16)
        fac_scr[slot, 3] = (kh * jnp.exp(b_last - mid)).astype(bf16)
        fac_scr[slot, 4] = qfig_scr[pl.ds(r0, ch), 2 * hw:3 * hw].astype(bf16)
        el_scr[slot] = jnp.exp(b_last)

    def chunk_matmuls(c):
        r0 = c * ch
        slot = c % 2
        heads = [slice(h * HG_DK, (h + 1) * HG_DK) for h in range(HG_HEADS)]
        scs = [jnp.where(causal, _dot_nt(fac_scr[slot, 0, :, hs], fac_scr[slot, 1, :, hs]), 0.0).astype(bf16)
               for hs in heads]
        e_last = el_scr[slot]
        outs = []
        for h, hs in enumerate(heads):
            st = st_scr[h]
            vb = fac_scr[slot, 4, :, hs]
            o = _dot(scs[h], vb) + _dot_nt(fac_scr[slot, 2, :, hs], st.astype(bf16))
            st_scr[h] = st * e_last[:, hs] + _dot_tn(vb, fac_scr[slot, 3, :, hs])
            outs.append(_rms(o))
        g_raw = qfig_scr[pl.ds(r0, ch), 3 * hw:4 * hw]
        on = jnp.concatenate(outs, axis=1) * hgn * (g_raw * _sigmoid(g_raw))
        yhg_scr[pl.ds(r0, ch), :] = on.astype(bf16)

    chunk_factors(0)
    for c in range(n_chunks):
        if c + 1 < n_chunks:
            chunk_factors(c + 1)
        chunk_matmuls(c)
        if c < n_side:
            project(half, tile, c * hw, (c + 1) * hw)
        else:
            j = c - n_side
            merge_hg(0, half, j * (dm // n_side), (j + 1) * (dm // n_side))
    for r0 in (0, half):
        if r0:
            merge_hg(half, tile, 0, dm)
        mo = _dot(m_scr[r0:r0 + half, :].astype(bf16), wout_ref[...])
        y_ref[0, r0:r0 + half, :] = x_ref[0, r0:r0 + half, :] + _rms(mo) * gpost_ref[...]

    @pl.when(t_idx == n_t - 1)
    def _():
        s5r_ref[0] = cr_scr[...]
        s5i_ref[0] = ci_scr[...]
        for h in range(HG_HEADS):
            hgs_ref[0, h] = st_scr[h].T


def _const_spec(shape):
    nd = len(shape)
    return pl.BlockSpec(shape, lambda *_: (0,) * nd, pipeline_mode=pl.Buffered(1))


def _mixer_call(x, w):
    bsz, seq, dm = x.shape
    tile = min(PROMPT_TILE, seq)
    assert seq % tile == 0 and tile % HG_CHUNK == 0
    rows = tile // S5_CHUNK
    assert rows & (rows - 1) == 0 and w["apr"].shape[0] >= int(math.log2(rows))
    s5w = w["d"].shape[1]
    n_cols = s5w // LANES
    n_pairs = w["w1"].shape[0]
    sdim = w["apr"].shape[1]
    consts = [w[k] for k in ("gpre", "win", "bin", "w1", "w2", "apr", "api", "d", "wglu", "bglu",
                             "lbrow", "hgnrow", "wbs", "wbh", "wout", "gpost")]
    in_specs = [pl.BlockSpec((1, tile, dm), lambda b, t: (b, t, 0))]
    in_specs += [_const_spec(c.shape) for c in consts]
    out_shape = (jax.ShapeDtypeStruct((bsz, seq, dm), f32),
                 jax.ShapeDtypeStruct((bsz, 1, sdim), f32),
                 jax.ShapeDtypeStruct((bsz, 1, sdim), f32),
                 jax.ShapeDtypeStruct((bsz, HG_HEADS, HG_DK, HG_DK), f32))
    out_specs = (pl.BlockSpec((1, tile, dm), lambda b, t: (b, t, 0)),
                 pl.BlockSpec((1, 1, sdim), lambda b, t: (b, 0, 0)),
                 pl.BlockSpec((1, 1, sdim), lambda b, t: (b, 0, 0)),
                 pl.BlockSpec((1, HG_HEADS, HG_DK, HG_DK), lambda b, t: (b, 0, 0, 0)))
    scratch = [
        pltpu.VMEM((tile, dm), bf16),
        pltpu.VMEM((n_cols, tile, LANES), f32),
        pltpu.VMEM((n_cols, tile, LANES), f32),
        pltpu.VMEM((n_pairs, rows, 2 * LANES), bf16),
        pltpu.VMEM((n_pairs, rows, 2 * LANES), f32),
        pltpu.VMEM((rows, sdim), f32),
        pltpu.VMEM((rows, sdim), f32),
        pltpu.VMEM((1, sdim), f32),
        pltpu.VMEM((1, sdim), f32),
        pltpu.VMEM((HG_HEADS, HG_DK, HG_DK), f32),
        pltpu.VMEM((tile, HG_HEADS * 4 * HG_DK), f32),
        pltpu.VMEM((tile, HG_HEADS * HG_DK), bf16),
        pltpu.VMEM((tile, dm), f32),
        pltpu.VMEM((2, 5, HG_CHUNK, HG_HEADS * HG_DK), bf16),
        pltpu.VMEM((2, 1, HG_HEADS * HG_DK), f32),
        pltpu.VMEM((tile, dm), f32),
    ]
    return pl.pallas_call(
        _mixer_kernel,
        grid=(bsz, seq // tile),
        in_specs=in_specs,
        out_specs=out_specs,
        out_shape=out_shape,
        scratch_shapes=scratch,
        compiler_params=pltpu.CompilerParams(
            dimension_semantics=("arbitrary", "arbitrary"), vmem_limit_bytes=VMEM_LIMIT),
        name="prompt_mixer",
    )(x, *consts)


def _mlp_body(x, gpre_ref, wup_scr, wdn_scr, gpost_ref):
    h2 = (_rms(x) * gpre_ref[...]).astype(bf16)
    acc = None
    for c in range(wup_scr.shape[0]):
        a = _dot(h2, wup_scr[c])
        a = jnp.maximum(a, 0.0)
        part = _dot((a * a).astype(bf16), wdn_scr[c])
        acc = part if acc is None else acc + part
    return x + _rms(acc) * gpost_ref[...]


def _mlp_kernel(xp_ref, xs_ref, gpre_ref, wup_ref, wdn_ref, gpost_ref, yp_ref, ys_ref, wup_scr, wdn_scr):
    step = pl.program_id(0)
    n_cast = wup_scr.shape[0]
    last = pl.num_programs(0) - 1
    dm = gpre_ref.shape[1]

    @pl.when(step < n_cast)
    def _():
        wup_scr[step] = wup_ref[...].astype(bf16)
        wdn_scr[step] = wdn_ref[...].astype(bf16)

    @pl.when(jnp.logical_and(step >= n_cast, step < last))
    def _():
        half = xp_ref.shape[0] // 2
        for r0 in (0, half):
            yp_ref[r0:r0 + half, :] = _mlp_body(xp_ref[r0:r0 + half, :], gpre_ref, wup_scr, wdn_scr, gpost_ref)

    @pl.when(step == last)
    def _():
        n_seq = xs_ref.shape[0]
        steps = xs_ref.shape[1] // dm
        x = jnp.concatenate([xs_ref[:, t * dm:(t + 1) * dm] for t in range(steps)], axis=0)
        y = _mlp_body(x, gpre_ref, wup_scr, wdn_scr, gpost_ref)
        for t in range(steps):
            ys_ref[:, t * dm:(t + 1) * dm] = y[t * n_seq:(t + 1) * n_seq, :]


def _mlp_call(xp2d, xs_sm, w):
    n, dm = xp2d.shape
    tile = min(MLP_TILE, n)
    assert n % tile == 0
    n_tiles = n // tile
    wup, wdn = w["wup_f32"], w["wdn_f32"]
    dff = wup.shape[1]
    fc = MLP_FCHUNK
    assert dff % fc == 0
    n_cast = dff // fc
    blk = lambda s: jnp.minimum(s, n_cast - 1)
    row = lambda s: jnp.clip(s - n_cast, 0, n_tiles - 1)
    return pl.pallas_call(
        _mlp_kernel,
        grid=(n_cast + n_tiles + 1,),
        in_specs=[pl.BlockSpec((tile, dm), lambda s: (row(s), 0)),
                  _const_spec(xs_sm.shape),
                  _const_spec(w["g2pre"].shape),
                  pl.BlockSpec((dm, fc), lambda s: (0, blk(s))),
                  pl.BlockSpec((fc, dm), lambda s: (blk(s), 0)),
                  _const_spec(w["g2post"].shape)],
        out_specs=(pl.BlockSpec((tile, dm), lambda s: (row(s), 0)),
                   pl.BlockSpec(xs_sm.shape, lambda s: (0, 0))),
        out_shape=(jax.ShapeDtypeStruct((n, dm), f32), jax.ShapeDtypeStruct(xs_sm.shape, f32)),
        scratch_shapes=[pltpu.VMEM((n_cast, dm, fc), bf16), pltpu.VMEM((n_cast, fc, dm), bf16)],
        compiler_params=pltpu.CompilerParams(
            dimension_semantics=("arbitrary",), vmem_limit_bytes=VMEM_LIMIT),
        name="mlp",
    )(xp2d, xs_sm, w["g2pre"], wup, wdn, w["g2post"])


def _sample_pre_kernel(x_ref, h0r_ref, h0i_ref, gpre_ref, win_ref, bin_ref, bblk_ref, cblkt_ref,
                       abr_ref, abi_ref, d_ref, wglu_ref, bglu_ref, lb_ref, hgn_ref, wbs_ref,
                       m5_ref, ghg_ref, gn_ref, qt_ref, cols_ref, v_ref, oin_ref,
                       s5r_ref, s5i_ref):
    n_seq = h0r_ref.shape[0]
    dm = gpre_ref.shape[1]
    steps = x_ref.shape[1] // dm
    sdim = h0r_ref.shape[1]
    hw = HG_HEADS * HG_DK
    s5w = d_ref.shape[1]
    x = jnp.concatenate([x_ref[:, t * dm:(t + 1) * dm] for t in range(steps)], axis=0)
    hb = (_rms(x) * gpre_ref[...]).astype(bf16)

    u = _dot(hb, win_ref[:, :s5w]) + bin_ref[:, :s5w]
    bu = _dot(u.astype(bf16), bblk_ref[...])
    hr = h0r_ref[...]
    hi = h0i_ref[...]
    a_r = abr_ref[...]
    a_i = abi_ref[...]
    hs = []
    for t in range(steps):
        sl = slice(t * n_seq, (t + 1) * n_seq)
        hr, hi = (a_r * hr - a_i * hi + bu[sl, :sdim], a_r * hi + a_i * hr + bu[sl, sdim:])
        hs.append(jnp.concatenate([hr, hi], axis=1).astype(bf16))
    s5r_ref[...] = hr
    s5i_ref[...] = hi
    y = _dot_nt(jnp.concatenate(hs, axis=0), cblkt_ref[...])
    y = _gelu_tanh(y + d_ref[...] * u)
    z = _dot(y.astype(bf16), wglu_ref[...]) + bglu_ref[...]
    ys5 = (y * _sigmoid(z)).astype(bf16)
    g0 = s5w + 4 * hw
    def put_steps(ref, val):
        wcol = val.shape[1]
        for t in range(steps):
            ref[:, t * wcol:(t + 1) * wcol] = val[t * n_seq:(t + 1) * n_seq, :]

    gates = _dot(hb, win_ref[:, g0:]) + bin_ref[:, g0:]
    put_steps(m5_ref, _sigmoid(gates[:, :dm]) * _dot(ys5, wbs_ref[...]))
    put_steps(ghg_ref, _sigmoid(gates[:, dm:]))

    def proj(i):
        c0 = s5w + i * hw
        return _dot(hb, win_ref[:, c0:c0 + hw]) + bin_ref[:, c0:c0 + hw]

    q_raw = proj(0)
    f_raw = proj(1)
    v = proj(2)
    g_raw = proj(3)
    q = q_raw * _sigmoid(q_raw)
    lb = lb_ref[...]
    f = lb + (1.0 - lb) * _sigmoid(f_raw)
    lf = jnp.log(f)
    kk = 1.0 - f
    put_steps(gn_ref, g_raw * _sigmoid(g_raw) * hgn_ref[...])
    bs = []
    acc = None
    for t in range(steps):
        sl = slice(t * n_seq, (t + 1) * n_seq)
        acc = lf[sl] if acc is None else acc + lf[sl]
        bs.append(acc)
    b_last = bs[-1]

    def put(ref, slot, val):
        for h in range(HG_HEADS):
            ref[pl.ds(h * 8 + slot, n_seq, stride=HG_HEADS * 8), :] = val[:, h * HG_DK:(h + 1) * HG_DK]

    for ref in (qt_ref, cols_ref, v_ref):
        ref[...] = jnp.zeros_like(ref)
    put(cols_ref, steps, jnp.exp(b_last))
    for t in range(steps):
        sl = slice(t * n_seq, (t + 1) * n_seq)
        put(qt_ref, t, q[sl] * jnp.exp(bs[t]))
        put(cols_ref, t, kk[sl] * jnp.exp(b_last - bs[t]))
        put(v_ref, t, v[sl])
        o_t = None
        for s in range(t + 1):
            sls = slice(s * n_seq, (s + 1) * n_seq)
            prod = q[sl] * kk[sls] * jnp.exp(bs[t] - bs[s])
            parts = []
            for h in range(HG_HEADS):
                hs = slice(h * HG_DK, (h + 1) * HG_DK)
                wgt = jnp.sum(prod[:, hs], axis=1, keepdims=True)
                parts.append(wgt * v[sls, hs])
            term = jnp.concatenate(parts, axis=1)
            o_t = term if o_t is None else o_t + term
        oin_ref[:, t * hw:(t + 1) * hw] = o_t


def _sample_pre_call(x_sm, h0r, h0i, w):
    n_seq, sdim = h0r.shape
    dm = w["gpre"].shape[1]
    steps = x_sm.shape[1] // dm
    hw = HG_HEADS * HG_DK
    nb = SAMPLE_PRE_BLOCK
    assert n_seq % nb == 0
    consts = [w[k] for k in ("gpre", "win", "bin", "bblk", "cblkt", "abr", "abi", "d",
                             "wglu", "bglu", "lbrow", "hgnrow", "wbs")]
    slots = HG_HEADS * 8
    shapes = ((1, steps * dm),
              (1, steps * dm),
              (1, steps * hw),
              (slots, HG_DK),
              (slots, HG_DK),
              (slots, HG_DK),
              (1, steps * hw),
              (1, sdim), (1, sdim))
    seq_spec = lambda shp: pl.BlockSpec((nb * shp[0], shp[1]), lambda i: (i, 0))
    return pl.pallas_call(
        _sample_pre_kernel,
        grid=(n_seq // nb,),
        in_specs=([seq_spec((1, steps * dm)), seq_spec((1, sdim)), seq_spec((1, sdim))]
                  + [_const_spec(c.shape) for c in consts]),
        out_specs=tuple(seq_spec(shp) for shp in shapes),
        out_shape=tuple(jax.ShapeDtypeStruct((n_seq * shp[0], shp[1]), f32) for shp in shapes),
        compiler_params=pltpu.CompilerParams(
            dimension_semantics=("arbitrary",), vmem_limit_bytes=VMEM_LIMIT),
        name="sample_pre",
    )(x_sm, h0r, h0i, *consts)


def _sample_state_kernel(s0_ref, qt_ref, v_ref, cols_ref, snew_ref, oint_ref):
    nb = s0_ref.shape[0]
    nrow = cols_ref.shape[1]
    hw = HG_HEADS * HG_DK
    row_head = lax.broadcasted_iota(jnp.int32, (nrow, hw), 0) // 8
    col_head = lax.broadcasted_iota(jnp.int32, (nrow, hw), 1) // HG_DK
    diag = row_head == col_head
    pad = jnp.zeros((LANES - nrow, HG_DK), f32)
    for i in range(nb):
        xt = jnp.concatenate([cols_ref[i], pad], axis=0).T
        vbd = jnp.where(diag, jnp.concatenate([v_ref[i]] * HG_HEADS, axis=1), 0.0)
        vbd = jnp.concatenate([vbd, jnp.zeros((LANES - nrow, hw), f32)], axis=0).astype(bf16)
        ds = _dot(xt.astype(bf16), vbd)
        for h in range(HG_HEADS):
            s0 = s0_ref[i, h]
            oint_ref[i, h] = _dot(qt_ref[i, h].astype(bf16), s0.astype(bf16))
            snew_ref[i, h] = xt[:, h * 8 + 4:h * 8 + 5] * s0 + ds[:, h * HG_DK:(h + 1) * HG_DK]


def _sample_state_call(s0, qt8, v8, cols):
    n_seq = s0.shape[0]
    nb = SAMPLE_SEQ_BLOCK
    blk4 = (nb, HG_HEADS, 8, HG_DK)
    return pl.pallas_call(
        _sample_state_kernel,
        grid=(n_seq // nb,),
        in_specs=[pl.BlockSpec((nb, HG_HEADS, HG_DK, HG_DK), lambda i: (i, 0, 0, 0)),
                  pl.BlockSpec(blk4, lambda i: (i, 0, 0, 0)),
                  pl.BlockSpec((nb, HG_HEADS * 8, HG_DK), lambda i: (i, 0, 0)),
                  pl.BlockSpec((nb, HG_HEADS * 8, HG_DK), lambda i: (i, 0, 0))],
        out_specs=(pl.BlockSpec((nb, HG_HEADS, HG_DK, HG_DK), lambda i: (i, 0, 0, 0)),
                   pl.BlockSpec(blk4, lambda i: (i, 0, 0, 0))),
        out_shape=(jax.ShapeDtypeStruct(s0.shape, f32),
                   jax.ShapeDtypeStruct((n_seq, HG_HEADS, 8, HG_DK), f32)),
        compiler_params=pltpu.CompilerParams(
            dimension_semantics=("arbitrary",), vmem_limit_bytes=VMEM_LIMIT),
        name="sample_state",
    )(s0, qt8.reshape((n_seq,) + blk4[1:]), v8.reshape(n_seq, HG_HEADS * 8, HG_DK),
      cols.reshape(n_seq, HG_HEADS * 8, HG_DK))


def _sample_post_kernel(x_ref, oint_ref, oin_ref, gn_ref, m5_ref, ghg_ref, wbh_ref, wout_ref, gpost_ref,
                        y_ref):
    n_seq = x_ref.shape[0]
    dm = gpost_ref.shape[1]
    steps = x_ref.shape[1] // dm
    hw = HG_HEADS * HG_DK

    def steps_major(ref, wcol):
        return jnp.concatenate([ref[:, t * wcol:(t + 1) * wcol] for t in range(steps)], axis=0)

    x = steps_major(x_ref, dm)
    oint = jnp.concatenate(
        [jnp.concatenate([oint_ref[pl.ds(h * 8 + t, n_seq, stride=HG_HEADS * 8), :] for h in range(HG_HEADS)],
                         axis=1) for t in range(steps)], axis=0)
    o = oint + steps_major(oin_ref, hw)
    parts = []
    for h in range(HG_HEADS):
        hs = slice(h * HG_DK, (h + 1) * HG_DK)
        parts.append(_rms(o[:, hs]))
    yhg = (jnp.concatenate(parts, axis=1) * steps_major(gn_ref, hw)).astype(bf16)
    merged = steps_major(m5_ref, dm) + steps_major(ghg_ref, dm) * _dot(yhg, wbh_ref[...])
    mo = _dot(merged.astype(bf16), wout_ref[...])
    y = x + _rms(mo) * gpost_ref[...]
    for t in range(steps):
        y_ref[:, t * dm:(t + 1) * dm] = y[t * n_seq:(t + 1) * n_seq, :]


def _sample_post_call(x_sm, oint, oin, gn, m5, ghg, w):
    ins = [x_sm, oint, oin, gn, m5, ghg] + [w[k] for k in ("wbh", "wout", "gpost")]
    return pl.pallas_call(
        _sample_post_kernel,
        out_shape=jax.ShapeDtypeStruct(x_sm.shape, f32),
        compiler_params=pltpu.CompilerParams(vmem_limit_bytes=VMEM_LIMIT),
        name="sample_post",
    )(*ins)


def _prep_weights(l, norm_mix_pre, norm_mix_post, norm_mlp_pre, norm_mlp_post, w_in, b_in,
                  s5_a_re, s5_a_im, s5_log_dt, s5_b_re, s5_b_im, s5_c_re, s5_c_im, s5_d, s5_w_glu,
                  s5_b_glu, hg_lb_logits, hg_norm, w_br_s5, w_br_hg, w_out, w_up, w_down, n_pow):
    hp = lax.Precision.HIGHEST
    dm = w_in.shape[1]
    s5w = s5_d.shape[1]
    kw = HG_HEADS * HG_DK
    n_groups = s5w // S5_GROUP
    n_pairs = n_groups // 2
    w = {}
    row = lambda a: a.astype(f32).reshape(1, -1)
    w["gpre"], w["gpost"] = row(norm_mix_pre[l]), row(norm_mix_post[l])
    w["g2pre"], w["g2post"] = row(norm_mlp_pre[l]), row(norm_mlp_post[l])
    w["win"], w["bin"] = w_in[l].astype(bf16), row(b_in[l])
    lb_all = jnp.cumsum(jax.nn.softmax(hg_lb_logits.astype(f32), axis=0), axis=0)
    w["lbrow"] = row(lb_all[l])
    w["hgnrow"] = row(jnp.tile(hg_norm[l].astype(f32), HG_HEADS))
    w["d"] = row(s5_d[l])
    w["wglu"], w["bglu"] = s5_w_glu[l].astype(bf16), row(s5_b_glu[l])
    w["wbs"], w["wbh"], w["wout"] = w_br_s5[l].astype(bf16), w_br_hg[l].astype(bf16), w_out[l].astype(bf16)
    w["wup_f32"], w["wdn_f32"] = w_up[l], w_down[l]

    a_re, a_im = s5_a_re[l].astype(f32), s5_a_im[l].astype(f32)
    dt = jnp.exp(s5_log_dt[l].astype(f32))[:, None]
    mag = jnp.exp(dt * a_re)
    abr, abi = mag * jnp.cos(dt * a_im), mag * jnp.sin(dt * a_im)
    den = a_re * a_re + a_im * a_im
    nr, ni = abr - 1.0, abi
    cfr, cfi = (nr * a_re + ni * a_im) / den, (ni * a_re - nr * a_im) / den
    b_re, b_im = s5_b_re[l].astype(f32), s5_b_im[l].astype(f32)
    bbr = cfr[..., None] * b_re - cfi[..., None] * b_im
    bbi = cfr[..., None] * b_im + cfi[..., None] * b_re
    c_re, c_im = s5_c_re[l].astype(f32), s5_c_im[l].astype(f32)
    w["abr"], w["abi"] = abr.reshape(1, -1), abi.reshape(1, -1)

    lam_r, lam_i = dt * a_re, dt * a_im
    grp = S5_GROUP
    n_state = a_re.shape[1]
    jj = jnp.arange(S5_CHUNK + 1, dtype=f32)[None, :, None]
    pmag = jnp.exp(jj * lam_r[:, None, :])
    pw_r, pw_i = pmag * jnp.cos(jj * lam_i[:, None, :]), pmag * jnp.sin(jj * lam_i[:, None, :])
    bt = jnp.transpose(jnp.stack([bbr, bbi]), (0, 1, 3, 2))
    bt_r, bt_i = bt[0], bt[1]
    c4r, c4i = c_re[:, None], c_im[:, None]

    def cmul(pr_, pi_, xr, xi):
        pr_, pi_ = pr_[:, :, None, :], pi_[:, :, None, :]
        return pr_ * xr - pi_ * xi, pr_ * xi + pi_ * xr

    rows128 = lambda t: t.reshape(n_groups, LANES, t.shape[-1])
    car_t, cai_t = cmul(pw_r[:, 1:], pw_i[:, 1:], c4r, c4i)
    ca = jnp.transpose(jnp.stack([rows128(car_t), -rows128(cai_t)]), (0, 1, 3, 2))
    ber, bei = cmul(pw_r[:, S5_CHUNK - 1::-1], pw_i[:, S5_CHUNK - 1::-1], bt_r[:, None], bt_i[:, None])
    ber, bei = rows128(ber), rows128(bei)
    zpad = jnp.zeros((n_groups, S5_CHUNK - 1, n_state), f32)
    lag_r = jnp.concatenate([zpad, pw_r[:, :S5_CHUNK]], axis=1)
    lag_i = jnp.concatenate([zpad, pw_i[:, :S5_CHUNK]], axis=1)
    cp_r, cp_i = cmul(lag_r, lag_i, c4r, c4i)
    n_blk = 2 * S5_CHUNK - 1
    cp_all = jnp.concatenate([cp_r, -cp_i], axis=3).reshape(n_groups, n_blk * grp, 2 * n_state)
    rpad_t = jnp.einsum("gmn,gnc->gmc", cp_all, jnp.concatenate([bbr, bbi], axis=1), precision=hp)
    toe_t = jnp.concatenate([rpad_t[:, (S5_CHUNK - 1 - s) * grp:(S5_CHUNK - 1 - s) * grp + LANES, :]
                             for s in range(S5_CHUNK)], axis=2)
    toe = jnp.transpose(toe_t, (0, 2, 1))

    def pair_diag(m):
        g, r, c = m.shape
        m = m.reshape(g // 2, 2, r, c)
        z = jnp.zeros_like(m[:, 0])
        return jnp.concatenate([jnp.concatenate([m[:, 0], z], axis=2),
                                jnp.concatenate([z, m[:, 1]], axis=2)], axis=1)

    w["w1"] = jnp.concatenate([pair_diag(toe), pair_diag(ber), pair_diag(bei)], axis=2).astype(bf16)
    w["w2"] = jnp.concatenate([pair_diag(ca[0]), pair_diag(ca[1])], axis=1).astype(bf16)
    qr, qi = pw_r[:, S5_CHUNK], pw_i[:, S5_CHUNK]
    aps_r, aps_i = [], []
    for _ in range(n_pow):
        aps_r.append(qr.reshape(-1))
        aps_i.append(qi.reshape(-1))
        qr, qi = qr * qr - qi * qi, 2.0 * qr * qi
    w["apr"], w["api"] = jnp.stack(aps_r), jnp.stack(aps_i)

    same_group = np.arange(s5w)[:, None] // grp == np.arange(n_groups * n_state)[None, :] // n_state
    lane_tile = np.tile(np.eye(n_state, dtype=np.float32), (1, n_groups))
    src = jnp.stack([bt_r, bt_i, c_re, -c_im]).reshape(4, s5w, n_state)
    tiled = jnp.einsum("zxn,nl->zxl", src, jnp.asarray(lane_tile), precision=hp) * jnp.asarray(same_group, f32)
    w["bblk"] = jnp.concatenate([tiled[0], tiled[1]], axis=1).astype(bf16)
    w["cblkt"] = jnp.concatenate([tiled[2], tiled[3]], axis=1).astype(bf16)
    return w


def kernel(x_prompt, x_sample, state_s5_re, state_s5_im, state_hg, norm_mix_pre, norm_mix_post,
           norm_mlp_pre, norm_mlp_post, w_in, b_in, s5_a_re, s5_a_im, s5_log_dt, s5_b_re, s5_b_im,
           s5_c_re, s5_c_im, s5_d, s5_w_glu, s5_b_glu, hg_lb_logits, hg_norm, w_br_s5, w_br_hg,
           w_out, w_up, w_down):
    depth = w_in.shape[0]
    bsz, seq, dm = x_prompt.shape
    n_seq, steps, _ = x_sample.shape
    n_groups, n_state = s5_a_re.shape[1], s5_a_re.shape[2]
    hw = HG_HEADS * HG_DK
    assert depth == 1 and steps == 4 and n_state == S5_STATE
    n_pow = max(1, int(math.log2(min(PROMPT_TILE, seq) // S5_CHUNK)))
    xp, xs = x_prompt, x_sample
    outs = [[] for _ in range(6)]
    for l in range(depth):
        w = _prep_weights(l, norm_mix_pre, norm_mix_post, norm_mlp_pre, norm_mlp_post, w_in, b_in,
                          s5_a_re, s5_a_im, s5_log_dt, s5_b_re, s5_b_im, s5_c_re, s5_c_im, s5_d,
                          s5_w_glu, s5_b_glu, hg_lb_logits, hg_norm, w_br_s5, w_br_hg, w_out, w_up,
                          w_down, n_pow)
        x1, p_re, p_im, p_hg = _mixer_call(xp, w)
        outs[0].append(p_re.reshape(bsz, n_groups, n_state))
        outs[1].append(p_im.reshape(bsz, n_groups, n_state))
        outs[2].append(p_hg)
        x_sm = xs.reshape(n_seq, steps * dm)
        h0r = state_s5_re[l].reshape(n_seq, n_groups * n_state)
        h0i = state_s5_im[l].reshape(n_seq, n_groups * n_state)
        m5, ghg, gn, qt8, cols, v8, oin, s_re, s_im = _sample_pre_call(x_sm, h0r, h0i, w)
        s_hg, oint = _sample_state_call(state_hg[l], qt8, v8, cols)
        x1_sm = _sample_post_call(x_sm, oint.reshape(n_seq * HG_HEADS * 8, HG_DK), oin, gn, m5, ghg, w)
        yp, y_sm = _mlp_call(x1.reshape(bsz * seq, dm), x1_sm, w)
        xp = yp.reshape(bsz, seq, dm)
        xs = y_sm.reshape(n_seq, steps, dm)
        outs[3].append(s_re.reshape(n_seq, n_groups, n_state))
        outs[4].append(s_im.reshape(n_seq, n_groups, n_state))
        outs[5].append(s_hg)
    return (xp, xs) + tuple(o[0][None] for o in outs)
```

```python
import functools
import math

import jax
import jax.numpy as jnp
import numpy as np
from jax import lax
from jax.experimental import pallas as pl
from jax.experimental.pallas import tpu as pltpu

f32 = jnp.float32
bf16 = jnp.bfloat16

NORM_EPS = 1e-6
S5_GROUP = 16
S5_STATE = 64
S5_CHUNK = 8
HG_HEADS = 8
HG_DK = 128
HG_CHUNK = 64
LANES = 128
VMEM_LIMIT = 56 * 1024 * 1024
PROMPT_TILE = 512
MLP_TILE = 1024
MLP_FCHUNK = 512
SAMPLE_SEQ_BLOCK = 16
SAMPLE_PRE_BLOCK = 32


def _sigmoid(x):
    return 0.5 * (jnp.tanh(0.5 * x) + 1.0)


def _silu(x):
    h = 0.5 * x
    return h + h * jnp.tanh(h)


def _gelu_tanh(x):
    c = math.sqrt(2.0 / math.pi)
    return 0.5 * x * (1.0 + jnp.tanh(c * (x + 0.044715 * (x * x * x))))


def _rms(x):
    return x * lax.rsqrt(jnp.mean(x * x, axis=-1, keepdims=True) + NORM_EPS)


def _dot(a, b):
    return jnp.dot(a, b, preferred_element_type=f32)


def _dot_nt(a, b):
    return lax.dot_general(a, b, (((1,), (1,)), ((), ())), preferred_element_type=f32)


def _dot_tn(a, b):
    return lax.dot_general(a, b, (((0,), (0,)), ((), ())), preferred_element_type=f32)


def _block_transpose8(xs, blk):
    xs = list(xs)
    for d in (4, 2, 1):
        keep = (blk & d) == 0
        for i in range(8):
            if i & d == 0:
                a, b = xs[i], xs[i + d]
                xs[i] = jnp.where(keep, a, pltpu.roll(b, d * S5_GROUP, 1))
                xs[i + d] = jnp.where(keep, pltpu.roll(a, LANES - d * S5_GROUP, 1), b)
    return xs


def _mixer_kernel(x_ref, gpre_ref, win_ref, bin_ref, w1_ref, w2_ref, apr_ref, api_ref, d_ref,
                  wglu_ref, bglu_ref, lb_ref, hgn_ref, wbs_ref, wbh_ref, wout_ref, gpost_ref,
                  y_ref, s5r_ref, s5i_ref, hgs_ref,
                  hb_scr, u_scr, ys_scr, o_scr, yloc_scr, er_scr, ei_scr, cr_scr, ci_scr,
                  st_scr, qfig_scr, yhg_scr, m_scr, fac_scr, el_scr, ghg_scr):
    t_idx = pl.program_id(1)
    n_t = pl.num_programs(1)
    tile = x_ref.shape[1]
    rows = tile // S5_CHUNK
    n_pairs = w1_ref.shape[0]
    n_cols = u_scr.shape[0]
    hw = HG_HEADS * HG_DK
    s5w = n_cols * LANES
    g0 = s5w + 4 * hw

    @pl.when(t_idx == 0)
    def _():
        cr_scr[...] = jnp.zeros_like(cr_scr)
        ci_scr[...] = jnp.zeros_like(ci_scr)
        st_scr[...] = jnp.zeros_like(st_scr)

    x = x_ref[0]
    hb_scr[...] = (_rms(x) * gpre_ref[...]).astype(bf16)
    dm = wbs_ref.shape[1]
    half = tile // 2

    def project(r0, r1, c0, c1):
        qfig_scr[r0:r1, c0:c1] = (_dot(hb_scr[r0:r1, :], win_ref[:, s5w + c0:s5w + c1])
                                  + bin_ref[:, s5w + c0:s5w + c1])

    def gate_hg(j):
        c0, c1 = j * (dm // 4), (j + 1) * (dm // 4)
        ghg_scr[:, c0:c1] = _sigmoid(_dot(hb_scr[...], win_ref[:, g0 + dm + c0:g0 + dm + c1])
                                     + bin_ref[:, g0 + dm + c0:g0 + dm + c1])


    u = _dot(hb_scr[...], win_ref[:, :s5w]) + bin_ref[:, :s5w]
    for j in range(n_cols):
        u_scr[j] = u[:, j * LANES:(j + 1) * LANES]

    project(0, half, 0, hw)
    gate_hg(0)
    blk = lax.broadcasted_iota(jnp.int32, (rows, LANES), 1) // S5_GROUP
    for j in range(n_cols):
        xs = [u_scr[j, pl.ds(s, rows, stride=S5_CHUNK), :] for s in range(S5_CHUNK)]
        ys = _block_transpose8(xs, blk)
        for q in range(4):
            o_scr[j * 4 + q] = jnp.concatenate([ys[2 * q], ys[2 * q + 1]], axis=1).astype(bf16)

    project(0, half, hw, 2 * hw)
    gate_hg(1)
    for p in range(n_pairs):
        r1 = _dot(o_scr[p], w1_ref[p])
        yloc_scr[p] = r1[:, :2 * LANES]
        er_scr[:, p * LANES:(p + 1) * LANES] = r1[:, 2 * LANES:3 * LANES]
        ei_scr[:, p * LANES:(p + 1) * LANES] = r1[:, 3 * LANES:]

    er = er_scr[...]
    ei = ei_scr[...]
    row = lax.broadcasted_iota(jnp.int32, (rows, 1), 0)
    first = row == 0
    cr = cr_scr[...]
    ci = ci_scr[...]
    a_r = apr_ref[0:1, :]
    a_i = api_ref[0:1, :]
    project(0, half, 2 * hw, 3 * hw)
    gate_hg(2)
    er = er + jnp.where(first, a_r * cr - a_i * ci, 0.0)
    ei = ei + jnp.where(first, a_r * ci + a_i * cr, 0.0)
    for k in range(int(math.log2(rows))):
        d = 1 << k
        p_r = apr_ref[k:k + 1, :]
        p_i = api_ref[k:k + 1, :]
        s_r = pltpu.roll(er, d, 0)
        s_i = pltpu.roll(ei, d, 0)
        valid = row >= d
        er, ei = (er + jnp.where(valid, p_r * s_r - p_i * s_i, 0.0),
                  ei + jnp.where(valid, p_r * s_i + p_i * s_r, 0.0))
    hs_r = jnp.where(first, cr, pltpu.roll(er, 1, 0))
    hs_i = jnp.where(first, ci, pltpu.roll(ei, 1, 0))
    cr_scr[...] = er[rows - 1:rows, :]
    ci_scr[...] = ei[rows - 1:rows, :]

    project(0, half, 3 * hw, 4 * hw)
    gate_hg(3)
    for j in range(n_cols):
        halves = []
        for q in range(4):
            p = j * 4 + q
            hp = jnp.concatenate([hs_r[:, p * LANES:(p + 1) * LANES],
                                  hs_i[:, p * LANES:(p + 1) * LANES]], axis=1).astype(bf16)
            yp = yloc_scr[p] + _dot(hp, w2_ref[p])
            halves += [yp[:, :LANES], yp[:, LANES:]]
        zs = _block_transpose8(halves, blk)
        for s in range(S5_CHUNK):
            ys_scr[j, pl.ds(s, rows, stride=S5_CHUNK), :] = zs[s]

    y = jnp.concatenate([ys_scr[j] for j in range(n_cols)], axis=1)
    u = jnp.concatenate([u_scr[j] for j in range(n_cols)], axis=1)
    y = _gelu_tanh(y + d_ref[...] * u)
    z = _dot(y.astype(bf16), wglu_ref[...]) + bglu_ref[...]
    ys5 = (y * _sigmoid(z)).astype(bf16)
    gate_s5 = _sigmoid(_dot(hb_scr[...], win_ref[:, g0:g0 + dm]) + bin_ref[:, g0:g0 + dm])
    m_scr[...] = gate_s5 * _dot(ys5, wbs_ref[...])

    ch = HG_CHUNK
    ri = lax.broadcasted_iota(jnp.int32, (ch, ch), 0)
    ci_ = lax.broadcasted_iota(jnp.int32, (ch, ch), 1)
    causal = ci_ <= ri
    tril2 = jnp.concatenate([causal.astype(bf16)] * 2, axis=1)
    lb = lb_ref[...]
    f_mid = 0.5 * (1.0 + lb)
    f_amp = 0.5 * (1.0 - lb)
    hgn = hgn_ref[...]
    n_chunks = tile // ch
    n_side = n_chunks // 2
    assert n_side == 4 and dm % n_side == 0

    def merge_hg(r0, r1, c0, c1):
        m_scr[r0:r1, c0:c1] = (m_scr[r0:r1, c0:c1]
                               + ghg_scr[r0:r1, c0:c1] * _dot(yhg_scr[r0:r1, :], wbh_ref[:, c0:c1]))

    def chunk_factors(c):
        r0 = c * ch
        slot = c % 2
        q_raw = qfig_scr[pl.ds(r0, ch), 0:hw]
        f_raw = qfig_scr[pl.ds(r0, ch), hw:2 * hw]
        q = _silu(q_raw)
        f = f_mid + f_amp * jnp.tanh(0.5 * f_raw)
        lf = jnp.log(f)
        kk = 1.0 - f
        hi = lf.astype(bf16)
        lo = (lf - hi.astype(f32)).astype(bf16)
        b = _dot(tril2, jnp.concatenate([hi, lo], axis=0))
        mid = b[ch // 2 - 1:ch // 2, :]
        b_last = b[ch - 1:ch, :]
        qh = q * jnp.exp(b - mid)
        kh = kk * jnp.exp(mid - b)
        fac_scr[slot, 0] = qh.astype(bf16)
        fac_scr[slot, 1] = kh.astype(bf16)
        fac_scr[slot, 2] = (qh * jnp.exp(mid)).astype(bf16)
        fac_scr[slot, 3] = (kh * jnp.exp(b_last - mid)).astype(bf16)
        fac_scr[slot, 4] = qfig_scr[pl.ds(r0, ch), 2 * hw:3 * hw].astype(bf16)
        el_scr[slot] = jnp.exp(b_last)

    def chunk_matmuls(c):
        r0 = c * ch
        slot = c % 2
        heads = [slice(h * HG_DK, (h + 1) * HG_DK) for h in range(HG_HEADS)]
        scs = [jnp.where(causal, _dot_nt(fac_scr[slot, 0, :, hs], fac_scr[slot, 1, :, hs]), 0.0).astype(bf16)
               for hs in heads]
        e_last = el_scr[slot]
        outs = []
        for h, hs in enumerate(heads):
            st = st_scr[h]
            vb = fac_scr[slot, 4, :, hs]
            o = _dot(scs[h], vb) + _dot_nt(fac_scr[slot, 2, :, hs], st.astype(bf16))
            st_scr[h] = st * e_last[:, hs] + _dot_tn(vb, fac_scr[slot, 3, :, hs])
            outs.append(_rms(o))
        g_raw = qfig_scr[pl.ds(r0, ch), 3 * hw:4 * hw]
        on = jnp.concatenate(outs, axis=1) * hgn * _silu(g_raw)
        yhg_scr[pl.ds(r0, ch), :] = on.astype(bf16)

    chunk_factors(0)
    for c in range(n_chunks):
        if c + 1 < n_chunks:
            chunk_factors(c + 1)
        chunk_matmuls(c)
        if c < n_side:
            project(half, tile, c * hw, (c + 1) * hw)
        else:
            j = c - n_side
            merge_hg(0, half, j * (dm // n_side), (j + 1) * (dm // n_side))
    for r0 in (0, half):
        if r0:
            merge_hg(half, tile, 0, dm)
        mo = _dot(m_scr[r0:r0 + half, :].astype(bf16), wout_ref[...])
        y_ref[0, r0:r0 + half, :] = x_ref[0, r0:r0 + half, :] + _rms(mo) * gpost_ref[...]

    @pl.when(t_idx == n_t - 1)
    def _():
        s5r_ref[0] = cr_scr[...]
        s5i_ref[0] = ci_scr[...]
        for h in range(HG_HEADS):
            hgs_ref[0, h] = st_scr[h].T


def _const_spec(shape):
    nd = len(shape)
    return pl.BlockSpec(shape, lambda *_: (0,) * nd, pipeline_mode=pl.Buffered(1))


def _mixer_call(x, w):
    bsz, seq, dm = x.shape
    tile = min(PROMPT_TILE, seq)
    assert seq % tile == 0 and tile % HG_CHUNK == 0
    rows = tile // S5_CHUNK
    assert rows & (rows - 1) == 0 and w["apr"].shape[0] >= int(math.log2(rows))
    s5w = w["d"].shape[1]
    n_cols = s5w // LANES
    n_pairs = w["w1"].shape[0]
    sdim = w["apr"].shape[1]
    consts = [w[k] for k in ("gpre", "win", "bin", "w1", "w2", "apr", "api", "d", "wglu", "bglu",
                             "lbrow", "hgnrow", "wbs", "wbh", "wout", "gpost")]
    in_specs = [pl.BlockSpec((1, tile, dm), lambda b, t: (b, t, 0))]
    in_specs += [_const_spec(c.shape) for c in consts]
    out_shape = (jax.ShapeDtypeStruct((bsz, seq, dm), f32),
                 jax.ShapeDtypeStruct((bsz, 1, sdim), f32),
                 jax.ShapeDtypeStruct((bsz, 1, sdim), f32),
                 jax.ShapeDtypeStruct((bsz, HG_HEADS, HG_DK, HG_DK), f32))
    out_specs = (pl.BlockSpec((1, tile, dm), lambda b, t: (b, t, 0)),
                 pl.BlockSpec((1, 1, sdim), lambda b, t: (b, 0, 0)),
                 pl.BlockSpec((1, 1, sdim), lambda b, t: (b, 0, 0)),
                 pl.BlockSpec((1, HG_HEADS, HG_DK, HG_DK), lambda b, t: (b, 0, 0, 0)))
    scratch = [
        pltpu.VMEM((tile, dm), bf16),
        pltpu.VMEM((n_cols, tile, LANES), f32),
        pltpu.VMEM((n_cols, tile, LANES), f32),
        pltpu.VMEM((n_pairs, rows, 2 * LANES), bf16),
        pltpu.VMEM((n_pairs, rows, 2 * LANES), f32),
        pltpu.VMEM((rows, sdim), f32),
        pltpu.VMEM((rows, sdim), f32),
        pltpu.VMEM((1, sdim), f32),
        pltpu.VMEM((1, sdim), f32),
        pltpu.VMEM((HG_HEADS, HG_DK, HG_DK), f32),
        pltpu.VMEM((tile, HG_HEADS * 4 * HG_DK), f32),
        pltpu.VMEM((tile, HG_HEADS * HG_DK), bf16),
        pltpu.VMEM((tile, dm), f32),
        pltpu.VMEM((2, 5, HG_CHUNK, HG_HEADS * HG_DK), bf16),
        pltpu.VMEM((2, 1, HG_HEADS * HG_DK), f32),
        pltpu.VMEM((tile, dm), f32),
    ]
    return pl.pallas_call(
        _mixer_kernel,
        grid=(bsz, seq // tile),
        in_specs=in_specs,
        out_specs=out_specs,
        out_shape=out_shape,
        scratch_shapes=scratch,
        compiler_params=pltpu.CompilerParams(
            dimension_semantics=("arbitrary", "arbitrary"), vmem_limit_bytes=VMEM_LIMIT),
        name="prompt_mixer",
    )(x, *consts)


def _mlp_body(x, gpre_ref, wup_scr, wdn_scr, gpost_ref):
    h2 = (_rms(x) * gpre_ref[...]).astype(bf16)
    acc = None
    for c in range(wup_scr.shape[0]):
        a = _dot(h2, wup_scr[c])
        a = jnp.maximum(a, 0.0)
        part = _dot((a * a).astype(bf16), wdn_scr[c])
        acc = part if acc is None else acc + part
    return x + _rms(acc) * gpost_ref[...]


def _mlp_kernel(xp_ref, xs_ref, gpre_ref, wup_ref, wdn_ref, gpost_ref, yp_ref, ys_ref, wup_scr, wdn_scr):
    step = pl.program_id(0)
    n_cast = wup_scr.shape[0]
    last = pl.num_programs(0) - 1
    dm = gpre_ref.shape[1]

    @pl.when(step < n_cast)
    def _():
        wup_scr[step] = wup_ref[...].astype(bf16)
        wdn_scr[step] = wdn_ref[...].astype(bf16)

    @pl.when(jnp.logical_and(step >= n_cast, step < last))
    def _():
        half = xp_ref.shape[0] // 2
        for r0 in (0, half):
            yp_ref[r0:r0 + half, :] = _mlp_body(xp_ref[r0:r0 + half, :], gpre_ref, wup_scr, wdn_scr, gpost_ref)

    @pl.when(step == last)
    def _():
        n_seq = xs_ref.shape[0]
        steps = xs_ref.shape[1] // dm
        x = jnp.concatenate([xs_ref[:, t * dm:(t + 1) * dm] for t in range(steps)], axis=0)
        y = _mlp_body(x, gpre_ref, wup_scr, wdn_scr, gpost_ref)
        for t in range(steps):
            ys_ref[:, t * dm:(t + 1) * dm] = y[t * n_seq:(t + 1) * n_seq, :]


def _mlp_call(xp2d, xs_sm, w):
    n, dm = xp2d.shape
    tile = min(MLP_TILE, n)
    assert n % tile == 0
    n_tiles = n // tile
    wup, wdn = w["wup_f32"], w["wdn_f32"]
    dff = wup.shape[1]
    fc = MLP_FCHUNK
    assert dff % fc == 0
    n_cast = dff // fc
    blk = lambda s: jnp.minimum(s, n_cast - 1)
    row = lambda s: jnp.clip(s - n_cast, 0, n_tiles - 1)
    return pl.pallas_call(
        _mlp_kernel,
        grid=(n_cast + n_tiles + 1,),
        in_specs=[pl.BlockSpec((tile, dm), lambda s: (row(s), 0)),
                  _const_spec(xs_sm.shape),
                  _const_spec(w["g2pre"].shape),
                  pl.BlockSpec((dm, fc), lambda s: (0, blk(s))),
                  pl.BlockSpec((fc, dm), lambda s: (blk(s), 0)),
                  _const_spec(w["g2post"].shape)],
        out_specs=(pl.BlockSpec((tile, dm), lambda s: (row(s), 0)),
                   pl.BlockSpec(xs_sm.shape, lambda s: (0, 0))),
        out_shape=(jax.ShapeDtypeStruct((n, dm), f32), jax.ShapeDtypeStruct(xs_sm.shape, f32)),
        scratch_shapes=[pltpu.VMEM((n_cast, dm, fc), bf16), pltpu.VMEM((n_cast, fc, dm), bf16)],
        compiler_params=pltpu.CompilerParams(
            dimension_semantics=("arbitrary",), vmem_limit_bytes=VMEM_LIMIT),
        name="mlp",
    )(xp2d, xs_sm, w["g2pre"], wup, wdn, w["g2post"])


def _sample_pre_kernel(x_ref, h0r_ref, h0i_ref, gpre_ref, win_ref, bin_ref, bblk_ref, cblkt_ref,
                       abr_ref, abi_ref, d_ref, wglu_ref, bglu_ref, lb_ref, hgn_ref, wbs_ref,
                       m5_ref, ghg_ref, gn_ref, qt_ref, cols_ref, v_ref, oin_ref,
                       s5r_ref, s5i_ref):
    n_seq = h0r_ref.shape[0]
    dm = gpre_ref.shape[1]
    steps = x_ref.shape[1] // dm
    sdim = h0r_ref.shape[1]
    hw = HG_HEADS * HG_DK
    s5w = d_ref.shape[1]
    x = jnp.concatenate([x_ref[:, t * dm:(t + 1) * dm] for t in range(steps)], axis=0)
    hb = (_rms(x) * gpre_ref[...]).astype(bf16)

    u = _dot(hb, win_ref[:, :s5w]) + bin_ref[:, :s5w]
    bu = _dot(u.astype(bf16), bblk_ref[...])
    hr = h0r_ref[...]
    hi = h0i_ref[...]
    a_r = abr_ref[...]
    a_i = abi_ref[...]
    hs = []
    for t in range(steps):
        sl = slice(t * n_seq, (t + 1) * n_seq)
        hr, hi = (a_r * hr - a_i * hi + bu[sl, :sdim], a_r * hi + a_i * hr + bu[sl, sdim:])
        hs.append(jnp.concatenate([hr, hi], axis=1).astype(bf16))
    s5r_ref[...] = hr
    s5i_ref[...] = hi
    y = _dot_nt(jnp.concatenate(hs, axis=0), cblkt_ref[...])
    y = _gelu_tanh(y + d_ref[...] * u)
    z = _dot(y.astype(bf16), wglu_ref[...]) + bglu_ref[...]
    ys5 = (y * _sigmoid(z)).astype(bf16)
    g0 = s5w + 4 * hw
    def put_steps(ref, val):
        wcol = val.shape[1]
        for t in range(steps):
            ref[:, t * wcol:(t + 1) * wcol] = val[t * n_seq:(t + 1) * n_seq, :]

    gates = _dot(hb, win_ref[:, g0:]) + bin_ref[:, g0:]
    put_steps(m5_ref, _sigmoid(gates[:, :dm]) * _dot(ys5, wbs_ref[...]))
    put_steps(ghg_ref, _sigmoid(gates[:, dm:]))

    def proj(i):
        c0 = s5w + i * hw
        return _dot(hb, win_ref[:, c0:c0 + hw]) + bin_ref[:, c0:c0 + hw]

    q_raw = proj(0)
    f_raw = proj(1)
    v = proj(2)
    g_raw = proj(3)
    q = q_raw * _sigmoid(q_raw)
    lb = lb_ref[...]
    f = lb + (1.0 - lb) * _sigmoid(f_raw)
    lf = jnp.log(f)
    kk = 1.0 - f
    put_steps(gn_ref, g_raw * _sigmoid(g_raw) * hgn_ref[...])
    bs = []
    acc = None
    for t in range(steps):
        sl = slice(t * n_seq, (t + 1) * n_seq)
        acc = lf[sl] if acc is None else acc + lf[sl]
        bs.append(acc)
    b_last = bs[-1]

    def put(ref, slot, val):
        for h in range(HG_HEADS):
            ref[pl.ds(h * 8 + slot, n_seq, stride=HG_HEADS * 8), :] = val[:, h * HG_DK:(h + 1) * HG_DK]

    for ref in (qt_ref, cols_ref, v_ref):
        ref[...] = jnp.zeros_like(ref)
    put(cols_ref, steps, jnp.exp(b_last))
    for t in range(steps):
        sl = slice(t * n_seq, (t + 1) * n_seq)
        put(qt_ref, t, q[sl] * jnp.exp(bs[t]))
        put(cols_ref, t, kk[sl] * jnp.exp(b_last - bs[t]))
        put(v_ref, t, v[sl])
        o_t = None
        for s in range(t + 1):
            sls = slice(s * n_seq, (s + 1) * n_seq)
            prod = q[sl] * kk[sls] * jnp.exp(bs[t] - bs[s])
            parts = []
            for h in range(HG_HEADS):
                hs = slice(h * HG_DK, (h + 1) * HG_DK)
                wgt = jnp.sum(prod[:, hs], axis=1, keepdims=True)
                parts.append(wgt * v[sls, hs])
            term = jnp.concatenate(parts, axis=1)
            o_t = term if o_t is None else o_t + term
        oin_ref[:, t * hw:(t + 1) * hw] = o_t


def _sample_pre_call(x_sm, h0r, h0i, w):
    n_seq, sdim = h0r.shape
    dm = w["gpre"].shape[1]
    steps = x_sm.shape[1] // dm
    hw = HG_HEADS * HG_DK
    nb = SAMPLE_PRE_BLOCK
    assert n_seq % nb == 0
    consts = [w[k] for k in ("gpre", "win", "bin", "bblk", "cblkt", "abr", "abi", "d",
                             "wglu", "bglu", "lbrow", "hgnrow", "wbs")]
    slots = HG_HEADS * 8
    shapes = ((1, steps * dm),
              (1, steps * dm),
              (1, steps * hw),
              (slots, HG_DK),
              (slots, HG_DK),
              (slots, HG_DK),
              (1, steps * hw),
              (1, sdim), (1, sdim))
    seq_spec = lambda shp: pl.BlockSpec((nb * shp[0], shp[1]), lambda i: (i, 0))
    return pl.pallas_call(
        _sample_pre_kernel,
        grid=(n_seq // nb,),
        in_specs=([seq_spec((1, steps * dm)), seq_spec((1, sdim)), seq_spec((1, sdim))]
                  + [_const_spec(c.shape) for c in consts]),
        out_specs=tuple(seq_spec(shp) for shp in shapes),
        out_shape=tuple(jax.ShapeDtypeStruct((n_seq * shp[0], shp[1]), f32) for shp in shapes),
        compiler_params=pltpu.CompilerParams(
            dimension_semantics=("arbitrary",), vmem_limit_bytes=VMEM_LIMIT),
        name="sample_pre",
    )(x_sm, h0r, h0i, *consts)


def _sample_state_kernel(s0_ref, qt_ref, v_ref, cols_ref, snew_ref, oint_ref):
    nb = s0_ref.shape[0]
    nrow = cols_ref.shape[1]
    hw = HG_HEADS * HG_DK
    row_head = lax.broadcasted_iota(jnp.int32, (nrow, hw), 0) // 8
    col_head = lax.broadcasted_iota(jnp.int32, (nrow, hw), 1) // HG_DK
    diag = row_head == col_head
    pad = jnp.zeros((LANES - nrow, HG_DK), f32)
    for i in range(nb):
        xt = jnp.concatenate([cols_ref[i], pad], axis=0).T
        vbd = jnp.where(diag, jnp.concatenate([v_ref[i]] * HG_HEADS, axis=1), 0.0)
        vbd = jnp.concatenate([vbd, jnp.zeros((LANES - nrow, hw), f32)], axis=0).astype(bf16)
        ds = _dot(xt.astype(bf16), vbd)
        for h in range(HG_HEADS):
            s0 = s0_ref[i, h]
            oint_ref[i, h] = _dot(qt_ref[i, h].astype(bf16), s0.astype(bf16))
            snew_ref[i, h] = xt[:, h * 8 + 4:h * 8 + 5] * s0 + ds[:, h * HG_DK:(h + 1) * HG_DK]


def _sample_state_call(s0, qt8, v8, cols):
    n_seq = s0.shape[0]
    nb = SAMPLE_SEQ_BLOCK
    blk4 = (nb, HG_HEADS, 8, HG_DK)
    return pl.pallas_call(
        _sample_state_kernel,
        grid=(n_seq // nb,),
        in_specs=[pl.BlockSpec((nb, HG_HEADS, HG_DK, HG_DK), lambda i: (i, 0, 0, 0)),
                  pl.BlockSpec(blk4, lambda i: (i, 0, 0, 0)),
                  pl.BlockSpec((nb, HG_HEADS * 8, HG_DK), lambda i: (i, 0, 0)),
                  pl.BlockSpec((nb, HG_HEADS * 8, HG_DK), lambda i: (i, 0, 0))],
        out_specs=(pl.BlockSpec((nb, HG_HEADS, HG_DK, HG_DK), lambda i: (i, 0, 0, 0)),
                   pl.BlockSpec(blk4, lambda i: (i, 0, 0, 0))),
        out_shape=(jax.ShapeDtypeStruct(s0.shape, f32),
                   jax.ShapeDtypeStruct((n_seq, HG_HEADS, 8, HG_DK), f32)),
        compiler_params=pltpu.CompilerParams(
            dimension_semantics=("arbitrary",), vmem_limit_bytes=VMEM_LIMIT),
        name="sample_state",
    )(s0, qt8.reshape((n_seq,) + blk4[1:]), v8.reshape(n_seq, HG_HEADS * 8, HG_DK),
      cols.reshape(n_seq, HG_HEADS * 8, HG_DK))


def _sample_post_kernel(x_ref, oint_ref, oin_ref, gn_ref, m5_ref, ghg_ref, wbh_ref, wout_ref, gpost_ref,
                        y_ref):
    n_seq = x_ref.shape[0]
    dm = gpost_ref.shape[1]
    steps = x_ref.shape[1] // dm
    hw = HG_HEADS * HG_DK

    def steps_major(ref, wcol):
        return jnp.concatenate([ref[:, t * wcol:(t + 1) * wcol] for t in range(steps)], axis=0)

    x = steps_major(x_ref, dm)
    oint = jnp.concatenate(
        [jnp.concatenate([oint_ref[pl.ds(h * 8 + t, n_seq, stride=HG_HEADS * 8), :] for h in range(HG_HEADS)],
                         axis=1) for t in range(steps)], axis=0)
    o = oint + steps_major(oin_ref, hw)
    parts = []
    for h in range(HG_HEADS):
        hs = slice(h * HG_DK, (h + 1) * HG_DK)
        parts.append(_rms(o[:, hs]))
    yhg = (jnp.concatenate(parts, axis=1) * steps_major(gn_ref, hw)).astype(bf16)
    merged = steps_major(m5_ref, dm) + steps_major(ghg_ref, dm) * _dot(yhg, wbh_ref[...])
    mo = _dot(merged.astype(bf16), wout_ref[...])
    y = x + _rms(mo) * gpost_ref[...]
    for t in range(steps):
        y_ref[:, t * dm:(t + 1) * dm] = y[t * n_seq:(t + 1) * n_seq, :]


def _sample_post_call(x_sm, oint, oin, gn, m5, ghg, w):
    ins = [x_sm, oint, oin, gn, m5, ghg] + [w[k] for k in ("wbh", "wout", "gpost")]
    return pl.pallas_call(
        _sample_post_kernel,
        out_shape=jax.ShapeDtypeStruct(x_sm.shape, f32),
        compiler_params=pltpu.CompilerParams(vmem_limit_bytes=VMEM_LIMIT),
        name="sample_post",
    )(*ins)


def _prep_weights(l, norm_mix_pre, norm_mix_post, norm_mlp_pre, norm_mlp_post, w_in, b_in,
                  s5_a_re, s5_a_im, s5_log_dt, s5_b_re, s5_b_im, s5_c_re, s5_c_im, s5_d, s5_w_glu,
                  s5_b_glu, hg_lb_logits, hg_norm, w_br_s5, w_br_hg, w_out, w_up, w_down, n_pow):
    hp = lax.Precision.HIGHEST
    dm = w_in.shape[1]
    s5w = s5_d.shape[1]
    kw = HG_HEADS * HG_DK
    n_groups = s5w // S5_GROUP
    n_pairs = n_groups // 2
    w = {}
    row = lambda a: a.astype(f32).reshape(1, -1)
    w["gpre"], w["gpost"] = row(norm_mix_pre[l]), row(norm_mix_post[l])
    w["g2pre"], w["g2post"] = row(norm_mlp_pre[l]), row(norm_mlp_post[l])
    w["win"], w["bin"] = w_in[l].astype(bf16), row(b_in[l])
    lb_all = jnp.cumsum(jax.nn.softmax(hg_lb_logits.astype(f32), axis=0), axis=0)
    w["lbrow"] = row(lb_all[l])
    w["hgnrow"] = row(jnp.tile(hg_norm[l].astype(f32), HG_HEADS))
    w["d"] = row(s5_d[l])
    w["wglu"], w["bglu"] = s5_w_glu[l].astype(bf16), row(s5_b_glu[l])
    w["wbs"], w["wbh"], w["wout"] = w_br_s5[l].astype(bf16), w_br_hg[l].astype(bf16), w_out[l].astype(bf16)
    w["wup_f32"], w["wdn_f32"] = w_up[l], w_down[l]

    a_re, a_im = s5_a_re[l].astype(f32), s5_a_im[l].astype(f32)
    dt = jnp.exp(s5_log_dt[l].astype(f32))[:, None]
    mag = jnp.exp(dt * a_re)
    abr, abi = mag * jnp.cos(dt * a_im), mag * jnp.sin(dt * a_im)
    den = a_re * a_re + a_im * a_im
    nr, ni = abr - 1.0, abi
    cfr, cfi = (nr * a_re + ni * a_im) / den, (ni * a_re - nr * a_im) / den
    b_re, b_im = s5_b_re[l].astype(f32), s5_b_im[l].astype(f32)
    bbr = cfr[..., None] * b_re - cfi[..., None] * b_im
    bbi = cfr[..., None] * b_im + cfi[..., None] * b_re
    c_re, c_im = s5_c_re[l].astype(f32), s5_c_im[l].astype(f32)
    w["abr"], w["abi"] = abr.reshape(1, -1), abi.reshape(1, -1)

    lam_r, lam_i = dt * a_re, dt * a_im
    grp = S5_GROUP
    n_state = a_re.shape[1]
    jj = jnp.arange(S5_CHUNK + 1, dtype=f32)[None, :, None]
    pmag = jnp.exp(jj * lam_r[:, None, :])
    pw_r, pw_i = pmag * jnp.cos(jj * lam_i[:, None, :]), pmag * jnp.sin(jj * lam_i[:, None, :])
    bt = jnp.transpose(jnp.stack([bbr, bbi]), (0, 1, 3, 2))
    bt_r, bt_i = bt[0], bt[1]
    c4r, c4i = c_re[:, None], c_im[:, None]

    def cmul(pr_, pi_, xr, xi):
        pr_, pi_ = pr_[:, :, None, :], pi_[:, :, None, :]
        return pr_ * xr - pi_ * xi, pr_ * xi + pi_ * xr

    rows128 = lambda t: t.reshape(n_groups, LANES, t.shape[-1])
    car_t, cai_t = cmul(pw_r[:, 1:], pw_i[:, 1:], c4r, c4i)
    ca = jnp.transpose(jnp.stack([rows128(car_t), -rows128(cai_t)]), (0, 1, 3, 2))
    ber, bei = cmul(pw_r[:, S5_CHUNK - 1::-1], pw_i[:, S5_CHUNK - 1::-1], bt_r[:, None], bt_i[:, None])
    ber, bei = rows128(ber), rows128(bei)
    zpad = jnp.zeros((n_groups, S5_CHUNK - 1, n_state), f32)
    lag_r = jnp.concatenate([zpad, pw_r[:, :S5_CHUNK]], axis=1)
    lag_i = jnp.concatenate([zpad, pw_i[:, :S5_CHUNK]], axis=1)
    cp_r, cp_i = cmul(lag_r, lag_i, c4r, c4i)
    n_blk = 2 * S5_CHUNK - 1
    cp_all = jnp.concatenate([cp_r, -cp_i], axis=3).reshape(n_groups, n_blk * grp, 2 * n_state)
    rpad_t = jnp.einsum("gmn,gnc->gmc", cp_all, jnp.concatenate([bbr, bbi], axis=1), precision=hp)
    toe_t = jnp.concatenate([rpad_t[:, (S5_CHUNK - 1 - s) * grp:(S5_CHUNK - 1 - s) * grp + LANES, :]
                             for s in range(S5_CHUNK)], axis=2)
    toe = jnp.transpose(toe_t, (0, 2, 1))

    def pair_diag(m):
        g, r, c = m.shape
        m = m.reshape(g // 2, 2, r, c)
        z = jnp.zeros_like(m[:, 0])
        return jnp.concatenate([jnp.concatenate([m[:, 0], z], axis=2),
                                jnp.concatenate([z, m[:, 1]], axis=2)], axis=1)

    w["w1"] = jnp.concatenate([pair_diag(toe), pair_diag(ber), pair_diag(bei)], axis=2).astype(bf16)
    w["w2"] = jnp.concatenate([pair_diag(ca[0]), pair_diag(ca[1])], axis=1).astype(bf16)
    qr, qi = pw_r[:, S5_CHUNK], pw_i[:, S5_CHUNK]
    aps_r, aps_i = [], []
    for _ in range(n_pow):
        aps_r.append(qr.reshape(-1))
        aps_i.append(qi.reshape(-1))
        qr, qi = qr * qr - qi * qi, 2.0 * qr * qi
    w["apr"], w["api"] = jnp.stack(aps_r), jnp.stack(aps_i)

    same_group = np.arange(s5w)[:, None] // grp == np.arange(n_groups * n_state)[None, :] // n_state
    lane_tile = np.tile(np.eye(n_state, dtype=np.float32), (1, n_groups))
    src = jnp.stack([bt_r, bt_i, c_re, -c_im]).reshape(4, s5w, n_state)
    tiled = jnp.einsum("zxn,nl->zxl", src, jnp.asarray(lane_tile), precision=hp,
                       preferred_element_type=f32).astype(bf16) * jnp.asarray(same_group, bf16)
    w["bblk"] = jnp.concatenate([tiled[0], tiled[1]], axis=1)
    w["cblkt"] = jnp.concatenate([tiled[2], tiled[3]], axis=1)
    return w


def kernel(x_prompt, x_sample, state_s5_re, state_s5_im, state_hg, norm_mix_pre, norm_mix_post,
           norm_mlp_pre, norm_mlp_post, w_in, b_in, s5_a_re, s5_a_im, s5_log_dt, s5_b_re, s5_b_im,
           s5_c_re, s5_c_im, s5_d, s5_w_glu, s5_b_glu, hg_lb_logits, hg_norm, w_br_s5, w_br_hg,
           w_out, w_up, w_down):
    depth = w_in.shape[0]
    bsz, seq, dm = x_prompt.shape
    n_seq, steps, _ = x_sample.shape
    n_groups, n_state = s5_a_re.shape[1], s5_a_re.shape[2]
    hw = HG_HEADS * HG_DK
    assert depth == 1 and steps == 4 and n_state == S5_STATE
    n_pow = max(1, int(math.log2(min(PROMPT_TILE, seq) // S5_CHUNK)))
    xp, xs = x_prompt, x_sample
    outs = [[] for _ in range(6)]
    for l in range(depth):
        w = _prep_weights(l, norm_mix_pre, norm_mix_post, norm_mlp_pre, norm_mlp_post, w_in, b_in,
                          s5_a_re, s5_a_im, s5_log_dt, s5_b_re, s5_b_im, s5_c_re, s5_c_im, s5_d,
                          s5_w_glu, s5_b_glu, hg_lb_logits, hg_norm, w_br_s5, w_br_hg, w_out, w_up,
                          w_down, n_pow)
        x1, p_re, p_im, p_hg = _mixer_call(xp, w)
        outs[0].append(p_re.reshape(bsz, n_groups, n_state))
        outs[1].append(p_im.reshape(bsz, n_groups, n_state))
        outs[2].append(p_hg)
        x_sm = xs.reshape(n_seq, steps * dm)
        h0r = state_s5_re[l].reshape(n_seq, n_groups * n_state)
        h0i = state_s5_im[l].reshape(n_seq, n_groups * n_state)
        m5, ghg, gn, qt8, cols, v8, oin, s_re, s_im = _sample_pre_call(x_sm, h0r, h0i, w)
        s_hg, oint = _sample_state_call(state_hg[l], qt8, v8, cols)
        x1_sm = _sample_post_call(x_sm, oint.reshape(n_seq * HG_HEADS * 8, HG_DK), oin, gn, m5, ghg, w)
        yp, y_sm = _mlp_call(x1.reshape(bsz * seq, dm), x1_sm, w)
        xp = yp.reshape(bsz, seq, dm)
        xs = y_sm.reshape(n_seq, steps, dm)
        outs[3].append(s_re.reshape(n_seq, n_groups, n_state))
        outs[4].append(s_im.reshape(n_seq, n_groups, n_state))
        outs[5].append(s_hg)
    return (xp, xs) + tuple(o[0][None] for o in outs)
```

```python
import functools
import math

import jax
import jax.numpy as jnp
import numpy as np
from jax import lax
from jax.experimental import pallas as pl
from jax.experimental.pallas import tpu as pltpu

f32 = jnp.float32
bf16 = jnp.bfloat16

NORM_EPS = 1e-6
S5_GROUP = 16
S5_STATE = 64
S5_CHUNK = 8
HG_HEADS = 8
HG_DK = 128
HG_CHUNK = 64
LANES = 128
VMEM_LIMIT = 56 * 1024 * 1024
PROMPT_TILE = 512
MLP_TILE = 1024
MLP_SUBTILES = 2
MLP_FCHUNK = 512
SAMPLE_SEQ_BLOCK = 16
SAMPLE_PRE_BLOCK = 64


def _sigmoid(x):
    return 0.5 * (jnp.tanh(0.5 * x) + 1.0)


def _silu(x):
    h = 0.5 * x
    return h + h * jnp.tanh(h)


def _gelu_tanh(x):
    c = math.sqrt(2.0 / math.pi)
    return 0.5 * x * (1.0 + jnp.tanh(c * (x + 0.044715 * (x * x * x))))


def _rms(x):
    return x * lax.rsqrt(jnp.mean(x * x, axis=-1, keepdims=True) + NORM_EPS)


def _dot(a, b):
    return jnp.dot(a, b, preferred_element_type=f32)


def _dot_nt(a, b):
    return lax.dot_general(a, b, (((1,), (1,)), ((), ())), preferred_element_type=f32)


def _dot_tn(a, b):
    return lax.dot_general(a, b, (((0,), (0,)), ((), ())), preferred_element_type=f32)


def _block_transpose8(xs, blk):
    xs = list(xs)
    for d in (4, 2, 1):
        keep = (blk & d) == 0
        for i in range(8):
            if i & d == 0:
                a, b = xs[i], xs[i + d]
                xs[i] = jnp.where(keep, a, pltpu.roll(b, d * S5_GROUP, 1))
                xs[i + d] = jnp.where(keep, pltpu.roll(a, LANES - d * S5_GROUP, 1), b)
    return xs


def _mixer_kernel(x_ref, gpre_ref, win_ref, bin_ref, w1_ref, w2_ref, apr_ref, api_ref, d_ref,
                  wglu_ref, bglu_ref, lb_ref, hgn_ref, wbs_ref, wbh_ref, wout_ref, gpost_ref,
                  y_ref, s5r_ref, s5i_ref, hgs_ref,
                  hb_scr, u_scr, ys_scr, o_scr, yloc_scr, er_scr, ei_scr, cr_scr, ci_scr,
                  st_scr, qfig_scr, yhg_scr, m_scr, fac_scr, el_scr, ghg_scr):
    t_idx = pl.program_id(1)
    n_t = pl.num_programs(1)
    tile = x_ref.shape[1]
    rows = tile // S5_CHUNK
    n_pairs = w1_ref.shape[0]
    n_cols = u_scr.shape[0]
    hw = HG_HEADS * HG_DK
    s5w = n_cols * LANES
    g0 = s5w + 4 * hw

    @pl.when(t_idx == 0)
    def _():
        cr_scr[...] = jnp.zeros_like(cr_scr)
        ci_scr[...] = jnp.zeros_like(ci_scr)
        st_scr[...] = jnp.zeros_like(st_scr)

    x = x_ref[0]
    hb_scr[...] = (_rms(x) * gpre_ref[...]).astype(bf16)
    dm = wbs_ref.shape[1]
    half = tile // 2

    def project(r0, r1, c0, c1):
        qfig_scr[r0:r1, c0:c1] = (_dot(hb_scr[r0:r1, :], win_ref[:, s5w + c0:s5w + c1])
                                  + bin_ref[:, s5w + c0:s5w + c1])

    def gate_hg(j):
        c0, c1 = j * (dm // 4), (j + 1) * (dm // 4)
        ghg_scr[:, c0:c1] = _sigmoid(_dot(hb_scr[...], win_ref[:, g0 + dm + c0:g0 + dm + c1])
                                     + bin_ref[:, g0 + dm + c0:g0 + dm + c1])


    u = _dot(hb_scr[...], win_ref[:, :s5w]) + bin_ref[:, :s5w]
    for j in range(n_cols):
        u_scr[j] = u[:, j * LANES:(j + 1) * LANES]

    project(0, half, 0, hw)
    gate_hg(0)
    blk = lax.broadcasted_iota(jnp.int32, (rows, LANES), 1) // S5_GROUP
    for j in range(n_cols):
        xs = [u_scr[j, pl.ds(s, rows, stride=S5_CHUNK), :] for s in range(S5_CHUNK)]
        ys = _block_transpose8(xs, blk)
        for q in range(4):
            o_scr[j * 4 + q] = jnp.concatenate([ys[2 * q], ys[2 * q + 1]], axis=1).astype(bf16)

    project(0, half, hw, 2 * hw)
    gate_hg(1)
    for p in range(n_pairs):
        r1 = _dot(o_scr[p], w1_ref[p])
        yloc_scr[p] = r1[:, :2 * LANES]
        er_scr[:, p * LANES:(p + 1) * LANES] = r1[:, 2 * LANES:3 * LANES]
        ei_scr[:, p * LANES:(p + 1) * LANES] = r1[:, 3 * LANES:]

    er = er_scr[...]
    ei = ei_scr[...]
    row = lax.broadcasted_iota(jnp.int32, (rows, 1), 0)
    first = row == 0
    cr = cr_scr[...]
    ci = ci_scr[...]
    a_r = apr_ref[0:1, :]
    a_i = api_ref[0:1, :]
    project(0, half, 2 * hw, 3 * hw)
    gate_hg(2)
    er = er + jnp.where(first, a_r * cr - a_i * ci, 0.0)
    ei = ei + jnp.where(first, a_r * ci + a_i * cr, 0.0)
    for k in range(int(math.log2(rows))):
        d = 1 << k
        p_r = apr_ref[k:k + 1, :]
        p_i = api_ref[k:k + 1, :]
        s_r = pltpu.roll(er, d, 0)
        s_i = pltpu.roll(ei, d, 0)
        valid = row >= d
        er, ei = (er + jnp.where(valid, p_r * s_r - p_i * s_i, 0.0),
                  ei + jnp.where(valid, p_r * s_i + p_i * s_r, 0.0))
    hs_r = jnp.where(first, cr, pltpu.roll(er, 1, 0))
    hs_i = jnp.where(first, ci, pltpu.roll(ei, 1, 0))
    cr_scr[...] = er[rows - 1:rows, :]
    ci_scr[...] = ei[rows - 1:rows, :]

    project(0, half, 3 * hw, 4 * hw)
    gate_hg(3)
    for j in range(n_cols):
        halves = []
        for q in range(4):
            p = j * 4 + q
            hp = jnp.concatenate([hs_r[:, p * LANES:(p + 1) * LANES],
                                  hs_i[:, p * LANES:(p + 1) * LANES]], axis=1).astype(bf16)
            yp = yloc_scr[p] + _dot(hp, w2_ref[p])
            halves += [yp[:, :LANES], yp[:, LANES:]]
        zs = _block_transpose8(halves, blk)
        for s in range(S5_CHUNK):
            ys_scr[j, pl.ds(s, rows, stride=S5_CHUNK), :] = zs[s]

    y = jnp.concatenate([ys_scr[j] for j in range(n_cols)], axis=1)
    u = jnp.concatenate([u_scr[j] for j in range(n_cols)], axis=1)
    y = _gelu_tanh(y + d_ref[...] * u)
    z = _dot(y.astype(bf16), wglu_ref[...]) + bglu_ref[...]
    ys5 = (y * _sigmoid(z)).astype(bf16)
    gate_s5 = _sigmoid(_dot(hb_scr[...], win_ref[:, g0:g0 + dm]) + bin_ref[:, g0:g0 + dm])
    m_scr[...] = gate_s5 * _dot(ys5, wbs_ref[...])

    ch = HG_CHUNK
    ri = lax.broadcasted_iota(jnp.int32, (ch, ch), 0)
    ci_ = lax.broadcasted_iota(jnp.int32, (ch, ch), 1)
    causal = ci_ <= ri
    tril2 = jnp.concatenate([causal.astype(bf16)] * 2, axis=1)
    lb = lb_ref[...]
    f_mid = 0.5 * (1.0 + lb)
    f_amp = 0.5 * (1.0 - lb)
    hgn = hgn_ref[...]
    n_chunks = tile // ch
    n_side = n_chunks // 2
    assert n_side == 4 and dm % n_side == 0

    def merge_hg(r0, r1, c0, c1):
        m_scr[r0:r1, c0:c1] = (m_scr[r0:r1, c0:c1]
                               + ghg_scr[r0:r1, c0:c1] * _dot(yhg_scr[r0:r1, :], wbh_ref[:, c0:c1]))

    def chunk_factors(c):
        r0 = c * ch
        slot = c % 2
        q_raw = qfig_scr[pl.ds(r0, ch), 0:hw]
        f_raw = qfig_scr[pl.ds(r0, ch), hw:2 * hw]
        q = _silu(q_raw)
        f = f_mid + f_amp * jnp.tanh(0.5 * f_raw)
        lf = jnp.log(f)
        kk = 1.0 - f
        hi = lf.astype(bf16)
        lo = (lf - hi.astype(f32)).astype(bf16)
        b = _dot(tril2, jnp.concatenate([hi, lo], axis=0))
        mid = b[ch // 2 - 1:ch // 2, :]
        b_last = b[ch - 1:ch, :]
        qh = q * jnp.exp(b - mid)
        kh = kk * jnp.exp(mid - b)
        fac_scr[slot, 0] = qh.astype(bf16)
        fac_scr[slot, 1] = kh.astype(bf16)
        fac_scr[slot, 2] = (qh * jnp.exp(mid)).astype(bf16)
        fac_scr[slot, 3] = (kh * jnp.exp(b_last - mid)).astype(bf16)
        fac_scr[slot, 4] = qfig_scr[pl.ds(r0, ch), 2 * hw:3 * hw].astype(bf16)
        el_scr[slot] = jnp.exp(b_last)

    def chunk_matmuls(c):
        r0 = c * ch
        slot = c % 2
        heads = [slice(h * HG_DK, (h + 1) * HG_DK) for h in range(HG_HEADS)]
        scs = [jnp.where(causal, _dot_nt(fac_scr[slot, 0, :, hs], fac_scr[slot, 1, :, hs]), 0.0).astype(bf16)
               for hs in heads]
        e_last = el_scr[slot]
        outs = []
        for h, hs in enumerate(heads):
            st = st_scr[h]
            vb = fac_scr[slot, 4, :, hs]
            o = _dot(scs[h], vb) + _dot_nt(fac_scr[slot, 2, :, hs], st.astype(bf16))
            st_scr[h] = st * e_last[:, hs] + _dot_tn(vb, fac_scr[slot, 3, :, hs])
            outs.append(_rms(o))
        g_raw = qfig_scr[pl.ds(r0, ch), 3 * hw:4 * hw]
        on = jnp.concatenate(outs, axis=1) * hgn * _silu(g_raw)
        yhg_scr[pl.ds(r0, ch), :] = on.astype(bf16)

    chunk_factors(0)
    for c in range(n_chunks):
        if c + 1 < n_chunks:
            chunk_factors(c + 1)
        chunk_matmuls(c)
        if c < n_side:
            project(half, tile, c * hw, (c + 1) * hw)
        else:
            j = c - n_side
            merge_hg(0, half, j * (dm // n_side), (j + 1) * (dm // n_side))
    for r0 in (0, half):
        if r0:
            merge_hg(half, tile, 0, dm)
        mo = _dot(m_scr[r0:r0 + half, :].astype(bf16), wout_ref[...])
        y_ref[0, r0:r0 + half, :] = x_ref[0, r0:r0 + half, :] + _rms(mo) * gpost_ref[...]

    @pl.when(t_idx == n_t - 1)
    def _():
        s5r_ref[0] = cr_scr[...]
        s5i_ref[0] = ci_scr[...]
        for h in range(HG_HEADS):
            hgs_ref[0, h] = st_scr[h].T


def _const_spec(shape):
    nd = len(shape)
    return pl.BlockSpec(shape, lambda *_: (0,) * nd, pipeline_mode=pl.Buffered(1))


def _mixer_call(x, w):
    bsz, seq, dm = x.shape
    tile = min(PROMPT_TILE, seq)
    assert seq % tile == 0 and tile % HG_CHUNK == 0
    rows = tile // S5_CHUNK
    assert rows & (rows - 1) == 0 and w["apr"].shape[0] >= int(math.log2(rows))
    s5w = w["d"].shape[1]
    n_cols = s5w // LANES
    n_pairs = w["w1"].shape[0]
    sdim = w["apr"].shape[1]
    consts = [w[k] for k in ("gpre", "win", "bin", "w1", "w2", "apr", "api", "d", "wglu", "bglu",
                             "lbrow", "hgnrow", "wbs", "wbh", "wout", "gpost")]
    in_specs = [pl.BlockSpec((1, tile, dm), lambda b, t: (b, t, 0))]
    in_specs += [_const_spec(c.shape) for c in consts]
    out_shape = (jax.ShapeDtypeStruct((bsz, seq, dm), f32),
                 jax.ShapeDtypeStruct((bsz, 1, sdim), f32),
                 jax.ShapeDtypeStruct((bsz, 1, sdim), f32),
                 jax.ShapeDtypeStruct((bsz, HG_HEADS, HG_DK, HG_DK), f32))
    out_specs = (pl.BlockSpec((1, tile, dm), lambda b, t: (b, t, 0)),
                 pl.BlockSpec((1, 1, sdim), lambda b, t: (b, 0, 0)),
                 pl.BlockSpec((1, 1, sdim), lambda b, t: (b, 0, 0)),
                 pl.BlockSpec((1, HG_HEADS, HG_DK, HG_DK), lambda b, t: (b, 0, 0, 0)))
    scratch = [
        pltpu.VMEM((tile, dm), bf16),
        pltpu.VMEM((n_cols, tile, LANES), f32),
        pltpu.VMEM((n_cols, tile, LANES), f32),
        pltpu.VMEM((n_pairs, rows, 2 * LANES), bf16),
        pltpu.VMEM((n_pairs, rows, 2 * LANES), f32),
        pltpu.VMEM((rows, sdim), f32),
        pltpu.VMEM((rows, sdim), f32),
        pltpu.VMEM((1, sdim), f32),
        pltpu.VMEM((1, sdim), f32),
        pltpu.VMEM((HG_HEADS, HG_DK, HG_DK), f32),
        pltpu.VMEM((tile, HG_HEADS * 4 * HG_DK), f32),
        pltpu.VMEM((tile, HG_HEADS * HG_DK), bf16),
        pltpu.VMEM((tile, dm), f32),
        pltpu.VMEM((2, 5, HG_CHUNK, HG_HEADS * HG_DK), bf16),
        pltpu.VMEM((2, 1, HG_HEADS * HG_DK), f32),
        pltpu.VMEM((tile, dm), f32),
    ]
    return pl.pallas_call(
        _mixer_kernel,
        grid=(bsz, seq // tile),
        in_specs=in_specs,
        out_specs=out_specs,
        out_shape=out_shape,
        scratch_shapes=scratch,
        compiler_params=pltpu.CompilerParams(
            dimension_semantics=("arbitrary", "arbitrary"), vmem_limit_bytes=VMEM_LIMIT),
        name="prompt_mixer",
    )(x, *consts)


def _mlp_body(x, gpre_ref, wup_scr, wdn_scr, gpost_ref):
    h2 = (_rms(x) * gpre_ref[...]).astype(bf16)
    acc = None
    for c in range(wup_scr.shape[0]):
        a = _dot(h2, wup_scr[c])
        a = jnp.maximum(a, 0.0)
        part = _dot((a * a).astype(bf16), wdn_scr[c])
        acc = part if acc is None else acc + part
    return x + _rms(acc) * gpost_ref[...]


def _mlp_kernel(xp_ref, xs_ref, gpre_ref, wup_ref, wdn_ref, gpost_ref, yp_ref, ys_ref, wup_scr, wdn_scr):
    step = pl.program_id(0)
    n_cast = wup_scr.shape[0]
    last = pl.num_programs(0) - 1
    dm = gpre_ref.shape[1]

    @pl.when(step < n_cast)
    def _():
        wup_scr[step] = wup_ref[...].astype(bf16)
        wdn_scr[step] = wdn_ref[...].astype(bf16)

    @pl.when(jnp.logical_and(step >= n_cast, step < last))
    def _():
        sub = xp_ref.shape[0] // MLP_SUBTILES
        for r0 in range(0, xp_ref.shape[0], sub):
            yp_ref[r0:r0 + sub, :] = _mlp_body(xp_ref[r0:r0 + sub, :], gpre_ref, wup_scr, wdn_scr, gpost_ref)

    @pl.when(step == last)
    def _():
        n_seq = xs_ref.shape[0]
        steps = xs_ref.shape[1] // dm
        x = jnp.concatenate([xs_ref[:, t * dm:(t + 1) * dm] for t in range(steps)], axis=0)
        y = _mlp_body(x, gpre_ref, wup_scr, wdn_scr, gpost_ref)
        for t in range(steps):
            ys_ref[:, t * dm:(t + 1) * dm] = y[t * n_seq:(t + 1) * n_seq, :]


def _mlp_call(xp2d, xs_sm, w):
    n, dm = xp2d.shape
    tile = min(MLP_TILE, n)
    assert n % tile == 0
    n_tiles = n // tile
    wup, wdn = w["wup_f32"], w["wdn_f32"]
    dff = wup.shape[1]
    fc = MLP_FCHUNK
    assert dff % fc == 0
    n_cast = dff // fc
    blk = lambda s: jnp.minimum(s, n_cast - 1)
    row = lambda s: jnp.clip(s - n_cast, 0, n_tiles - 1)
    return pl.pallas_call(
        _mlp_kernel,
        grid=(n_cast + n_tiles + 1,),
        in_specs=[pl.BlockSpec((tile, dm), lambda s: (row(s), 0)),
                  _const_spec(xs_sm.shape),
                  _const_spec(w["g2pre"].shape),
                  pl.BlockSpec((dm, fc), lambda s: (0, blk(s))),
                  pl.BlockSpec((fc, dm), lambda s: (blk(s), 0)),
                  _const_spec(w["g2post"].shape)],
        out_specs=(pl.BlockSpec((tile, dm), lambda s: (row(s), 0)),
                   pl.BlockSpec(xs_sm.shape, lambda s: (0, 0))),
        out_shape=(jax.ShapeDtypeStruct((n, dm), f32), jax.ShapeDtypeStruct(xs_sm.shape, f32)),
        scratch_shapes=[pltpu.VMEM((n_cast, dm, fc), bf16), pltpu.VMEM((n_cast, fc, dm), bf16)],
        compiler_params=pltpu.CompilerParams(
            dimension_semantics=("arbitrary",), vmem_limit_bytes=VMEM_LIMIT),
        name="mlp",
    )(xp2d, xs_sm, w["g2pre"], wup, wdn, w["g2post"])


def _sample_pre_kernel(x_ref, h0r_ref, h0i_ref, gpre_ref, win_ref, bin_ref, bblk_ref, cblkt_ref,
                       abr_ref, abi_ref, d_ref, wglu_ref, bglu_ref, lb_ref, hgn_ref, wbs_ref,
                       m5_ref, ghg_ref, gn_ref, qt_ref, cols_ref, v_ref, oin_ref,
                       s5r_ref, s5i_ref):
    n_seq = h0r_ref.shape[0]
    dm = gpre_ref.shape[1]
    steps = x_ref.shape[1] // dm
    sdim = h0r_ref.shape[1]
    hw = HG_HEADS * HG_DK
    s5w = d_ref.shape[1]
    x = jnp.concatenate([x_ref[:, t * dm:(t + 1) * dm] for t in range(steps)], axis=0)
    hb = (_rms(x) * gpre_ref[...]).astype(bf16)

    u = _dot(hb, win_ref[:, :s5w]) + bin_ref[:, :s5w]
    bu = _dot(u.astype(bf16), bblk_ref[...])
    hr = h0r_ref[...]
    hi = h0i_ref[...]
    a_r = abr_ref[...]
    a_i = abi_ref[...]
    hs = []
    for t in range(steps):
        sl = slice(t * n_seq, (t + 1) * n_seq)
        hr, hi = (a_r * hr - a_i * hi + bu[sl, :sdim], a_r * hi + a_i * hr + bu[sl, sdim:])
        hs.append(jnp.concatenate([hr, hi], axis=1).astype(bf16))
    s5r_ref[...] = hr
    s5i_ref[...] = hi
    y = _dot_nt(jnp.concatenate(hs, axis=0), cblkt_ref[...])
    y = _gelu_tanh(y + d_ref[...] * u)
    z = _dot(y.astype(bf16), wglu_ref[...]) + bglu_ref[...]
    ys5 = (y * _sigmoid(z)).astype(bf16)
    g0 = s5w + 4 * hw
    def put_steps(ref, val):
        wcol = val.shape[1]
        for t in range(steps):
            ref[:, t * wcol:(t + 1) * wcol] = val[t * n_seq:(t + 1) * n_seq, :]

    gates = _dot(hb, win_ref[:, g0:]) + bin_ref[:, g0:]
    put_steps(m5_ref, _sigmoid(gates[:, :dm]) * _dot(ys5, wbs_ref[...]))
    put_steps(ghg_ref, _sigmoid(gates[:, dm:]))

    def proj(i):
        c0 = s5w + i * hw
        return _dot(hb, win_ref[:, c0:c0 + hw]) + bin_ref[:, c0:c0 + hw]

    q_raw = proj(0)
    f_raw = proj(1)
    v = proj(2)
    g_raw = proj(3)
    q = q_raw * _sigmoid(q_raw)
    lb = lb_ref[...]
    f = lb + (1.0 - lb) * _sigmoid(f_raw)
    lf = jnp.log(f)
    kk = 1.0 - f
    put_steps(gn_ref, g_raw * _sigmoid(g_raw) * hgn_ref[...])
    bs = []
    acc = None
    for t in range(steps):
        sl = slice(t * n_seq, (t + 1) * n_seq)
        acc = lf[sl] if acc is None else acc + lf[sl]
        bs.append(acc)
    b_last = bs[-1]

    def put(ref, slot, val):
        for h in range(HG_HEADS):
            ref[pl.ds(h * 8 + slot, n_seq, stride=HG_HEADS * 8), :] = val[:, h * HG_DK:(h + 1) * HG_DK]

    for ref in (qt_ref, cols_ref, v_ref):
        ref[...] = jnp.zeros_like(ref)
    put(cols_ref, steps, jnp.exp(b_last))
    for t in range(steps):
        sl = slice(t * n_seq, (t + 1) * n_seq)
        put(qt_ref, t, q[sl] * jnp.exp(bs[t]))
        put(cols_ref, t, kk[sl] * jnp.exp(b_last - bs[t]))
        put(v_ref, t, v[sl])
        o_t = None
        for s in range(t + 1):
            sls = slice(s * n_seq, (s + 1) * n_seq)
            prod = q[sl] * kk[sls] * jnp.exp(bs[t] - bs[s])
            parts = []
            for h in range(HG_HEADS):
                hs = slice(h * HG_DK, (h + 1) * HG_DK)
                wgt = jnp.sum(prod[:, hs], axis=1, keepdims=True)
                parts.append(wgt * v[sls, hs])
            term = jnp.concatenate(parts, axis=1)
            o_t = term if o_t is None else o_t + term
        oin_ref[:, t * hw:(t + 1) * hw] = o_t


def _sample_pre_call(x_sm, h0r, h0i, w):
    n_seq, sdim = h0r.shape
    dm = w["gpre"].shape[1]
    steps = x_sm.shape[1] // dm
    hw = HG_HEADS * HG_DK
    nb = SAMPLE_PRE_BLOCK
    assert n_seq % nb == 0
    consts = [w[k] for k in ("gpre", "win", "bin", "bblk", "cblkt", "abr", "abi", "d",
                             "wglu", "bglu", "lbrow", "hgnrow", "wbs")]
    slots = HG_HEADS * 8
    shapes = ((1, steps * dm),
              (1, steps * dm),
              (1, steps * hw),
              (slots, HG_DK),
              (slots, HG_DK),
              (slots, HG_DK),
              (1, steps * hw),
              (1, sdim), (1, sdim))
    seq_spec = lambda shp: pl.BlockSpec((nb * shp[0], shp[1]), lambda i: (i, 0))
    return pl.pallas_call(
        _sample_pre_kernel,
        grid=(n_seq // nb,),
        in_specs=([seq_spec((1, steps * dm)), seq_spec((1, sdim)), seq_spec((1, sdim))]
                  + [_const_spec(c.shape) for c in consts]),
        out_specs=tuple(seq_spec(shp) for shp in shapes),
        out_shape=tuple(jax.ShapeDtypeStruct((n_seq * shp[0], shp[1]), f32) for shp in shapes),
        compiler_params=pltpu.CompilerParams(
            dimension_semantics=("arbitrary",), vmem_limit_bytes=VMEM_LIMIT),
        name="sample_pre",
    )(x_sm, h0r, h0i, *consts)


def _sample_state_kernel(s0_ref, qt_ref, v_ref, cols_ref, snew_ref, oint_ref):
    nb = s0_ref.shape[0]
    nrow = cols_ref.shape[1]
    hw = HG_HEADS * HG_DK
    row_head = lax.broadcasted_iota(jnp.int32, (nrow, hw), 0) // 8
    col_head = lax.broadcasted_iota(jnp.int32, (nrow, hw), 1) // HG_DK
    diag = row_head == col_head
    pad = jnp.zeros((LANES - nrow, HG_DK), f32)
    for i in range(nb):
        xt = jnp.concatenate([cols_ref[i], pad], axis=0).T
        vbd = jnp.where(diag, jnp.concatenate([v_ref[i]] * HG_HEADS, axis=1), 0.0)
        vbd = jnp.concatenate([vbd, jnp.zeros((LANES - nrow, hw), f32)], axis=0).astype(bf16)
        ds = _dot(xt.astype(bf16), vbd)
        for h in range(HG_HEADS):
            s0 = s0_ref[i, h]
            oint_ref[i, h] = _dot(qt_ref[i, h].astype(bf16), s0.astype(bf16))
            snew_ref[i, h] = xt[:, h * 8 + 4:h * 8 + 5] * s0 + ds[:, h * HG_DK:(h + 1) * HG_DK]


def _sample_state_call(s0, qt8, v8, cols):
    n_seq = s0.shape[0]
    nb = SAMPLE_SEQ_BLOCK
    blk4 = (nb, HG_HEADS, 8, HG_DK)
    return pl.pallas_call(
        _sample_state_kernel,
        grid=(n_seq // nb,),
        in_specs=[pl.BlockSpec((nb, HG_HEADS, HG_DK, HG_DK), lambda i: (i, 0, 0, 0)),
                  pl.BlockSpec(blk4, lambda i: (i, 0, 0, 0)),
                  pl.BlockSpec((nb, HG_HEADS * 8, HG_DK), lambda i: (i, 0, 0)),
                  pl.BlockSpec((nb, HG_HEADS * 8, HG_DK), lambda i: (i, 0, 0))],
        out_specs=(pl.BlockSpec((nb, HG_HEADS, HG_DK, HG_DK), lambda i: (i, 0, 0, 0)),
                   pl.BlockSpec(blk4, lambda i: (i, 0, 0, 0))),
        out_shape=(jax.ShapeDtypeStruct(s0.shape, f32),
                   jax.ShapeDtypeStruct((n_seq, HG_HEADS, 8, HG_DK), f32)),
        compiler_params=pltpu.CompilerParams(
            dimension_semantics=("arbitrary",), vmem_limit_bytes=VMEM_LIMIT),
        name="sample_state",
    )(s0, qt8.reshape((n_seq,) + blk4[1:]), v8.reshape(n_seq, HG_HEADS * 8, HG_DK),
      cols.reshape(n_seq, HG_HEADS * 8, HG_DK))


def _sample_post_kernel(x_ref, oint_ref, oin_ref, gn_ref, m5_ref, ghg_ref, wbh_ref, wout_ref, gpost_ref,
                        y_ref):
    n_seq = x_ref.shape[0]
    dm = gpost_ref.shape[1]
    steps = x_ref.shape[1] // dm
    hw = HG_HEADS * HG_DK

    def steps_major(ref, wcol):
        return jnp.concatenate([ref[:, t * wcol:(t + 1) * wcol] for t in range(steps)], axis=0)

    x = steps_major(x_ref, dm)
    oint = jnp.concatenate(
        [jnp.concatenate([oint_ref[pl.ds(h * 8 + t, n_seq, stride=HG_HEADS * 8), :] for h in range(HG_HEADS)],
                         axis=1) for t in range(steps)], axis=0)
    o = oint + steps_major(oin_ref, hw)
    parts = []
    for h in range(HG_HEADS):
        hs = slice(h * HG_DK, (h + 1) * HG_DK)
        parts.append(_rms(o[:, hs]))
    yhg = (jnp.concatenate(parts, axis=1) * steps_major(gn_ref, hw)).astype(bf16)
    merged = steps_major(m5_ref, dm) + steps_major(ghg_ref, dm) * _dot(yhg, wbh_ref[...])
    mo = _dot(merged.astype(bf16), wout_ref[...])
    y = x + _rms(mo) * gpost_ref[...]
    for t in range(steps):
        y_ref[:, t * dm:(t + 1) * dm] = y[t * n_seq:(t + 1) * n_seq, :]


def _sample_post_call(x_sm, oint, oin, gn, m5, ghg, w):
    ins = [x_sm, oint, oin, gn, m5, ghg] + [w[k] for k in ("wbh", "wout", "gpost")]
    return pl.pallas_call(
        _sample_post_kernel,
        out_shape=jax.ShapeDtypeStruct(x_sm.shape, f32),
        compiler_params=pltpu.CompilerParams(vmem_limit_bytes=VMEM_LIMIT),
        name="sample_post",
    )(*ins)


def _prep_weights(l, norm_mix_pre, norm_mix_post, norm_mlp_pre, norm_mlp_post, w_in, b_in,
                  s5_a_re, s5_a_im, s5_log_dt, s5_b_re, s5_b_im, s5_c_re, s5_c_im, s5_d, s5_w_glu,
                  s5_b_glu, hg_lb_logits, hg_norm, w_br_s5, w_br_hg, w_out, w_up, w_down, n_pow):
    hp = lax.Precision.HIGHEST
    dm = w_in.shape[1]
    s5w = s5_d.shape[1]
    kw = HG_HEADS * HG_DK
    n_groups = s5w // S5_GROUP
    n_pairs = n_groups // 2
    w = {}
    row = lambda a: a.astype(f32).reshape(1, -1)
    w["gpre"], w["gpost"] = row(norm_mix_pre[l]), row(norm_mix_post[l])
    w["g2pre"], w["g2post"] = row(norm_mlp_pre[l]), row(norm_mlp_post[l])
    w["win"], w["bin"] = w_in[l].astype(bf16), row(b_in[l])
    lb_all = jnp.cumsum(jax.nn.softmax(hg_lb_logits.astype(f32), axis=0), axis=0)
    w["lbrow"] = row(lb_all[l])
    w["hgnrow"] = row(jnp.tile(hg_norm[l].astype(f32), HG_HEADS))
    w["d"] = row(s5_d[l])
    w["wglu"], w["bglu"] = s5_w_glu[l].astype(bf16), row(s5_b_glu[l])
    w["wbs"], w["wbh"], w["wout"] = w_br_s5[l].astype(bf16), w_br_hg[l].astype(bf16), w_out[l].astype(bf16)
    w["wup_f32"], w["wdn_f32"] = w_up[l], w_down[l]

    a_re, a_im = s5_a_re[l].astype(f32), s5_a_im[l].astype(f32)
    dt = jnp.exp(s5_log_dt[l].astype(f32))[:, None]
    mag = jnp.exp(dt * a_re)
    abr, abi = mag * jnp.cos(dt * a_im), mag * jnp.sin(dt * a_im)
    den = a_re * a_re + a_im * a_im
    nr, ni = abr - 1.0, abi
    cfr, cfi = (nr * a_re + ni * a_im) / den, (ni * a_re - nr * a_im) / den
    b_re, b_im = s5_b_re[l].astype(f32), s5_b_im[l].astype(f32)
    bbr = cfr[..., None] * b_re - cfi[..., None] * b_im
    bbi = cfr[..., None] * b_im + cfi[..., None] * b_re
    c_re, c_im = s5_c_re[l].astype(f32), s5_c_im[l].astype(f32)
    w["abr"], w["abi"] = abr.reshape(1, -1), abi.reshape(1, -1)

    lam_r, lam_i = dt * a_re, dt * a_im
    grp = S5_GROUP
    n_state = a_re.shape[1]
    jj = jnp.arange(S5_CHUNK + 1, dtype=f32)[None, :, None]
    pmag = jnp.exp(jj * lam_r[:, None, :])
    pw_r, pw_i = pmag * jnp.cos(jj * lam_i[:, None, :]), pmag * jnp.sin(jj * lam_i[:, None, :])
    bt = jnp.transpose(jnp.stack([bbr, bbi]), (0, 1, 3, 2))
    bt_r, bt_i = bt[0], bt[1]
    c4r, c4i = c_re[:, None], c_im[:, None]

    def cmul(pr_, pi_, xr, xi):
        pr_, pi_ = pr_[:, :, None, :], pi_[:, :, None, :]
        return pr_ * xr - pi_ * xi, pr_ * xi + pi_ * xr

    rows128 = lambda t: t.reshape(n_groups, LANES, t.shape[-1])
    car_t, cai_t = cmul(pw_r[:, 1:], pw_i[:, 1:], c4r, c4i)
    ca = jnp.transpose(jnp.stack([rows128(car_t), -rows128(cai_t)]), (0, 1, 3, 2))
    ber, bei = cmul(pw_r[:, S5_CHUNK - 1::-1], pw_i[:, S5_CHUNK - 1::-1], bt_r[:, None], bt_i[:, None])
    ber, bei = rows128(ber), rows128(bei)
    zpad = jnp.zeros((n_groups, S5_CHUNK - 1, n_state), f32)
    lag_r = jnp.concatenate([zpad, pw_r[:, :S5_CHUNK]], axis=1)
    lag_i = jnp.concatenate([zpad, pw_i[:, :S5_CHUNK]], axis=1)
    cp_r, cp_i = cmul(lag_r, lag_i, c4r, c4i)
    n_blk = 2 * S5_CHUNK - 1
    cp_all = jnp.concatenate([cp_r, -cp_i], axis=3).reshape(n_groups, n_blk * grp, 2 * n_state)
    rpad_t = jnp.einsum("gmn,gnc->gmc", cp_all, jnp.concatenate([bbr, bbi], axis=1), precision=hp)
    toe_t = jnp.concatenate([rpad_t[:, (S5_CHUNK - 1 - s) * grp:(S5_CHUNK - 1 - s) * grp + LANES, :]
                             for s in range(S5_CHUNK)], axis=2)
    toe = jnp.transpose(toe_t, (0, 2, 1))

    def pair_diag(m):
        g, r, c = m.shape
        m = m.reshape(g // 2, 2, r, c)
        z = jnp.zeros_like(m[:, 0])
        return jnp.concatenate([jnp.concatenate([m[:, 0], z], axis=2),
                                jnp.concatenate([z, m[:, 1]], axis=2)], axis=1)

    w["w1"] = jnp.concatenate([pair_diag(toe), pair_diag(ber), pair_diag(bei)], axis=2).astype(bf16)
    w["w2"] = jnp.concatenate([pair_diag(ca[0]), pair_diag(ca[1])], axis=1).astype(bf16)
    qr, qi = pw_r[:, S5_CHUNK], pw_i[:, S5_CHUNK]
    aps_r, aps_i = [], []
    for _ in range(n_pow):
        aps_r.append(qr.reshape(-1))
        aps_i.append(qi.reshape(-1))
        qr, qi = qr * qr - qi * qi, 2.0 * qr * qi
    w["apr"], w["api"] = jnp.stack(aps_r), jnp.stack(aps_i)

    same_group = np.arange(s5w)[:, None] // grp == np.arange(n_groups * n_state)[None, :] // n_state
    lane_tile = np.tile(np.eye(n_state, dtype=np.float32), (1, n_groups))
    src = jnp.stack([bt_r, bt_i, c_re, -c_im]).reshape(4, s5w, n_state)
    tiled = jnp.einsum("zxn,nl->zxl", src, jnp.asarray(lane_tile), precision=hp,
                       preferred_element_type=f32).astype(bf16) * jnp.asarray(same_group, bf16)
    w["bblk"] = jnp.concatenate([tiled[0], tiled[1]], axis=1)
    w["cblkt"] = jnp.concatenate([tiled[2], tiled[3]], axis=1)
    return w


def kernel(x_prompt, x_sample, state_s5_re, state_s5_im, state_hg, norm_mix_pre, norm_mix_post,
           norm_mlp_pre, norm_mlp_post, w_in, b_in, s5_a_re, s5_a_im, s5_log_dt, s5_b_re, s5_b_im,
           s5_c_re, s5_c_im, s5_d, s5_w_glu, s5_b_glu, hg_lb_logits, hg_norm, w_br_s5, w_br_hg,
           w_out, w_up, w_down):
    depth = w_in.shape[0]
    bsz, seq, dm = x_prompt.shape
    n_seq, steps, _ = x_sample.shape
    n_groups, n_state = s5_a_re.shape[1], s5_a_re.shape[2]
    hw = HG_HEADS * HG_DK
    assert depth == 1 and steps == 4 and n_state == S5_STATE
    n_pow = max(1, int(math.log2(min(PROMPT_TILE, seq) // S5_CHUNK)))
    xp, xs = x_prompt, x_sample
    outs = [[] for _ in range(6)]
    for l in range(depth):
        w = _prep_weights(l, norm_mix_pre, norm_mix_post, norm_mlp_pre, norm_mlp_post, w_in, b_in,
                          s5_a_re, s5_a_im, s5_log_dt, s5_b_re, s5_b_im, s5_c_re, s5_c_im, s5_d,
                          s5_w_glu, s5_b_glu, hg_lb_logits, hg_norm, w_br_s5, w_br_hg, w_out, w_up,
                          w_down, n_pow)
        x1, p_re, p_im, p_hg = _mixer_call(xp, w)
        outs[0].append(p_re.reshape(bsz, n_groups, n_state))
        outs[1].append(p_im.reshape(bsz, n_groups, n_state))
        outs[2].append(p_hg)
        x_sm = xs.reshape(n_seq, steps * dm)
        h0r = state_s5_re[l].reshape(n_seq, n_groups * n_state)
        h0i = state_s5_im[l].reshape(n_seq, n_groups * n_state)
        m5, ghg, gn, qt8, cols, v8, oin, s_re, s_im = _sample_pre_call(x_sm, h0r, h0i, w)
        s_hg, oint = _sample_state_call(state_hg[l], qt8, v8, cols)
        x1_sm = _sample_post_call(x_sm, oint.reshape(n_seq * HG_HEADS * 8, HG_DK), oin, gn, m5, ghg, w)
        yp, y_sm = _mlp_call(x1.reshape(bsz * seq, dm), x1_sm, w)
        xp = yp.reshape(bsz, seq, dm)
        xs = y_sm.reshape(n_seq, steps, dm)
        outs[3].append(s_re.reshape(n_seq, n_groups, n_state))
        outs[4].append(s_im.reshape(n_seq, n_groups, n_state))
        outs[5].append(s_hg)
    return (xp, xs) + tuple(o[0][None] for o in outs)
```

```python
import math

import jax
import jax.numpy as jnp
import numpy as np
from jax import lax
from jax.experimental import pallas as pl
from jax.experimental.pallas import tpu as pltpu

f32 = jnp.float32
bf16 = jnp.bfloat16

NORM_EPS = 1e-6
S5_GROUP = 16
S5_STATE = 64
S5_CHUNK = 8
HG_HEADS = 8
HG_DK = 128
HG_CHUNK = 64
LANES = 128
VMEM_LIMIT = 56 * 1024 * 1024
PROMPT_TILE = 512
MLP_TILE = 1024
MLP_SUBTILES = 2
MLP_FCHUNK = 512
SAMPLE_SEQ_BLOCK = 16
SAMPLE_PRE_BLOCK = 64


def _sigmoid(x):
    return 0.5 * (jnp.tanh(0.5 * x) + 1.0)


def _silu(x):
    h = 0.5 * x
    return h + h * jnp.tanh(h)


def _gelu_tanh(x):
    c = math.sqrt(2.0 / math.pi)
    return 0.5 * x * (1.0 + jnp.tanh(c * (x + 0.044715 * (x * x * x))))


def _rms(x):
    return x * lax.rsqrt(jnp.mean(x * x, axis=-1, keepdims=True) + NORM_EPS)


def _dot(a, b):
    return jnp.dot(a, b, preferred_element_type=f32)


def _dot_nt(a, b):
    return lax.dot_general(a, b, (((1,), (1,)), ((), ())), preferred_element_type=f32)


def _dot_tn(a, b):
    return lax.dot_general(a, b, (((0,), (0,)), ((), ())), preferred_element_type=f32)


def _block_transpose8(xs, blk):
    xs = list(xs)
    for d in (4, 2, 1):
        keep = (blk & d) == 0
        for i in range(8):
            if i & d == 0:
                a, b = xs[i], xs[i + d]
                xs[i] = jnp.where(keep, a, pltpu.roll(b, d * S5_GROUP, 1))
                xs[i + d] = jnp.where(keep, pltpu.roll(a, LANES - d * S5_GROUP, 1), b)
    return xs


def _mixer_kernel(x_ref, gpre_ref, win_ref, bin_ref, w1_ref, w2_ref, apr_ref, api_ref, d_ref,
                  wglu_ref, bglu_ref, lb_ref, hgn_ref, wbs_ref, wbh_ref, wout_ref, gpost_ref,
                  y_ref, s5r_ref, s5i_ref, hgs_ref,
                  hb_scr, u_scr, ys_scr, o_scr, yloc_scr, er_scr, ei_scr, cr_scr, ci_scr,
                  st_scr, qfig_scr, yhg_scr, m_scr, fac_scr, el_scr, ghg_scr):
    t_idx = pl.program_id(1)
    n_t = pl.num_programs(1)
    tile = x_ref.shape[1]
    rows = tile // S5_CHUNK
    n_pairs = w1_ref.shape[0]
    n_cols = u_scr.shape[0]
    hw = HG_HEADS * HG_DK
    s5w = n_cols * LANES
    g0 = s5w + 4 * hw

    @pl.when(t_idx == 0)
    def _():
        cr_scr[...] = jnp.zeros_like(cr_scr)
        ci_scr[...] = jnp.zeros_like(ci_scr)
        st_scr[...] = jnp.zeros_like(st_scr)

    x = x_ref[0]
    hb_scr[...] = (_rms(x) * gpre_ref[...]).astype(bf16)
    dm = wbs_ref.shape[1]
    half = tile // 2

    def project(r0, r1, c0, c1):
        qfig_scr[r0:r1, c0:c1] = (_dot(hb_scr[r0:r1, :], win_ref[:, s5w + c0:s5w + c1])
                                  + bin_ref[:, s5w + c0:s5w + c1])

    def gate_hg(j):
        c0, c1 = j * (dm // 4), (j + 1) * (dm // 4)
        ghg_scr[:, c0:c1] = _sigmoid(_dot(hb_scr[...], win_ref[:, g0 + dm + c0:g0 + dm + c1])
                                     + bin_ref[:, g0 + dm + c0:g0 + dm + c1])


    u = _dot(hb_scr[...], win_ref[:, :s5w]) + bin_ref[:, :s5w]
    for j in range(n_cols):
        u_scr[j] = u[:, j * LANES:(j + 1) * LANES]

    project(0, half, 0, hw)
    gate_hg(0)
    blk = lax.broadcasted_iota(jnp.int32, (rows, LANES), 1) // S5_GROUP
    for j in range(n_cols):
        xs = [u_scr[j, pl.ds(s, rows, stride=S5_CHUNK), :] for s in range(S5_CHUNK)]
        ys = _block_transpose8(xs, blk)
        for q in range(4):
            o_scr[j * 4 + q] = jnp.concatenate([ys[2 * q], ys[2 * q + 1]], axis=1).astype(bf16)

    project(0, half, hw, 2 * hw)
    gate_hg(1)
    for p in range(n_pairs):
        r1 = _dot(o_scr[p], w1_ref[p])
        yloc_scr[p] = r1[:, :2 * LANES]
        er_scr[:, p * LANES:(p + 1) * LANES] = r1[:, 2 * LANES:3 * LANES]
        ei_scr[:, p * LANES:(p + 1) * LANES] = r1[:, 3 * LANES:]

    er = er_scr[...]
    ei = ei_scr[...]
    row = lax.broadcasted_iota(jnp.int32, (rows, 1), 0)
    first = row == 0
    cr = cr_scr[...]
    ci = ci_scr[...]
    a_r = apr_ref[0:1, :]
    a_i = api_ref[0:1, :]
    project(0, half, 2 * hw, 3 * hw)
    gate_hg(2)
    er = er + jnp.where(first, a_r * cr - a_i * ci, 0.0)
    ei = ei + jnp.where(first, a_r * ci + a_i * cr, 0.0)
    for k in range(int(math.log2(rows))):
        d = 1 << k
        p_r = apr_ref[k:k + 1, :]
        p_i = api_ref[k:k + 1, :]
        s_r = pltpu.roll(er, d, 0)
        s_i = pltpu.roll(ei, d, 0)
        valid = row >= d
        er, ei = (er + jnp.where(valid, p_r * s_r - p_i * s_i, 0.0),
                  ei + jnp.where(valid, p_r * s_i + p_i * s_r, 0.0))
    hs_r = jnp.where(first, cr, pltpu.roll(er, 1, 0))
    hs_i = jnp.where(first, ci, pltpu.roll(ei, 1, 0))
    cr_scr[...] = er[rows - 1:rows, :]
    ci_scr[...] = ei[rows - 1:rows, :]

    project(0, half, 3 * hw, 4 * hw)
    gate_hg(3)
    for j in range(n_cols):
        halves = []
        for q in range(4):
            p = j * 4 + q
            hp = jnp.concatenate([hs_r[:, p * LANES:(p + 1) * LANES],
                                  hs_i[:, p * LANES:(p + 1) * LANES]], axis=1).astype(bf16)
            yp = yloc_scr[p] + _dot(hp, w2_ref[p])
            halves += [yp[:, :LANES], yp[:, LANES:]]
        zs = _block_transpose8(halves, blk)
        for s in range(S5_CHUNK):
            ys_scr[j, pl.ds(s, rows, stride=S5_CHUNK), :] = zs[s]

    y = jnp.concatenate([ys_scr[j] for j in range(n_cols)], axis=1)
    u = jnp.concatenate([u_scr[j] for j in range(n_cols)], axis=1)
    y = _gelu_tanh(y + d_ref[...] * u)
    z = _dot(y.astype(bf16), wglu_ref[...]) + bglu_ref[...]
    ys5 = (y * _sigmoid(z)).astype(bf16)
    gate_s5 = _sigmoid(_dot(hb_scr[...], win_ref[:, g0:g0 + dm]) + bin_ref[:, g0:g0 + dm])
    m_scr[...] = gate_s5 * _dot(ys5, wbs_ref[...])

    ch = HG_CHUNK
    ri = lax.broadcasted_iota(jnp.int32, (ch, ch), 0)
    ci_ = lax.broadcasted_iota(jnp.int32, (ch, ch), 1)
    causal = ci_ <= ri
    tril2 = jnp.concatenate([causal.astype(bf16)] * 2, axis=1)
    lb = lb_ref[...]
    f_mid = 0.5 * (1.0 + lb)
    f_amp = 0.5 * (1.0 - lb)
    hgn = hgn_ref[...]
    n_chunks = tile // ch
    n_side = n_chunks // 2
    assert n_side == 4 and dm % n_side == 0

    def merge_hg(r0, r1, c0, c1):
        m_scr[r0:r1, c0:c1] = (m_scr[r0:r1, c0:c1]
                               + ghg_scr[r0:r1, c0:c1] * _dot(yhg_scr[r0:r1, :], wbh_ref[:, c0:c1]))

    def chunk_factors(c):
        r0 = c * ch
        slot = c % 2
        q_raw = qfig_scr[pl.ds(r0, ch), 0:hw]
        f_raw = qfig_scr[pl.ds(r0, ch), hw:2 * hw]
        q = _silu(q_raw)
        f = f_mid + f_amp * jnp.tanh(0.5 * f_raw)
        lf = jnp.log(f)
        kk = 1.0 - f
        hi = lf.astype(bf16)
        lo = (lf - hi.astype(f32)).astype(bf16)
        b = _dot(tril2, jnp.concatenate([hi, lo], axis=0))
        mid = b[ch // 2 - 1:ch // 2, :]
        b_last = b[ch - 1:ch, :]
        qh = q * jnp.exp(b - mid)
        kh = kk * jnp.exp(mid - b)
        fac_scr[slot, 0] = qh.astype(bf16)
        fac_scr[slot, 1] = kh.astype(bf16)
        fac_scr[slot, 2] = (qh * jnp.exp(mid)).astype(bf16)
        fac_scr[slot, 3] = (kh * jnp.exp(b_last - mid)).astype(bf16)
        fac_scr[slot, 4] = qfig_scr[pl.ds(r0, ch), 2 * hw:3 * hw].astype(bf16)
        el_scr[slot] = jnp.exp(b_last)

    def chunk_matmuls(c):
        r0 = c * ch
        slot = c % 2
        heads = [slice(h * HG_DK, (h + 1) * HG_DK) for h in range(HG_HEADS)]
        scs = [jnp.where(causal, _dot_nt(fac_scr[slot, 0, :, hs], fac_scr[slot, 1, :, hs]), 0.0).astype(bf16)
               for hs in heads]
        e_last = el_scr[slot]
        outs = []
        for h, hs in enumerate(heads):
            st = st_scr[h]
            vb = fac_scr[slot, 4, :, hs]
            o = _dot(scs[h], vb) + _dot_nt(fac_scr[slot, 2, :, hs], st.astype(bf16))
            st_scr[h] = st * e_last[:, hs] + _dot_tn(vb, fac_scr[slot, 3, :, hs])
            outs.append(_rms(o))
        g_raw = qfig_scr[pl.ds(r0, ch), 3 * hw:4 * hw]
        on = jnp.concatenate(outs, axis=1) * hgn * _silu(g_raw)
        yhg_scr[pl.ds(r0, ch), :] = on.astype(bf16)

    def finish(r0, r1):
        mo = _dot(m_scr[r0:r1, :].astype(bf16), wout_ref[...])
        y_ref[0, r0:r1, :] = x_ref[0, r0:r1, :] + _rms(mo) * gpost_ref[...]

    quarter = half // 2
    side_work = {4: lambda: merge_hg(0, half, 0, dm // 2),
                 5: lambda: merge_hg(0, half, dm // 2, dm),
                 6: lambda: finish(0, half),
                 7: lambda: merge_hg(half, half + quarter, 0, dm)}
    chunk_factors(0)
    for c in range(n_chunks):
        if c + 1 < n_chunks:
            chunk_factors(c + 1)
        chunk_matmuls(c)
        if c < n_side:
            project(half, tile, c * hw, (c + 1) * hw)
        else:
            side_work[c]()
    merge_hg(half + quarter, tile, 0, dm)
    finish(half, tile)

    @pl.when(t_idx == n_t - 1)
    def _():
        s5r_ref[0] = cr_scr[...]
        s5i_ref[0] = ci_scr[...]
        for h in range(HG_HEADS):
            hgs_ref[0, h] = st_scr[h].T


def _const_spec(shape):
    nd = len(shape)
    return pl.BlockSpec(shape, lambda *_: (0,) * nd, pipeline_mode=pl.Buffered(1))


def _mixer_call(x, w):
    bsz, seq, dm = x.shape
    tile = min(PROMPT_TILE, seq)
    assert seq % tile == 0 and tile % HG_CHUNK == 0
    rows = tile // S5_CHUNK
    assert rows & (rows - 1) == 0 and w["apr"].shape[0] >= int(math.log2(rows))
    s5w = w["d"].shape[1]
    n_cols = s5w // LANES
    n_pairs = w["w1"].shape[0]
    sdim = w["apr"].shape[1]
    consts = [w[k] for k in ("gpre", "win", "bin", "w1", "w2", "apr", "api", "d", "wglu", "bglu",
                             "lbrow", "hgnrow", "wbs", "wbh", "wout", "gpost")]
    in_specs = [pl.BlockSpec((1, tile, dm), lambda b, t: (b, t, 0))]
    in_specs += [_const_spec(c.shape) for c in consts]
    out_shape = (jax.ShapeDtypeStruct((bsz, seq, dm), f32),
                 jax.ShapeDtypeStruct((bsz, 1, sdim), f32),
                 jax.ShapeDtypeStruct((bsz, 1, sdim), f32),
                 jax.ShapeDtypeStruct((bsz, HG_HEADS, HG_DK, HG_DK), f32))
    out_specs = (pl.BlockSpec((1, tile, dm), lambda b, t: (b, t, 0)),
                 pl.BlockSpec((1, 1, sdim), lambda b, t: (b, 0, 0)),
                 pl.BlockSpec((1, 1, sdim), lambda b, t: (b, 0, 0)),
                 pl.BlockSpec((1, HG_HEADS, HG_DK, HG_DK), lambda b, t: (b, 0, 0, 0)))
    scratch = [
        pltpu.VMEM((tile, dm), bf16),
        pltpu.VMEM((n_cols, tile, LANES), f32),
        pltpu.VMEM((n_cols, tile, LANES), f32),
        pltpu.VMEM((n_pairs, rows, 2 * LANES), bf16),
        pltpu.VMEM((n_pairs, rows, 2 * LANES), f32),
        pltpu.VMEM((rows, sdim), f32),
        pltpu.VMEM((rows, sdim), f32),
        pltpu.VMEM((1, sdim), f32),
        pltpu.VMEM((1, sdim), f32),
        pltpu.VMEM((HG_HEADS, HG_DK, HG_DK), f32),
        pltpu.VMEM((tile, HG_HEADS * 4 * HG_DK), f32),
        pltpu.VMEM((tile, HG_HEADS * HG_DK), bf16),
        pltpu.VMEM((tile, dm), f32),
        pltpu.VMEM((2, 5, HG_CHUNK, HG_HEADS * HG_DK), bf16),
        pltpu.VMEM((2, 1, HG_HEADS * HG_DK), f32),
        pltpu.VMEM((tile, dm), f32),
    ]
    return pl.pallas_call(
        _mixer_kernel,
        grid=(bsz, seq // tile),
        in_specs=in_specs,
        out_specs=out_specs,
        out_shape=out_shape,
        scratch_shapes=scratch,
        compiler_params=pltpu.CompilerParams(
            dimension_semantics=("arbitrary", "arbitrary"), vmem_limit_bytes=VMEM_LIMIT),
        name="prompt_mixer",
    )(x, *consts)


def _mlp_body(x, gpre_ref, wup_scr, wdn_scr, gpost_ref):
    h2 = (_rms(x) * gpre_ref[...]).astype(bf16)
    acc = None
    for c in range(wup_scr.shape[0]):
        a = _dot(h2, wup_scr[c])
        a = jnp.maximum(a, 0.0)
        part = _dot((a * a).astype(bf16), wdn_scr[c])
        acc = part if acc is None else acc + part
    return x + _rms(acc) * gpost_ref[...]


def _mlp_kernel(xp_ref, xs_ref, gpre_ref, wup_ref, wdn_ref, gpost_ref, yp_ref, ys_ref, wup_scr, wdn_scr):
    step = pl.program_id(0)
    n_cast = wup_scr.shape[0]
    last = pl.num_programs(0) - 1
    dm = gpre_ref.shape[1]

    @pl.when(step < n_cast)
    def _():
        wup_scr[step] = wup_ref[...].astype(bf16)
        wdn_scr[step] = wdn_ref[...].astype(bf16)

    @pl.when(jnp.logical_and(step >= n_cast, step < last))
    def _():
        sub = xp_ref.shape[0] // MLP_SUBTILES
        for r0 in range(0, xp_ref.shape[0], sub):
            yp_ref[r0:r0 + sub, :] = _mlp_body(xp_ref[r0:r0 + sub, :], gpre_ref, wup_scr, wdn_scr, gpost_ref)

    @pl.when(step == last)
    def _():
        n_seq = xs_ref.shape[0]
        steps = xs_ref.shape[1] // dm
        x = jnp.concatenate([xs_ref[:, t * dm:(t + 1) * dm] for t in range(steps)], axis=0)
        y = _mlp_body(x, gpre_ref, wup_scr, wdn_scr, gpost_ref)
        for t in range(steps):
            ys_ref[:, t * dm:(t + 1) * dm] = y[t * n_seq:(t + 1) * n_seq, :]


def _mlp_call(xp2d, xs_sm, w):
    n, dm = xp2d.shape
    tile = min(MLP_TILE, n)
    assert n % tile == 0
    n_tiles = n // tile
    wup, wdn = w["wup_f32"], w["wdn_f32"]
    dff = wup.shape[1]
    fc = MLP_FCHUNK
    assert dff % fc == 0
    n_cast = dff // fc
    blk = lambda s: jnp.minimum(s, n_cast - 1)
    row = lambda s: jnp.clip(s - n_cast, 0, n_tiles - 1)
    return pl.pallas_call(
        _mlp_kernel,
        grid=(n_cast + n_tiles + 1,),
        in_specs=[pl.BlockSpec((tile, dm), lambda s: (row(s), 0)),
                  _const_spec(xs_sm.shape),
                  _const_spec(w["g2pre"].shape),
                  pl.BlockSpec((dm, fc), lambda s: (0, blk(s))),
                  pl.BlockSpec((fc, dm), lambda s: (blk(s), 0)),
                  _const_spec(w["g2post"].shape)],
        out_specs=(pl.BlockSpec((tile, dm), lambda s: (row(s), 0)),
                   pl.BlockSpec(xs_sm.shape, lambda s: (0, 0))),
        out_shape=(jax.ShapeDtypeStruct((n, dm), f32), jax.ShapeDtypeStruct(xs_sm.shape, f32)),
        scratch_shapes=[pltpu.VMEM((n_cast, dm, fc), bf16), pltpu.VMEM((n_cast, fc, dm), bf16)],
        compiler_params=pltpu.CompilerParams(
            dimension_semantics=("arbitrary",), vmem_limit_bytes=VMEM_LIMIT),
        name="mlp",
    )(xp2d, xs_sm, w["g2pre"], wup, wdn, w["g2post"])


def _sample_pre_kernel(x_ref, h0r_ref, h0i_ref, gpre_ref, win_ref, bin_ref, bblk_ref, cblkt_ref,
                       abr_ref, abi_ref, d_ref, wglu_ref, bglu_ref, lb_ref, hgn_ref, wbs_ref,
                       m5_ref, ghg_ref, gn_ref, qt_ref, cols_ref, v_ref, oin_ref,
                       s5r_ref, s5i_ref):
    n_seq = h0r_ref.shape[0]
    dm = gpre_ref.shape[1]
    steps = x_ref.shape[1] // dm
    sdim = h0r_ref.shape[1]
    hw = HG_HEADS * HG_DK
    s5w = d_ref.shape[1]
    x = jnp.concatenate([x_ref[:, t * dm:(t + 1) * dm] for t in range(steps)], axis=0)
    hb = (_rms(x) * gpre_ref[...]).astype(bf16)

    u = _dot(hb, win_ref[:, :s5w]) + bin_ref[:, :s5w]
    bu = _dot(u.astype(bf16), bblk_ref[...])
    hr = h0r_ref[...]
    hi = h0i_ref[...]
    a_r = abr_ref[...]
    a_i = abi_ref[...]
    hs = []
    for t in range(steps):
        sl = slice(t * n_seq, (t + 1) * n_seq)
        hr, hi = (a_r * hr - a_i * hi + bu[sl, :sdim], a_r * hi + a_i * hr + bu[sl, sdim:])
        hs.append(jnp.concatenate([hr, hi], axis=1).astype(bf16))
    s5r_ref[...] = hr
    s5i_ref[...] = hi
    y = _dot_nt(jnp.concatenate(hs, axis=0), cblkt_ref[...])
    y = _gelu_tanh(y + d_ref[...] * u)
    z = _dot(y.astype(bf16), wglu_ref[...]) + bglu_ref[...]
    ys5 = (y * _sigmoid(z)).astype(bf16)
    g0 = s5w + 4 * hw
    def put_steps(ref, val):
        wcol = val.shape[1]
        for t in range(steps):
            ref[:, t * wcol:(t + 1) * wcol] = val[t * n_seq:(t + 1) * n_seq, :]

    gates = _dot(hb, win_ref[:, g0:]) + bin_ref[:, g0:]
    put_steps(m5_ref, _sigmoid(gates[:, :dm]) * _dot(ys5, wbs_ref[...]))
    put_steps(ghg_ref, _sigmoid(gates[:, dm:]))

    def proj(i):
        c0 = s5w + i * hw
        return _dot(hb, win_ref[:, c0:c0 + hw]) + bin_ref[:, c0:c0 + hw]

    q_raw = proj(0)
    f_raw = proj(1)
    v = proj(2)
    g_raw = proj(3)
    q = _silu(q_raw)
    lb = lb_ref[...]
    f = 0.5 * (1.0 + lb) + 0.5 * (1.0 - lb) * jnp.tanh(0.5 * f_raw)
    lf = jnp.log(f)
    kk = 1.0 - f
    put_steps(gn_ref, _silu(g_raw) * hgn_ref[...])
    bs = []
    acc = None
    for t in range(steps):
        sl = slice(t * n_seq, (t + 1) * n_seq)
        acc = lf[sl] if acc is None else acc + lf[sl]
        bs.append(acc)
    b_last = bs[-1]

    def put(ref, slot, val):
        for h in range(HG_HEADS):
            ref[pl.ds(h * 8 + slot, n_seq, stride=HG_HEADS * 8), :] = val[:, h * HG_DK:(h + 1) * HG_DK]

    for ref in (qt_ref, cols_ref, v_ref):
        ref[...] = jnp.zeros_like(ref)
    put(cols_ref, steps, jnp.exp(b_last))
    for t in range(steps):
        sl = slice(t * n_seq, (t + 1) * n_seq)
        put(qt_ref, t, q[sl] * jnp.exp(bs[t]))
        put(cols_ref, t, kk[sl] * jnp.exp(b_last - bs[t]))
        put(v_ref, t, v[sl])
        o_t = None
        for s in range(t + 1):
            sls = slice(s * n_seq, (s + 1) * n_seq)
            prod = q[sl] * kk[sls] * jnp.exp(bs[t] - bs[s])
            parts = []
            for h in range(HG_HEADS):
                hs = slice(h * HG_DK, (h + 1) * HG_DK)
                wgt = jnp.sum(prod[:, hs], axis=1, keepdims=True)
                parts.append(wgt * v[sls, hs])
            term = jnp.concatenate(parts, axis=1)
            o_t = term if o_t is None else o_t + term
        oin_ref[:, t * hw:(t + 1) * hw] = o_t


def _sample_pre_call(x_sm, h0r, h0i, w):
    n_seq, sdim = h0r.shape
    dm = w["gpre"].shape[1]
    steps = x_sm.shape[1] // dm
    hw = HG_HEADS * HG_DK
    nb = SAMPLE_PRE_BLOCK
    assert n_seq % nb == 0
    consts = [w[k] for k in ("gpre", "win", "bin", "bblk", "cblkt", "abr", "abi", "d",
                             "wglu", "bglu", "lbrow", "hgnrow", "wbs")]
    slots = HG_HEADS * 8
    shapes = ((1, steps * dm),
              (1, steps * dm),
              (1, steps * hw),
              (slots, HG_DK),
              (slots, HG_DK),
              (slots, HG_DK),
              (1, steps * hw),
              (1, sdim), (1, sdim))
    seq_spec = lambda shp: pl.BlockSpec((nb * shp[0], shp[1]), lambda i: (i, 0))
    return pl.pallas_call(
        _sample_pre_kernel,
        grid=(n_seq // nb,),
        in_specs=([seq_spec((1, steps * dm)), seq_spec((1, sdim)), seq_spec((1, sdim))]
                  + [_const_spec(c.shape) for c in consts]),
        out_specs=tuple(seq_spec(shp) for shp in shapes),
        out_shape=tuple(jax.ShapeDtypeStruct((n_seq * shp[0], shp[1]), f32) for shp in shapes),
        compiler_params=pltpu.CompilerParams(
            dimension_semantics=("arbitrary",), vmem_limit_bytes=VMEM_LIMIT),
        name="sample_pre",
    )(x_sm, h0r, h0i, *consts)


def _sample_state_kernel(s0_ref, qt_ref, v_ref, cols_ref, snew_ref, oint_ref):
    nb = s0_ref.shape[0]
    nrow = cols_ref.shape[1]
    hw = HG_HEADS * HG_DK
    row_head = lax.broadcasted_iota(jnp.int32, (nrow, hw), 0) // 8
    col_head = lax.broadcasted_iota(jnp.int32, (nrow, hw), 1) // HG_DK
    diag = row_head == col_head
    pad = jnp.zeros((LANES - nrow, HG_DK), f32)
    for i in range(nb):
        xt = jnp.concatenate([cols_ref[i], pad], axis=0).T
        vbd = jnp.where(diag, jnp.concatenate([v_ref[i]] * HG_HEADS, axis=1), 0.0)
        vbd = jnp.concatenate([vbd, jnp.zeros((LANES - nrow, hw), f32)], axis=0).astype(bf16)
        ds = _dot(xt.astype(bf16), vbd)
        for h in range(HG_HEADS):
            s0 = s0_ref[i, h]
            oint_ref[i, h] = _dot(qt_ref[i, h].astype(bf16), s0.astype(bf16))
            snew_ref[i, h] = xt[:, h * 8 + 4:h * 8 + 5] * s0 + ds[:, h * HG_DK:(h + 1) * HG_DK]


def _sample_state_call(s0, qt8, v8, cols):
    n_seq = s0.shape[0]
    nb = SAMPLE_SEQ_BLOCK
    blk4 = (nb, HG_HEADS, 8, HG_DK)
    return pl.pallas_call(
        _sample_state_kernel,
        grid=(n_seq // nb,),
        in_specs=[pl.BlockSpec((nb, HG_HEADS, HG_DK, HG_DK), lambda i: (i, 0, 0, 0)),
                  pl.BlockSpec(blk4, lambda i: (i, 0, 0, 0)),
                  pl.BlockSpec((nb, HG_HEADS * 8, HG_DK), lambda i: (i, 0, 0)),
                  pl.BlockSpec((nb, HG_HEADS * 8, HG_DK), lambda i: (i, 0, 0))],
        out_specs=(pl.BlockSpec((nb, HG_HEADS, HG_DK, HG_DK), lambda i: (i, 0, 0, 0)),
                   pl.BlockSpec(blk4, lambda i: (i, 0, 0, 0))),
        out_shape=(jax.ShapeDtypeStruct(s0.shape, f32),
                   jax.ShapeDtypeStruct((n_seq, HG_HEADS, 8, HG_DK), f32)),
        compiler_params=pltpu.CompilerParams(
            dimension_semantics=("arbitrary",), vmem_limit_bytes=VMEM_LIMIT),
        name="sample_state",
    )(s0, qt8.reshape((n_seq,) + blk4[1:]), v8.reshape(n_seq, HG_HEADS * 8, HG_DK),
      cols.reshape(n_seq, HG_HEADS * 8, HG_DK))


def _sample_post_kernel(x_ref, oint_ref, oin_ref, gn_ref, m5_ref, ghg_ref, wbh_ref, wout_ref, gpost_ref,
                        y_ref):
    n_seq = x_ref.shape[0]
    dm = gpost_ref.shape[1]
    steps = x_ref.shape[1] // dm
    hw = HG_HEADS * HG_DK

    def steps_major(ref, wcol):
        return jnp.concatenate([ref[:, t * wcol:(t + 1) * wcol] for t in range(steps)], axis=0)

    x = steps_major(x_ref, dm)
    oint = jnp.concatenate(
        [jnp.concatenate([oint_ref[pl.ds(h * 8 + t, n_seq, stride=HG_HEADS * 8), :] for h in range(HG_HEADS)],
                         axis=1) for t in range(steps)], axis=0)
    o = oint + steps_major(oin_ref, hw)
    parts = []
    for h in range(HG_HEADS):
        hs = slice(h * HG_DK, (h + 1) * HG_DK)
        parts.append(_rms(o[:, hs]))
    yhg = (jnp.concatenate(parts, axis=1) * steps_major(gn_ref, hw)).astype(bf16)
    merged = steps_major(m5_ref, dm) + steps_major(ghg_ref, dm) * _dot(yhg, wbh_ref[...])
    mo = _dot(merged.astype(bf16), wout_ref[...])
    y = x + _rms(mo) * gpost_ref[...]
    for t in range(steps):
        y_ref[:, t * dm:(t + 1) * dm] = y[t * n_seq:(t + 1) * n_seq, :]


def _sample_post_call(x_sm, oint, oin, gn, m5, ghg, w):
    ins = [x_sm, oint, oin, gn, m5, ghg] + [w[k] for k in ("wbh", "wout", "gpost")]
    return pl.pallas_call(
        _sample_post_kernel,
        out_shape=jax.ShapeDtypeStruct(x_sm.shape, f32),
        compiler_params=pltpu.CompilerParams(vmem_limit_bytes=VMEM_LIMIT),
        name="sample_post",
    )(*ins)


def _prep_weights(l, norm_mix_pre, norm_mix_post, norm_mlp_pre, norm_mlp_post, w_in, b_in,
                  s5_a_re, s5_a_im, s5_log_dt, s5_b_re, s5_b_im, s5_c_re, s5_c_im, s5_d, s5_w_glu,
                  s5_b_glu, hg_lb_logits, hg_norm, w_br_s5, w_br_hg, w_out, w_up, w_down, n_pow):
    hp = lax.Precision.HIGHEST
    s5w = s5_d.shape[1]
    n_groups = s5w // S5_GROUP
    w = {}
    row = lambda a: a.astype(f32).reshape(1, -1)
    w["gpre"], w["gpost"] = row(norm_mix_pre[l]), row(norm_mix_post[l])
    w["g2pre"], w["g2post"] = row(norm_mlp_pre[l]), row(norm_mlp_post[l])
    w["win"], w["bin"] = w_in[l].astype(bf16), row(b_in[l])
    lb_all = jnp.cumsum(jax.nn.softmax(hg_lb_logits.astype(f32), axis=0), axis=0)
    w["lbrow"] = row(lb_all[l])
    w["hgnrow"] = row(jnp.tile(hg_norm[l].astype(f32), HG_HEADS))
    w["d"] = row(s5_d[l])
    w["wglu"], w["bglu"] = s5_w_glu[l].astype(bf16), row(s5_b_glu[l])
    w["wbs"], w["wbh"], w["wout"] = w_br_s5[l].astype(bf16), w_br_hg[l].astype(bf16), w_out[l].astype(bf16)
    w["wup_f32"], w["wdn_f32"] = w_up[l], w_down[l]

    a_re, a_im = s5_a_re[l].astype(f32), s5_a_im[l].astype(f32)
    dt = jnp.exp(s5_log_dt[l].astype(f32))[:, None]
    mag = jnp.exp(dt * a_re)
    abr, abi = mag * jnp.cos(dt * a_im), mag * jnp.sin(dt * a_im)
    den = a_re * a_re + a_im * a_im
    nr, ni = abr - 1.0, abi
    cfr, cfi = (nr * a_re + ni * a_im) / den, (ni * a_re - nr * a_im) / den
    b_re, b_im = s5_b_re[l].astype(f32), s5_b_im[l].astype(f32)
    bbr = cfr[..., None] * b_re - cfi[..., None] * b_im
    bbi = cfr[..., None] * b_im + cfi[..., None] * b_re
    c_re, c_im = s5_c_re[l].astype(f32), s5_c_im[l].astype(f32)
    w["abr"], w["abi"] = abr.reshape(1, -1), abi.reshape(1, -1)

    lam_r, lam_i = dt * a_re, dt * a_im
    grp = S5_GROUP
    n_state = a_re.shape[1]
    jj = jnp.arange(S5_CHUNK + 1, dtype=f32)[None, :, None]
    pmag = jnp.exp(jj * lam_r[:, None, :])
    pw_r, pw_i = pmag * jnp.cos(jj * lam_i[:, None, :]), pmag * jnp.sin(jj * lam_i[:, None, :])
    bt = jnp.transpose(jnp.stack([bbr, bbi]), (0, 1, 3, 2))
    bt_r, bt_i = bt[0], bt[1]
    c4r, c4i = c_re[:, None], c_im[:, None]

    def cmul(pr_, pi_, xr, xi):
        pr_, pi_ = pr_[:, :, None, :], pi_[:, :, None, :]
        return pr_ * xr - pi_ * xi, pr_ * xi + pi_ * xr

    rows128 = lambda t: t.reshape(n_groups, LANES, t.shape[-1])
    car_t, cai_t = cmul(pw_r[:, 1:], pw_i[:, 1:], c4r, c4i)
    ca = jnp.transpose(jnp.stack([rows128(car_t), -rows128(cai_t)]), (0, 1, 3, 2))
    ber, bei = cmul(pw_r[:, S5_CHUNK - 1::-1], pw_i[:, S5_CHUNK - 1::-1], bt_r[:, None], bt_i[:, None])
    ber, bei = rows128(ber), rows128(bei)
    zpad = jnp.zeros((n_groups, S5_CHUNK - 1, n_state), f32)
    lag_r = jnp.concatenate([zpad, pw_r[:, :S5_CHUNK]], axis=1)
    lag_i = jnp.concatenate([zpad, pw_i[:, :S5_CHUNK]], axis=1)
    cp_r, cp_i = cmul(lag_r, lag_i, c4r, c4i)
    n_blk = 2 * S5_CHUNK - 1
    cp_all = jnp.concatenate([cp_r, -cp_i], axis=3).reshape(n_groups, n_blk * grp, 2 * n_state)
    rpad_t = jnp.einsum("gmn,gnc->gmc", cp_all, jnp.concatenate([bbr, bbi], axis=1), precision=hp)
    toe_t = jnp.concatenate([rpad_t[:, (S5_CHUNK - 1 - s) * grp:(S5_CHUNK - 1 - s) * grp + LANES, :]
                             for s in range(S5_CHUNK)], axis=2)
    toe = jnp.transpose(toe_t, (0, 2, 1))

    def pair_diag(m):
        g, r, c = m.shape
        m = m.reshape(g // 2, 2, r, c)
        z = jnp.zeros_like(m[:, 0])
        return jnp.concatenate([jnp.concatenate([m[:, 0], z], axis=2),
                                jnp.concatenate([z, m[:, 1]], axis=2)], axis=1)

    w["w1"] = jnp.concatenate([pair_diag(toe), pair_diag(ber), pair_diag(bei)], axis=2).astype(bf16)
    w["w2"] = jnp.concatenate([pair_diag(ca[0]), pair_diag(ca[1])], axis=1).astype(bf16)
    qr, qi = pw_r[:, S5_CHUNK], pw_i[:, S5_CHUNK]
    aps_r, aps_i = [], []
    for _ in range(n_pow):
        aps_r.append(qr.reshape(-1))
        aps_i.append(qi.reshape(-1))
        qr, qi = qr * qr - qi * qi, 2.0 * qr * qi
    w["apr"], w["api"] = jnp.stack(aps_r), jnp.stack(aps_i)

    same_group = np.arange(s5w)[:, None] // grp == np.arange(n_groups * n_state)[None, :] // n_state
    lane_tile = np.tile(np.eye(n_state, dtype=np.float32), (1, n_groups))
    src = jnp.stack([bt_r, bt_i, c_re, -c_im]).reshape(4, s5w, n_state)
    tiled = jnp.einsum("zxn,nl->zxl", src, jnp.asarray(lane_tile), precision=hp,
                       preferred_element_type=f32).astype(bf16) * jnp.asarray(same_group, bf16)
    w["bblk"] = jnp.concatenate([tiled[0], tiled[1]], axis=1)
    w["cblkt"] = jnp.concatenate([tiled[2], tiled[3]], axis=1)
    return w


def kernel(x_prompt, x_sample, state_s5_re, state_s5_im, state_hg, norm_mix_pre, norm_mix_post,
           norm_mlp_pre, norm_mlp_post, w_in, b_in, s5_a_re, s5_a_im, s5_log_dt, s5_b_re, s5_b_im,
           s5_c_re, s5_c_im, s5_d, s5_w_glu, s5_b_glu, hg_lb_logits, hg_norm, w_br_s5, w_br_hg,
           w_out, w_up, w_down):
    depth = w_in.shape[0]
    bsz, seq, dm = x_prompt.shape
    n_seq, steps, _ = x_sample.shape
    n_groups, n_state = s5_a_re.shape[1], s5_a_re.shape[2]
    hw = HG_HEADS * HG_DK
    assert depth == 1 and steps == 4 and n_state == S5_STATE
    n_pow = max(1, int(math.log2(min(PROMPT_TILE, seq) // S5_CHUNK)))
    xp, xs = x_prompt, x_sample
    outs = [[] for _ in range(6)]
    for l in range(depth):
        w = _prep_weights(l, norm_mix_pre, norm_mix_post, norm_mlp_pre, norm_mlp_post, w_in, b_in,
                          s5_a_re, s5_a_im, s5_log_dt, s5_b_re, s5_b_im, s5_c_re, s5_c_im, s5_d,
                          s5_w_glu, s5_b_glu, hg_lb_logits, hg_norm, w_br_s5, w_br_hg, w_out, w_up,
                          w_down, n_pow)
        x1, p_re, p_im, p_hg = _mixer_call(xp, w)
        outs[0].append(p_re.reshape(bsz, n_groups, n_state))
        outs[1].append(p_im.reshape(bsz, n_groups, n_state))
        outs[2].append(p_hg)
        x_sm = xs.reshape(n_seq, steps * dm)
        h0r = state_s5_re[l].reshape(n_seq, n_groups * n_state)
        h0i = state_s5_im[l].reshape(n_seq, n_groups * n_state)
        m5, ghg, gn, qt8, cols, v8, oin, s_re, s_im = _sample_pre_call(x_sm, h0r, h0i, w)
        s_hg, oint = _sample_state_call(state_hg[l], qt8, v8, cols)
        x1_sm = _sample_post_call(x_sm, oint.reshape(n_seq * HG_HEADS * 8, HG_DK), oin, gn, m5, ghg, w)
        yp, y_sm = _mlp_call(x1.reshape(bsz * seq, dm), x1_sm, w)
        xp = yp.reshape(bsz, seq, dm)
        xs = y_sm.reshape(n_seq, steps, dm)
        outs[3].append(s_re.reshape(n_seq, n_groups, n_state))
        outs[4].append(s_im.reshape(n_seq, n_groups, n_state))
        outs[5].append(s_hg)
    return (xp, xs) + tuple(o[0][None] for o in outs)
```

```python
import math

import jax
import jax.numpy as jnp
import numpy as np
from jax import lax
from jax.experimental import pallas as pl
from jax.experimental.pallas import tpu as pltpu

f32 = jnp.float32
bf16 = jnp.bfloat16

NORM_EPS = 1e-6
S5_GROUP = 16
S5_STATE = 64
S5_CHUNK = 8
HG_HEADS = 8
HG_DK = 128
HG_CHUNK = 64
LANES = 128
VMEM_LIMIT = 56 * 1024 * 1024
PROMPT_TILE = 512
MLP_TILE = 1024
MLP_SUBTILES = 2
MLP_FCHUNK = 512
SAMPLE_SEQ_BLOCK = 16
SAMPLE_PRE_BLOCK = 64


def _sigmoid(x):
    return 0.5 * (jnp.tanh(0.5 * x) + 1.0)


def _silu(x):
    h = 0.5 * x
    return h + h * jnp.tanh(h)


def _gelu_tanh(x):
    c = math.sqrt(2.0 / math.pi)
    return 0.5 * x * (1.0 + jnp.tanh(c * (x + 0.044715 * (x * x * x))))


def _rms(x):
    return x * lax.rsqrt(jnp.mean(x * x, axis=-1, keepdims=True) + NORM_EPS)


def _dot(a, b):
    return jnp.dot(a, b, preferred_element_type=f32)


def _dot_nt(a, b):
    return lax.dot_general(a, b, (((1,), (1,)), ((), ())), preferred_element_type=f32)


def _dot_tn(a, b):
    return lax.dot_general(a, b, (((0,), (0,)), ((), ())), preferred_element_type=f32)


def _block_transpose8(xs, blk):
    xs = list(xs)
    for d in (4, 2, 1):
        keep = (blk & d) == 0
        for i in range(8):
            if i & d == 0:
                a, b = xs[i], xs[i + d]
                xs[i] = jnp.where(keep, a, pltpu.roll(b, d * S5_GROUP, 1))
                xs[i + d] = jnp.where(keep, pltpu.roll(a, LANES - d * S5_GROUP, 1), b)
    return xs


def _mixer_kernel(x_ref, gpre_ref, win_ref, bin_ref, w1_ref, w2_ref, apr_ref, api_ref, d_ref,
                  wglu_ref, bglu_ref, lb_ref, hgn_ref, wbs_ref, wbh_ref, wout_ref, gpost_ref,
                  y_ref, s5r_ref, s5i_ref, hgs_ref,
                  hb_scr, u_scr, ys_scr, o_scr, yloc_scr, er_scr, ei_scr, cr_scr, ci_scr,
                  st_scr, qfig_scr, yhg_scr, m_scr, fac_scr, el_scr, ghg_scr):
    t_idx = pl.program_id(1)
    n_t = pl.num_programs(1)
    tile = x_ref.shape[1]
    rows = tile // S5_CHUNK
    n_pairs = w1_ref.shape[0]
    n_cols = u_scr.shape[0]
    hw = HG_HEADS * HG_DK
    s5w = n_cols * LANES
    g0 = s5w + 4 * hw

    @pl.when(t_idx == 0)
    def _():
        cr_scr[...] = jnp.zeros_like(cr_scr)
        ci_scr[...] = jnp.zeros_like(ci_scr)
        st_scr[...] = jnp.zeros_like(st_scr)

    x = x_ref[0]
    hb_scr[...] = (_rms(x) * gpre_ref[...]).astype(bf16)
    dm = wbs_ref.shape[1]
    half = tile // 2

    def project(r0, r1, c0, c1):
        qfig_scr[r0:r1, c0:c1] = (_dot(hb_scr[r0:r1, :], win_ref[:, s5w + c0:s5w + c1])
                                  + bin_ref[:, s5w + c0:s5w + c1])

    def gate_hg(j):
        c0, c1 = j * (dm // 4), (j + 1) * (dm // 4)
        ghg_scr[:, c0:c1] = _sigmoid(_dot(hb_scr[...], win_ref[:, g0 + dm + c0:g0 + dm + c1])
                                     + bin_ref[:, g0 + dm + c0:g0 + dm + c1])


    u = _dot(hb_scr[...], win_ref[:, :s5w]) + bin_ref[:, :s5w]
    for j in range(n_cols):
        u_scr[j] = u[:, j * LANES:(j + 1) * LANES]

    project(0, half, 0, hw)
    gate_hg(0)
    blk = lax.broadcasted_iota(jnp.int32, (rows, LANES), 1) // S5_GROUP
    for j in range(n_cols):
        xs = [u_scr[j, pl.ds(s, rows, stride=S5_CHUNK), :] for s in range(S5_CHUNK)]
        ys = _block_transpose8(xs, blk)
        for q in range(4):
            o_scr[j * 4 + q] = jnp.concatenate([ys[2 * q], ys[2 * q + 1]], axis=1).astype(bf16)

    project(0, half, hw, 2 * hw)
    gate_hg(1)
    for p in range(n_pairs):
        r1 = _dot(o_scr[p], w1_ref[p])
        yloc_scr[p] = r1[:, :2 * LANES]
        er_scr[:, p * LANES:(p + 1) * LANES] = r1[:, 2 * LANES:3 * LANES]
        ei_scr[:, p * LANES:(p + 1) * LANES] = r1[:, 3 * LANES:]

    er = er_scr[...]
    ei = ei_scr[...]
    row = lax.broadcasted_iota(jnp.int32, (rows, 1), 0)
    first = row == 0
    cr = cr_scr[...]
    ci = ci_scr[...]
    a_r = apr_ref[0:1, :]
    a_i = api_ref[0:1, :]
    project(0, half, 2 * hw, 3 * hw)
    gate_hg(2)
    er = er + jnp.where(first, a_r * cr - a_i * ci, 0.0)
    ei = ei + jnp.where(first, a_r * ci + a_i * cr, 0.0)
    for k in range(int(math.log2(rows))):
        d = 1 << k
        p_r = apr_ref[k:k + 1, :]
        p_i = api_ref[k:k + 1, :]
        s_r = pltpu.roll(er, d, 0)
        s_i = pltpu.roll(ei, d, 0)
        valid = row >= d
        er, ei = (er + jnp.where(valid, p_r * s_r - p_i * s_i, 0.0),
                  ei + jnp.where(valid, p_r * s_i + p_i * s_r, 0.0))
    hs_r = jnp.where(first, cr, pltpu.roll(er, 1, 0))
    hs_i = jnp.where(first, ci, pltpu.roll(ei, 1, 0))
    cr_scr[...] = er[rows - 1:rows, :]
    ci_scr[...] = ei[rows - 1:rows, :]

    project(0, half, 3 * hw, 4 * hw)
    gate_hg(3)
    for j in range(n_cols):
        halves = []
        for q in range(4):
            p = j * 4 + q
            hp = jnp.concatenate([hs_r[:, p * LANES:(p + 1) * LANES],
                                  hs_i[:, p * LANES:(p + 1) * LANES]], axis=1).astype(bf16)
            yp = yloc_scr[p] + _dot(hp, w2_ref[p])
            halves += [yp[:, :LANES], yp[:, LANES:]]
        zs = _block_transpose8(halves, blk)
        for s in range(S5_CHUNK):
            ys_scr[j, pl.ds(s, rows, stride=S5_CHUNK), :] = zs[s]

    y = jnp.concatenate([ys_scr[j] for j in range(n_cols)], axis=1)
    u = jnp.concatenate([u_scr[j] for j in range(n_cols)], axis=1)
    y = _gelu_tanh(y + d_ref[...] * u)
    z = _dot(y.astype(bf16), wglu_ref[...]) + bglu_ref[...]
    ys5 = (y * _sigmoid(z)).astype(bf16)
    gate_s5 = _sigmoid(_dot(hb_scr[...], win_ref[:, g0:g0 + dm]) + bin_ref[:, g0:g0 + dm])
    m_scr[...] = gate_s5 * _dot(ys5, wbs_ref[...])

    ch = HG_CHUNK
    ri = lax.broadcasted_iota(jnp.int32, (ch, ch), 0)
    ci_ = lax.broadcasted_iota(jnp.int32, (ch, ch), 1)
    causal = ci_ <= ri
    tril2 = jnp.concatenate([causal.astype(bf16)] * 2, axis=1)
    lb = lb_ref[...]
    f_mid = 0.5 * (1.0 + lb)
    f_amp = 0.5 * (1.0 - lb)
    hgn = hgn_ref[...]
    n_chunks = tile // ch
    n_side = n_chunks // 2
    assert n_side == 4 and dm % n_side == 0

    def merge_hg(r0, r1, c0, c1):
        m_scr[r0:r1, c0:c1] = (m_scr[r0:r1, c0:c1]
                               + ghg_scr[r0:r1, c0:c1] * _dot(yhg_scr[r0:r1, :], wbh_ref[:, c0:c1]))

    def chunk_factors(c):
        r0 = c * ch
        slot = c % 2
        q_raw = qfig_scr[pl.ds(r0, ch), 0:hw]
        f_raw = qfig_scr[pl.ds(r0, ch), hw:2 * hw]
        q = _silu(q_raw)
        f = f_mid + f_amp * jnp.tanh(0.5 * f_raw)
        lf = jnp.log(f)
        kk = 1.0 - f
        hi = lf.astype(bf16)
        lo = (lf - hi.astype(f32)).astype(bf16)
        b = _dot(tril2, jnp.concatenate([hi, lo], axis=0))
        mid = b[ch // 2 - 1:ch // 2, :]
        b_last = b[ch - 1:ch, :]
        qh = q * jnp.exp(b - mid)
        kh = kk * jnp.exp(mid - b)
        fac_scr[slot, 0] = qh.astype(bf16)
        fac_scr[slot, 1] = kh.astype(bf16)
        fac_scr[slot, 2] = (qh * jnp.exp(mid)).astype(bf16)
        fac_scr[slot, 3] = (kh * jnp.exp(b_last - mid)).astype(bf16)
        fac_scr[slot, 4] = qfig_scr[pl.ds(r0, ch), 2 * hw:3 * hw].astype(bf16)
        el_scr[slot] = jnp.exp(b_last)

    def chunk_matmuls(c):
        r0 = c * ch
        slot = c % 2
        heads = [slice(h * HG_DK, (h + 1) * HG_DK) for h in range(HG_HEADS)]
        scs = [jnp.where(causal, _dot_nt(fac_scr[slot, 0, :, hs], fac_scr[slot, 1, :, hs]), 0.0).astype(bf16)
               for hs in heads]
        e_last = el_scr[slot]
        outs = []
        for h, hs in enumerate(heads):
            st = st_scr[h]
            vb = fac_scr[slot, 4, :, hs]
            o = _dot(scs[h], vb) + _dot_nt(fac_scr[slot, 2, :, hs], st.astype(bf16))
            st_scr[h] = st * e_last[:, hs] + _dot_tn(vb, fac_scr[slot, 3, :, hs])
            outs.append(_rms(o))
        g_raw = qfig_scr[pl.ds(r0, ch), 3 * hw:4 * hw]
        on = jnp.concatenate(outs, axis=1) * hgn * _silu(g_raw)
        yhg_scr[pl.ds(r0, ch), :] = on.astype(bf16)

    def finish(r0, r1):
        mo = _dot(m_scr[r0:r1, :].astype(bf16), wout_ref[...])
        y_ref[0, r0:r1, :] = x_ref[0, r0:r1, :] + _rms(mo) * gpost_ref[...]

    quarter = half // 2
    side_work = {4: lambda: merge_hg(0, half, 0, dm // 2),
                 5: lambda: merge_hg(0, half, dm // 2, dm),
                 6: lambda: finish(0, half),
                 7: lambda: merge_hg(half, half + quarter, 0, dm)}
    chunk_factors(0)
    for c in range(n_chunks):
        if c + 1 < n_chunks:
            chunk_factors(c + 1)
        chunk_matmuls(c)
        if c < n_side:
            project(half, tile, c * hw, (c + 1) * hw)
        else:
            side_work[c]()
    merge_hg(half + quarter, tile, 0, dm)
    finish(half, tile)

    @pl.when(t_idx == n_t - 1)
    def _():
        s5r_ref[0] = cr_scr[...]
        s5i_ref[0] = ci_scr[...]
        for h in range(HG_HEADS):
            hgs_ref[0, h] = st_scr[h].T


def _const_spec(shape):
    nd = len(shape)
    return pl.BlockSpec(shape, lambda *_: (0,) * nd, pipeline_mode=pl.Buffered(1))


def _mixer_call(x, w):
    bsz, seq, dm = x.shape
    tile = min(PROMPT_TILE, seq)
    assert seq % tile == 0 and tile % HG_CHUNK == 0
    rows = tile // S5_CHUNK
    assert rows & (rows - 1) == 0 and w["apr"].shape[0] >= int(math.log2(rows))
    s5w = w["d"].shape[1]
    n_cols = s5w // LANES
    n_pairs = w["w1"].shape[0]
    sdim = w["apr"].shape[1]
    consts = [w[k] for k in ("gpre", "win", "bin", "w1", "w2", "apr", "api", "d", "wglu", "bglu",
                             "lbrow", "hgnrow", "wbs", "wbh", "wout", "gpost")]
    in_specs = [pl.BlockSpec((1, tile, dm), lambda b, t: (b, t, 0))]
    in_specs += [_const_spec(c.shape) for c in consts]
    out_shape = (jax.ShapeDtypeStruct((bsz, seq, dm), f32),
                 jax.ShapeDtypeStruct((bsz, 1, sdim), f32),
                 jax.ShapeDtypeStruct((bsz, 1, sdim), f32),
                 jax.ShapeDtypeStruct((bsz, HG_HEADS, HG_DK, HG_DK), f32))
    out_specs = (pl.BlockSpec((1, tile, dm), lambda b, t: (b, t, 0)),
                 pl.BlockSpec((1, 1, sdim), lambda b, t: (b, 0, 0)),
                 pl.BlockSpec((1, 1, sdim), lambda b, t: (b, 0, 0)),
                 pl.BlockSpec((1, HG_HEADS, HG_DK, HG_DK), lambda b, t: (b, 0, 0, 0)))
    scratch = [
        pltpu.VMEM((tile, dm), bf16),
        pltpu.VMEM((n_cols, tile, LANES), f32),
        pltpu.VMEM((n_cols, tile, LANES), f32),
        pltpu.VMEM((n_pairs, rows, 2 * LANES), bf16),
        pltpu.VMEM((n_pairs, rows, 2 * LANES), f32),
        pltpu.VMEM((rows, sdim), f32),
        pltpu.VMEM((rows, sdim), f32),
        pltpu.VMEM((1, sdim), f32),
        pltpu.VMEM((1, sdim), f32),
        pltpu.VMEM((HG_HEADS, HG_DK, HG_DK), f32),
        pltpu.VMEM((tile, HG_HEADS * 4 * HG_DK), f32),
        pltpu.VMEM((tile, HG_HEADS * HG_DK), bf16),
        pltpu.VMEM((tile, dm), f32),
        pltpu.VMEM((2, 5, HG_CHUNK, HG_HEADS * HG_DK), bf16),
        pltpu.VMEM((2, 1, HG_HEADS * HG_DK), f32),
        pltpu.VMEM((tile, dm), f32),
    ]
    return pl.pallas_call(
        _mixer_kernel,
        grid=(bsz, seq // tile),
        in_specs=in_specs,
        out_specs=out_specs,
        out_shape=out_shape,
        scratch_shapes=scratch,
        compiler_params=pltpu.CompilerParams(
            dimension_semantics=("arbitrary", "arbitrary"), vmem_limit_bytes=VMEM_LIMIT),
        name="prompt_mixer",
    )(x, *consts)


def _mlp_body(x, gpre_ref, wup_scr, wdn_scr, gpost_ref):
    h2 = (_rms(x) * gpre_ref[...]).astype(bf16)
    acc = None
    for c in range(wup_scr.shape[0]):
        a = _dot(h2, wup_scr[c])
        a = jnp.maximum(a, 0.0)
        part = _dot((a * a).astype(bf16), wdn_scr[c])
        acc = part if acc is None else acc + part
    return x + _rms(acc) * gpost_ref[...]


def _mlp_kernel(xp_ref, xs_ref, gpre_ref, wup_ref, wdn_ref, gpost_ref, yp_ref, ys_ref, wup_scr, wdn_scr):
    step = pl.program_id(0)
    n_cast = wup_scr.shape[0]
    last = pl.num_programs(0) - 1
    dm = gpre_ref.shape[1]

    @pl.when(step < n_cast)
    def _():
        wup_scr[step] = wup_ref[...].astype(bf16)
        wdn_scr[step] = wdn_ref[...].astype(bf16)

    @pl.when(jnp.logical_and(step >= n_cast, step < last))
    def _():
        sub = xp_ref.shape[0] // MLP_SUBTILES
        for r0 in range(0, xp_ref.shape[0], sub):
            yp_ref[r0:r0 + sub, :] = _mlp_body(xp_ref[r0:r0 + sub, :], gpre_ref, wup_scr, wdn_scr, gpost_ref)

    @pl.when(step == last)
    def _():
        n_seq = xs_ref.shape[0]
        steps = xs_ref.shape[1] // dm
        x = jnp.concatenate([xs_ref[:, t * dm:(t + 1) * dm] for t in range(steps)], axis=0)
        y = _mlp_body(x, gpre_ref, wup_scr, wdn_scr, gpost_ref)
        for t in range(steps):
            ys_ref[:, t * dm:(t + 1) * dm] = y[t * n_seq:(t + 1) * n_seq, :]


def _mlp_call(xp2d, xs_sm, w):
    n, dm = xp2d.shape
    tile = min(MLP_TILE, n)
    assert n % tile == 0
    n_tiles = n // tile
    wup, wdn = w["wup_f32"], w["wdn_f32"]
    dff = wup.shape[1]
    fc = MLP_FCHUNK
    assert dff % fc == 0
    n_cast = dff // fc
    blk = lambda s: jnp.minimum(s, n_cast - 1)
    row = lambda s: jnp.clip(s - n_cast, 0, n_tiles - 1)
    return pl.pallas_call(
        _mlp_kernel,
        grid=(n_cast + n_tiles + 1,),
        in_specs=[pl.BlockSpec((tile, dm), lambda s: (row(s), 0)),
                  _const_spec(xs_sm.shape),
                  _const_spec(w["g2pre"].shape),
                  pl.BlockSpec((dm, fc), lambda s: (0, blk(s))),
                  pl.BlockSpec((fc, dm), lambda s: (blk(s), 0)),
                  _const_spec(w["g2post"].shape)],
        out_specs=(pl.BlockSpec((tile, dm), lambda s: (row(s), 0)),
                   pl.BlockSpec(xs_sm.shape, lambda s: (0, 0))),
        out_shape=(jax.ShapeDtypeStruct((n, dm), f32), jax.ShapeDtypeStruct(xs_sm.shape, f32)),
        scratch_shapes=[pltpu.VMEM((n_cast, dm, fc), bf16), pltpu.VMEM((n_cast, fc, dm), bf16)],
        compiler_params=pltpu.CompilerParams(
            dimension_semantics=("arbitrary",), vmem_limit_bytes=VMEM_LIMIT),
        name="mlp",
    )(xp2d, xs_sm, w["g2pre"], wup, wdn, w["g2post"])


def _sample_pre_kernel(x_ref, h0r_ref, h0i_ref, gpre_ref, win_ref, bin_ref, bc_ref,
                       abr_ref, abi_ref, d_ref, wglu_ref, bglu_ref, lb_ref, hgn_ref, wbs_ref,
                       m5_ref, ghg_ref, gn_ref, qt_ref, cols_ref, v_ref, oin_ref,
                       s5r_ref, s5i_ref, bblk_ref, cblkt_ref):
    n_seq = h0r_ref.shape[0]
    dm = gpre_ref.shape[1]
    steps = x_ref.shape[1] // dm
    sdim = h0r_ref.shape[1]
    hw = HG_HEADS * HG_DK
    s5w = d_ref.shape[1]

    @pl.when(pl.program_id(0) == 0)
    def _():
        n_state = bc_ref.shape[2]
        row_grp = lax.broadcasted_iota(jnp.int32, (s5w, sdim), 0) // S5_GROUP
        col_grp = lax.broadcasted_iota(jnp.int32, (s5w, sdim), 1) // n_state
        own = row_grp == col_grp
        for k, ref in enumerate((bblk_ref, bblk_ref, cblkt_ref, cblkt_ref)):
            tiled = jnp.concatenate([bc_ref[k]] * (sdim // n_state), axis=1)
            c0 = (k % 2) * sdim
            ref[:, c0:c0 + sdim] = jnp.where(own, tiled, 0.0).astype(bf16)

    x = jnp.concatenate([x_ref[:, t * dm:(t + 1) * dm] for t in range(steps)], axis=0)
    hb = (_rms(x) * gpre_ref[...]).astype(bf16)

    u = _dot(hb, win_ref[:, :s5w]) + bin_ref[:, :s5w]
    bu = _dot(u.astype(bf16), bblk_ref[...])
    hr = h0r_ref[...]
    hi = h0i_ref[...]
    a_r = abr_ref[...]
    a_i = abi_ref[...]
    hs = []
    for t in range(steps):
        sl = slice(t * n_seq, (t + 1) * n_seq)
        hr, hi = (a_r * hr - a_i * hi + bu[sl, :sdim], a_r * hi + a_i * hr + bu[sl, sdim:])
        hs.append(jnp.concatenate([hr, hi], axis=1).astype(bf16))
    s5r_ref[...] = hr
    s5i_ref[...] = hi
    y = _dot_nt(jnp.concatenate(hs, axis=0), cblkt_ref[...])
    y = _gelu_tanh(y + d_ref[...] * u)
    z = _dot(y.astype(bf16), wglu_ref[...]) + bglu_ref[...]
    ys5 = (y * _sigmoid(z)).astype(bf16)
    g0 = s5w + 4 * hw
    def put_steps(ref, val):
        wcol = val.shape[1]
        for t in range(steps):
            ref[:, t * wcol:(t + 1) * wcol] = val[t * n_seq:(t + 1) * n_seq, :]

    gates = _dot(hb, win_ref[:, g0:]) + bin_ref[:, g0:]
    put_steps(m5_ref, _sigmoid(gates[:, :dm]) * _dot(ys5, wbs_ref[...]))
    put_steps(ghg_ref, _sigmoid(gates[:, dm:]))

    def proj(i):
        c0 = s5w + i * hw
        return _dot(hb, win_ref[:, c0:c0 + hw]) + bin_ref[:, c0:c0 + hw]

    q_raw = proj(0)
    f_raw = proj(1)
    v = proj(2)
    g_raw = proj(3)
    q = _silu(q_raw)
    lb = lb_ref[...]
    f = 0.5 * (1.0 + lb) + 0.5 * (1.0 - lb) * jnp.tanh(0.5 * f_raw)
    lf = jnp.log(f)
    kk = 1.0 - f
    put_steps(gn_ref, _silu(g_raw) * hgn_ref[...])
    bs = []
    acc = None
    for t in range(steps):
        sl = slice(t * n_seq, (t + 1) * n_seq)
        acc = lf[sl] if acc is None else acc + lf[sl]
        bs.append(acc)
    b_last = bs[-1]

    def put(ref, slot, val):
        for h in range(HG_HEADS):
            ref[pl.ds(h * 8 + slot, n_seq, stride=HG_HEADS * 8), :] = val[:, h * HG_DK:(h + 1) * HG_DK]

    for ref in (qt_ref, cols_ref, v_ref):
        ref[...] = jnp.zeros_like(ref)
    put(cols_ref, steps, jnp.exp(b_last))
    for t in range(steps):
        sl = slice(t * n_seq, (t + 1) * n_seq)
        put(qt_ref, t, q[sl] * jnp.exp(bs[t]))
        put(cols_ref, t, kk[sl] * jnp.exp(b_last - bs[t]))
        put(v_ref, t, v[sl])
        o_t = None
        for s in range(t + 1):
            sls = slice(s * n_seq, (s + 1) * n_seq)
            prod = q[sl] * kk[sls] * jnp.exp(bs[t] - bs[s])
            parts = []
            for h in range(HG_HEADS):
                hs = slice(h * HG_DK, (h + 1) * HG_DK)
                wgt = jnp.sum(prod[:, hs], axis=1, keepdims=True)
                parts.append(wgt * v[sls, hs])
            term = jnp.concatenate(parts, axis=1)
            o_t = term if o_t is None else o_t + term
        oin_ref[:, t * hw:(t + 1) * hw] = o_t


def _sample_pre_call(x_sm, h0r, h0i, w):
    n_seq, sdim = h0r.shape
    dm = w["gpre"].shape[1]
    steps = x_sm.shape[1] // dm
    hw = HG_HEADS * HG_DK
    nb = SAMPLE_PRE_BLOCK
    assert n_seq % nb == 0
    s5w = w["d"].shape[1]
    consts = [w[k] for k in ("gpre", "win", "bin", "bc", "abr", "abi", "d",
                             "wglu", "bglu", "lbrow", "hgnrow", "wbs")]
    slots = HG_HEADS * 8
    shapes = ((1, steps * dm),
              (1, steps * dm),
              (1, steps * hw),
              (slots, HG_DK),
              (slots, HG_DK),
              (slots, HG_DK),
              (1, steps * hw),
              (1, sdim), (1, sdim))
    seq_spec = lambda shp: pl.BlockSpec((nb * shp[0], shp[1]), lambda i: (i, 0))
    return pl.pallas_call(
        _sample_pre_kernel,
        grid=(n_seq // nb,),
        in_specs=([seq_spec((1, steps * dm)), seq_spec((1, sdim)), seq_spec((1, sdim))]
                  + [_const_spec(c.shape) for c in consts]),
        out_specs=tuple(seq_spec(shp) for shp in shapes),
        out_shape=tuple(jax.ShapeDtypeStruct((n_seq * shp[0], shp[1]), f32) for shp in shapes),
        scratch_shapes=[pltpu.VMEM((s5w, 2 * sdim), bf16),
                        pltpu.VMEM((s5w, 2 * sdim), bf16)],
        compiler_params=pltpu.CompilerParams(
            dimension_semantics=("arbitrary",), vmem_limit_bytes=VMEM_LIMIT),
        name="sample_pre",
    )(x_sm, h0r, h0i, *consts)


def _sample_state_kernel(s0_ref, qt_ref, v_ref, cols_ref, snew_ref, oint_ref):
    nb = s0_ref.shape[0]
    nrow = cols_ref.shape[1]
    hw = HG_HEADS * HG_DK
    row_head = lax.broadcasted_iota(jnp.int32, (nrow, hw), 0) // 8
    col_head = lax.broadcasted_iota(jnp.int32, (nrow, hw), 1) // HG_DK
    diag = row_head == col_head
    pad = jnp.zeros((LANES - nrow, HG_DK), f32)
    for i in range(nb):
        xt = jnp.concatenate([cols_ref[i], pad], axis=0).T
        vbd = jnp.where(diag, jnp.concatenate([v_ref[i]] * HG_HEADS, axis=1), 0.0)
        vbd = jnp.concatenate([vbd, jnp.zeros((LANES - nrow, hw), f32)], axis=0).astype(bf16)
        ds = _dot(xt.astype(bf16), vbd)
        for h in range(HG_HEADS):
            s0 = s0_ref[i, h]
            oint_ref[i, h] = _dot(qt_ref[i, h].astype(bf16), s0.astype(bf16))
            snew_ref[i, h] = xt[:, h * 8 + 4:h * 8 + 5] * s0 + ds[:, h * HG_DK:(h + 1) * HG_DK]


def _sample_state_call(s0, qt8, v8, cols):
    n_seq = s0.shape[0]
    nb = SAMPLE_SEQ_BLOCK
    blk4 = (nb, HG_HEADS, 8, HG_DK)
    return pl.pallas_call(
        _sample_state_kernel,
        grid=(n_seq // nb,),
        in_specs=[pl.BlockSpec((nb, HG_HEADS, HG_DK, HG_DK), lambda i: (i, 0, 0, 0)),
                  pl.BlockSpec(blk4, lambda i: (i, 0, 0, 0)),
                  pl.BlockSpec((nb, HG_HEADS * 8, HG_DK), lambda i: (i, 0, 0)),
                  pl.BlockSpec((nb, HG_HEADS * 8, HG_DK), lambda i: (i, 0, 0))],
        out_specs=(pl.BlockSpec((nb, HG_HEADS, HG_DK, HG_DK), lambda i: (i, 0, 0, 0)),
                   pl.BlockSpec(blk4, lambda i: (i, 0, 0, 0))),
        out_shape=(jax.ShapeDtypeStruct(s0.shape, f32),
                   jax.ShapeDtypeStruct((n_seq, HG_HEADS, 8, HG_DK), f32)),
        compiler_params=pltpu.CompilerParams(
            dimension_semantics=("arbitrary",), vmem_limit_bytes=VMEM_LIMIT),
        name="sample_state",
    )(s0, qt8.reshape((n_seq,) + blk4[1:]), v8.reshape(n_seq, HG_HEADS * 8, HG_DK),
      cols.reshape(n_seq, HG_HEADS * 8, HG_DK))


def _sample_post_kernel(x_ref, oint_ref, oin_ref, gn_ref, m5_ref, ghg_ref, wbh_ref, wout_ref, gpost_ref,
                        y_ref):
    n_seq = x_ref.shape[0]
    dm = gpost_ref.shape[1]
    steps = x_ref.shape[1] // dm
    hw = HG_HEADS * HG_DK

    def steps_major(ref, wcol):
        return jnp.concatenate([ref[:, t * wcol:(t + 1) * wcol] for t in range(steps)], axis=0)

    x = steps_major(x_ref, dm)
    oint = jnp.concatenate(
        [jnp.concatenate([oint_ref[pl.ds(h * 8 + t, n_seq, stride=HG_HEADS * 8), :] for h in range(HG_HEADS)],
                         axis=1) for t in range(steps)], axis=0)
    o = oint + steps_major(oin_ref, hw)
    parts = []
    for h in range(HG_HEADS):
        hs = slice(h * HG_DK, (h + 1) * HG_DK)
        parts.append(_rms(o[:, hs]))
    yhg = (jnp.concatenate(parts, axis=1) * steps_major(gn_ref, hw)).astype(bf16)
    merged = steps_major(m5_ref, dm) + steps_major(ghg_ref, dm) * _dot(yhg, wbh_ref[...])
    mo = _dot(merged.astype(bf16), wout_ref[...])
    y = x + _rms(mo) * gpost_ref[...]
    for t in range(steps):
        y_ref[:, t * dm:(t + 1) * dm] = y[t * n_seq:(t + 1) * n_seq, :]


def _sample_post_call(x_sm, oint, oin, gn, m5, ghg, w):
    ins = [x_sm, oint, oin, gn, m5, ghg] + [w[k] for k in ("wbh", "wout", "gpost")]
    return pl.pallas_call(
        _sample_post_kernel,
        out_shape=jax.ShapeDtypeStruct(x_sm.shape, f32),
        compiler_params=pltpu.CompilerParams(vmem_limit_bytes=VMEM_LIMIT),
        name="sample_post",
    )(*ins)


def _prep_weights(l, norm_mix_pre, norm_mix_post, norm_mlp_pre, norm_mlp_post, w_in, b_in,
                  s5_a_re, s5_a_im, s5_log_dt, s5_b_re, s5_b_im, s5_c_re, s5_c_im, s5_d, s5_w_glu,
                  s5_b_glu, hg_lb_logits, hg_norm, w_br_s5, w_br_hg, w_out, w_up, w_down, n_pow):
    hp = lax.Precision.HIGHEST
    s5w = s5_d.shape[1]
    n_groups = s5w // S5_GROUP
    w = {}
    row = lambda a: a.astype(f32).reshape(1, -1)
    w["gpre"], w["gpost"] = row(norm_mix_pre[l]), row(norm_mix_post[l])
    w["g2pre"], w["g2post"] = row(norm_mlp_pre[l]), row(norm_mlp_post[l])
    w["win"], w["bin"] = w_in[l].astype(bf16), row(b_in[l])
    lb_all = jnp.cumsum(jax.nn.softmax(hg_lb_logits.astype(f32), axis=0), axis=0)
    w["lbrow"] = row(lb_all[l])
    w["hgnrow"] = row(jnp.tile(hg_norm[l].astype(f32), HG_HEADS))
    w["d"] = row(s5_d[l])
    w["wglu"], w["bglu"] = s5_w_glu[l].astype(bf16), row(s5_b_glu[l])
    w["wbs"], w["wbh"], w["wout"] = w_br_s5[l].astype(bf16), w_br_hg[l].astype(bf16), w_out[l].astype(bf16)
    w["wup_f32"], w["wdn_f32"] = w_up[l], w_down[l]

    a_re, a_im = s5_a_re[l].astype(f32), s5_a_im[l].astype(f32)
    dt = jnp.exp(s5_log_dt[l].astype(f32))[:, None]
    mag = jnp.exp(dt * a_re)
    abr, abi = mag * jnp.cos(dt * a_im), mag * jnp.sin(dt * a_im)
    den = a_re * a_re + a_im * a_im
    nr, ni = abr - 1.0, abi
    cfr, cfi = (nr * a_re + ni * a_im) / den, (ni * a_re - nr * a_im) / den
    b_re, b_im = s5_b_re[l].astype(f32), s5_b_im[l].astype(f32)
    bbr = cfr[..., None] * b_re - cfi[..., None] * b_im
    bbi = cfr[..., None] * b_im + cfi[..., None] * b_re
    c_re, c_im = s5_c_re[l].astype(f32), s5_c_im[l].astype(f32)
    w["abr"], w["abi"] = abr.reshape(1, -1), abi.reshape(1, -1)

    lam_r, lam_i = dt * a_re, dt * a_im
    grp = S5_GROUP
    n_state = a_re.shape[1]
    jj = jnp.arange(S5_CHUNK + 1, dtype=f32)[None, :, None]
    pmag = jnp.exp(jj * lam_r[:, None, :])
    pw_r, pw_i = pmag * jnp.cos(jj * lam_i[:, None, :]), pmag * jnp.sin(jj * lam_i[:, None, :])
    bt = jnp.transpose(jnp.stack([bbr, bbi]), (0, 1, 3, 2))
    bt_r, bt_i = bt[0], bt[1]
    c4r, c4i = c_re[:, None], c_im[:, None]

    def cmul(pr_, pi_, xr, xi):
        pr_, pi_ = pr_[:, :, None, :], pi_[:, :, None, :]
        return pr_ * xr - pi_ * xi, pr_ * xi + pi_ * xr

    rows128 = lambda t: t.reshape(n_groups, LANES, t.shape[-1])
    car_t, cai_t = cmul(pw_r[:, 1:], pw_i[:, 1:], c4r, c4i)
    ca = jnp.transpose(jnp.stack([rows128(car_t), -rows128(cai_t)]), (0, 1, 3, 2))
    ber, bei = cmul(pw_r[:, S5_CHUNK - 1::-1], pw_i[:, S5_CHUNK - 1::-1], bt_r[:, None], bt_i[:, None])
    ber, bei = rows128(ber), rows128(bei)
    zpad = jnp.zeros((n_groups, S5_CHUNK - 1, n_state), f32)
    lag_r = jnp.concatenate([zpad, pw_r[:, :S5_CHUNK]], axis=1)
    lag_i = jnp.concatenate([zpad, pw_i[:, :S5_CHUNK]], axis=1)
    cp_r, cp_i = cmul(lag_r, lag_i, c4r, c4i)
    n_blk = 2 * S5_CHUNK - 1
    cp_all = jnp.concatenate([cp_r, -cp_i], axis=3).reshape(n_groups, n_blk * grp, 2 * n_state)
    rpad_t = jnp.einsum("gmn,gnc->gmc", cp_all, jnp.concatenate([bbr, bbi], axis=1), precision=hp)
    toe_t = jnp.concatenate([rpad_t[:, (S5_CHUNK - 1 - s) * grp:(S5_CHUNK - 1 - s) * grp + LANES, :]
                             for s in range(S5_CHUNK)], axis=2)
    toe = jnp.transpose(toe_t, (0, 2, 1))

    def pair_diag(m):
        g, r, c = m.shape
        m = m.reshape(g // 2, 2, r, c)
        z = jnp.zeros_like(m[:, 0])
        return jnp.concatenate([jnp.concatenate([m[:, 0], z], axis=2),
                                jnp.concatenate([z, m[:, 1]], axis=2)], axis=1)

    w["w1"] = jnp.concatenate([pair_diag(toe), pair_diag(ber), pair_diag(bei)], axis=2).astype(bf16)
    w["w2"] = jnp.concatenate([pair_diag(ca[0]), pair_diag(ca[1])], axis=1).astype(bf16)
    qr, qi = pw_r[:, S5_CHUNK], pw_i[:, S5_CHUNK]
    aps_r, aps_i = [], []
    for _ in range(n_pow):
        aps_r.append(qr.reshape(-1))
        aps_i.append(qi.reshape(-1))
        qr, qi = qr * qr - qi * qi, 2.0 * qr * qi
    w["apr"], w["api"] = jnp.stack(aps_r), jnp.stack(aps_i)

    w["bc"] = jnp.stack([bt_r, bt_i, c_re, -c_im]).reshape(4, s5w, n_state)
    return w


def kernel(x_prompt, x_sample, state_s5_re, state_s5_im, state_hg, norm_mix_pre, norm_mix_post,
           norm_mlp_pre, norm_mlp_post, w_in, b_in, s5_a_re, s5_a_im, s5_log_dt, s5_b_re, s5_b_im,
           s5_c_re, s5_c_im, s5_d, s5_w_glu, s5_b_glu, hg_lb_logits, hg_norm, w_br_s5, w_br_hg,
           w_out, w_up, w_down):
    depth = w_in.shape[0]
    bsz, seq, dm = x_prompt.shape
    n_seq, steps, _ = x_sample.shape
    n_groups, n_state = s5_a_re.shape[1], s5_a_re.shape[2]
    hw = HG_HEADS * HG_DK
    assert depth == 1 and steps == 4 and n_state == S5_STATE
    n_pow = max(1, int(math.log2(min(PROMPT_TILE, seq) // S5_CHUNK)))
    xp, xs = x_prompt, x_sample
    outs = [[] for _ in range(6)]
    for l in range(depth):
        w = _prep_weights(l, norm_mix_pre, norm_mix_post, norm_mlp_pre, norm_mlp_post, w_in, b_in,
                          s5_a_re, s5_a_im, s5_log_dt, s5_b_re, s5_b_im, s5_c_re, s5_c_im, s5_d,
                          s5_w_glu, s5_b_glu, hg_lb_logits, hg_norm, w_br_s5, w_br_hg, w_out, w_up,
                          w_down, n_pow)
        x1, p_re, p_im, p_hg = _mixer_call(xp, w)
        outs[0].append(p_re.reshape(bsz, n_groups, n_state))
        outs[1].append(p_im.reshape(bsz, n_groups, n_state))
        outs[2].append(p_hg)
        x_sm = xs.reshape(n_seq, steps * dm)
        h0r = state_s5_re[l].reshape(n_seq, n_groups * n_state)
        h0i = state_s5_im[l].reshape(n_seq, n_groups * n_state)
        m5, ghg, gn, qt8, cols, v8, oin, s_re, s_im = _sample_pre_call(x_sm, h0r, h0i, w)
        s_hg, oint = _sample_state_call(state_hg[l], qt8, v8, cols)
        x1_sm = _sample_post_call(x_sm, oint.reshape(n_seq * HG_HEADS * 8, HG_DK), oin, gn, m5, ghg, w)
        yp, y_sm = _mlp_call(x1.reshape(bsz * seq, dm), x1_sm, w)
        xp = yp.reshape(bsz, seq, dm)
        xs = y_sm.reshape(n_seq, steps, dm)
        outs[3].append(s_re.reshape(n_seq, n_groups, n_state))
        outs[4].append(s_im.reshape(n_seq, n_groups, n_state))
        outs[5].append(s_hg)
    return (xp, xs) + tuple(o[0][None] for o in outs)
```

```python
import math

import jax
import jax.numpy as jnp
import numpy as np
from jax import lax
from jax.experimental import pallas as pl
from jax.experimental.pallas import tpu as pltpu

f32 = jnp.float32
bf16 = jnp.bfloat16

NORM_EPS = 1e-6
S5_GROUP = 16
S5_STATE = 64
S5_CHUNK = 8
HG_HEADS = 8
HG_DK = 128
HG_CHUNK = 64
LANES = 128
VMEM_LIMIT = 56 * 1024 * 1024
PROMPT_TILE = 512
MLP_TILE = 1024
MLP_SUBTILES = 2
MLP_FCHUNK = 512
SAMPLE_SEQ_BLOCK = 16
SAMPLE_PRE_BLOCK = 64


def _sigmoid(x):
    return 0.5 * (jnp.tanh(0.5 * x) + 1.0)


def _silu(x):
    h = 0.5 * x
    return h + h * jnp.tanh(h)


def _gelu_tanh(x):
    c = math.sqrt(2.0 / math.pi)
    return 0.5 * x * (1.0 + jnp.tanh(c * (x + 0.044715 * (x * x * x))))


def _rms(x):
    return x * lax.rsqrt(jnp.mean(x * x, axis=-1, keepdims=True) + NORM_EPS)


def _dot(a, b):
    return jnp.dot(a, b, preferred_element_type=f32)


def _dot_nt(a, b):
    return lax.dot_general(a, b, (((1,), (1,)), ((), ())), preferred_element_type=f32)


def _dot_tn(a, b):
    return lax.dot_general(a, b, (((0,), (0,)), ((), ())), preferred_element_type=f32)


def _block_transpose8(xs, blk):
    xs = list(xs)
    for d in (4, 2, 1):
        keep = (blk & d) == 0
        for i in range(8):
            if i & d == 0:
                a, b = xs[i], xs[i + d]
                xs[i] = jnp.where(keep, a, pltpu.roll(b, d * S5_GROUP, 1))
                xs[i + d] = jnp.where(keep, pltpu.roll(a, LANES - d * S5_GROUP, 1), b)
    return xs


def _mixer_kernel(x_ref, gpre_ref, win_ref, bin_ref, w1_ref, w2_ref, apr_ref, api_ref, d_ref,
                  wglu_ref, bglu_ref, lb_ref, hgn_ref, wbs_ref, wbh_ref, wout_ref, gpost_ref,
                  y_ref, s5r_ref, s5i_ref, hgs_ref,
                  hb_scr, u_scr, ys_scr, o_scr, yloc_scr, er_scr, ei_scr, cr_scr, ci_scr,
                  st_scr, qfig_scr, yhg_scr, m_scr, fac_scr, el_scr, ghg_scr):
    t_idx = pl.program_id(1)
    n_t = pl.num_programs(1)
    tile = x_ref.shape[1]
    rows = tile // S5_CHUNK
    n_pairs = w1_ref.shape[0]
    n_cols = u_scr.shape[0]
    hw = HG_HEADS * HG_DK
    s5w = n_cols * LANES
    g0 = s5w + 4 * hw

    @pl.when(t_idx == 0)
    def _():
        cr_scr[...] = jnp.zeros_like(cr_scr)
        ci_scr[...] = jnp.zeros_like(ci_scr)
        st_scr[...] = jnp.zeros_like(st_scr)

    x = x_ref[0]
    hb_scr[...] = (_rms(x) * gpre_ref[...]).astype(bf16)
    dm = wbs_ref.shape[1]
    half = tile // 2

    def project(r0, r1, c0, c1):
        qfig_scr[r0:r1, c0:c1] = (_dot(hb_scr[r0:r1, :], win_ref[:, s5w + c0:s5w + c1])
                                  + bin_ref[:, s5w + c0:s5w + c1])

    def gate_hg(j):
        c0, c1 = j * (dm // 4), (j + 1) * (dm // 4)
        ghg_scr[:, c0:c1] = _sigmoid(_dot(hb_scr[...], win_ref[:, g0 + dm + c0:g0 + dm + c1])
                                     + bin_ref[:, g0 + dm + c0:g0 + dm + c1])


    u = _dot(hb_scr[...], win_ref[:, :s5w]) + bin_ref[:, :s5w]
    for j in range(n_cols):
        u_scr[j] = u[:, j * LANES:(j + 1) * LANES]

    project(0, half, 0, hw)
    gate_hg(0)
    blk = lax.broadcasted_iota(jnp.int32, (rows, LANES), 1) // S5_GROUP
    for j in range(n_cols):
        xs = [u_scr[j, pl.ds(s, rows, stride=S5_CHUNK), :] for s in range(S5_CHUNK)]
        ys = _block_transpose8(xs, blk)
        for q in range(4):
            o_scr[j * 4 + q] = jnp.concatenate([ys[2 * q], ys[2 * q + 1]], axis=1).astype(bf16)

    project(0, half, hw, 2 * hw)
    gate_hg(1)
    for p in range(n_pairs):
        r1 = _dot(o_scr[p], w1_ref[p])
        yloc_scr[p] = r1[:, :2 * LANES]
        er_scr[:, p * LANES:(p + 1) * LANES] = r1[:, 2 * LANES:3 * LANES]
        ei_scr[:, p * LANES:(p + 1) * LANES] = r1[:, 3 * LANES:]

    er = er_scr[...]
    ei = ei_scr[...]
    row = lax.broadcasted_iota(jnp.int32, (rows, 1), 0)
    first = row == 0
    cr = cr_scr[...]
    ci = ci_scr[...]
    a_r = apr_ref[0:1, :]
    a_i = api_ref[0:1, :]
    project(0, half, 2 * hw, 3 * hw)
    gate_hg(2)
    er = er + jnp.where(first, a_r * cr - a_i * ci, 0.0)
    ei = ei + jnp.where(first, a_r * ci + a_i * cr, 0.0)
    for k in range(int(math.log2(rows))):
        d = 1 << k
        p_r = apr_ref[k:k + 1, :]
        p_i = api_ref[k:k + 1, :]
        s_r = pltpu.roll(er, d, 0)
        s_i = pltpu.roll(ei, d, 0)
        valid = row >= d
        er, ei = (er + jnp.where(valid, p_r * s_r - p_i * s_i, 0.0),
                  ei + jnp.where(valid, p_r * s_i + p_i * s_r, 0.0))
    hs_r = jnp.where(first, cr, pltpu.roll(er, 1, 0))
    hs_i = jnp.where(first, ci, pltpu.roll(ei, 1, 0))
    cr_scr[...] = er[rows - 1:rows, :]
    ci_scr[...] = ei[rows - 1:rows, :]

    project(0, half, 3 * hw, 4 * hw)
    gate_hg(3)
    for j in range(n_cols):
        halves = []
        for q in range(4):
            p = j * 4 + q
            hp = jnp.concatenate([hs_r[:, p * LANES:(p + 1) * LANES],
                                  hs_i[:, p * LANES:(p + 1) * LANES]], axis=1).astype(bf16)
            yp = yloc_scr[p] + _dot(hp, w2_ref[p])
            halves += [yp[:, :LANES], yp[:, LANES:]]
        zs = _block_transpose8(halves, blk)
        for s in range(S5_CHUNK):
            ys_scr[j, pl.ds(s, rows, stride=S5_CHUNK), :] = zs[s]

    y = jnp.concatenate([ys_scr[j] for j in range(n_cols)], axis=1)
    u = jnp.concatenate([u_scr[j] for j in range(n_cols)], axis=1)
    y = _gelu_tanh(y + d_ref[...] * u)
    z = _dot(y.astype(bf16), wglu_ref[...]) + bglu_ref[...]
    ys5 = (y * _sigmoid(z)).astype(bf16)
    gate_s5 = _sigmoid(_dot(hb_scr[...], win_ref[:, g0:g0 + dm]) + bin_ref[:, g0:g0 + dm])
    m_scr[...] = gate_s5 * _dot(ys5, wbs_ref[...])

    ch = HG_CHUNK
    ri = lax.broadcasted_iota(jnp.int32, (ch, ch), 0)
    ci_ = lax.broadcasted_iota(jnp.int32, (ch, ch), 1)
    causal = ci_ <= ri
    tril2 = jnp.concatenate([causal.astype(bf16)] * 2, axis=1)
    lb = lb_ref[...]
    f_mid = 0.5 * (1.0 + lb)
    f_amp = 0.5 * (1.0 - lb)
    hgn = hgn_ref[...]
    n_chunks = tile // ch
    n_side = n_chunks // 2
    assert n_side == 4 and dm % n_side == 0

    def merge_hg(r0, r1, c0, c1):
        m_scr[r0:r1, c0:c1] = (m_scr[r0:r1, c0:c1]
                               + ghg_scr[r0:r1, c0:c1] * _dot(yhg_scr[r0:r1, :], wbh_ref[:, c0:c1]))

    def chunk_factors(c):
        r0 = c * ch
        slot = c % 2
        q_raw = qfig_scr[pl.ds(r0, ch), 0:hw]
        f_raw = qfig_scr[pl.ds(r0, ch), hw:2 * hw]
        q = _silu(q_raw)
        f = f_mid + f_amp * jnp.tanh(0.5 * f_raw)
        lf = jnp.log(f)
        kk = 1.0 - f
        hi = lf.astype(bf16)
        lo = (lf - hi.astype(f32)).astype(bf16)
        b = _dot(tril2, jnp.concatenate([hi, lo], axis=0))
        mid = b[ch // 2 - 1:ch // 2, :]
        b_last = b[ch - 1:ch, :]
        qh = q * jnp.exp(b - mid)
        kh = kk * jnp.exp(mid - b)
        fac_scr[slot, 0] = qh.astype(bf16)
        fac_scr[slot, 1] = kh.astype(bf16)
        fac_scr[slot, 2] = (qh * jnp.exp(mid)).astype(bf16)
        fac_scr[slot, 3] = (kh * jnp.exp(b_last - mid)).astype(bf16)
        fac_scr[slot, 4] = qfig_scr[pl.ds(r0, ch), 2 * hw:3 * hw].astype(bf16)
        el_scr[slot] = jnp.exp(b_last)

    def chunk_matmuls(c):
        r0 = c * ch
        slot = c % 2
        heads = [slice(h * HG_DK, (h + 1) * HG_DK) for h in range(HG_HEADS)]
        scs = [jnp.where(causal, _dot_nt(fac_scr[slot, 0, :, hs], fac_scr[slot, 1, :, hs]), 0.0).astype(bf16)
               for hs in heads]
        e_last = el_scr[slot]
        outs = []
        for h, hs in enumerate(heads):
            st = st_scr[h]
            vb = fac_scr[slot, 4, :, hs]
            o = _dot(scs[h], vb) + _dot_nt(fac_scr[slot, 2, :, hs], st.astype(bf16))
            st_scr[h] = st * e_last[:, hs] + _dot_tn(vb, fac_scr[slot, 3, :, hs])
            outs.append(_rms(o))
        g_raw = qfig_scr[pl.ds(r0, ch), 3 * hw:4 * hw]
        on = jnp.concatenate(outs, axis=1) * hgn * _silu(g_raw)
        yhg_scr[pl.ds(r0, ch), :] = on.astype(bf16)

    def finish(r0, r1):
        mo = _dot(m_scr[r0:r1, :].astype(bf16), wout_ref[...])
        y_ref[0, r0:r1, :] = x_ref[0, r0:r1, :] + _rms(mo) * gpost_ref[...]

    quarter = half // 2
    side_work = {4: lambda: merge_hg(0, half, 0, dm // 2),
                 5: lambda: merge_hg(0, half, dm // 2, dm),
                 6: lambda: finish(0, half),
                 7: lambda: merge_hg(half, half + quarter, 0, dm)}
    chunk_factors(0)
    for c in range(n_chunks):
        if c + 1 < n_chunks:
            chunk_factors(c + 1)
        chunk_matmuls(c)
        if c < n_side:
            project(half, tile, c * hw, (c + 1) * hw)
        else:
            side_work[c]()
    merge_hg(half + quarter, tile, 0, dm)
    finish(half, tile)

    @pl.when(t_idx == n_t - 1)
    def _():
        s5r_ref[0] = cr_scr[...]
        s5i_ref[0] = ci_scr[...]
        for h in range(HG_HEADS):
            hgs_ref[0, h] = st_scr[h].T


def _const_spec(shape):
    nd = len(shape)
    return pl.BlockSpec(shape, lambda *_: (0,) * nd, pipeline_mode=pl.Buffered(1))


def _mixer_call(x, w):
    bsz, seq, dm = x.shape
    tile = min(PROMPT_TILE, seq)
    assert seq % tile == 0 and tile % HG_CHUNK == 0
    rows = tile // S5_CHUNK
    assert rows & (rows - 1) == 0 and w["apr"].shape[0] >= int(math.log2(rows))
    s5w = w["d"].shape[1]
    n_cols = s5w // LANES
    n_pairs = w["w1"].shape[0]
    sdim = w["apr"].shape[1]
    consts = [w[k] for k in ("gpre", "win", "bin", "w1", "w2", "apr", "api", "d", "wglu", "bglu",
                             "lbrow", "hgnrow", "wbs", "wbh", "wout", "gpost")]
    in_specs = [pl.BlockSpec((1, tile, dm), lambda b, t: (b, t, 0))]
    in_specs += [_const_spec(c.shape) for c in consts]
    out_shape = (jax.ShapeDtypeStruct((bsz, seq, dm), f32),
                 jax.ShapeDtypeStruct((bsz, 1, sdim), f32),
                 jax.ShapeDtypeStruct((bsz, 1, sdim), f32),
                 jax.ShapeDtypeStruct((bsz, HG_HEADS, HG_DK, HG_DK), f32))
    out_specs = (pl.BlockSpec((1, tile, dm), lambda b, t: (b, t, 0)),
                 pl.BlockSpec((1, 1, sdim), lambda b, t: (b, 0, 0)),
                 pl.BlockSpec((1, 1, sdim), lambda b, t: (b, 0, 0)),
                 pl.BlockSpec((1, HG_HEADS, HG_DK, HG_DK), lambda b, t: (b, 0, 0, 0)))
    scratch = [
        pltpu.VMEM((tile, dm), bf16),
        pltpu.VMEM((n_cols, tile, LANES), f32),
        pltpu.VMEM((n_cols, tile, LANES), f32),
        pltpu.VMEM((n_pairs, rows, 2 * LANES), bf16),
        pltpu.VMEM((n_pairs, rows, 2 * LANES), f32),
        pltpu.VMEM((rows, sdim), f32),
        pltpu.VMEM((rows, sdim), f32),
        pltpu.VMEM((1, sdim), f32),
        pltpu.VMEM((1, sdim), f32),
        pltpu.VMEM((HG_HEADS, HG_DK, HG_DK), f32),
        pltpu.VMEM((tile, HG_HEADS * 4 * HG_DK), f32),
        pltpu.VMEM((tile, HG_HEADS * HG_DK), bf16),
        pltpu.VMEM((tile, dm), f32),
        pltpu.VMEM((2, 5, HG_CHUNK, HG_HEADS * HG_DK), bf16),
        pltpu.VMEM((2, 1, HG_HEADS * HG_DK), f32),
        pltpu.VMEM((tile, dm), f32),
    ]
    return pl.pallas_call(
        _mixer_kernel,
        grid=(bsz, seq // tile),
        in_specs=in_specs,
        out_specs=out_specs,
        out_shape=out_shape,
        scratch_shapes=scratch,
        compiler_params=pltpu.CompilerParams(
            dimension_semantics=("arbitrary", "arbitrary"), vmem_limit_bytes=VMEM_LIMIT),
        name="prompt_mixer",
    )(x, *consts)


def _mlp_body(x, gpre_ref, wup_scr, wdn_scr, gpost_ref):
    h2 = (_rms(x) * gpre_ref[...]).astype(bf16)
    acc = None
    for c in range(wup_scr.shape[0]):
        a = _dot(h2, wup_scr[c])
        a = jnp.maximum(a, 0.0)
        part = _dot((a * a).astype(bf16), wdn_scr[c])
        acc = part if acc is None else acc + part
    return x + _rms(acc) * gpost_ref[...]


def _mlp_kernel(xp_ref, xs_ref, gpre_ref, wup_ref, wdn_ref, gpost_ref, yp_ref, ys_ref,
                wup_scr, wdn_scr, h2_scr, acc_scr):
    step = pl.program_id(0)
    n_cast = wup_scr.shape[0]
    last = pl.num_programs(0) - 1
    dm = gpre_ref.shape[1]

    @pl.when(step < n_cast)
    def _():
        wup_scr[step] = wup_ref[...].astype(bf16)
        wdn_scr[step] = wdn_ref[...].astype(bf16)

        @pl.when(step == 0)
        def _():
            h2_scr[...] = (_rms(xp_ref[...]) * gpre_ref[...]).astype(bf16)
            acc_scr[...] = jnp.zeros_like(acc_scr)

        a = jnp.maximum(_dot(h2_scr[...], wup_scr[step]), 0.0)
        acc_scr[...] += _dot((a * a).astype(bf16), wdn_scr[step])

        @pl.when(step == n_cast - 1)
        def _():
            yp_ref[...] = xp_ref[...] + _rms(acc_scr[...]) * gpost_ref[...]

    @pl.when(jnp.logical_and(step >= n_cast, step < last))
    def _():
        sub = xp_ref.shape[0] // MLP_SUBTILES
        for r0 in range(0, xp_ref.shape[0], sub):
            yp_ref[r0:r0 + sub, :] = _mlp_body(xp_ref[r0:r0 + sub, :], gpre_ref, wup_scr, wdn_scr, gpost_ref)

    @pl.when(step == last)
    def _():
        n_seq = xs_ref.shape[0]
        steps = xs_ref.shape[1] // dm
        x = jnp.concatenate([xs_ref[:, t * dm:(t + 1) * dm] for t in range(steps)], axis=0)
        y = _mlp_body(x, gpre_ref, wup_scr, wdn_scr, gpost_ref)
        for t in range(steps):
            ys_ref[:, t * dm:(t + 1) * dm] = y[t * n_seq:(t + 1) * n_seq, :]


def _mlp_call(xp2d, xs_sm, w):
    n, dm = xp2d.shape
    tile = min(MLP_TILE, n)
    assert n % tile == 0
    n_tiles = n // tile
    wup, wdn = w["wup_f32"], w["wdn_f32"]
    dff = wup.shape[1]
    fc = MLP_FCHUNK
    assert dff % fc == 0
    n_cast = dff // fc
    blk = lambda s: jnp.minimum(s, n_cast - 1)
    row = lambda s: jnp.clip(s - n_cast + 1, 0, n_tiles - 1)
    return pl.pallas_call(
        _mlp_kernel,
        grid=(n_cast + n_tiles,),
        in_specs=[pl.BlockSpec((tile, dm), lambda s: (row(s), 0)),
                  _const_spec(xs_sm.shape),
                  _const_spec(w["g2pre"].shape),
                  pl.BlockSpec((dm, fc), lambda s: (0, blk(s))),
                  pl.BlockSpec((fc, dm), lambda s: (blk(s), 0)),
                  _const_spec(w["g2post"].shape)],
        out_specs=(pl.BlockSpec((tile, dm), lambda s: (row(s), 0)),
                   pl.BlockSpec(xs_sm.shape, lambda s: (0, 0))),
        out_shape=(jax.ShapeDtypeStruct((n, dm), f32), jax.ShapeDtypeStruct(xs_sm.shape, f32)),
        scratch_shapes=[pltpu.VMEM((n_cast, dm, fc), bf16), pltpu.VMEM((n_cast, fc, dm), bf16),
                        pltpu.VMEM((tile, dm), bf16), pltpu.VMEM((tile, dm), f32)],
        compiler_params=pltpu.CompilerParams(
            dimension_semantics=("arbitrary",), vmem_limit_bytes=VMEM_LIMIT),
        name="mlp",
    )(xp2d, xs_sm, w["g2pre"], wup, wdn, w["g2post"])


def _sample_pre_kernel(x_ref, h0r_ref, h0i_ref, gpre_ref, win_ref, bin_ref, bc_ref,
                       abr_ref, abi_ref, d_ref, wglu_ref, bglu_ref, lb_ref, hgn_ref, wbs_ref,
                       m5_ref, ghg_ref, gn_ref, qt_ref, cols_ref, v_ref, oin_ref,
                       s5r_ref, s5i_ref, bblk_ref, cblkt_ref):
    n_seq = h0r_ref.shape[0]
    dm = gpre_ref.shape[1]
    steps = x_ref.shape[1] // dm
    sdim = h0r_ref.shape[1]
    hw = HG_HEADS * HG_DK
    s5w = d_ref.shape[1]

    @pl.when(pl.program_id(0) == 0)
    def _():
        n_state = bc_ref.shape[2]
        row_grp = lax.broadcasted_iota(jnp.int32, (s5w, sdim), 0) // S5_GROUP
        col_grp = lax.broadcasted_iota(jnp.int32, (s5w, sdim), 1) // n_state
        own = row_grp == col_grp
        for k, ref in enumerate((bblk_ref, bblk_ref, cblkt_ref, cblkt_ref)):
            tiled = jnp.concatenate([bc_ref[k]] * (sdim // n_state), axis=1)
            c0 = (k % 2) * sdim
            ref[:, c0:c0 + sdim] = jnp.where(own, tiled, 0.0).astype(bf16)

    x = jnp.concatenate([x_ref[:, t * dm:(t + 1) * dm] for t in range(steps)], axis=0)
    hb = (_rms(x) * gpre_ref[...]).astype(bf16)

    u = _dot(hb, win_ref[:, :s5w]) + bin_ref[:, :s5w]
    bu = _dot(u.astype(bf16), bblk_ref[...])
    hr = h0r_ref[...]
    hi = h0i_ref[...]
    a_r = abr_ref[...]
    a_i = abi_ref[...]
    hs = []
    for t in range(steps):
        sl = slice(t * n_seq, (t + 1) * n_seq)
        hr, hi = (a_r * hr - a_i * hi + bu[sl, :sdim], a_r * hi + a_i * hr + bu[sl, sdim:])
        hs.append(jnp.concatenate([hr, hi], axis=1).astype(bf16))
    s5r_ref[...] = hr
    s5i_ref[...] = hi
    y = _dot_nt(jnp.concatenate(hs, axis=0), cblkt_ref[...])
    y = _gelu_tanh(y + d_ref[...] * u)
    z = _dot(y.astype(bf16), wglu_ref[...]) + bglu_ref[...]
    ys5 = (y * _sigmoid(z)).astype(bf16)
    g0 = s5w + 4 * hw
    def put_steps(ref, val):
        wcol = val.shape[1]
        for t in range(steps):
            ref[:, t * wcol:(t + 1) * wcol] = val[t * n_seq:(t + 1) * n_seq, :]

    gates = _dot(hb, win_ref[:, g0:]) + bin_ref[:, g0:]
    put_steps(m5_ref, _sigmoid(gates[:, :dm]) * _dot(ys5, wbs_ref[...]))
    put_steps(ghg_ref, _sigmoid(gates[:, dm:]))

    def proj(i):
        c0 = s5w + i * hw
        return _dot(hb, win_ref[:, c0:c0 + hw]) + bin_ref[:, c0:c0 + hw]

    q_raw = proj(0)
    f_raw = proj(1)
    v = proj(2)
    g_raw = proj(3)
    q = _silu(q_raw)
    lb = lb_ref[...]
    f = 0.5 * (1.0 + lb) + 0.5 * (1.0 - lb) * jnp.tanh(0.5 * f_raw)
    lf = jnp.log(f)
    kk = 1.0 - f
    put_steps(gn_ref, _silu(g_raw) * hgn_ref[...])
    bs = []
    acc = None
    for t in range(steps):
        sl = slice(t * n_seq, (t + 1) * n_seq)
        acc = lf[sl] if acc is None else acc + lf[sl]
        bs.append(acc)
    b_last = bs[-1]

    def put(ref, slot, val):
        for h in range(HG_HEADS):
            ref[pl.ds(h * 8 + slot, n_seq, stride=HG_HEADS * 8), :] = val[:, h * HG_DK:(h + 1) * HG_DK]

    for ref in (qt_ref, cols_ref, v_ref):
        ref[...] = jnp.zeros_like(ref)
    put(cols_ref, steps, jnp.exp(b_last))
    for t in range(steps):
        sl = slice(t * n_seq, (t + 1) * n_seq)
        put(qt_ref, t, q[sl] * jnp.exp(bs[t]))
        put(cols_ref, t, kk[sl] * jnp.exp(b_last - bs[t]))
        put(v_ref, t, v[sl])
        o_t = None
        for s in range(t + 1):
            sls = slice(s * n_seq, (s + 1) * n_seq)
            prod = q[sl] * kk[sls] * jnp.exp(bs[t] - bs[s])
            parts = []
            for h in range(HG_HEADS):
                hs = slice(h * HG_DK, (h + 1) * HG_DK)
                wgt = jnp.sum(prod[:, hs], axis=1, keepdims=True)
                parts.append(wgt * v[sls, hs])
            term = jnp.concatenate(parts, axis=1)
            o_t = term if o_t is None else o_t + term
        oin_ref[:, t * hw:(t + 1) * hw] = o_t


def _sample_pre_call(x_sm, h0r, h0i, w):
    n_seq, sdim = h0r.shape
    dm = w["gpre"].shape[1]
    steps = x_sm.shape[1] // dm
    hw = HG_HEADS * HG_DK
    nb = SAMPLE_PRE_BLOCK
    assert n_seq % nb == 0
    s5w = w["d"].shape[1]
    consts = [w[k] for k in ("gpre", "win", "bin", "bc", "abr", "abi", "d",
                             "wglu", "bglu", "lbrow", "hgnrow", "wbs")]
    slots = HG_HEADS * 8
    shapes = ((1, steps * dm),
              (1, steps * dm),
              (1, steps * hw),
              (slots, HG_DK),
              (slots, HG_DK),
              (slots, HG_DK),
              (1, steps * hw),
              (1, sdim), (1, sdim))
    seq_spec = lambda shp: pl.BlockSpec((nb * shp[0], shp[1]), lambda i: (i, 0))
    return pl.pallas_call(
        _sample_pre_kernel,
        grid=(n_seq // nb,),
        in_specs=([seq_spec((1, steps * dm)), seq_spec((1, sdim)), seq_spec((1, sdim))]
                  + [_const_spec(c.shape) for c in consts]),
        out_specs=tuple(seq_spec(shp) for shp in shapes),
        out_shape=tuple(jax.ShapeDtypeStruct((n_seq * shp[0], shp[1]), f32) for shp in shapes),
        scratch_shapes=[pltpu.VMEM((s5w, 2 * sdim), bf16),
                        pltpu.VMEM((s5w, 2 * sdim), bf16)],
        compiler_params=pltpu.CompilerParams(
            dimension_semantics=("arbitrary",), vmem_limit_bytes=VMEM_LIMIT),
        name="sample_pre",
    )(x_sm, h0r, h0i, *consts)


def _sample_state_kernel(s0_ref, qt_ref, v_ref, cols_ref, snew_ref, oint_ref):
    nb = s0_ref.shape[0]
    nrow = cols_ref.shape[1]
    hw = HG_HEADS * HG_DK
    row_head = lax.broadcasted_iota(jnp.int32, (nrow, hw), 0) // 8
    col_head = lax.broadcasted_iota(jnp.int32, (nrow, hw), 1) // HG_DK
    diag = row_head == col_head
    pad = jnp.zeros((LANES - nrow, HG_DK), f32)
    for i in range(nb):
        xt = jnp.concatenate([cols_ref[i], pad], axis=0).T
        vbd = jnp.where(diag, jnp.concatenate([v_ref[i]] * HG_HEADS, axis=1), 0.0)
        vbd = jnp.concatenate([vbd, jnp.zeros((LANES - nrow, hw), f32)], axis=0).astype(bf16)
        ds = _dot(xt.astype(bf16), vbd)
        for h in range(HG_HEADS):
            s0 = s0_ref[i, h]
            oint_ref[i, h] = _dot(qt_ref[i, h].astype(bf16), s0.astype(bf16))
            snew_ref[i, h] = xt[:, h * 8 + 4:h * 8 + 5] * s0 + ds[:, h * HG_DK:(h + 1) * HG_DK]


def _sample_state_call(s0, qt8, v8, cols):
    n_seq = s0.shape[0]
    nb = SAMPLE_SEQ_BLOCK
    blk4 = (nb, HG_HEADS, 8, HG_DK)
    return pl.pallas_call(
        _sample_state_kernel,
        grid=(n_seq // nb,),
        in_specs=[pl.BlockSpec((nb, HG_HEADS, HG_DK, HG_DK), lambda i: (i, 0, 0, 0)),
                  pl.BlockSpec(blk4, lambda i: (i, 0, 0, 0)),
                  pl.BlockSpec((nb, HG_HEADS * 8, HG_DK), lambda i: (i, 0, 0)),
                  pl.BlockSpec((nb, HG_HEADS * 8, HG_DK), lambda i: (i, 0, 0))],
        out_specs=(pl.BlockSpec((nb, HG_HEADS, HG_DK, HG_DK), lambda i: (i, 0, 0, 0)),
                   pl.BlockSpec(blk4, lambda i: (i, 0, 0, 0))),
        out_shape=(jax.ShapeDtypeStruct(s0.shape, f32),
                   jax.ShapeDtypeStruct((n_seq, HG_HEADS, 8, HG_DK), f32)),
        compiler_params=pltpu.CompilerParams(
            dimension_semantics=("arbitrary",), vmem_limit_bytes=VMEM_LIMIT),
        name="sample_state",
    )(s0, qt8.reshape((n_seq,) + blk4[1:]), v8.reshape(n_seq, HG_HEADS * 8, HG_DK),
      cols.reshape(n_seq, HG_HEADS * 8, HG_DK))


def _sample_post_kernel(x_ref, oint_ref, oin_ref, gn_ref, m5_ref, ghg_ref, wbh_ref, wout_ref, gpost_ref,
                        y_ref):
    n_seq = x_ref.shape[0]
    dm = gpost_ref.shape[1]
    steps = x_ref.shape[1] // dm
    hw = HG_HEADS * HG_DK

    def steps_major(ref, wcol):
        return jnp.concatenate([ref[:, t * wcol:(t + 1) * wcol] for t in range(steps)], axis=0)

    x = steps_major(x_ref, dm)
    oint = jnp.concatenate(
        [jnp.concatenate([oint_ref[pl.ds(h * 8 + t, n_seq, stride=HG_HEADS * 8), :] for h in range(HG_HEADS)],
                         axis=1) for t in range(steps)], axis=0)
    o = oint + steps_major(oin_ref, hw)
    parts = []
    for h in range(HG_HEADS):
        hs = slice(h * HG_DK, (h + 1) * HG_DK)
        parts.append(_rms(o[:, hs]))
    yhg = (jnp.concatenate(parts, axis=1) * steps_major(gn_ref, hw)).astype(bf16)
    merged = steps_major(m5_ref, dm) + steps_major(ghg_ref, dm) * _dot(yhg, wbh_ref[...])
    mo = _dot(merged.astype(bf16), wout_ref[...])
    y = x + _rms(mo) * gpost_ref[...]
    for t in range(steps):
        y_ref[:, t * dm:(t + 1) * dm] = y[t * n_seq:(t + 1) * n_seq, :]


def _sample_post_call(x_sm, oint, oin, gn, m5, ghg, w):
    ins = [x_sm, oint, oin, gn, m5, ghg] + [w[k] for k in ("wbh", "wout", "gpost")]
    return pl.pallas_call(
        _sample_post_kernel,
        out_shape=jax.ShapeDtypeStruct(x_sm.shape, f32),
        compiler_params=pltpu.CompilerParams(vmem_limit_bytes=VMEM_LIMIT),
        name="sample_post",
    )(*ins)


def _prep_weights(l, norm_mix_pre, norm_mix_post, norm_mlp_pre, norm_mlp_post, w_in, b_in,
                  s5_a_re, s5_a_im, s5_log_dt, s5_b_re, s5_b_im, s5_c_re, s5_c_im, s5_d, s5_w_glu,
                  s5_b_glu, hg_lb_logits, hg_norm, w_br_s5, w_br_hg, w_out, w_up, w_down, n_pow):
    hp = lax.Precision.HIGHEST
    s5w = s5_d.shape[1]
    n_groups = s5w // S5_GROUP
    w = {}
    row = lambda a: a.astype(f32).reshape(1, -1)
    w["gpre"], w["gpost"] = row(norm_mix_pre[l]), row(norm_mix_post[l])
    w["g2pre"], w["g2post"] = row(norm_mlp_pre[l]), row(norm_mlp_post[l])
    w["win"], w["bin"] = w_in[l].astype(bf16), row(b_in[l])
    lb_all = jnp.cumsum(jax.nn.softmax(hg_lb_logits.astype(f32), axis=0), axis=0)
    w["lbrow"] = row(lb_all[l])
    w["hgnrow"] = row(jnp.tile(hg_norm[l].astype(f32), HG_HEADS))
    w["d"] = row(s5_d[l])
    w["wglu"], w["bglu"] = s5_w_glu[l].astype(bf16), row(s5_b_glu[l])
    w["wbs"], w["wbh"], w["wout"] = w_br_s5[l].astype(bf16), w_br_hg[l].astype(bf16), w_out[l].astype(bf16)
    w["wup_f32"], w["wdn_f32"] = w_up[l], w_down[l]

    a_re, a_im = s5_a_re[l].astype(f32), s5_a_im[l].astype(f32)
    dt = jnp.exp(s5_log_dt[l].astype(f32))[:, None]
    mag = jnp.exp(dt * a_re)
    abr, abi = mag * jnp.cos(dt * a_im), mag * jnp.sin(dt * a_im)
    den = a_re * a_re + a_im * a_im
    nr, ni = abr - 1.0, abi
    cfr, cfi = (nr * a_re + ni * a_im) / den, (ni * a_re - nr * a_im) / den
    b_re, b_im = s5_b_re[l].astype(f32), s5_b_im[l].astype(f32)
    bbr = cfr[..., None] * b_re - cfi[..., None] * b_im
    bbi = cfr[..., None] * b_im + cfi[..., None] * b_re
    c_re, c_im = s5_c_re[l].astype(f32), s5_c_im[l].astype(f32)
    w["abr"], w["abi"] = abr.reshape(1, -1), abi.reshape(1, -1)

    lam_r, lam_i = dt * a_re, dt * a_im
    grp = S5_GROUP
    n_state = a_re.shape[1]
    jj = jnp.arange(S5_CHUNK + 1, dtype=f32)[None, :, None]
    pmag = jnp.exp(jj * lam_r[:, None, :])
    pw_r, pw_i = pmag * jnp.cos(jj * lam_i[:, None, :]), pmag * jnp.sin(jj * lam_i[:, None, :])
    bt = jnp.transpose(jnp.stack([bbr, bbi]), (0, 1, 3, 2))
    bt_r, bt_i = bt[0], bt[1]
    c4r, c4i = c_re[:, None], c_im[:, None]

    def cmul(pr_, pi_, xr, xi):
        pr_, pi_ = pr_[:, :, None, :], pi_[:, :, None, :]
        return pr_ * xr - pi_ * xi, pr_ * xi + pi_ * xr

    rows128 = lambda t: t.reshape(n_groups, LANES, t.shape[-1])
    car_t, cai_t = cmul(pw_r[:, 1:], pw_i[:, 1:], c4r, c4i)
    ca = jnp.transpose(jnp.stack([rows128(car_t), -rows128(cai_t)]), (0, 1, 3, 2))
    ber, bei = cmul(pw_r[:, S5_CHUNK - 1::-1], pw_i[:, S5_CHUNK - 1::-1], bt_r[:, None], bt_i[:, None])
    ber, bei = rows128(ber), rows128(bei)
    zpad = jnp.zeros((n_groups, S5_CHUNK - 1, n_state), f32)
    lag_r = jnp.concatenate([zpad, pw_r[:, :S5_CHUNK]], axis=1)
    lag_i = jnp.concatenate([zpad, pw_i[:, :S5_CHUNK]], axis=1)
    cp_r, cp_i = cmul(lag_r, lag_i, c4r, c4i)
    n_blk = 2 * S5_CHUNK - 1
    cp_all = jnp.concatenate([cp_r, -cp_i], axis=3).reshape(n_groups, n_blk * grp, 2 * n_state)
    rpad_t = jnp.einsum("gmn,gnc->gmc", cp_all, jnp.concatenate([bbr, bbi], axis=1), precision=hp)
    toe_t = jnp.concatenate([rpad_t[:, (S5_CHUNK - 1 - s) * grp:(S5_CHUNK - 1 - s) * grp + LANES, :]
                             for s in range(S5_CHUNK)], axis=2)
    toe = jnp.transpose(toe_t, (0, 2, 1))

    def pair_diag(m):
        g, r, c = m.shape
        m = m.reshape(g // 2, 2, r, c)
        z = jnp.zeros_like(m[:, 0])
        return jnp.concatenate([jnp.concatenate([m[:, 0], z], axis=2),
                                jnp.concatenate([z, m[:, 1]], axis=2)], axis=1)

    w["w1"] = jnp.concatenate([pair_diag(toe), pair_diag(ber), pair_diag(bei)], axis=2).astype(bf16)
    w["w2"] = jnp.concatenate([pair_diag(ca[0]), pair_diag(ca[1])], axis=1).astype(bf16)
    qr, qi = pw_r[:, S5_CHUNK], pw_i[:, S5_CHUNK]
    aps_r, aps_i = [], []
    for _ in range(n_pow):
        aps_r.append(qr.reshape(-1))
        aps_i.append(qi.reshape(-1))
        qr, qi = qr * qr - qi * qi, 2.0 * qr * qi
    w["apr"], w["api"] = jnp.stack(aps_r), jnp.stack(aps_i)

    w["bc"] = jnp.stack([bt_r, bt_i, c_re, -c_im]).reshape(4, s5w, n_state)
    return w


def kernel(x_prompt, x_sample, state_s5_re, state_s5_im, state_hg, norm_mix_pre, norm_mix_post,
           norm_mlp_pre, norm_mlp_post, w_in, b_in, s5_a_re, s5_a_im, s5_log_dt, s5_b_re, s5_b_im,
           s5_c_re, s5_c_im, s5_d, s5_w_glu, s5_b_glu, hg_lb_logits, hg_norm, w_br_s5, w_br_hg,
           w_out, w_up, w_down):
    depth = w_in.shape[0]
    bsz, seq, dm = x_prompt.shape
    n_seq, steps, _ = x_sample.shape
    n_groups, n_state = s5_a_re.shape[1], s5_a_re.shape[2]
    hw = HG_HEADS * HG_DK
    assert depth == 1 and steps == 4 and n_state == S5_STATE
    n_pow = max(1, int(math.log2(min(PROMPT_TILE, seq) // S5_CHUNK)))
    xp, xs = x_prompt, x_sample
    outs = [[] for _ in range(6)]
    for l in range(depth):
        w = _prep_weights(l, norm_mix_pre, norm_mix_post, norm_mlp_pre, norm_mlp_post, w_in, b_in,
                          s5_a_re, s5_a_im, s5_log_dt, s5_b_re, s5_b_im, s5_c_re, s5_c_im, s5_d,
                          s5_w_glu, s5_b_glu, hg_lb_logits, hg_norm, w_br_s5, w_br_hg, w_out, w_up,
                          w_down, n_pow)
        x1, p_re, p_im, p_hg = _mixer_call(xp, w)
        outs[0].append(p_re.reshape(bsz, n_groups, n_state))
        outs[1].append(p_im.reshape(bsz, n_groups, n_state))
        outs[2].append(p_hg)
        x_sm = xs.reshape(n_seq, steps * dm)
        h0r = state_s5_re[l].reshape(n_seq, n_groups * n_state)
        h0i = state_s5_im[l].reshape(n_seq, n_groups * n_state)
        m5, ghg, gn, qt8, cols, v8, oin, s_re, s_im = _sample_pre_call(x_sm, h0r, h0i, w)
        s_hg, oint = _sample_state_call(state_hg[l], qt8, v8, cols)
        x1_sm = _sample_post_call(x_sm, oint.reshape(n_seq * HG_HEADS * 8, HG_DK), oin, gn, m5, ghg, w)
        yp, y_sm = _mlp_call(x1.reshape(bsz * seq, dm), x1_sm, w)
        xp = yp.reshape(bsz, seq, dm)
        xs = y_sm.reshape(n_seq, steps, dm)
        outs[3].append(s_re.reshape(n_seq, n_groups, n_state))
        outs[4].append(s_im.reshape(n_seq, n_groups, n_state))
        outs[5].append(s_hg)
    return (xp, xs) + tuple(o[0][None] for o in outs)
```

```python
import math

import jax
import jax.numpy as jnp
import numpy as np
from jax import lax
from jax.experimental import pallas as pl
from jax.experimental.pallas import tpu as pltpu

f32 = jnp.float32
bf16 = jnp.bfloat16

NORM_EPS = 1e-6
S5_GROUP = 16
S5_STATE = 64
S5_CHUNK = 8
HG_HEADS = 8
HG_DK = 128
HG_CHUNK = 64
LANES = 128
VMEM_LIMIT = 56 * 1024 * 1024
PROMPT_TILE = 512
MLP_TILE = 512
MLP_SUBTILES = 2
MLP_FCHUNK = 512
SAMPLE_PRE_BLOCK = 64


def _sigmoid(x):
    return 0.5 * (jnp.tanh(0.5 * x) + 1.0)


def _silu(x):
    h = 0.5 * x
    return h + h * jnp.tanh(h)


def _gelu_tanh(x):
    c = math.sqrt(2.0 / math.pi)
    return 0.5 * x * (1.0 + jnp.tanh(c * (x + 0.044715 * (x * x * x))))


def _rms(x):
    return x * lax.rsqrt(jnp.mean(x * x, axis=-1, keepdims=True) + NORM_EPS)


def _dot(a, b):
    return jnp.dot(a, b, preferred_element_type=f32)


def _dot_nt(a, b):
    return lax.dot_general(a, b, (((1,), (1,)), ((), ())), preferred_element_type=f32)


def _dot_tn(a, b):
    return lax.dot_general(a, b, (((0,), (0,)), ((), ())), preferred_element_type=f32)


def _block_transpose8(xs, blk):
    xs = list(xs)
    for d in (4, 2, 1):
        keep = (blk & d) == 0
        for i in range(8):
            if i & d == 0:
                a, b = xs[i], xs[i + d]
                xs[i] = jnp.where(keep, a, pltpu.roll(b, d * S5_GROUP, 1))
                xs[i + d] = jnp.where(keep, pltpu.roll(a, LANES - d * S5_GROUP, 1), b)
    return xs


def _mixer_kernel(x_ref, gpre_ref, win_ref, bin_ref, w1_ref, w2_ref, apr_ref, api_ref, d_ref,
                  wglu_ref, bglu_ref, lb_ref, hgn_ref, wbs_ref, wbh_ref, wout_ref, gpost_ref,
                  y_ref, s5r_ref, s5i_ref, hgs_ref,
                  hb_scr, u_scr, ys_scr, o_scr, yloc_scr, er_scr, ei_scr, cr_scr, ci_scr,
                  st_scr, qfig_scr, yhg_scr, m_scr, fac_scr, el_scr, ghg_scr):
    t_idx = pl.program_id(1)
    n_t = pl.num_programs(1)
    tile = x_ref.shape[1]
    rows = tile // S5_CHUNK
    n_pairs = w1_ref.shape[0]
    n_cols = u_scr.shape[0]
    hw = HG_HEADS * HG_DK
    s5w = n_cols * LANES
    g0 = s5w + 4 * hw

    @pl.when(t_idx == 0)
    def _():
        cr_scr[...] = jnp.zeros_like(cr_scr)
        ci_scr[...] = jnp.zeros_like(ci_scr)
        st_scr[...] = jnp.zeros_like(st_scr)

    x = x_ref[0]
    hb_scr[...] = (_rms(x) * gpre_ref[...]).astype(bf16)
    dm = wbs_ref.shape[1]
    half = tile // 2

    def project(r0, r1, c0, c1):
        qfig_scr[r0:r1, c0:c1] = (_dot(hb_scr[r0:r1, :], win_ref[:, s5w + c0:s5w + c1])
                                  + bin_ref[:, s5w + c0:s5w + c1])

    def gate_hg(j):
        c0, c1 = j * (dm // 4), (j + 1) * (dm // 4)
        ghg_scr[:, c0:c1] = _sigmoid(_dot(hb_scr[...], win_ref[:, g0 + dm + c0:g0 + dm + c1])
                                     + bin_ref[:, g0 + dm + c0:g0 + dm + c1])


    u = _dot(hb_scr[...], win_ref[:, :s5w]) + bin_ref[:, :s5w]
    for j in range(n_cols):
        u_scr[j] = u[:, j * LANES:(j + 1) * LANES]

    project(0, half, 0, hw)
    gate_hg(0)
    blk = lax.broadcasted_iota(jnp.int32, (rows, LANES), 1) // S5_GROUP
    for j in range(n_cols):
        xs = [u_scr[j, pl.ds(s, rows, stride=S5_CHUNK), :] for s in range(S5_CHUNK)]
        ys = _block_transpose8(xs, blk)
        for q in range(4):
            o_scr[j * 4 + q] = jnp.concatenate([ys[2 * q], ys[2 * q + 1]], axis=1).astype(bf16)

    project(0, half, hw, 2 * hw)
    gate_hg(1)
    for p in range(n_pairs):
        r1 = _dot(o_scr[p], w1_ref[p])
        yloc_scr[p] = r1[:, :2 * LANES]
        er_scr[:, p * LANES:(p + 1) * LANES] = r1[:, 2 * LANES:3 * LANES]
        ei_scr[:, p * LANES:(p + 1) * LANES] = r1[:, 3 * LANES:]

    er = er_scr[...]
    ei = ei_scr[...]
    row = lax.broadcasted_iota(jnp.int32, (rows, 1), 0)
    first = row == 0
    cr = cr_scr[...]
    ci = ci_scr[...]
    a_r = apr_ref[0:1, :]
    a_i = api_ref[0:1, :]
    project(0, half, 2 * hw, 3 * hw)
    gate_hg(2)
    er = er + jnp.where(first, a_r * cr - a_i * ci, 0.0)
    ei = ei + jnp.where(first, a_r * ci + a_i * cr, 0.0)
    for k in range(int(math.log2(rows))):
        d = 1 << k
        p_r = apr_ref[k:k + 1, :]
        p_i = api_ref[k:k + 1, :]
        s_r = pltpu.roll(er, d, 0)
        s_i = pltpu.roll(ei, d, 0)
        valid = row >= d
        er, ei = (er + jnp.where(valid, p_r * s_r - p_i * s_i, 0.0),
                  ei + jnp.where(valid, p_r * s_i + p_i * s_r, 0.0))
    hs_r = jnp.where(first, cr, pltpu.roll(er, 1, 0))
    hs_i = jnp.where(first, ci, pltpu.roll(ei, 1, 0))
    cr_scr[...] = er[rows - 1:rows, :]
    ci_scr[...] = ei[rows - 1:rows, :]

    project(0, half, 3 * hw, 4 * hw)
    gate_hg(3)
    for j in range(n_cols):
        halves = []
        for q in range(4):
            p = j * 4 + q
            hp = jnp.concatenate([hs_r[:, p * LANES:(p + 1) * LANES],
                                  hs_i[:, p * LANES:(p + 1) * LANES]], axis=1).astype(bf16)
            yp = yloc_scr[p] + _dot(hp, w2_ref[p])
            halves += [yp[:, :LANES], yp[:, LANES:]]
        zs = _block_transpose8(halves, blk)
        for s in range(S5_CHUNK):
            ys_scr[j, pl.ds(s, rows, stride=S5_CHUNK), :] = zs[s]

    y = jnp.concatenate([ys_scr[j] for j in range(n_cols)], axis=1)
    u = jnp.concatenate([u_scr[j] for j in range(n_cols)], axis=1)
    y = _gelu_tanh(y + d_ref[...] * u)
    z = _dot(y.astype(bf16), wglu_ref[...]) + bglu_ref[...]
    ys5 = (y * _sigmoid(z)).astype(bf16)
    gate_s5 = _sigmoid(_dot(hb_scr[...], win_ref[:, g0:g0 + dm]) + bin_ref[:, g0:g0 + dm])
    m_scr[...] = gate_s5 * _dot(ys5, wbs_ref[...])

    ch = HG_CHUNK
    ri = lax.broadcasted_iota(jnp.int32, (ch, ch), 0)
    ci_ = lax.broadcasted_iota(jnp.int32, (ch, ch), 1)
    causal = ci_ <= ri
    tril2 = jnp.concatenate([causal.astype(bf16)] * 2, axis=1)
    lb = lb_ref[...]
    f_mid = 0.5 * (1.0 + lb)
    f_amp = 0.5 * (1.0 - lb)
    hgn = hgn_ref[...]
    n_chunks = tile // ch
    n_side = n_chunks // 2
    assert n_side == 4 and dm % n_side == 0

    def merge_hg(r0, r1, c0, c1):
        m_scr[r0:r1, c0:c1] = (m_scr[r0:r1, c0:c1]
                               + ghg_scr[r0:r1, c0:c1] * _dot(yhg_scr[r0:r1, :], wbh_ref[:, c0:c1]))

    def chunk_factors(c):
        r0 = c * ch
        slot = c % 2
        q_raw = qfig_scr[pl.ds(r0, ch), 0:hw]
        f_raw = qfig_scr[pl.ds(r0, ch), hw:2 * hw]
        q = _silu(q_raw)
        f = f_mid + f_amp * jnp.tanh(0.5 * f_raw)
        lf = jnp.log(f)
        kk = 1.0 - f
        hi = lf.astype(bf16)
        lo = (lf - hi.astype(f32)).astype(bf16)
        b = _dot(tril2, jnp.concatenate([hi, lo], axis=0))
        mid = b[ch // 2 - 1:ch // 2, :]
        b_last = b[ch - 1:ch, :]
        qh = q * jnp.exp(b - mid)
        kh = kk * jnp.exp(mid - b)
        fac_scr[slot, 0] = qh.astype(bf16)
        fac_scr[slot, 1] = kh.astype(bf16)
        fac_scr[slot, 2] = (qh * jnp.exp(mid)).astype(bf16)
        fac_scr[slot, 3] = (kh * jnp.exp(b_last - mid)).astype(bf16)
        fac_scr[slot, 4] = qfig_scr[pl.ds(r0, ch), 2 * hw:3 * hw].astype(bf16)
        el_scr[slot] = jnp.exp(b_last)

    def chunk_matmuls(c):
        r0 = c * ch
        slot = c % 2
        heads = [slice(h * HG_DK, (h + 1) * HG_DK) for h in range(HG_HEADS)]
        scs = [jnp.where(causal, _dot_nt(fac_scr[slot, 0, :, hs], fac_scr[slot, 1, :, hs]), 0.0).astype(bf16)
               for hs in heads]
        e_last = el_scr[slot]
        outs = []
        for h, hs in enumerate(heads):
            st = st_scr[h]
            vb = fac_scr[slot, 4, :, hs]
            o = _dot(scs[h], vb) + _dot_nt(fac_scr[slot, 2, :, hs], st.astype(bf16))
            st_scr[h] = st * e_last[:, hs] + _dot_tn(vb, fac_scr[slot, 3, :, hs])
            outs.append(_rms(o))
        g_raw = qfig_scr[pl.ds(r0, ch), 3 * hw:4 * hw]
        on = jnp.concatenate(outs, axis=1) * hgn * _silu(g_raw)
        yhg_scr[pl.ds(r0, ch), :] = on.astype(bf16)

    def finish(r0, r1):
        mo = _dot(m_scr[r0:r1, :].astype(bf16), wout_ref[...])
        y_ref[0, r0:r1, :] = x_ref[0, r0:r1, :] + _rms(mo) * gpost_ref[...]

    quarter = half // 2
    side_work = {4: lambda: merge_hg(0, half, 0, dm // 2),
                 5: lambda: merge_hg(0, half, dm // 2, dm),
                 6: lambda: finish(0, half),
                 7: lambda: merge_hg(half, half + quarter, 0, dm)}
    chunk_factors(0)
    for c in range(n_chunks):
        if c + 1 < n_chunks:
            chunk_factors(c + 1)
        chunk_matmuls(c)
        if c < n_side:
            project(half, tile, c * hw, (c + 1) * hw)
        else:
            side_work[c]()
    merge_hg(half + quarter, tile, 0, dm)
    finish(half, tile)

    @pl.when(t_idx == n_t - 1)
    def _():
        s5r_ref[0] = cr_scr[...]
        s5i_ref[0] = ci_scr[...]
        for h in range(HG_HEADS):
            hgs_ref[0, h] = st_scr[h].T


def _const_spec(shape):
    nd = len(shape)
    return pl.BlockSpec(shape, lambda *_: (0,) * nd, pipeline_mode=pl.Buffered(1))


def _mixer_call(x, w):
    bsz, seq, dm = x.shape
    tile = min(PROMPT_TILE, seq)
    assert seq % tile == 0 and tile % HG_CHUNK == 0
    rows = tile // S5_CHUNK
    assert rows & (rows - 1) == 0 and w["apr"].shape[0] >= int(math.log2(rows))
    s5w = w["d"].shape[1]
    n_cols = s5w // LANES
    n_pairs = w["w1"].shape[0]
    sdim = w["apr"].shape[1]
    consts = [w[k] for k in ("gpre", "win", "bin", "w1", "w2", "apr", "api", "d", "wglu", "bglu",
                             "lbrow", "hgnrow", "wbs", "wbh", "wout", "gpost")]
    in_specs = [pl.BlockSpec((1, tile, dm), lambda b, t: (b, t, 0))]
    in_specs += [_const_spec(c.shape) for c in consts]
    out_shape = (jax.ShapeDtypeStruct((bsz, seq, dm), f32),
                 jax.ShapeDtypeStruct((bsz, 1, sdim), f32),
                 jax.ShapeDtypeStruct((bsz, 1, sdim), f32),
                 jax.ShapeDtypeStruct((bsz, HG_HEADS, HG_DK, HG_DK), f32))
    out_specs = (pl.BlockSpec((1, tile, dm), lambda b, t: (b, t, 0)),
                 pl.BlockSpec((1, 1, sdim), lambda b, t: (b, 0, 0)),
                 pl.BlockSpec((1, 1, sdim), lambda b, t: (b, 0, 0)),
                 pl.BlockSpec((1, HG_HEADS, HG_DK, HG_DK), lambda b, t: (b, 0, 0, 0)))
    scratch = [
        pltpu.VMEM((tile, dm), bf16),
        pltpu.VMEM((n_cols, tile, LANES), f32),
        pltpu.VMEM((n_cols, tile, LANES), f32),
        pltpu.VMEM((n_pairs, rows, 2 * LANES), bf16),
        pltpu.VMEM((n_pairs, rows, 2 * LANES), f32),
        pltpu.VMEM((rows, sdim), f32),
        pltpu.VMEM((rows, sdim), f32),
        pltpu.VMEM((1, sdim), f32),
        pltpu.VMEM((1, sdim), f32),
        pltpu.VMEM((HG_HEADS, HG_DK, HG_DK), f32),
        pltpu.VMEM((tile, HG_HEADS * 4 * HG_DK), f32),
        pltpu.VMEM((tile, HG_HEADS * HG_DK), bf16),
        pltpu.VMEM((tile, dm), f32),
        pltpu.VMEM((2, 5, HG_CHUNK, HG_HEADS * HG_DK), bf16),
        pltpu.VMEM((2, 1, HG_HEADS * HG_DK), f32),
        pltpu.VMEM((tile, dm), f32),
    ]
    return pl.pallas_call(
        _mixer_kernel,
        grid=(bsz, seq // tile),
        in_specs=in_specs,
        out_specs=out_specs,
        out_shape=out_shape,
        scratch_shapes=scratch,
        compiler_params=pltpu.CompilerParams(
            dimension_semantics=("arbitrary", "arbitrary"), vmem_limit_bytes=VMEM_LIMIT),
        name="prompt_mixer",
    )(x, *consts)


def _mlp_body(x, gpre_ref, wup, wdn, n_chunks, gpost_ref):
    h2 = (_rms(x) * gpre_ref[...]).astype(bf16)
    acc = None
    for c in range(n_chunks):
        a = _dot(h2, wup(c))
        a = jnp.maximum(a, 0.0)
        part = _dot((a * a).astype(bf16), wdn(c))
        acc = part if acc is None else acc + part
    return x + _rms(acc) * gpost_ref[...]


def _mlp_kernel(xp_ref, gpre_ref, wup_ref, wdn_ref, gpost_ref, s0_ref, qt_ref, v_ref, cols_ref,
                yp_ref, wupb_ref, wdnb_ref, snew_ref, oint_ref, wup_scr, wdn_scr):
    step = pl.program_id(0)
    n_cast = wup_scr.shape[0]

    @pl.when(step < n_cast)
    def _():
        wu = wup_ref[...].astype(bf16)
        wd = wdn_ref[...].astype(bf16)
        wup_scr[step] = wu
        wdn_scr[step] = wd
        wupb_ref[...] = wu
        wdnb_ref[...] = wd

    @pl.when(step >= n_cast)
    def _():
        sub = xp_ref.shape[0] // MLP_SUBTILES
        for r0 in range(0, xp_ref.shape[0], sub):
            yp_ref[r0:r0 + sub, :] = _mlp_body(xp_ref[r0:r0 + sub, :], gpre_ref, lambda c: wup_scr[c],
                                               lambda c: wdn_scr[c], n_cast, gpost_ref)
        _sample_state_body(s0_ref, qt_ref, v_ref, cols_ref, snew_ref, oint_ref)


def _mlp_call(xp2d, w, s0, qt8, v8, cols):
    n, dm = xp2d.shape
    tile = min(MLP_TILE, n)
    assert n % tile == 0
    n_tiles = n // tile
    wup, wdn = w["wup_f32"], w["wdn_f32"]
    dff = wup.shape[1]
    fc = MLP_FCHUNK
    assert dff % fc == 0
    n_cast = dff // fc
    n_seq = s0.shape[0]
    assert n_seq % n_tiles == 0
    nb = n_seq // n_tiles
    slots = HG_HEADS * 8
    blk = lambda s: jnp.minimum(s, n_cast - 1)
    row = lambda s: jnp.maximum(s - n_cast, 0)
    st_blk = (nb, HG_HEADS, HG_DK, HG_DK)
    t4_blk = (nb, HG_HEADS, 8, HG_DK)
    t3_blk = (nb, slots, HG_DK)
    return pl.pallas_call(
        _mlp_kernel,
        grid=(n_cast + n_tiles,),
        in_specs=[pl.BlockSpec((tile, dm), lambda s: (row(s), 0)),
                  _const_spec(w["g2pre"].shape),
                  pl.BlockSpec((dm, fc), lambda s: (0, blk(s))),
                  pl.BlockSpec((fc, dm), lambda s: (blk(s), 0)),
                  _const_spec(w["g2post"].shape),
                  pl.BlockSpec(st_blk, lambda s: (row(s), 0, 0, 0)),
                  pl.BlockSpec(t4_blk, lambda s: (row(s), 0, 0, 0)),
                  pl.BlockSpec(t3_blk, lambda s: (row(s), 0, 0)),
                  pl.BlockSpec(t3_blk, lambda s: (row(s), 0, 0))],
        out_specs=(pl.BlockSpec((tile, dm), lambda s: (row(s), 0)),
                   pl.BlockSpec((dm, fc), lambda s: (0, blk(s))),
                   pl.BlockSpec((fc, dm), lambda s: (blk(s), 0)),
                   pl.BlockSpec(st_blk, lambda s: (row(s), 0, 0, 0)),
                   pl.BlockSpec(t4_blk, lambda s: (row(s), 0, 0, 0))),
        out_shape=(jax.ShapeDtypeStruct((n, dm), f32),
                   jax.ShapeDtypeStruct((dm, dff), bf16),
                   jax.ShapeDtypeStruct((dff, dm), bf16),
                   jax.ShapeDtypeStruct(s0.shape, f32),
                   jax.ShapeDtypeStruct((n_seq, HG_HEADS, 8, HG_DK), f32)),
        scratch_shapes=[pltpu.VMEM((n_cast, dm, fc), bf16), pltpu.VMEM((n_cast, fc, dm), bf16)],
        compiler_params=pltpu.CompilerParams(
            dimension_semantics=("arbitrary",), vmem_limit_bytes=VMEM_LIMIT),
        name="mlp",
    )(xp2d, w["g2pre"], wup, wdn, w["g2post"], s0, qt8.reshape((n_seq,) + t4_blk[1:]),
      v8.reshape(n_seq, slots, HG_DK), cols.reshape(n_seq, slots, HG_DK))


def _sample_mlp_kernel(x_ref, gpre_ref, wup_ref, wdn_ref, gpost_ref, y_ref):
    n_seq = x_ref.shape[0]
    dm = gpre_ref.shape[1]
    steps = x_ref.shape[1] // dm
    fc = MLP_FCHUNK
    x = jnp.concatenate([x_ref[:, t * dm:(t + 1) * dm] for t in range(steps)], axis=0)
    y = _mlp_body(x, gpre_ref, lambda c: wup_ref[:, c * fc:(c + 1) * fc], lambda c: wdn_ref[c * fc:(c + 1) * fc, :],
                  wup_ref.shape[1] // fc, gpost_ref)
    for t in range(steps):
        y_ref[:, t * dm:(t + 1) * dm] = y[t * n_seq:(t + 1) * n_seq, :]


def _sample_mlp_call(x_sm, wup_b, wdn_b, w):
    return pl.pallas_call(
        _sample_mlp_kernel,
        out_shape=jax.ShapeDtypeStruct(x_sm.shape, f32),
        compiler_params=pltpu.CompilerParams(vmem_limit_bytes=VMEM_LIMIT),
        name="sample_mlp",
    )(x_sm, w["g2pre"], wup_b, wdn_b, w["g2post"])


def _sample_pre_kernel(x_ref, h0r_ref, h0i_ref, gpre_ref, win_ref, bin_ref, bc_ref,
                       abr_ref, abi_ref, d_ref, wglu_ref, bglu_ref, lb_ref, hgn_ref, wbs_ref,
                       m5_ref, ghg_ref, gn_ref, qt_ref, cols_ref, v_ref, oin_ref,
                       s5r_ref, s5i_ref, bblk_ref, cblkt_ref):
    n_seq = h0r_ref.shape[0]
    dm = gpre_ref.shape[1]
    steps = x_ref.shape[1] // dm
    sdim = h0r_ref.shape[1]
    hw = HG_HEADS * HG_DK
    s5w = d_ref.shape[1]

    @pl.when(pl.program_id(0) == 0)
    def _():
        n_state = bc_ref.shape[2]
        row_grp = lax.broadcasted_iota(jnp.int32, (s5w, sdim), 0) // S5_GROUP
        col_grp = lax.broadcasted_iota(jnp.int32, (s5w, sdim), 1) // n_state
        own = row_grp == col_grp
        for k, ref in enumerate((bblk_ref, bblk_ref, cblkt_ref, cblkt_ref)):
            tiled = jnp.concatenate([bc_ref[k]] * (sdim // n_state), axis=1)
            c0 = (k % 2) * sdim
            ref[:, c0:c0 + sdim] = jnp.where(own, tiled, 0.0).astype(bf16)

    x = jnp.concatenate([x_ref[:, t * dm:(t + 1) * dm] for t in range(steps)], axis=0)
    hb = (_rms(x) * gpre_ref[...]).astype(bf16)

    u = _dot(hb, win_ref[:, :s5w]) + bin_ref[:, :s5w]
    bu = _dot(u.astype(bf16), bblk_ref[...])
    hr = h0r_ref[...]
    hi = h0i_ref[...]
    a_r = abr_ref[...]
    a_i = abi_ref[...]
    hs = []
    for t in range(steps):
        sl = slice(t * n_seq, (t + 1) * n_seq)
        hr, hi = (a_r * hr - a_i * hi + bu[sl, :sdim], a_r * hi + a_i * hr + bu[sl, sdim:])
        hs.append(jnp.concatenate([hr, hi], axis=1).astype(bf16))
    s5r_ref[...] = hr
    s5i_ref[...] = hi
    y = _dot_nt(jnp.concatenate(hs, axis=0), cblkt_ref[...])
    y = _gelu_tanh(y + d_ref[...] * u)
    z = _dot(y.astype(bf16), wglu_ref[...]) + bglu_ref[...]
    ys5 = (y * _sigmoid(z)).astype(bf16)
    g0 = s5w + 4 * hw
    def put_steps(ref, val):
        wcol = val.shape[1]
        for t in range(steps):
            ref[:, t * wcol:(t + 1) * wcol] = val[t * n_seq:(t + 1) * n_seq, :]

    gates = _dot(hb, win_ref[:, g0:]) + bin_ref[:, g0:]
    put_steps(m5_ref, _sigmoid(gates[:, :dm]) * _dot(ys5, wbs_ref[...]))
    put_steps(ghg_ref, _sigmoid(gates[:, dm:]))

    def proj(i):
        c0 = s5w + i * hw
        return _dot(hb, win_ref[:, c0:c0 + hw]) + bin_ref[:, c0:c0 + hw]

    q_raw = proj(0)
    f_raw = proj(1)
    v = proj(2)
    g_raw = proj(3)
    q = _silu(q_raw)
    lb = lb_ref[...]
    f = 0.5 * (1.0 + lb) + 0.5 * (1.0 - lb) * jnp.tanh(0.5 * f_raw)
    lf = jnp.log(f)
    kk = 1.0 - f
    put_steps(gn_ref, _silu(g_raw) * hgn_ref[...])
    bs = []
    acc = None
    for t in range(steps):
        sl = slice(t * n_seq, (t + 1) * n_seq)
        acc = lf[sl] if acc is None else acc + lf[sl]
        bs.append(acc)
    b_last = bs[-1]

    def put(ref, slot, val):
        for h in range(HG_HEADS):
            ref[pl.ds(h * 8 + slot, n_seq, stride=HG_HEADS * 8), :] = val[:, h * HG_DK:(h + 1) * HG_DK]

    for ref in (qt_ref, cols_ref, v_ref):
        ref[...] = jnp.zeros_like(ref)
    put(cols_ref, steps, jnp.exp(b_last))
    for t in range(steps):
        sl = slice(t * n_seq, (t + 1) * n_seq)
        put(qt_ref, t, q[sl] * jnp.exp(bs[t]))
        put(cols_ref, t, kk[sl] * jnp.exp(b_last - bs[t]))
        put(v_ref, t, v[sl])
        o_t = None
        for s in range(t + 1):
            sls = slice(s * n_seq, (s + 1) * n_seq)
            prod = q[sl] * kk[sls] * jnp.exp(bs[t] - bs[s])
            parts = []
            for h in range(HG_HEADS):
                hs = slice(h * HG_DK, (h + 1) * HG_DK)
                wgt = jnp.sum(prod[:, hs], axis=1, keepdims=True)
                parts.append(wgt * v[sls, hs])
            term = jnp.concatenate(parts, axis=1)
            o_t = term if o_t is None else o_t + term
        oin_ref[:, t * hw:(t + 1) * hw] = o_t


def _sample_pre_call(x_sm, h0r, h0i, w):
    n_seq, sdim = h0r.shape
    dm = w["gpre"].shape[1]
    steps = x_sm.shape[1] // dm
    hw = HG_HEADS * HG_DK
    nb = SAMPLE_PRE_BLOCK
    assert n_seq % nb == 0
    s5w = w["d"].shape[1]
    consts = [w[k] for k in ("gpre", "win", "bin", "bc", "abr", "abi", "d",
                             "wglu", "bglu", "lbrow", "hgnrow", "wbs")]
    slots = HG_HEADS * 8
    shapes = ((1, steps * dm),
              (1, steps * dm),
              (1, steps * hw),
              (slots, HG_DK),
              (slots, HG_DK),
              (slots, HG_DK),
              (1, steps * hw),
              (1, sdim), (1, sdim))
    seq_spec = lambda shp: pl.BlockSpec((nb * shp[0], shp[1]), lambda i: (i, 0))
    return pl.pallas_call(
        _sample_pre_kernel,
        grid=(n_seq // nb,),
        in_specs=([seq_spec((1, steps * dm)), seq_spec((1, sdim)), seq_spec((1, sdim))]
                  + [_const_spec(c.shape) for c in consts]),
        out_specs=tuple(seq_spec(shp) for shp in shapes),
        out_shape=tuple(jax.ShapeDtypeStruct((n_seq * shp[0], shp[1]), f32) for shp in shapes),
        scratch_shapes=[pltpu.VMEM((s5w, 2 * sdim), bf16),
                        pltpu.VMEM((s5w, 2 * sdim), bf16)],
        compiler_params=pltpu.CompilerParams(
            dimension_semantics=("arbitrary",), vmem_limit_bytes=VMEM_LIMIT),
        name="sample_pre",
    )(x_sm, h0r, h0i, *consts)


def _sample_state_body(s0_ref, qt_ref, v_ref, cols_ref, snew_ref, oint_ref):
    nb = s0_ref.shape[0]
    nrow = cols_ref.shape[1]
    hw = HG_HEADS * HG_DK
    row_head = lax.broadcasted_iota(jnp.int32, (nrow, hw), 0) // 8
    col_head = lax.broadcasted_iota(jnp.int32, (nrow, hw), 1) // HG_DK
    diag = row_head == col_head
    pad = jnp.zeros((LANES - nrow, HG_DK), f32)
    for i in range(nb):
        xt = jnp.concatenate([cols_ref[i], pad], axis=0).T
        vbd = jnp.where(diag, jnp.concatenate([v_ref[i]] * HG_HEADS, axis=1), 0.0)
        vbd = jnp.concatenate([vbd, jnp.zeros((LANES - nrow, hw), f32)], axis=0).astype(bf16)
        ds = _dot(xt.astype(bf16), vbd)
        for h in range(HG_HEADS):
            s0 = s0_ref[i, h]
            oint_ref[i, h] = _dot(qt_ref[i, h].astype(bf16), s0.astype(bf16))
            snew_ref[i, h] = xt[:, h * 8 + 4:h * 8 + 5] * s0 + ds[:, h * HG_DK:(h + 1) * HG_DK]


def _sample_post_kernel(x_ref, oint_ref, oin_ref, gn_ref, m5_ref, ghg_ref, wbh_ref, wout_ref, gpost_ref,
                        y_ref):
    n_seq = x_ref.shape[0]
    dm = gpost_ref.shape[1]
    steps = x_ref.shape[1] // dm
    hw = HG_HEADS * HG_DK

    def steps_major(ref, wcol):
        return jnp.concatenate([ref[:, t * wcol:(t + 1) * wcol] for t in range(steps)], axis=0)

    x = steps_major(x_ref, dm)
    oint = jnp.concatenate(
        [jnp.concatenate([oint_ref[pl.ds(h * 8 + t, n_seq, stride=HG_HEADS * 8), :] for h in range(HG_HEADS)],
                         axis=1) for t in range(steps)], axis=0)
    o = oint + steps_major(oin_ref, hw)
    parts = []
    for h in range(HG_HEADS):
        hs = slice(h * HG_DK, (h + 1) * HG_DK)
        parts.append(_rms(o[:, hs]))
    yhg = (jnp.concatenate(parts, axis=1) * steps_major(gn_ref, hw)).astype(bf16)
    merged = steps_major(m5_ref, dm) + steps_major(ghg_ref, dm) * _dot(yhg, wbh_ref[...])
    mo = _dot(merged.astype(bf16), wout_ref[...])
    y = x + _rms(mo) * gpost_ref[...]
    for t in range(steps):
        y_ref[:, t * dm:(t + 1) * dm] = y[t * n_seq:(t + 1) * n_seq, :]


def _sample_post_call(x_sm, oint, oin, gn, m5, ghg, w):
    ins = [x_sm, oint, oin, gn, m5, ghg] + [w[k] for k in ("wbh", "wout", "gpost")]
    return pl.pallas_call(
        _sample_post_kernel,
        out_shape=jax.ShapeDtypeStruct(x_sm.shape, f32),
        compiler_params=pltpu.CompilerParams(vmem_limit_bytes=VMEM_LIMIT),
        name="sample_post",
    )(*ins)


def _prep_weights(l, norm_mix_pre, norm_mix_post, norm_mlp_pre, norm_mlp_post, w_in, b_in,
                  s5_a_re, s5_a_im, s5_log_dt, s5_b_re, s5_b_im, s5_c_re, s5_c_im, s5_d, s5_w_glu,
                  s5_b_glu, hg_lb_logits, hg_norm, w_br_s5, w_br_hg, w_out, w_up, w_down, n_pow):
    hp = lax.Precision.HIGHEST
    s5w = s5_d.shape[1]
    n_groups = s5w // S5_GROUP
    w = {}
    row = lambda a: a.astype(f32).reshape(1, -1)
    w["gpre"], w["gpost"] = row(norm_mix_pre[l]), row(norm_mix_post[l])
    w["g2pre"], w["g2post"] = row(norm_mlp_pre[l]), row(norm_mlp_post[l])
    w["win"], w["bin"] = w_in[l].astype(bf16), row(b_in[l])
    lb_all = jnp.cumsum(jax.nn.softmax(hg_lb_logits.astype(f32), axis=0), axis=0)
    w["lbrow"] = row(lb_all[l])
    w["hgnrow"] = row(jnp.tile(hg_norm[l].astype(f32), HG_HEADS))
    w["d"] = row(s5_d[l])
    w["wglu"], w["bglu"] = s5_w_glu[l].astype(bf16), row(s5_b_glu[l])
    w["wbs"], w["wbh"], w["wout"] = w_br_s5[l].astype(bf16), w_br_hg[l].astype(bf16), w_out[l].astype(bf16)
    w["wup_f32"], w["wdn_f32"] = w_up[l], w_down[l]

    a_re, a_im = s5_a_re[l].astype(f32), s5_a_im[l].astype(f32)
    dt = jnp.exp(s5_log_dt[l].astype(f32))[:, None]
    mag = jnp.exp(dt * a_re)
    abr, abi = mag * jnp.cos(dt * a_im), mag * jnp.sin(dt * a_im)
    den = a_re * a_re + a_im * a_im
    nr, ni = abr - 1.0, abi
    cfr, cfi = (nr * a_re + ni * a_im) / den, (ni * a_re - nr * a_im) / den
    b_re, b_im = s5_b_re[l].astype(f32), s5_b_im[l].astype(f32)
    bbr = cfr[..., None] * b_re - cfi[..., None] * b_im
    bbi = cfr[..., None] * b_im + cfi[..., None] * b_re
    c_re, c_im = s5_c_re[l].astype(f32), s5_c_im[l].astype(f32)
    w["abr"], w["abi"] = abr.reshape(1, -1), abi.reshape(1, -1)

    lam_r, lam_i = dt * a_re, dt * a_im
    grp = S5_GROUP
    n_state = a_re.shape[1]
    jj = jnp.arange(S5_CHUNK + 1, dtype=f32)[None, :, None]
    pmag = jnp.exp(jj * lam_r[:, None, :])
    pw_r, pw_i = pmag * jnp.cos(jj * lam_i[:, None, :]), pmag * jnp.sin(jj * lam_i[:, None, :])
    bt = jnp.transpose(jnp.stack([bbr, bbi]), (0, 1, 3, 2))
    bt_r, bt_i = bt[0], bt[1]
    c4r, c4i = c_re[:, None], c_im[:, None]

    def cmul(pr_, pi_, xr, xi):
        pr_, pi_ = pr_[:, :, None, :], pi_[:, :, None, :]
        return pr_ * xr - pi_ * xi, pr_ * xi + pi_ * xr

    rows128 = lambda t: t.reshape(n_groups, LANES, t.shape[-1])
    car_t, cai_t = cmul(pw_r[:, 1:], pw_i[:, 1:], c4r, c4i)
    ca = jnp.transpose(jnp.stack([rows128(car_t), -rows128(cai_t)]), (0, 1, 3, 2))
    ber, bei = cmul(pw_r[:, S5_CHUNK - 1::-1], pw_i[:, S5_CHUNK - 1::-1], bt_r[:, None], bt_i[:, None])
    ber, bei = rows128(ber), rows128(bei)
    zpad = jnp.zeros((n_groups, S5_CHUNK - 1, n_state), f32)
    lag_r = jnp.concatenate([zpad, pw_r[:, :S5_CHUNK]], axis=1)
    lag_i = jnp.concatenate([zpad, pw_i[:, :S5_CHUNK]], axis=1)
    cp_r, cp_i = cmul(lag_r, lag_i, c4r, c4i)
    n_blk = 2 * S5_CHUNK - 1
    cp_all = jnp.concatenate([cp_r, -cp_i], axis=3).reshape(n_groups, n_blk * grp, 2 * n_state)
    rpad_t = jnp.einsum("gmn,gnc->gmc", cp_all, jnp.concatenate([bbr, bbi], axis=1), precision=hp)
    toe_t = jnp.concatenate([rpad_t[:, (S5_CHUNK - 1 - s) * grp:(S5_CHUNK - 1 - s) * grp + LANES, :]
                             for s in range(S5_CHUNK)], axis=2)
    toe = jnp.transpose(toe_t, (0, 2, 1))

    def pair_diag(m):
        g, r, c = m.shape
        m = m.reshape(g // 2, 2, r, c)
        z = jnp.zeros_like(m[:, 0])
        return jnp.concatenate([jnp.concatenate([m[:, 0], z], axis=2),
                                jnp.concatenate([z, m[:, 1]], axis=2)], axis=1)

    w["w1"] = jnp.concatenate([pair_diag(toe), pair_diag(ber), pair_diag(bei)], axis=2).astype(bf16)
    w["w2"] = jnp.concatenate([pair_diag(ca[0]), pair_diag(ca[1])], axis=1).astype(bf16)
    qr, qi = pw_r[:, S5_CHUNK], pw_i[:, S5_CHUNK]
    aps_r, aps_i = [], []
    for _ in range(n_pow):
        aps_r.append(qr.reshape(-1))
        aps_i.append(qi.reshape(-1))
        qr, qi = qr * qr - qi * qi, 2.0 * qr * qi
    w["apr"], w["api"] = jnp.stack(aps_r), jnp.stack(aps_i)

    w["bc"] = jnp.stack([bt_r, bt_i, c_re, -c_im]).reshape(4, s5w, n_state)
    return w


def kernel(x_prompt, x_sample, state_s5_re, state_s5_im, state_hg, norm_mix_pre, norm_mix_post,
           norm_mlp_pre, norm_mlp_post, w_in, b_in, s5_a_re, s5_a_im, s5_log_dt, s5_b_re, s5_b_im,
           s5_c_re, s5_c_im, s5_d, s5_w_glu, s5_b_glu, hg_lb_logits, hg_norm, w_br_s5, w_br_hg,
           w_out, w_up, w_down):
    depth = w_in.shape[0]
    bsz, seq, dm = x_prompt.shape
    n_seq, steps, _ = x_sample.shape
    n_groups, n_state = s5_a_re.shape[1], s5_a_re.shape[2]
    hw = HG_HEADS * HG_DK
    assert depth == 1 and steps == 4 and n_state == S5_STATE
    n_pow = max(1, int(math.log2(min(PROMPT_TILE, seq) // S5_CHUNK)))
    xp, xs = x_prompt, x_sample
    outs = [[] for _ in range(6)]
    for l in range(depth):
        w = _prep_weights(l, norm_mix_pre, norm_mix_post, norm_mlp_pre, norm_mlp_post, w_in, b_in,
                          s5_a_re, s5_a_im, s5_log_dt, s5_b_re, s5_b_im, s5_c_re, s5_c_im, s5_d,
                          s5_w_glu, s5_b_glu, hg_lb_logits, hg_norm, w_br_s5, w_br_hg, w_out, w_up,
                          w_down, n_pow)
        x1, p_re, p_im, p_hg = _mixer_call(xp, w)
        outs[0].append(p_re.reshape(bsz, n_groups, n_state))
        outs[1].append(p_im.reshape(bsz, n_groups, n_state))
        outs[2].append(p_hg)
        x_sm = xs.reshape(n_seq, steps * dm)
        h0r = state_s5_re[l].reshape(n_seq, n_groups * n_state)
        h0i = state_s5_im[l].reshape(n_seq, n_groups * n_state)
        m5, ghg, gn, qt8, cols, v8, oin, s_re, s_im = _sample_pre_call(x_sm, h0r, h0i, w)
        yp, wup_b, wdn_b, s_hg, oint = _mlp_call(x1.reshape(bsz * seq, dm), w, state_hg[l], qt8, v8, cols)
        xp = yp.reshape(bsz, seq, dm)
        x1_sm = _sample_post_call(x_sm, oint.reshape(n_seq * HG_HEADS * 8, HG_DK), oin, gn, m5, ghg, w)
        xs = _sample_mlp_call(x1_sm, wup_b, wdn_b, w).reshape(n_seq, steps, dm)
        outs[3].append(s_re.reshape(n_seq, n_groups, n_state))
        outs[4].append(s_im.reshape(n_seq, n_groups, n_state))
        outs[5].append(s_hg)
    return (xp, xs) + tuple(o[0][None] for o in outs)
```

```python
import math

import jax
import jax.numpy as jnp
import numpy as np
from jax import lax
from jax.experimental import pallas as pl
from jax.experimental.pallas import tpu as pltpu

f32 = jnp.float32
bf16 = jnp.bfloat16

NORM_EPS = 1e-6
S5_GROUP = 16
S5_STATE = 64
S5_CHUNK = 8
HG_HEADS = 8
HG_DK = 128
HG_CHUNK = 64
LANES = 128
VMEM_LIMIT = 56 * 1024 * 1024
PROMPT_TILE = 512
MLP_TILE = 1024
MLP_SUBTILES = 2
MLP_FCHUNK = 512
SAMPLE_SEQ_BLOCK = 16
SAMPLE_PRE_BLOCK = 64


def _sigmoid(x):
    return 0.5 * (jnp.tanh(0.5 * x) + 1.0)


def _silu(x):
    h = 0.5 * x
    return h + h * jnp.tanh(h)


def _gelu_tanh(x):
    c = math.sqrt(2.0 / math.pi)
    return 0.5 * x * (1.0 + jnp.tanh(c * (x + 0.044715 * (x * x * x))))


def _rms(x):
    return x * lax.rsqrt(jnp.mean(x * x, axis=-1, keepdims=True) + NORM_EPS)


def _dot(a, b):
    return jnp.dot(a, b, preferred_element_type=f32)


def _dot_nt(a, b):
    return lax.dot_general(a, b, (((1,), (1,)), ((), ())), preferred_element_type=f32)


def _dot_tn(a, b):
    return lax.dot_general(a, b, (((0,), (0,)), ((), ())), preferred_element_type=f32)


def _block_transpose8(xs, blk):
    xs = list(xs)
    for d in (4, 2, 1):
        keep = (blk & d) == 0
        for i in range(8):
            if i & d == 0:
                a, b = xs[i], xs[i + d]
                xs[i] = jnp.where(keep, a, pltpu.roll(b, d * S5_GROUP, 1))
                xs[i + d] = jnp.where(keep, pltpu.roll(a, LANES - d * S5_GROUP, 1), b)
    return xs


def _mixer_kernel(x_ref, gpre_ref, win_ref, bin_ref, w1_ref, w2_ref, apr_ref, api_ref, d_ref,
                  wglu_ref, bglu_ref, lb_ref, hgn_ref, wbs_ref, wbh_ref, wout_ref, gpost_ref,
                  y_ref, s5r_ref, s5i_ref, hgs_ref,
                  hb_scr, u_scr, ys_scr, o_scr, yloc_scr, er_scr, ei_scr, cr_scr, ci_scr,
                  st_scr, qfig_scr, yhg_scr, m_scr, fac_scr, el_scr, ghg_scr):
    t_idx = pl.program_id(1)
    n_t = pl.num_programs(1)
    tile = x_ref.shape[1]
    rows = tile // S5_CHUNK
    n_pairs = w1_ref.shape[0]
    n_cols = u_scr.shape[0]
    hw = HG_HEADS * HG_DK
    s5w = n_cols * LANES
    g0 = s5w + 4 * hw

    @pl.when(t_idx == 0)
    def _():
        cr_scr[...] = jnp.zeros_like(cr_scr)
        ci_scr[...] = jnp.zeros_like(ci_scr)
        st_scr[...] = jnp.zeros_like(st_scr)

    x = x_ref[0]
    hb_scr[...] = (_rms(x) * gpre_ref[...]).astype(bf16)
    dm = wbs_ref.shape[1]
    half = tile // 2

    def project(r0, r1, c0, c1):
        qfig_scr[r0:r1, c0:c1] = (_dot(hb_scr[r0:r1, :], win_ref[:, s5w + c0:s5w + c1])
                                  + bin_ref[:, s5w + c0:s5w + c1])

    def gate_hg(j):
        c0, c1 = j * (dm // 4), (j + 1) * (dm // 4)
        ghg_scr[:, c0:c1] = _sigmoid(_dot(hb_scr[...], win_ref[:, g0 + dm + c0:g0 + dm + c1])
                                     + bin_ref[:, g0 + dm + c0:g0 + dm + c1])


    u = _dot(hb_scr[...], win_ref[:, :s5w]) + bin_ref[:, :s5w]
    for j in range(n_cols):
        u_scr[j] = u[:, j * LANES:(j + 1) * LANES]

    project(0, half, 0, hw)
    gate_hg(0)
    blk = lax.broadcasted_iota(jnp.int32, (rows, LANES), 1) // S5_GROUP
    for j in range(n_cols):
        xs = [u_scr[j, pl.ds(s, rows, stride=S5_CHUNK), :] for s in range(S5_CHUNK)]
        ys = _block_transpose8(xs, blk)
        for q in range(4):
            o_scr[j * 4 + q] = jnp.concatenate([ys[2 * q], ys[2 * q + 1]], axis=1).astype(bf16)

    project(0, half, hw, 2 * hw)
    gate_hg(1)
    for p in range(n_pairs):
        r1 = _dot(o_scr[p], w1_ref[p])
        yloc_scr[p] = r1[:, :2 * LANES]
        er_scr[:, p * LANES:(p + 1) * LANES] = r1[:, 2 * LANES:3 * LANES]
        ei_scr[:, p * LANES:(p + 1) * LANES] = r1[:, 3 * LANES:]

    er = er_scr[...]
    ei = ei_scr[...]
    row = lax.broadcasted_iota(jnp.int32, (rows, 1), 0)
    first = row == 0
    cr = cr_scr[...]
    ci = ci_scr[...]
    a_r = apr_ref[0:1, :]
    a_i = api_ref[0:1, :]
    project(0, half, 2 * hw, 3 * hw)
    gate_hg(2)
    er = er + jnp.where(first, a_r * cr - a_i * ci, 0.0)
    ei = ei + jnp.where(first, a_r * ci + a_i * cr, 0.0)
    for k in range(int(math.log2(rows))):
        d = 1 << k
        p_r = apr_ref[k:k + 1, :]
        p_i = api_ref[k:k + 1, :]
        s_r = pltpu.roll(er, d, 0)
        s_i = pltpu.roll(ei, d, 0)
        valid = row >= d
        er, ei = (er + jnp.where(valid, p_r * s_r - p_i * s_i, 0.0),
                  ei + jnp.where(valid, p_r * s_i + p_i * s_r, 0.0))
    hs_r = jnp.where(first, cr, pltpu.roll(er, 1, 0))
    hs_i = jnp.where(first, ci, pltpu.roll(ei, 1, 0))
    cr_scr[...] = er[rows - 1:rows, :]
    ci_scr[...] = ei[rows - 1:rows, :]

    project(0, half, 3 * hw, 4 * hw)
    gate_hg(3)
    for j in range(n_cols):
        halves = []
        for q in range(4):
            p = j * 4 + q
            hp = jnp.concatenate([hs_r[:, p * LANES:(p + 1) * LANES],
                                  hs_i[:, p * LANES:(p + 1) * LANES]], axis=1).astype(bf16)
            yp = yloc_scr[p] + _dot(hp, w2_ref[p])
            halves += [yp[:, :LANES], yp[:, LANES:]]
        zs = _block_transpose8(halves, blk)
        for s in range(S5_CHUNK):
            ys_scr[j, pl.ds(s, rows, stride=S5_CHUNK), :] = zs[s]

    y = jnp.concatenate([ys_scr[j] for j in range(n_cols)], axis=1)
    u = jnp.concatenate([u_scr[j] for j in range(n_cols)], axis=1)
    y = _gelu_tanh(y + d_ref[...] * u)
    z = _dot(y.astype(bf16), wglu_ref[...]) + bglu_ref[...]
    ys5 = (y * _sigmoid(z)).astype(bf16)
    gate_s5 = _sigmoid(_dot(hb_scr[...], win_ref[:, g0:g0 + dm]) + bin_ref[:, g0:g0 + dm])
    m_scr[...] = gate_s5 * _dot(ys5, wbs_ref[...])

    ch = HG_CHUNK
    ri = lax.broadcasted_iota(jnp.int32, (ch, ch), 0)
    ci_ = lax.broadcasted_iota(jnp.int32, (ch, ch), 1)
    causal = ci_ <= ri
    tril2 = jnp.concatenate([causal.astype(bf16)] * 2, axis=1)
    lb = lb_ref[...]
    f_mid = 0.5 * (1.0 + lb)
    f_amp = 0.5 * (1.0 - lb)
    hgn = hgn_ref[...]
    n_chunks = tile // ch
    n_side = n_chunks // 2
    assert n_side == 4 and dm % n_side == 0

    def merge_hg(r0, r1, c0, c1):
        m_scr[r0:r1, c0:c1] = (m_scr[r0:r1, c0:c1]
                               + ghg_scr[r0:r1, c0:c1] * _dot(yhg_scr[r0:r1, :], wbh_ref[:, c0:c1]))

    def chunk_factors(c):
        r0 = c * ch
        slot = c % 2
        q_raw = qfig_scr[pl.ds(r0, ch), 0:hw]
        f_raw = qfig_scr[pl.ds(r0, ch), hw:2 * hw]
        q = _silu(q_raw)
        f = f_mid + f_amp * jnp.tanh(0.5 * f_raw)
        lf = jnp.log(f)
        kk = 1.0 - f
        hi = lf.astype(bf16)
        lo = (lf - hi.astype(f32)).astype(bf16)
        b = _dot(tril2, jnp.concatenate([hi, lo], axis=0))
        mid = b[ch // 2 - 1:ch // 2, :]
        b_last = b[ch - 1:ch, :]
        qh = q * jnp.exp(b - mid)
        kh = kk * jnp.exp(mid - b)
        fac_scr[slot, 0] = qh.astype(bf16)
        fac_scr[slot, 1] = kh.astype(bf16)
        fac_scr[slot, 2] = (qh * jnp.exp(mid)).astype(bf16)
        fac_scr[slot, 3] = (kh * jnp.exp(b_last - mid)).astype(bf16)
        fac_scr[slot, 4] = qfig_scr[pl.ds(r0, ch), 2 * hw:3 * hw].astype(bf16)
        el_scr[slot] = jnp.exp(b_last)

    def chunk_matmuls(c):
        r0 = c * ch
        slot = c % 2
        heads = [slice(h * HG_DK, (h + 1) * HG_DK) for h in range(HG_HEADS)]
        scs = [jnp.where(causal, _dot_nt(fac_scr[slot, 0, :, hs], fac_scr[slot, 1, :, hs]), 0.0).astype(bf16)
               for hs in heads]
        e_last = el_scr[slot]
        outs = []
        for h, hs in enumerate(heads):
            st = st_scr[h]
            vb = fac_scr[slot, 4, :, hs]
            o = _dot(scs[h], vb) + _dot_nt(fac_scr[slot, 2, :, hs], st.astype(bf16))
            st_scr[h] = st * e_last[:, hs] + _dot_tn(vb, fac_scr[slot, 3, :, hs])
            outs.append(_rms(o))
        g_raw = qfig_scr[pl.ds(r0, ch), 3 * hw:4 * hw]
        on = jnp.concatenate(outs, axis=1) * hgn * _silu(g_raw)
        yhg_scr[pl.ds(r0, ch), :] = on.astype(bf16)

    def finish(r0, r1):
        mo = _dot(m_scr[r0:r1, :].astype(bf16), wout_ref[...])
        y_ref[0, r0:r1, :] = x_ref[0, r0:r1, :] + _rms(mo) * gpost_ref[...]

    quarter = half // 2
    side_work = {4: lambda: merge_hg(0, half, 0, dm // 2),
                 5: lambda: merge_hg(0, half, dm // 2, dm),
                 6: lambda: finish(0, half),
                 7: lambda: merge_hg(half, half + quarter, 0, dm)}
    chunk_factors(0)
    for c in range(n_chunks):
        if c + 1 < n_chunks:
            chunk_factors(c + 1)
        chunk_matmuls(c)
        if c < n_side:
            project(half, tile, c * hw, (c + 1) * hw)
        else:
            side_work[c]()
    merge_hg(half + quarter, tile, 0, dm)
    finish(half, tile)

    @pl.when(t_idx == n_t - 1)
    def _():
        s5r_ref[0] = cr_scr[...]
        s5i_ref[0] = ci_scr[...]
        for h in range(HG_HEADS):
            hgs_ref[0, h] = st_scr[h].T


def _const_spec(shape):
    nd = len(shape)
    return pl.BlockSpec(shape, lambda *_: (0,) * nd, pipeline_mode=pl.Buffered(1))


def _mixer_call(x, w):
    bsz, seq, dm = x.shape
    tile = min(PROMPT_TILE, seq)
    assert seq % tile == 0 and tile % HG_CHUNK == 0
    rows = tile // S5_CHUNK
    assert rows & (rows - 1) == 0 and w["apr"].shape[0] >= int(math.log2(rows))
    s5w = w["d"].shape[1]
    n_cols = s5w // LANES
    n_pairs = w["w1"].shape[0]
    sdim = w["apr"].shape[1]
    consts = [w[k] for k in ("gpre", "win", "bin", "w1", "w2", "apr", "api", "d", "wglu", "bglu",
                             "lbrow", "hgnrow", "wbs", "wbh", "wout", "gpost")]
    in_specs = [pl.BlockSpec((1, tile, dm), lambda b, t: (b, t, 0))]
    in_specs += [_const_spec(c.shape) for c in consts]
    out_shape = (jax.ShapeDtypeStruct((bsz, seq, dm), f32),
                 jax.ShapeDtypeStruct((bsz, 1, sdim), f32),
                 jax.ShapeDtypeStruct((bsz, 1, sdim), f32),
                 jax.ShapeDtypeStruct((bsz, HG_HEADS, HG_DK, HG_DK), f32))
    out_specs = (pl.BlockSpec((1, tile, dm), lambda b, t: (b, t, 0)),
                 pl.BlockSpec((1, 1, sdim), lambda b, t: (b, 0, 0)),
                 pl.BlockSpec((1, 1, sdim), lambda b, t: (b, 0, 0)),
                 pl.BlockSpec((1, HG_HEADS, HG_DK, HG_DK), lambda b, t: (b, 0, 0, 0)))
    scratch = [
        pltpu.VMEM((tile, dm), bf16),
        pltpu.VMEM((n_cols, tile, LANES), f32),
        pltpu.VMEM((n_cols, tile, LANES), f32),
        pltpu.VMEM((n_pairs, rows, 2 * LANES), bf16),
        pltpu.VMEM((n_pairs, rows, 2 * LANES), f32),
        pltpu.VMEM((rows, sdim), f32),
        pltpu.VMEM((rows, sdim), f32),
        pltpu.VMEM((1, sdim), f32),
        pltpu.VMEM((1, sdim), f32),
        pltpu.VMEM((HG_HEADS, HG_DK, HG_DK), f32),
        pltpu.VMEM((tile, HG_HEADS * 4 * HG_DK), f32),
        pltpu.VMEM((tile, HG_HEADS * HG_DK), bf16),
        pltpu.VMEM((tile, dm), f32),
        pltpu.VMEM((2, 5, HG_CHUNK, HG_HEADS * HG_DK), bf16),
        pltpu.VMEM((2, 1, HG_HEADS * HG_DK), f32),
        pltpu.VMEM((tile, dm), f32),
    ]
    return pl.pallas_call(
        _mixer_kernel,
        grid=(bsz, seq // tile),
        in_specs=in_specs,
        out_specs=out_specs,
        out_shape=out_shape,
        scratch_shapes=scratch,
        compiler_params=pltpu.CompilerParams(
            dimension_semantics=("arbitrary", "arbitrary"), vmem_limit_bytes=VMEM_LIMIT),
        name="prompt_mixer",
    )(x, *consts)


def _mlp_body(x, gpre_ref, wup_scr, wdn_scr, gpost_ref):
    h2 = (_rms(x) * gpre_ref[...]).astype(bf16)
    acc = None
    for c in range(wup_scr.shape[0]):
        a = _dot(h2, wup_scr[c])
        a = jnp.maximum(a, 0.0)
        part = _dot((a * a).astype(bf16), wdn_scr[c])
        acc = part if acc is None else acc + part
    return x + _rms(acc) * gpost_ref[...]


def _mlp_kernel(xp_ref, xs_ref, gpre_ref, wup_ref, wdn_ref, gpost_ref, yp_ref, ys_ref, wup_scr, wdn_scr):
    step = pl.program_id(0)
    n_cast = wup_scr.shape[0]
    last = pl.num_programs(0) - 1
    dm = gpre_ref.shape[1]

    @pl.when(step < n_cast)
    def _():
        wup_scr[step] = wup_ref[...].astype(bf16)
        wdn_scr[step] = wdn_ref[...].astype(bf16)

    @pl.when(jnp.logical_and(step >= n_cast, step < last))
    def _():
        sub = xp_ref.shape[0] // MLP_SUBTILES
        for r0 in range(0, xp_ref.shape[0], sub):
            yp_ref[r0:r0 + sub, :] = _mlp_body(xp_ref[r0:r0 + sub, :], gpre_ref, wup_scr, wdn_scr, gpost_ref)

    @pl.when(step == last)
    def _():
        n_seq = xs_ref.shape[0]
        steps = xs_ref.shape[1] // dm
        x = jnp.concatenate([xs_ref[:, t * dm:(t + 1) * dm] for t in range(steps)], axis=0)
        y = _mlp_body(x, gpre_ref, wup_scr, wdn_scr, gpost_ref)
        for t in range(steps):
            ys_ref[:, t, :] = y[t * n_seq:(t + 1) * n_seq, :]


def _mlp_call(xp2d, xs_sm, w):
    n, dm = xp2d.shape
    tile = min(MLP_TILE, n)
    assert n % tile == 0
    n_tiles = n // tile
    wup, wdn = w["wup_f32"], w["wdn_f32"]
    dff = wup.shape[1]
    fc = MLP_FCHUNK
    assert dff % fc == 0
    n_cast = dff // fc
    blk = lambda s: jnp.minimum(s, n_cast - 1)
    row = lambda s: jnp.clip(s - n_cast, 0, n_tiles - 1)
    ys_shape = (xs_sm.shape[0], xs_sm.shape[1] // dm, dm)
    return pl.pallas_call(
        _mlp_kernel,
        grid=(n_cast + n_tiles + 1,),
        in_specs=[pl.BlockSpec((tile, dm), lambda s: (row(s), 0)),
                  _const_spec(xs_sm.shape),
                  _const_spec(w["g2pre"].shape),
                  pl.BlockSpec((dm, fc), lambda s: (0, blk(s))),
                  pl.BlockSpec((fc, dm), lambda s: (blk(s), 0)),
                  _const_spec(w["g2post"].shape)],
        out_specs=(pl.BlockSpec((tile, dm), lambda s: (row(s), 0)),
                   pl.BlockSpec(ys_shape, lambda s: (0, 0, 0))),
        out_shape=(jax.ShapeDtypeStruct((n, dm), f32), jax.ShapeDtypeStruct(ys_shape, f32)),
        scratch_shapes=[pltpu.VMEM((n_cast, dm, fc), bf16), pltpu.VMEM((n_cast, fc, dm), bf16)],
        compiler_params=pltpu.CompilerParams(
            dimension_semantics=("arbitrary",), vmem_limit_bytes=VMEM_LIMIT),
        name="mlp",
    )(xp2d, xs_sm, w["g2pre"], wup, wdn, w["g2post"])


def _sample_pre_kernel(x_ref, h0r_ref, h0i_ref, gpre_ref, win_ref, bin_ref, bc_ref,
                       abr_ref, abi_ref, d_ref, wglu_ref, bglu_ref, lb_ref, hgn_ref, wbs_ref,
                       m5_ref, ghg_ref, gn_ref, qt_ref, cols_ref, v_ref, oin_ref,
                       s5r_ref, s5i_ref, bblk_ref, cblkt_ref):
    n_seq = h0r_ref.shape[0]
    dm = gpre_ref.shape[1]
    steps = x_ref.shape[1]
    sdim = h0r_ref.shape[1]
    hw = HG_HEADS * HG_DK
    s5w = d_ref.shape[1]

    @pl.when(pl.program_id(0) == 0)
    def _():
        n_state = bc_ref.shape[2]
        row_grp = lax.broadcasted_iota(jnp.int32, (s5w, sdim), 0) // S5_GROUP
        col_grp = lax.broadcasted_iota(jnp.int32, (s5w, sdim), 1) // n_state
        own = row_grp == col_grp
        for k, ref in enumerate((bblk_ref, bblk_ref, cblkt_ref, cblkt_ref)):
            tiled = jnp.concatenate([bc_ref[k]] * (sdim // n_state), axis=1)
            c0 = (k % 2) * sdim
            ref[:, c0:c0 + sdim] = jnp.where(own, tiled, 0.0).astype(bf16)

    x = jnp.concatenate([x_ref[:, t, :] for t in range(steps)], axis=0)
    hb = (_rms(x) * gpre_ref[...]).astype(bf16)

    u = _dot(hb, win_ref[:, :s5w]) + bin_ref[:, :s5w]
    bu = _dot(u.astype(bf16), bblk_ref[...])
    hr = h0r_ref[...]
    hi = h0i_ref[...]
    a_r = abr_ref[...]
    a_i = abi_ref[...]
    hs = []
    for t in range(steps):
        sl = slice(t * n_seq, (t + 1) * n_seq)
        hr, hi = (a_r * hr - a_i * hi + bu[sl, :sdim], a_r * hi + a_i * hr + bu[sl, sdim:])
        hs.append(jnp.concatenate([hr, hi], axis=1).astype(bf16))
    s5r_ref[...] = hr
    s5i_ref[...] = hi
    y = _dot_nt(jnp.concatenate(hs, axis=0), cblkt_ref[...])
    y = _gelu_tanh(y + d_ref[...] * u)
    z = _dot(y.astype(bf16), wglu_ref[...]) + bglu_ref[...]
    ys5 = (y * _sigmoid(z)).astype(bf16)
    g0 = s5w + 4 * hw
    def put_steps(ref, val):
        wcol = val.shape[1]
        for t in range(steps):
            ref[:, t * wcol:(t + 1) * wcol] = val[t * n_seq:(t + 1) * n_seq, :]

    gates = _dot(hb, win_ref[:, g0:]) + bin_ref[:, g0:]
    put_steps(m5_ref, _sigmoid(gates[:, :dm]) * _dot(ys5, wbs_ref[...]))
    put_steps(ghg_ref, _sigmoid(gates[:, dm:]))

    def proj(i):
        c0 = s5w + i * hw
        return _dot(hb, win_ref[:, c0:c0 + hw]) + bin_ref[:, c0:c0 + hw]

    q_raw = proj(0)
    f_raw = proj(1)
    v = proj(2)
    g_raw = proj(3)
    q = _silu(q_raw)
    lb = lb_ref[...]
    f = 0.5 * (1.0 + lb) + 0.5 * (1.0 - lb) * jnp.tanh(0.5 * f_raw)
    lf = jnp.log(f)
    kk = 1.0 - f
    put_steps(gn_ref, _silu(g_raw) * hgn_ref[...])
    bs = []
    acc = None
    for t in range(steps):
        sl = slice(t * n_seq, (t + 1) * n_seq)
        acc = lf[sl] if acc is None else acc + lf[sl]
        bs.append(acc)
    b_last = bs[-1]

    def put(ref, slot, val):
        for h in range(HG_HEADS):
            ref[pl.ds(h * 8 + slot, n_seq, stride=HG_HEADS * 8), :] = val[:, h * HG_DK:(h + 1) * HG_DK]

    for ref in (qt_ref, cols_ref, v_ref):
        ref[...] = jnp.zeros_like(ref)
    put(cols_ref, steps, jnp.exp(b_last))
    for t in range(steps):
        sl = slice(t * n_seq, (t + 1) * n_seq)
        put(qt_ref, t, q[sl] * jnp.exp(bs[t]))
        put(cols_ref, t, kk[sl] * jnp.exp(b_last - bs[t]))
        put(v_ref, t, v[sl])
        o_t = None
        for s in range(t + 1):
            sls = slice(s * n_seq, (s + 1) * n_seq)
            prod = q[sl] * kk[sls] * jnp.exp(bs[t] - bs[s])
            parts = []
            for h in range(HG_HEADS):
                hs = slice(h * HG_DK, (h + 1) * HG_DK)
                wgt = jnp.sum(prod[:, hs], axis=1, keepdims=True)
                parts.append(wgt * v[sls, hs])
            term = jnp.concatenate(parts, axis=1)
            o_t = term if o_t is None else o_t + term
        oin_ref[:, t * hw:(t + 1) * hw] = o_t


def _sample_pre_call(x_s, h0r, h0i, w):
    n_seq, sdim = h0r.shape
    steps, dm = x_s.shape[1], x_s.shape[2]
    hw = HG_HEADS * HG_DK
    nb = SAMPLE_PRE_BLOCK
    assert n_seq % nb == 0
    s5w = w["d"].shape[1]
    consts = [w[k] for k in ("gpre", "win", "bin", "bc", "abr", "abi", "d",
                             "wglu", "bglu", "lbrow", "hgnrow", "wbs")]
    slots = HG_HEADS * 8
    shapes = ((1, steps * dm),
              (1, steps * dm),
              (1, steps * hw),
              (slots, HG_DK),
              (slots, HG_DK),
              (slots, HG_DK),
              (1, steps * hw),
              (1, sdim), (1, sdim))
    seq_spec = lambda shp: pl.BlockSpec((nb * shp[0], shp[1]), lambda i: (i, 0))
    return pl.pallas_call(
        _sample_pre_kernel,
        grid=(n_seq // nb,),
        in_specs=([pl.BlockSpec((nb, steps, dm), lambda i: (i, 0, 0)), seq_spec((1, sdim)), seq_spec((1, sdim))]
                  + [_const_spec(c.shape) for c in consts]),
        out_specs=tuple(seq_spec(shp) for shp in shapes),
        out_shape=tuple(jax.ShapeDtypeStruct((n_seq * shp[0], shp[1]), f32) for shp in shapes),
        scratch_shapes=[pltpu.VMEM((s5w, 2 * sdim), bf16),
                        pltpu.VMEM((s5w, 2 * sdim), bf16)],
        compiler_params=pltpu.CompilerParams(
            dimension_semantics=("arbitrary",), vmem_limit_bytes=VMEM_LIMIT),
        name="sample_pre",
    )(x_s, h0r, h0i, *consts)


def _sample_state_kernel(s0_ref, qt_ref, v_ref, cols_ref, snew_ref, oint_ref):
    nb = s0_ref.shape[0]
    nrow = cols_ref.shape[1]
    hw = HG_HEADS * HG_DK
    row_head = lax.broadcasted_iota(jnp.int32, (nrow, hw), 0) // 8
    col_head = lax.broadcasted_iota(jnp.int32, (nrow, hw), 1) // HG_DK
    diag = row_head == col_head
    pad = jnp.zeros((LANES - nrow, HG_DK), f32)
    for i in range(nb):
        xt = jnp.concatenate([cols_ref[i], pad], axis=0).T
        vbd = jnp.where(diag, jnp.concatenate([v_ref[i]] * HG_HEADS, axis=1), 0.0)
        vbd = jnp.concatenate([vbd, jnp.zeros((LANES - nrow, hw), f32)], axis=0).astype(bf16)
        ds = _dot(xt.astype(bf16), vbd)
        for h in range(HG_HEADS):
            s0 = s0_ref[i, h]
            oint_ref[i, h] = _dot(qt_ref[i, h].astype(bf16), s0.astype(bf16))
            snew_ref[i, h] = xt[:, h * 8 + 4:h * 8 + 5] * s0 + ds[:, h * HG_DK:(h + 1) * HG_DK]


def _sample_state_call(s0, qt8, v8, cols):
    n_seq = s0.shape[0]
    nb = SAMPLE_SEQ_BLOCK
    blk4 = (nb, HG_HEADS, 8, HG_DK)
    return pl.pallas_call(
        _sample_state_kernel,
        grid=(n_seq // nb,),
        in_specs=[pl.BlockSpec((nb, HG_HEADS, HG_DK, HG_DK), lambda i: (i, 0, 0, 0)),
                  pl.BlockSpec(blk4, lambda i: (i, 0, 0, 0)),
                  pl.BlockSpec((nb, HG_HEADS * 8, HG_DK), lambda i: (i, 0, 0)),
                  pl.BlockSpec((nb, HG_HEADS * 8, HG_DK), lambda i: (i, 0, 0))],
        out_specs=(pl.BlockSpec((nb, HG_HEADS, HG_DK, HG_DK), lambda i: (i, 0, 0, 0)),
                   pl.BlockSpec(blk4, lambda i: (i, 0, 0, 0))),
        out_shape=(jax.ShapeDtypeStruct(s0.shape, f32),
                   jax.ShapeDtypeStruct((n_seq, HG_HEADS, 8, HG_DK), f32)),
        compiler_params=pltpu.CompilerParams(
            dimension_semantics=("arbitrary",), vmem_limit_bytes=VMEM_LIMIT),
        name="sample_state",
    )(s0, qt8.reshape((n_seq,) + blk4[1:]), v8.reshape(n_seq, HG_HEADS * 8, HG_DK),
      cols.reshape(n_seq, HG_HEADS * 8, HG_DK))


def _sample_post_kernel(x_ref, oint_ref, oin_ref, gn_ref, m5_ref, ghg_ref, wbh_ref, wout_ref, gpost_ref,
                        y_ref):
    n_seq = x_ref.shape[0]
    dm = gpost_ref.shape[1]
    steps = x_ref.shape[1]
    hw = HG_HEADS * HG_DK

    def steps_major(ref, wcol):
        return jnp.concatenate([ref[:, t * wcol:(t + 1) * wcol] for t in range(steps)], axis=0)

    x = jnp.concatenate([x_ref[:, t, :] for t in range(steps)], axis=0)
    oint = jnp.concatenate(
        [jnp.concatenate([oint_ref[pl.ds(h * 8 + t, n_seq, stride=HG_HEADS * 8), :] for h in range(HG_HEADS)],
                         axis=1) for t in range(steps)], axis=0)
    o = oint + steps_major(oin_ref, hw)
    parts = []
    for h in range(HG_HEADS):
        hs = slice(h * HG_DK, (h + 1) * HG_DK)
        parts.append(_rms(o[:, hs]))
    yhg = (jnp.concatenate(parts, axis=1) * steps_major(gn_ref, hw)).astype(bf16)
    merged = steps_major(m5_ref, dm) + steps_major(ghg_ref, dm) * _dot(yhg, wbh_ref[...])
    mo = _dot(merged.astype(bf16), wout_ref[...])
    y = x + _rms(mo) * gpost_ref[...]
    for t in range(steps):
        y_ref[:, t * dm:(t + 1) * dm] = y[t * n_seq:(t + 1) * n_seq, :]


def _sample_post_call(x_s, oint, oin, gn, m5, ghg, w):
    ins = [x_s, oint, oin, gn, m5, ghg] + [w[k] for k in ("wbh", "wout", "gpost")]
    n_seq, steps, dm = x_s.shape
    return pl.pallas_call(
        _sample_post_kernel,
        out_shape=jax.ShapeDtypeStruct((n_seq, steps * dm), f32),
        compiler_params=pltpu.CompilerParams(vmem_limit_bytes=VMEM_LIMIT),
        name="sample_post",
    )(*ins)


def _prep_weights(l, norm_mix_pre, norm_mix_post, norm_mlp_pre, norm_mlp_post, w_in, b_in,
                  s5_a_re, s5_a_im, s5_log_dt, s5_b_re, s5_b_im, s5_c_re, s5_c_im, s5_d, s5_w_glu,
                  s5_b_glu, hg_lb_logits, hg_norm, w_br_s5, w_br_hg, w_out, w_up, w_down, n_pow):
    hp = lax.Precision.HIGHEST
    s5w = s5_d.shape[1]
    n_groups = s5w // S5_GROUP
    w = {}
    row = lambda a: a.astype(f32).reshape(1, -1)
    w["gpre"], w["gpost"] = row(norm_mix_pre[l]), row(norm_mix_post[l])
    w["g2pre"], w["g2post"] = row(norm_mlp_pre[l]), row(norm_mlp_post[l])
    w["win"], w["bin"] = w_in[l].astype(bf16), row(b_in[l])
    lb_all = jnp.cumsum(jax.nn.softmax(hg_lb_logits.astype(f32), axis=0), axis=0)
    w["lbrow"] = row(lb_all[l])
    w["hgnrow"] = row(jnp.tile(hg_norm[l].astype(f32), HG_HEADS))
    w["d"] = row(s5_d[l])
    w["wglu"], w["bglu"] = s5_w_glu[l].astype(bf16), row(s5_b_glu[l])
    w["wbs"], w["wbh"], w["wout"] = w_br_s5[l].astype(bf16), w_br_hg[l].astype(bf16), w_out[l].astype(bf16)
    w["wup_f32"], w["wdn_f32"] = w_up[l], w_down[l]

    a_re, a_im = s5_a_re[l].astype(f32), s5_a_im[l].astype(f32)
    dt = jnp.exp(s5_log_dt[l].astype(f32))[:, None]
    mag = jnp.exp(dt * a_re)
    abr, abi = mag * jnp.cos(dt * a_im), mag * jnp.sin(dt * a_im)
    den = a_re * a_re + a_im * a_im
    nr, ni = abr - 1.0, abi
    cfr, cfi = (nr * a_re + ni * a_im) / den, (ni * a_re - nr * a_im) / den
    b_re, b_im = s5_b_re[l].astype(f32), s5_b_im[l].astype(f32)
    bbr = cfr[..., None] * b_re - cfi[..., None] * b_im
    bbi = cfr[..., None] * b_im + cfi[..., None] * b_re
    c_re, c_im = s5_c_re[l].astype(f32), s5_c_im[l].astype(f32)
    w["abr"], w["abi"] = abr.reshape(1, -1), abi.reshape(1, -1)

    lam_r, lam_i = dt * a_re, dt * a_im
    grp = S5_GROUP
    n_state = a_re.shape[1]
    jj = jnp.arange(S5_CHUNK + 1, dtype=f32)[None, :, None]
    pmag = jnp.exp(jj * lam_r[:, None, :])
    pw_r, pw_i = pmag * jnp.cos(jj * lam_i[:, None, :]), pmag * jnp.sin(jj * lam_i[:, None, :])
    bt = jnp.transpose(jnp.stack([bbr, bbi]), (0, 1, 3, 2))
    bt_r, bt_i = bt[0], bt[1]
    c4r, c4i = c_re[:, None], c_im[:, None]

    def cmul(pr_, pi_, xr, xi):
        pr_, pi_ = pr_[:, :, None, :], pi_[:, :, None, :]
        return pr_ * xr - pi_ * xi, pr_ * xi + pi_ * xr

    rows128 = lambda t: t.reshape(n_groups, LANES, t.shape[-1])
    car_t, cai_t = cmul(pw_r[:, 1:], pw_i[:, 1:], c4r, c4i)
    ca = jnp.transpose(jnp.stack([rows128(car_t), -rows128(cai_t)]), (0, 1, 3, 2))
    ber, bei = cmul(pw_r[:, S5_CHUNK - 1::-1], pw_i[:, S5_CHUNK - 1::-1], bt_r[:, None], bt_i[:, None])
    ber, bei = rows128(ber), rows128(bei)
    zpad = jnp.zeros((n_groups, S5_CHUNK - 1, n_state), f32)
    lag_r = jnp.concatenate([zpad, pw_r[:, :S5_CHUNK]], axis=1)
    lag_i = jnp.concatenate([zpad, pw_i[:, :S5_CHUNK]], axis=1)
    cp_r, cp_i = cmul(lag_r, lag_i, c4r, c4i)
    n_blk = 2 * S5_CHUNK - 1
    cp_all = jnp.concatenate([cp_r, -cp_i], axis=3).reshape(n_groups, n_blk * grp, 2 * n_state)
    rpad_t = jnp.einsum("gmn,gnc->gmc", cp_all, jnp.concatenate([bbr, bbi], axis=1), precision=hp)
    toe_t = jnp.concatenate([rpad_t[:, (S5_CHUNK - 1 - s) * grp:(S5_CHUNK - 1 - s) * grp + LANES, :]
                             for s in range(S5_CHUNK)], axis=2)
    toe = jnp.transpose(toe_t, (0, 2, 1))

    def pair_diag(m):
        g, r, c = m.shape
        m = m.reshape(g // 2, 2, r, c)
        z = jnp.zeros_like(m[:, 0])
        return jnp.concatenate([jnp.concatenate([m[:, 0], z], axis=2),
                                jnp.concatenate([z, m[:, 1]], axis=2)], axis=1)

    w["w1"] = jnp.concatenate([pair_diag(toe), pair_diag(ber), pair_diag(bei)], axis=2).astype(bf16)
    w["w2"] = jnp.concatenate([pair_diag(ca[0]), pair_diag(ca[1])], axis=1).astype(bf16)
    qr, qi = pw_r[:, S5_CHUNK], pw_i[:, S5_CHUNK]
    aps_r, aps_i = [], []
    for _ in range(n_pow):
        aps_r.append(qr.reshape(-1))
        aps_i.append(qi.reshape(-1))
        qr, qi = qr * qr - qi * qi, 2.0 * qr * qi
    w["apr"], w["api"] = jnp.stack(aps_r), jnp.stack(aps_i)

    w["bc"] = jnp.stack([bt_r, bt_i, c_re, -c_im]).reshape(4, s5w, n_state)
    return w


def kernel(x_prompt, x_sample, state_s5_re, state_s5_im, state_hg, norm_mix_pre, norm_mix_post,
           norm_mlp_pre, norm_mlp_post, w_in, b_in, s5_a_re, s5_a_im, s5_log_dt, s5_b_re, s5_b_im,
           s5_c_re, s5_c_im, s5_d, s5_w_glu, s5_b_glu, hg_lb_logits, hg_norm, w_br_s5, w_br_hg,
           w_out, w_up, w_down):
    depth = w_in.shape[0]
    bsz, seq, dm = x_prompt.shape
    n_seq, steps, _ = x_sample.shape
    n_groups, n_state = s5_a_re.shape[1], s5_a_re.shape[2]
    hw = HG_HEADS * HG_DK
    assert depth == 1 and steps == 4 and n_state == S5_STATE
    n_pow = max(1, int(math.log2(min(PROMPT_TILE, seq) // S5_CHUNK)))
    xp, xs = x_prompt, x_sample
    outs = [[] for _ in range(6)]
    for l in range(depth):
        w = _prep_weights(l, norm_mix_pre, norm_mix_post, norm_mlp_pre, norm_mlp_post, w_in, b_in,
                          s5_a_re, s5_a_im, s5_log_dt, s5_b_re, s5_b_im, s5_c_re, s5_c_im, s5_d,
                          s5_w_glu, s5_b_glu, hg_lb_logits, hg_norm, w_br_s5, w_br_hg, w_out, w_up,
                          w_down, n_pow)
        x1, p_re, p_im, p_hg = _mixer_call(xp, w)
        outs[0].append(p_re.reshape(bsz, n_groups, n_state))
        outs[1].append(p_im.reshape(bsz, n_groups, n_state))
        outs[2].append(p_hg)
        h0r = state_s5_re[l].reshape(n_seq, n_groups * n_state)
        h0i = state_s5_im[l].reshape(n_seq, n_groups * n_state)
        m5, ghg, gn, qt8, cols, v8, oin, s_re, s_im = _sample_pre_call(xs, h0r, h0i, w)
        s_hg, oint = _sample_state_call(state_hg[l], qt8, v8, cols)
        x1_sm = _sample_post_call(xs, oint.reshape(n_seq * HG_HEADS * 8, HG_DK), oin, gn, m5, ghg, w)
        yp, xs = _mlp_call(x1.reshape(bsz * seq, dm), x1_sm, w)
        xp = yp.reshape(bsz, seq, dm)
        outs[3].append(s_re.reshape(n_seq, n_groups, n_state))
        outs[4].append(s_im.reshape(n_seq, n_groups, n_state))
        outs[5].append(s_hg)
    return (xp, xs) + tuple(o[0][None] for o in outs)
```

```python
import math

import jax
import jax.numpy as jnp
import numpy as np
from jax import lax
from jax.experimental import pallas as pl
from jax.experimental.pallas import tpu as pltpu

f32 = jnp.float32
bf16 = jnp.bfloat16

NORM_EPS = 1e-6
S5_GROUP = 16
S5_STATE = 64
S5_CHUNK = 8
HG_HEADS = 8
HG_DK = 128
HG_CHUNK = 64
LANES = 128
VMEM_LIMIT = 56 * 1024 * 1024
PROMPT_TILE = 512
MLP_TILE = 1024
MLP_SUBTILES = 2
MLP_FCHUNK = 512
SAMPLE_SEQ_BLOCK = 16
SAMPLE_PRE_BLOCK = 64


def _sigmoid(x):
    return 0.5 * (jnp.tanh(0.5 * x) + 1.0)


def _silu(x):
    h = 0.5 * x
    return h + h * jnp.tanh(h)


def _gelu_tanh(x):
    c = math.sqrt(2.0 / math.pi)
    return 0.5 * x * (1.0 + jnp.tanh(c * (x + 0.044715 * (x * x * x))))


def _rms(x):
    return x * lax.rsqrt(jnp.mean(x * x, axis=-1, keepdims=True) + NORM_EPS)


def _dot(a, b):
    return jnp.dot(a, b, preferred_element_type=f32)


def _dot_nt(a, b):
    return lax.dot_general(a, b, (((1,), (1,)), ((), ())), preferred_element_type=f32)


def _dot_tn(a, b):
    return lax.dot_general(a, b, (((0,), (0,)), ((), ())), preferred_element_type=f32)


def _block_transpose8(xs, blk):
    xs = list(xs)
    for d in (4, 2, 1):
        keep = (blk & d) == 0
        for i in range(8):
            if i & d == 0:
                a, b = xs[i], xs[i + d]
                xs[i] = jnp.where(keep, a, pltpu.roll(b, d * S5_GROUP, 1))
                xs[i + d] = jnp.where(keep, pltpu.roll(a, LANES - d * S5_GROUP, 1), b)
    return xs


def _mixer_kernel(x_ref, gpre_ref, win_ref, bin_ref, w1_ref, w2_ref, apr_ref, api_ref, d_ref,
                  wglu_ref, bglu_ref, lb_ref, hgn_ref, wbs_ref, wbh_ref, wout_ref, gpost_ref,
                  y_ref, s5r_ref, s5i_ref, hgs_ref,
                  hb_scr, u_scr, ys_scr, o_scr, yloc_scr, er_scr, ei_scr, cr_scr, ci_scr,
                  st_scr, qfig_scr, yhg_scr, m_scr, fac_scr, el_scr, ghg_scr):
    t_idx = pl.program_id(1)
    n_t = pl.num_programs(1)
    tile = x_ref.shape[1]
    rows = tile // S5_CHUNK
    n_pairs = w1_ref.shape[0]
    n_cols = u_scr.shape[0]
    hw = HG_HEADS * HG_DK
    s5w = n_cols * LANES
    g0 = s5w + 4 * hw

    @pl.when(t_idx == 0)
    def _():
        cr_scr[...] = jnp.zeros_like(cr_scr)
        ci_scr[...] = jnp.zeros_like(ci_scr)
        st_scr[...] = jnp.zeros_like(st_scr)

    x = x_ref[0]
    hb_scr[...] = (_rms(x) * gpre_ref[...]).astype(bf16)
    dm = wbs_ref.shape[1]
    half = tile // 2

    def project(r0, r1, c0, c1):
        qfig_scr[r0:r1, c0:c1] = (_dot(hb_scr[r0:r1, :], win_ref[:, s5w + c0:s5w + c1])
                                  + bin_ref[:, s5w + c0:s5w + c1])

    def gate_hg(j):
        c0, c1 = j * (dm // 4), (j + 1) * (dm // 4)
        ghg_scr[:, c0:c1] = _sigmoid(_dot(hb_scr[...], win_ref[:, g0 + dm + c0:g0 + dm + c1])
                                     + bin_ref[:, g0 + dm + c0:g0 + dm + c1])


    u = _dot(hb_scr[...], win_ref[:, :s5w]) + bin_ref[:, :s5w]
    for j in range(n_cols):
        u_scr[j] = u[:, j * LANES:(j + 1) * LANES]

    project(0, half, 0, hw)
    gate_hg(0)
    blk = lax.broadcasted_iota(jnp.int32, (rows, LANES), 1) // S5_GROUP
    for j in range(n_cols):
        xs = [u_scr[j, pl.ds(s, rows, stride=S5_CHUNK), :] for s in range(S5_CHUNK)]
        ys = _block_transpose8(xs, blk)
        for q in range(4):
            o_scr[j * 4 + q] = jnp.concatenate([ys[2 * q], ys[2 * q + 1]], axis=1).astype(bf16)

    project(0, half, hw, 2 * hw)
    gate_hg(1)
    for p in range(n_pairs):
        r1 = _dot(o_scr[p], w1_ref[p])
        yloc_scr[p] = r1[:, :2 * LANES]
        er_scr[:, p * LANES:(p + 1) * LANES] = r1[:, 2 * LANES:3 * LANES]
        ei_scr[:, p * LANES:(p + 1) * LANES] = r1[:, 3 * LANES:]

    er = er_scr[...]
    ei = ei_scr[...]
    row = lax.broadcasted_iota(jnp.int32, (rows, 1), 0)
    first = row == 0
    cr = cr_scr[...]
    ci = ci_scr[...]
    a_r = apr_ref[0:1, :]
    a_i = api_ref[0:1, :]
    project(0, half, 2 * hw, 3 * hw)
    gate_hg(2)
    er = er + jnp.where(first, a_r * cr - a_i * ci, 0.0)
    ei = ei + jnp.where(first, a_r * ci + a_i * cr, 0.0)
    for k in range(int(math.log2(rows))):
        d = 1 << k
        p_r = apr_ref[k:k + 1, :]
        p_i = api_ref[k:k + 1, :]
        s_r = pltpu.roll(er, d, 0)
        s_i = pltpu.roll(ei, d, 0)
        valid = row >= d
        er, ei = (er + jnp.where(valid, p_r * s_r - p_i * s_i, 0.0),
                  ei + jnp.where(valid, p_r * s_i + p_i * s_r, 0.0))
    hs_r = jnp.where(first, cr, pltpu.roll(er, 1, 0))
    hs_i = jnp.where(first, ci, pltpu.roll(ei, 1, 0))
    cr_scr[...] = er[rows - 1:rows, :]
    ci_scr[...] = ei[rows - 1:rows, :]

    project(0, half, 3 * hw, 4 * hw)
    gate_hg(3)
    for j in range(n_cols):
        halves = []
        for q in range(4):
            p = j * 4 + q
            hp = jnp.concatenate([hs_r[:, p * LANES:(p + 1) * LANES],
                                  hs_i[:, p * LANES:(p + 1) * LANES]], axis=1).astype(bf16)
            yp = yloc_scr[p] + _dot(hp, w2_ref[p])
            halves += [yp[:, :LANES], yp[:, LANES:]]
        zs = _block_transpose8(halves, blk)
        for s in range(S5_CHUNK):
            ys_scr[j, pl.ds(s, rows, stride=S5_CHUNK), :] = zs[s]

    y = jnp.concatenate([ys_scr[j] for j in range(n_cols)], axis=1)
    u = jnp.concatenate([u_scr[j] for j in range(n_cols)], axis=1)
    y = _gelu_tanh(y + d_ref[...] * u)
    z = _dot(y.astype(bf16), wglu_ref[...]) + bglu_ref[...]
    ys5 = (y * _sigmoid(z)).astype(bf16)
    gate_s5 = _sigmoid(_dot(hb_scr[...], win_ref[:, g0:g0 + dm]) + bin_ref[:, g0:g0 + dm])
    m_scr[...] = gate_s5 * _dot(ys5, wbs_ref[...])

    ch = HG_CHUNK
    ri = lax.broadcasted_iota(jnp.int32, (ch, ch), 0)
    ci_ = lax.broadcasted_iota(jnp.int32, (ch, ch), 1)
    causal = ci_ <= ri
    tril2 = jnp.concatenate([causal.astype(bf16)] * 2, axis=1)
    lb = lb_ref[...]
    f_mid = 0.5 * (1.0 + lb)
    f_amp = 0.5 * (1.0 - lb)
    hgn = hgn_ref[...]
    n_chunks = tile // ch
    n_side = n_chunks // 2
    assert n_side == 4 and dm % n_side == 0

    def merge_hg(r0, r1, c0, c1):
        m_scr[r0:r1, c0:c1] = (m_scr[r0:r1, c0:c1]
                               + ghg_scr[r0:r1, c0:c1] * _dot(yhg_scr[r0:r1, :], wbh_ref[:, c0:c1]))

    def chunk_factors(c):
        r0 = c * ch
        slot = c % 2
        q_raw = qfig_scr[pl.ds(r0, ch), 0:hw]
        f_raw = qfig_scr[pl.ds(r0, ch), hw:2 * hw]
        q = _silu(q_raw)
        f = f_mid + f_amp * jnp.tanh(0.5 * f_raw)
        lf = jnp.log(f)
        kk = 1.0 - f
        hi = lf.astype(bf16)
        lo = (lf - hi.astype(f32)).astype(bf16)
        b = _dot(tril2, jnp.concatenate([hi, lo], axis=0))
        mid = b[ch // 2 - 1:ch // 2, :]
        b_last = b[ch - 1:ch, :]
        qh = q * jnp.exp(b - mid)
        kh = kk * jnp.exp(mid - b)
        fac_scr[slot, 0] = qh.astype(bf16)
        fac_scr[slot, 1] = kh.astype(bf16)
        fac_scr[slot, 2] = (qh * jnp.exp(mid)).astype(bf16)
        fac_scr[slot, 3] = (kh * jnp.exp(b_last - mid)).astype(bf16)
        fac_scr[slot, 4] = qfig_scr[pl.ds(r0, ch), 2 * hw:3 * hw].astype(bf16)
        el_scr[slot] = jnp.exp(b_last)

    def chunk_matmuls(c):
        r0 = c * ch
        slot = c % 2
        heads = [slice(h * HG_DK, (h + 1) * HG_DK) for h in range(HG_HEADS)]
        scs = [jnp.where(causal, _dot_nt(fac_scr[slot, 0, :, hs], fac_scr[slot, 1, :, hs]), 0.0).astype(bf16)
               for hs in heads]
        e_last = el_scr[slot]
        outs = []
        for h, hs in enumerate(heads):
            st = st_scr[h]
            vb = fac_scr[slot, 4, :, hs]
            o = _dot(scs[h], vb) + _dot_nt(fac_scr[slot, 2, :, hs], st.astype(bf16))
            st_scr[h] = st * e_last[:, hs] + _dot_tn(vb, fac_scr[slot, 3, :, hs])
            outs.append(_rms(o))
        g_raw = qfig_scr[pl.ds(r0, ch), 3 * hw:4 * hw]
        on = jnp.concatenate(outs, axis=1) * hgn * _silu(g_raw)
        yhg_scr[pl.ds(r0, ch), :] = on.astype(bf16)

    def finish(r0, r1):
        mo = _dot(m_scr[r0:r1, :].astype(bf16), wout_ref[...])
        y_ref[0, r0:r1, :] = x_ref[0, r0:r1, :] + _rms(mo) * gpost_ref[...]

    quarter = half // 2
    side_work = {4: lambda: merge_hg(0, half, 0, dm // 2),
                 5: lambda: merge_hg(0, half, dm // 2, dm),
                 6: lambda: finish(0, half),
                 7: lambda: merge_hg(half, half + quarter, 0, dm)}
    chunk_factors(0)
    for c in range(n_chunks):
        if c + 1 < n_chunks:
            chunk_factors(c + 1)
        chunk_matmuls(c)
        if c < n_side:
            project(half, tile, c * hw, (c + 1) * hw)
        else:
            side_work[c]()
    merge_hg(half + quarter, tile, 0, dm)
    finish(half, tile)

    @pl.when(t_idx == n_t - 1)
    def _():
        s5r_ref[0] = cr_scr[...]
        s5i_ref[0] = ci_scr[...]
        for h in range(HG_HEADS):
            hgs_ref[0, h] = st_scr[h].T


def _const_spec(shape):
    nd = len(shape)
    return pl.BlockSpec(shape, lambda *_: (0,) * nd, pipeline_mode=pl.Buffered(1))


def _mixer_call(x, w):
    bsz, seq, dm = x.shape
    tile = min(PROMPT_TILE, seq)
    assert seq % tile == 0 and tile % HG_CHUNK == 0
    rows = tile // S5_CHUNK
    assert rows & (rows - 1) == 0 and w["apr"].shape[0] >= int(math.log2(rows))
    s5w = w["d"].shape[1]
    n_cols = s5w // LANES
    n_pairs = w["w1"].shape[0]
    sdim = w["apr"].shape[1]
    consts = [w[k] for k in ("gpre", "win", "bin", "w1", "w2", "apr", "api", "d", "wglu", "bglu",
                             "lbrow", "hgnrow", "wbs", "wbh", "wout", "gpost")]
    in_specs = [pl.BlockSpec((1, tile, dm), lambda b, t: (b, t, 0))]
    in_specs += [_const_spec(c.shape) for c in consts]
    out_shape = (jax.ShapeDtypeStruct((bsz, seq, dm), f32),
                 jax.ShapeDtypeStruct((bsz, 1, sdim), f32),
                 jax.ShapeDtypeStruct((bsz, 1, sdim), f32),
                 jax.ShapeDtypeStruct((bsz, HG_HEADS, HG_DK, HG_DK), f32))
    out_specs = (pl.BlockSpec((1, tile, dm), lambda b, t: (b, t, 0)),
                 pl.BlockSpec((1, 1, sdim), lambda b, t: (b, 0, 0)),
                 pl.BlockSpec((1, 1, sdim), lambda b, t: (b, 0, 0)),
                 pl.BlockSpec((1, HG_HEADS, HG_DK, HG_DK), lambda b, t: (b, 0, 0, 0)))
    scratch = [
        pltpu.VMEM((tile, dm), bf16),
        pltpu.VMEM((n_cols, tile, LANES), f32),
        pltpu.VMEM((n_cols, tile, LANES), f32),
        pltpu.VMEM((n_pairs, rows, 2 * LANES), bf16),
        pltpu.VMEM((n_pairs, rows, 2 * LANES), f32),
        pltpu.VMEM((rows, sdim), f32),
        pltpu.VMEM((rows, sdim), f32),
        pltpu.VMEM((1, sdim), f32),
        pltpu.VMEM((1, sdim), f32),
        pltpu.VMEM((HG_HEADS, HG_DK, HG_DK), f32),
        pltpu.VMEM((tile, HG_HEADS * 4 * HG_DK), f32),
        pltpu.VMEM((tile, HG_HEADS * HG_DK), bf16),
        pltpu.VMEM((tile, dm), f32),
        pltpu.VMEM((2, 5, HG_CHUNK, HG_HEADS * HG_DK), bf16),
        pltpu.VMEM((2, 1, HG_HEADS * HG_DK), f32),
        pltpu.VMEM((tile, dm), f32),
    ]
    return pl.pallas_call(
        _mixer_kernel,
        grid=(bsz, seq // tile),
        in_specs=in_specs,
        out_specs=out_specs,
        out_shape=out_shape,
        scratch_shapes=scratch,
        compiler_params=pltpu.CompilerParams(
            dimension_semantics=("arbitrary", "arbitrary"), vmem_limit_bytes=VMEM_LIMIT),
        name="prompt_mixer",
    )(x, *consts)


def _mlp_body(x, gpre_ref, wup_scr, wdn_scr, gpost_ref):
    h2 = (_rms(x) * gpre_ref[...]).astype(bf16)
    acc = None
    for c in range(wup_scr.shape[0]):
        a = _dot(h2, wup_scr[c])
        a = jnp.maximum(a, 0.0)
        part = _dot((a * a).astype(bf16), wdn_scr[c])
        acc = part if acc is None else acc + part
    return x + _rms(acc) * gpost_ref[...]


def _mlp_kernel(xp_ref, xs_ref, gpre_ref, wup_ref, wdn_ref, gpost_ref, yp_ref, ys_ref, wup_scr, wdn_scr):
    step = pl.program_id(0)
    n_cast = wup_scr.shape[0]
    last = pl.num_programs(0) - 1
    dm = gpre_ref.shape[1]

    @pl.when(step < n_cast)
    def _():
        wup_scr[step] = wup_ref[...].astype(bf16)
        wdn_scr[step] = wdn_ref[...].astype(bf16)

    @pl.when(jnp.logical_and(step >= n_cast, step < last))
    def _():
        sub = xp_ref.shape[0] // MLP_SUBTILES
        for r0 in range(0, xp_ref.shape[0], sub):
            yp_ref[r0:r0 + sub, :] = _mlp_body(xp_ref[r0:r0 + sub, :], gpre_ref, wup_scr, wdn_scr, gpost_ref)

    @pl.when(step == last)
    def _():
        n_seq = xs_ref.shape[0]
        steps = xs_ref.shape[1] // dm
        x = jnp.concatenate([xs_ref[:, t * dm:(t + 1) * dm] for t in range(steps)], axis=0)
        y = _mlp_body(x, gpre_ref, wup_scr, wdn_scr, gpost_ref)
        for t in range(steps):
            ys_ref[:, t, :] = y[t * n_seq:(t + 1) * n_seq, :]


def _mlp_call(xp2d, xs_sm, w):
    n, dm = xp2d.shape
    tile = min(MLP_TILE, n)
    assert n % tile == 0
    n_tiles = n // tile
    wup, wdn = w["wup_f32"], w["wdn_f32"]
    dff = wup.shape[1]
    fc = MLP_FCHUNK
    assert dff % fc == 0
    n_cast = dff // fc
    blk = lambda s: jnp.minimum(s, n_cast - 1)
    row = lambda s: jnp.clip(s - n_cast, 0, n_tiles - 1)
    ys_shape = (xs_sm.shape[0], xs_sm.shape[1] // dm, dm)
    return pl.pallas_call(
        _mlp_kernel,
        grid=(n_cast + n_tiles + 1,),
        in_specs=[pl.BlockSpec((tile, dm), lambda s: (row(s), 0)),
                  _const_spec(xs_sm.shape),
                  _const_spec(w["g2pre"].shape),
                  pl.BlockSpec((dm, fc), lambda s: (0, blk(s))),
                  pl.BlockSpec((fc, dm), lambda s: (blk(s), 0)),
                  _const_spec(w["g2post"].shape)],
        out_specs=(pl.BlockSpec((tile, dm), lambda s: (row(s), 0)),
                   pl.BlockSpec(ys_shape, lambda s: (0, 0, 0))),
        out_shape=(jax.ShapeDtypeStruct((n, dm), f32), jax.ShapeDtypeStruct(ys_shape, f32)),
        scratch_shapes=[pltpu.VMEM((n_cast, dm, fc), bf16), pltpu.VMEM((n_cast, fc, dm), bf16)],
        compiler_params=pltpu.CompilerParams(
            dimension_semantics=("arbitrary",), vmem_limit_bytes=VMEM_LIMIT),
        name="mlp",
    )(xp2d, xs_sm, w["g2pre"], wup, wdn, w["g2post"])


def _sample_pre_kernel(x_ref, h0r_ref, h0i_ref, gpre_ref, win_ref, bin_ref, bc_ref,
                       abr_ref, abi_ref, d_ref, wglu_ref, bglu_ref, lb_ref, hgn_ref, wbs_ref,
                       m5_ref, ghg_ref, gn_ref, qt_ref, cols_ref, v_ref, oin_ref,
                       s5r_ref, s5i_ref, bblk_ref, cblkt_ref):
    n_seq = h0r_ref.shape[0]
    dm = gpre_ref.shape[1]
    steps = x_ref.shape[1]
    sdim = h0r_ref.shape[1]
    hw = HG_HEADS * HG_DK
    s5w = d_ref.shape[1]

    @pl.when(pl.program_id(0) == 0)
    def _():
        n_state = bc_ref.shape[2]
        row_grp = lax.broadcasted_iota(jnp.int32, (s5w, sdim), 0) // S5_GROUP
        col_grp = lax.broadcasted_iota(jnp.int32, (s5w, sdim), 1) // n_state
        own = row_grp == col_grp
        for k, ref in enumerate((bblk_ref, bblk_ref, cblkt_ref, cblkt_ref)):
            tiled = jnp.concatenate([bc_ref[k]] * (sdim // n_state), axis=1)
            c0 = (k % 2) * sdim
            ref[:, c0:c0 + sdim] = jnp.where(own, tiled, 0.0).astype(bf16)

    x = jnp.concatenate([x_ref[:, t, :] for t in range(steps)], axis=0)
    hb = (_rms(x) * gpre_ref[...]).astype(bf16)

    u = _dot(hb, win_ref[:, :s5w]) + bin_ref[:, :s5w]
    bu = _dot(u.astype(bf16), bblk_ref[...])
    hr = h0r_ref[...]
    hi = h0i_ref[...]
    a_r = abr_ref[...]
    a_i = abi_ref[...]
    hs = []
    for t in range(steps):
        sl = slice(t * n_seq, (t + 1) * n_seq)
        hr, hi = (a_r * hr - a_i * hi + bu[sl, :sdim], a_r * hi + a_i * hr + bu[sl, sdim:])
        hs.append(jnp.concatenate([hr, hi], axis=1).astype(bf16))
    s5r_ref[...] = hr
    s5i_ref[...] = hi
    y = _dot_nt(jnp.concatenate(hs, axis=0), cblkt_ref[...])
    y = _gelu_tanh(y + d_ref[...] * u)
    z = _dot(y.astype(bf16), wglu_ref[...]) + bglu_ref[...]
    ys5 = (y * _sigmoid(z)).astype(bf16)
    g0 = s5w + 4 * hw
    def put_steps(ref, val):
        wcol = val.shape[1]
        for t in range(steps):
            ref[:, t * wcol:(t + 1) * wcol] = val[t * n_seq:(t + 1) * n_seq, :]

    gates = _dot(hb, win_ref[:, g0:]) + bin_ref[:, g0:]
    put_steps(m5_ref, _sigmoid(gates[:, :dm]) * _dot(ys5, wbs_ref[...]))
    put_steps(ghg_ref, _sigmoid(gates[:, dm:]))

    def proj(i):
        c0 = s5w + i * hw
        return _dot(hb, win_ref[:, c0:c0 + hw]) + bin_ref[:, c0:c0 + hw]

    q_raw = proj(0)
    f_raw = proj(1)
    v = proj(2)
    g_raw = proj(3)
    q = _silu(q_raw)
    lb = lb_ref[...]
    f = 0.5 * (1.0 + lb) + 0.5 * (1.0 - lb) * jnp.tanh(0.5 * f_raw)
    lf = jnp.log(f)
    kk = 1.0 - f
    put_steps(gn_ref, _silu(g_raw) * hgn_ref[...])
    bs = []
    acc = None
    for t in range(steps):
        sl = slice(t * n_seq, (t + 1) * n_seq)
        acc = lf[sl] if acc is None else acc + lf[sl]
        bs.append(acc)
    b_last = bs[-1]

    def put(ref, slot, val):
        for h in range(HG_HEADS):
            ref[pl.ds(h * 8 + slot, n_seq, stride=HG_HEADS * 8), :] = val[:, h * HG_DK:(h + 1) * HG_DK]

    for ref in (qt_ref, cols_ref, v_ref):
        ref[...] = jnp.zeros_like(ref)
    put(cols_ref, steps, jnp.exp(b_last))
    for t in range(steps):
        sl = slice(t * n_seq, (t + 1) * n_seq)
        put(qt_ref, t, q[sl] * jnp.exp(bs[t]))
        put(cols_ref, t, kk[sl] * jnp.exp(b_last - bs[t]))
        put(v_ref, t, v[sl])
        o_t = None
        for s in range(t + 1):
            sls = slice(s * n_seq, (s + 1) * n_seq)
            prod = q[sl] * kk[sls] * jnp.exp(bs[t] - bs[s])
            parts = []
            for h in range(HG_HEADS):
                hs = slice(h * HG_DK, (h + 1) * HG_DK)
                wgt = jnp.sum(prod[:, hs], axis=1, keepdims=True)
                parts.append(wgt * v[sls, hs])
            term = jnp.concatenate(parts, axis=1)
            o_t = term if o_t is None else o_t + term
        oin_ref[:, t * hw:(t + 1) * hw] = o_t


def _sample_pre_call(x_s, h0r, h0i, w):
    n_seq, sdim = h0r.shape
    steps, dm = x_s.shape[1], x_s.shape[2]
    hw = HG_HEADS * HG_DK
    nb = SAMPLE_PRE_BLOCK
    assert n_seq % nb == 0
    s5w = w["d"].shape[1]
    consts = [w[k] for k in ("gpre", "win", "bin", "bc", "abr", "abi", "d",
                             "wglu", "bglu", "lbrow", "hgnrow", "wbs")]
    slots = HG_HEADS * 8
    shapes = ((1, steps * dm),
              (1, steps * dm),
              (1, steps * hw),
              (slots, HG_DK),
              (slots, HG_DK),
              (slots, HG_DK),
              (1, steps * hw),
              (1, sdim), (1, sdim))
    seq_spec = lambda shp: pl.BlockSpec((nb * shp[0], shp[1]), lambda i: (i, 0))
    return pl.pallas_call(
        _sample_pre_kernel,
        grid=(n_seq // nb,),
        in_specs=([pl.BlockSpec((nb, steps, dm), lambda i: (i, 0, 0)), seq_spec((1, sdim)), seq_spec((1, sdim))]
                  + [_const_spec(c.shape) for c in consts]),
        out_specs=tuple(seq_spec(shp) for shp in shapes),
        out_shape=tuple(jax.ShapeDtypeStruct((n_seq * shp[0], shp[1]), f32) for shp in shapes),
        scratch_shapes=[pltpu.VMEM((s5w, 2 * sdim), bf16),
                        pltpu.VMEM((s5w, 2 * sdim), bf16)],
        compiler_params=pltpu.CompilerParams(
            dimension_semantics=("arbitrary",), vmem_limit_bytes=VMEM_LIMIT),
        name="sample_pre",
    )(x_s, h0r, h0i, *consts)


def _sample_state_kernel(s0_ref, qt_ref, v_ref, cols_ref, snew_ref, oint_ref):
    nb = s0_ref.shape[0]
    nrow = cols_ref.shape[1]
    hw = HG_HEADS * HG_DK
    row_head = lax.broadcasted_iota(jnp.int32, (nrow, hw), 0) // 8
    col_head = lax.broadcasted_iota(jnp.int32, (nrow, hw), 1) // HG_DK
    diag = row_head == col_head
    pad = jnp.zeros((LANES - nrow, HG_DK), f32)
    for i in range(nb):
        xt = jnp.concatenate([cols_ref[i], pad], axis=0).T
        vbd = jnp.where(diag, jnp.concatenate([v_ref[i]] * HG_HEADS, axis=1), 0.0)
        vbd = jnp.concatenate([vbd, jnp.zeros((LANES - nrow, hw), f32)], axis=0).astype(bf16)
        ds = _dot(xt.astype(bf16), vbd)
        for h in range(HG_HEADS):
            s0 = s0_ref[i, h]
            oint_ref[i, h] = _dot(qt_ref[i, h].astype(bf16), s0.astype(bf16))
            snew_ref[i, h] = xt[:, h * 8 + 4:h * 8 + 5] * s0 + ds[:, h * HG_DK:(h + 1) * HG_DK]


def _sample_state_call(s0, qt8, v8, cols):
    n_seq = s0.shape[0]
    nb = SAMPLE_SEQ_BLOCK
    blk4 = (nb, HG_HEADS, 8, HG_DK)
    return pl.pallas_call(
        _sample_state_kernel,
        grid=(n_seq // nb,),
        in_specs=[pl.BlockSpec((nb, HG_HEADS, HG_DK, HG_DK), lambda i: (i, 0, 0, 0)),
                  pl.BlockSpec(blk4, lambda i: (i, 0, 0, 0)),
                  pl.BlockSpec((nb, HG_HEADS * 8, HG_DK), lambda i: (i, 0, 0)),
                  pl.BlockSpec((nb, HG_HEADS * 8, HG_DK), lambda i: (i, 0, 0))],
        out_specs=(pl.BlockSpec((nb, HG_HEADS, HG_DK, HG_DK), lambda i: (i, 0, 0, 0)),
                   pl.BlockSpec(blk4, lambda i: (i, 0, 0, 0))),
        out_shape=(jax.ShapeDtypeStruct(s0.shape, f32),
                   jax.ShapeDtypeStruct((n_seq, HG_HEADS, 8, HG_DK), f32)),
        compiler_params=pltpu.CompilerParams(
            dimension_semantics=("arbitrary",), vmem_limit_bytes=VMEM_LIMIT),
        name="sample_state",
    )(s0, qt8.reshape((n_seq,) + blk4[1:]), v8.reshape(n_seq, HG_HEADS * 8, HG_DK),
      cols.reshape(n_seq, HG_HEADS * 8, HG_DK))


def _sample_post_kernel(x_ref, oint_ref, oin_ref, gn_ref, m5_ref, ghg_ref, wbh_ref, wout_ref, gpost_ref,
                        y_ref):
    n_seq = x_ref.shape[0]
    dm = gpost_ref.shape[1]
    steps = x_ref.shape[1]
    hw = HG_HEADS * HG_DK

    def steps_major(ref, wcol):
        return jnp.concatenate([ref[:, t * wcol:(t + 1) * wcol] for t in range(steps)], axis=0)

    x = jnp.concatenate([x_ref[:, t, :] for t in range(steps)], axis=0)
    oint = jnp.concatenate(
        [jnp.concatenate([oint_ref[pl.ds(h * 8 + t, n_seq, stride=HG_HEADS * 8), :] for h in range(HG_HEADS)],
                         axis=1) for t in range(steps)], axis=0)
    o = oint + steps_major(oin_ref, hw)
    parts = []
    for h in range(HG_HEADS):
        hs = slice(h * HG_DK, (h + 1) * HG_DK)
        parts.append(_rms(o[:, hs]))
    yhg = (jnp.concatenate(parts, axis=1) * steps_major(gn_ref, hw)).astype(bf16)
    merged = steps_major(m5_ref, dm) + steps_major(ghg_ref, dm) * _dot(yhg, wbh_ref[...])
    mo = _dot(merged.astype(bf16), wout_ref[...])
    y = x + _rms(mo) * gpost_ref[...]
    for t in range(steps):
        y_ref[:, t * dm:(t + 1) * dm] = y[t * n_seq:(t + 1) * n_seq, :]


def _sample_post_call(x_s, oint, oin, gn, m5, ghg, w):
    ins = [x_s, oint, oin, gn, m5, ghg] + [w[k] for k in ("wbh", "wout", "gpost")]
    n_seq, steps, dm = x_s.shape
    return pl.pallas_call(
        _sample_post_kernel,
        out_shape=jax.ShapeDtypeStruct((n_seq, steps * dm), f32),
        compiler_params=pltpu.CompilerParams(vmem_limit_bytes=VMEM_LIMIT),
        name="sample_post",
    )(*ins)


def _prep_weights(l, norm_mix_pre, norm_mix_post, norm_mlp_pre, norm_mlp_post, w_in, b_in,
                  s5_a_re, s5_a_im, s5_log_dt, s5_b_re, s5_b_im, s5_c_re, s5_c_im, s5_d, s5_w_glu,
                  s5_b_glu, hg_lb_logits, hg_norm, w_br_s5, w_br_hg, w_out, w_up, w_down, n_pow):
    hp = lax.Precision.HIGHEST
    s5w = s5_d.shape[1]
    n_groups = s5w // S5_GROUP
    w = {}
    row = lambda a: a.astype(f32).reshape(1, -1)
    w["gpre"], w["gpost"] = row(norm_mix_pre[l]), row(norm_mix_post[l])
    w["g2pre"], w["g2post"] = row(norm_mlp_pre[l]), row(norm_mlp_post[l])
    w["win"], w["bin"] = w_in[l].astype(bf16), row(b_in[l])
    lb_all = jnp.cumsum(jax.nn.softmax(hg_lb_logits.astype(f32), axis=0), axis=0)
    w["lbrow"] = row(lb_all[l])
    w["hgnrow"] = row(jnp.tile(hg_norm[l].astype(f32), HG_HEADS))
    w["d"] = row(s5_d[l])
    w["wglu"], w["bglu"] = s5_w_glu[l].astype(bf16), row(s5_b_glu[l])
    w["wbs"], w["wbh"], w["wout"] = w_br_s5[l].astype(bf16), w_br_hg[l].astype(bf16), w_out[l].astype(bf16)
    w["wup_f32"], w["wdn_f32"] = w_up[l], w_down[l]

    a_re, a_im = s5_a_re[l].astype(f32), s5_a_im[l].astype(f32)
    dt = jnp.exp(s5_log_dt[l].astype(f32))[:, None]
    lam_r, lam_i = dt * a_re, dt * a_im
    jj = jnp.arange(S5_CHUNK + 1, dtype=f32)[None, :, None]
    pmag = jnp.exp(jj * lam_r[:, None, :])
    pw_r, pw_i = pmag * jnp.cos(jj * lam_i[:, None, :]), pmag * jnp.sin(jj * lam_i[:, None, :])
    abr, abi = pw_r[:, 1], pw_i[:, 1]
    den = a_re * a_re + a_im * a_im
    nr, ni = abr - 1.0, abi
    cfr, cfi = (nr * a_re + ni * a_im) / den, (ni * a_re - nr * a_im) / den
    b_re, b_im = s5_b_re[l].astype(f32), s5_b_im[l].astype(f32)
    bbr = cfr[..., None] * b_re - cfi[..., None] * b_im
    bbi = cfr[..., None] * b_im + cfi[..., None] * b_re
    c_re, c_im = s5_c_re[l].astype(f32), s5_c_im[l].astype(f32)
    w["abr"], w["abi"] = abr.reshape(1, -1), abi.reshape(1, -1)

    grp = S5_GROUP
    n_state = a_re.shape[1]
    bt = jnp.transpose(jnp.stack([bbr, bbi]), (0, 1, 3, 2))
    bt_r, bt_i = bt[0], bt[1]
    c4r, c4i = c_re[:, None], c_im[:, None]

    def cmul(pr_, pi_, xr, xi):
        pr_, pi_ = pr_[:, :, None, :], pi_[:, :, None, :]
        return pr_ * xr - pi_ * xi, pr_ * xi + pi_ * xr

    rows128 = lambda t: t.reshape(n_groups, LANES, t.shape[-1])
    car_t, cai_t = cmul(pw_r[:, 1:], pw_i[:, 1:], c4r, c4i)
    ca = jnp.transpose(jnp.stack([rows128(car_t), -rows128(cai_t)]), (0, 1, 3, 2))
    ber, bei = cmul(pw_r[:, S5_CHUNK - 1::-1], pw_i[:, S5_CHUNK - 1::-1], bt_r[:, None], bt_i[:, None])
    ber, bei = rows128(ber), rows128(bei)
    zpad = jnp.zeros((n_groups, S5_CHUNK - 1, n_state), f32)
    lag_r = jnp.concatenate([zpad, pw_r[:, :S5_CHUNK]], axis=1)
    lag_i = jnp.concatenate([zpad, pw_i[:, :S5_CHUNK]], axis=1)
    cp_r, cp_i = cmul(lag_r, lag_i, c4r, c4i)
    n_blk = 2 * S5_CHUNK - 1
    cp_all = jnp.concatenate([cp_r, -cp_i], axis=3).reshape(n_groups, n_blk * grp, 2 * n_state)
    rpad_t = jnp.einsum("gmn,gnc->gmc", cp_all, jnp.concatenate([bbr, bbi], axis=1), precision=hp)
    toe_t = jnp.concatenate([rpad_t[:, (S5_CHUNK - 1 - s) * grp:(S5_CHUNK - 1 - s) * grp + LANES, :]
                             for s in range(S5_CHUNK)], axis=2)
    toe = jnp.transpose(toe_t, (0, 2, 1))

    def pair_diag(m):
        g, r, c = m.shape
        m = m.reshape(g // 2, 2, r, c)
        z = jnp.zeros_like(m[:, 0])
        return jnp.concatenate([jnp.concatenate([m[:, 0], z], axis=2),
                                jnp.concatenate([z, m[:, 1]], axis=2)], axis=1)

    w["w1"] = jnp.concatenate([pair_diag(toe), pair_diag(ber), pair_diag(bei)], axis=2).astype(bf16)
    w["w2"] = jnp.concatenate([pair_diag(ca[0]), pair_diag(ca[1])], axis=1).astype(bf16)
    qr, qi = pw_r[:, S5_CHUNK], pw_i[:, S5_CHUNK]
    aps_r, aps_i = [], []
    for _ in range(n_pow):
        aps_r.append(qr.reshape(-1))
        aps_i.append(qi.reshape(-1))
        qr, qi = qr * qr - qi * qi, 2.0 * qr * qi
    w["apr"], w["api"] = jnp.stack(aps_r), jnp.stack(aps_i)

    w["bc"] = jnp.stack([bt_r, bt_i, c_re, -c_im]).reshape(4, s5w, n_state)
    return w


def kernel(x_prompt, x_sample, state_s5_re, state_s5_im, state_hg, norm_mix_pre, norm_mix_post,
           norm_mlp_pre, norm_mlp_post, w_in, b_in, s5_a_re, s5_a_im, s5_log_dt, s5_b_re, s5_b_im,
           s5_c_re, s5_c_im, s5_d, s5_w_glu, s5_b_glu, hg_lb_logits, hg_norm, w_br_s5, w_br_hg,
           w_out, w_up, w_down):
    depth = w_in.shape[0]
    bsz, seq, dm = x_prompt.shape
    n_seq, steps, _ = x_sample.shape
    n_groups, n_state = s5_a_re.shape[1], s5_a_re.shape[2]
    hw = HG_HEADS * HG_DK
    assert depth == 1 and steps == 4 and n_state == S5_STATE
    n_pow = max(1, int(math.log2(min(PROMPT_TILE, seq) // S5_CHUNK)))
    xp, xs = x_prompt, x_sample
    outs = [[] for _ in range(6)]
    for l in range(depth):
        w = _prep_weights(l, norm_mix_pre, norm_mix_post, norm_mlp_pre, norm_mlp_post, w_in, b_in,
                          s5_a_re, s5_a_im, s5_log_dt, s5_b_re, s5_b_im, s5_c_re, s5_c_im, s5_d,
                          s5_w_glu, s5_b_glu, hg_lb_logits, hg_norm, w_br_s5, w_br_hg, w_out, w_up,
                          w_down, n_pow)
        x1, p_re, p_im, p_hg = _mixer_call(xp, w)
        outs[0].append(p_re.reshape(bsz, n_groups, n_state))
        outs[1].append(p_im.reshape(bsz, n_groups, n_state))
        outs[2].append(p_hg)
        h0r = state_s5_re[l].reshape(n_seq, n_groups * n_state)
        h0i = state_s5_im[l].reshape(n_seq, n_groups * n_state)
        m5, ghg, gn, qt8, cols, v8, oin, s_re, s_im = _sample_pre_call(xs, h0r, h0i, w)
        s_hg, oint = _sample_state_call(state_hg[l], qt8, v8, cols)
        x1_sm = _sample_post_call(xs, oint.reshape(n_seq * HG_HEADS * 8, HG_DK), oin, gn, m5, ghg, w)
        yp, xs = _mlp_call(x1.reshape(bsz * seq, dm), x1_sm, w)
        xp = yp.reshape(bsz, seq, dm)
        outs[3].append(s_re.reshape(n_seq, n_groups, n_state))
        outs[4].append(s_im.reshape(n_seq, n_groups, n_state))
        outs[5].append(s_hg)
    return (xp, xs) + tuple(o[0][None] for o in outs)
```

```python
import math

import jax
import jax.numpy as jnp
import numpy as np
from jax import lax
from jax.experimental import pallas as pl
from jax.experimental.pallas import tpu as pltpu

f32 = jnp.float32
bf16 = jnp.bfloat16

NORM_EPS = 1e-6
S5_GROUP = 16
S5_STATE = 64
S5_CHUNK = 8
HG_HEADS = 8
HG_DK = 128
HG_CHUNK = 64
LANES = 128
VMEM_LIMIT = 56 * 1024 * 1024
PROMPT_TILE = 512
MLP_TILE = 1024
MLP_SUBTILES = 2
MLP_FCHUNK = 512
SAMPLE_SEQ_BLOCK = 16
STATE_RING = 3
SAMPLE_PRE_BLOCK = 64


def _sigmoid(x):
    return 0.5 * (jnp.tanh(0.5 * x) + 1.0)


def _silu(x):
    h = 0.5 * x
    return h + h * jnp.tanh(h)


def _gelu_tanh(x):
    c = math.sqrt(2.0 / math.pi)
    return 0.5 * x * (1.0 + jnp.tanh(c * (x + 0.044715 * (x * x * x))))


def _rms(x):
    return x * lax.rsqrt(jnp.mean(x * x, axis=-1, keepdims=True) + NORM_EPS)


def _dot(a, b):
    return jnp.dot(a, b, preferred_element_type=f32)


def _dot_nt(a, b):
    return lax.dot_general(a, b, (((1,), (1,)), ((), ())), preferred_element_type=f32)


def _dot_tn(a, b):
    return lax.dot_general(a, b, (((0,), (0,)), ((), ())), preferred_element_type=f32)


def _block_transpose8(xs, blk):
    xs = list(xs)
    for d in (4, 2, 1):
        keep = (blk & d) == 0
        for i in range(8):
            if i & d == 0:
                a, b = xs[i], xs[i + d]
                xs[i] = jnp.where(keep, a, pltpu.roll(b, d * S5_GROUP, 1))
                xs[i + d] = jnp.where(keep, pltpu.roll(a, LANES - d * S5_GROUP, 1), b)
    return xs


def _mixer_kernel(x_ref, gpre_ref, win_ref, bin_ref, w1_ref, w2_ref, apr_ref, api_ref, d_ref,
                  wglu_ref, bglu_ref, lb_ref, hgn_ref, wbs_ref, wbh_ref, wout_ref, gpost_ref,
                  y_ref, s5r_ref, s5i_ref, hgs_ref,
                  hb_scr, u_scr, ys_scr, o_scr, yloc_scr, er_scr, ei_scr, cr_scr, ci_scr,
                  st_scr, qfig_scr, yhg_scr, m_scr, fac_scr, el_scr, ghg_scr):
    t_idx = pl.program_id(1)
    n_t = pl.num_programs(1)
    tile = x_ref.shape[1]
    rows = tile // S5_CHUNK
    n_pairs = w1_ref.shape[0]
    n_cols = u_scr.shape[0]
    hw = HG_HEADS * HG_DK
    s5w = n_cols * LANES
    g0 = s5w + 4 * hw

    @pl.when(t_idx == 0)
    def _():
        cr_scr[...] = jnp.zeros_like(cr_scr)
        ci_scr[...] = jnp.zeros_like(ci_scr)
        st_scr[...] = jnp.zeros_like(st_scr)

    x = x_ref[0]
    hb_scr[...] = (_rms(x) * gpre_ref[...]).astype(bf16)
    dm = wbs_ref.shape[1]
    half = tile // 2

    def project(r0, r1, c0, c1):
        qfig_scr[r0:r1, c0:c1] = (_dot(hb_scr[r0:r1, :], win_ref[:, s5w + c0:s5w + c1])
                                  + bin_ref[:, s5w + c0:s5w + c1])

    def gate_hg(j):
        c0, c1 = j * (dm // 4), (j + 1) * (dm // 4)
        ghg_scr[:, c0:c1] = _sigmoid(_dot(hb_scr[...], win_ref[:, g0 + dm + c0:g0 + dm + c1])
                                     + bin_ref[:, g0 + dm + c0:g0 + dm + c1])


    u = _dot(hb_scr[...], win_ref[:, :s5w]) + bin_ref[:, :s5w]
    for j in range(n_cols):
        u_scr[j] = u[:, j * LANES:(j + 1) * LANES]

    project(0, half, 0, hw)
    gate_hg(0)
    blk = lax.broadcasted_iota(jnp.int32, (rows, LANES), 1) // S5_GROUP
    for j in range(n_cols):
        xs = [u_scr[j, pl.ds(s, rows, stride=S5_CHUNK), :] for s in range(S5_CHUNK)]
        ys = _block_transpose8(xs, blk)
        for q in range(4):
            o_scr[j * 4 + q] = jnp.concatenate([ys[2 * q], ys[2 * q + 1]], axis=1).astype(bf16)

    project(0, half, hw, 2 * hw)
    gate_hg(1)
    for p in range(n_pairs):
        r1 = _dot(o_scr[p], w1_ref[p])
        yloc_scr[p] = r1[:, :2 * LANES]
        er_scr[:, p * LANES:(p + 1) * LANES] = r1[:, 2 * LANES:3 * LANES]
        ei_scr[:, p * LANES:(p + 1) * LANES] = r1[:, 3 * LANES:]

    er = er_scr[...]
    ei = ei_scr[...]
    row = lax.broadcasted_iota(jnp.int32, (rows, 1), 0)
    first = row == 0
    cr = cr_scr[...]
    ci = ci_scr[...]
    a_r = apr_ref[0:1, :]
    a_i = api_ref[0:1, :]
    project(0, half, 2 * hw, 3 * hw)
    gate_hg(2)
    er = er + jnp.where(first, a_r * cr - a_i * ci, 0.0)
    ei = ei + jnp.where(first, a_r * ci + a_i * cr, 0.0)
    for k in range(int(math.log2(rows))):
        d = 1 << k
        p_r = apr_ref[k:k + 1, :]
        p_i = api_ref[k:k + 1, :]
        s_r = pltpu.roll(er, d, 0)
        s_i = pltpu.roll(ei, d, 0)
        valid = row >= d
        er, ei = (er + jnp.where(valid, p_r * s_r - p_i * s_i, 0.0),
                  ei + jnp.where(valid, p_r * s_i + p_i * s_r, 0.0))
    hs_r = jnp.where(first, cr, pltpu.roll(er, 1, 0))
    hs_i = jnp.where(first, ci, pltpu.roll(ei, 1, 0))
    cr_scr[...] = er[rows - 1:rows, :]
    ci_scr[...] = ei[rows - 1:rows, :]

    project(0, half, 3 * hw, 4 * hw)
    gate_hg(3)
    for j in range(n_cols):
        halves = []
        for q in range(4):
            p = j * 4 + q
            hp = jnp.concatenate([hs_r[:, p * LANES:(p + 1) * LANES],
                                  hs_i[:, p * LANES:(p + 1) * LANES]], axis=1).astype(bf16)
            yp = yloc_scr[p] + _dot(hp, w2_ref[p])
            halves += [yp[:, :LANES], yp[:, LANES:]]
        zs = _block_transpose8(halves, blk)
        for s in range(S5_CHUNK):
            ys_scr[j, pl.ds(s, rows, stride=S5_CHUNK), :] = zs[s]

    y = jnp.concatenate([ys_scr[j] for j in range(n_cols)], axis=1)
    u = jnp.concatenate([u_scr[j] for j in range(n_cols)], axis=1)
    y = _gelu_tanh(y + d_ref[...] * u)
    z = _dot(y.astype(bf16), wglu_ref[...]) + bglu_ref[...]
    ys5 = (y * _sigmoid(z)).astype(bf16)
    gate_s5 = _sigmoid(_dot(hb_scr[...], win_ref[:, g0:g0 + dm]) + bin_ref[:, g0:g0 + dm])
    m_scr[...] = gate_s5 * _dot(ys5, wbs_ref[...])

    ch = HG_CHUNK
    ri = lax.broadcasted_iota(jnp.int32, (ch, ch), 0)
    ci_ = lax.broadcasted_iota(jnp.int32, (ch, ch), 1)
    causal = ci_ <= ri
    tril2 = jnp.concatenate([causal.astype(bf16)] * 2, axis=1)
    lb = lb_ref[...]
    f_mid = 0.5 * (1.0 + lb)
    f_amp = 0.5 * (1.0 - lb)
    hgn = hgn_ref[...]
    n_chunks = tile // ch
    n_side = n_chunks // 2
    assert n_side == 4 and dm % n_side == 0

    def merge_hg(r0, r1, c0, c1):
        m_scr[r0:r1, c0:c1] = (m_scr[r0:r1, c0:c1]
                               + ghg_scr[r0:r1, c0:c1] * _dot(yhg_scr[r0:r1, :], wbh_ref[:, c0:c1]))

    def chunk_factors(c):
        r0 = c * ch
        slot = c % 2
        q_raw = qfig_scr[pl.ds(r0, ch), 0:hw]
        f_raw = qfig_scr[pl.ds(r0, ch), hw:2 * hw]
        q = _silu(q_raw)
        f = f_mid + f_amp * jnp.tanh(0.5 * f_raw)
        lf = jnp.log(f)
        kk = 1.0 - f
        hi = lf.astype(bf16)
        lo = (lf - hi.astype(f32)).astype(bf16)
        b = _dot(tril2, jnp.concatenate([hi, lo], axis=0))
        mid = b[ch // 2 - 1:ch // 2, :]
        b_last = b[ch - 1:ch, :]
        qh = q * jnp.exp(b - mid)
        kh = kk * jnp.exp(mid - b)
        fac_scr[slot, 0] = qh.astype(bf16)
        fac_scr[slot, 1] = kh.astype(bf16)
        fac_scr[slot, 2] = (qh * jnp.exp(mid)).astype(bf16)
        fac_scr[slot, 3] = (kh * jnp.exp(b_last - mid)).astype(bf16)
        fac_scr[slot, 4] = qfig_scr[pl.ds(r0, ch), 2 * hw:3 * hw].astype(bf16)
        el_scr[slot] = jnp.exp(b_last)

    def chunk_matmuls(c):
        r0 = c * ch
        slot = c % 2
        heads = [slice(h * HG_DK, (h + 1) * HG_DK) for h in range(HG_HEADS)]
        scs = [jnp.where(causal, _dot_nt(fac_scr[slot, 0, :, hs], fac_scr[slot, 1, :, hs]), 0.0).astype(bf16)
               for hs in heads]
        e_last = el_scr[slot]
        outs = []
        for h, hs in enumerate(heads):
            st = st_scr[h]
            vb = fac_scr[slot, 4, :, hs]
            o = _dot(scs[h], vb) + _dot_nt(fac_scr[slot, 2, :, hs], st.astype(bf16))
            st_scr[h] = st * e_last[:, hs] + _dot_tn(vb, fac_scr[slot, 3, :, hs])
            outs.append(_rms(o))
        g_raw = qfig_scr[pl.ds(r0, ch), 3 * hw:4 * hw]
        on = jnp.concatenate(outs, axis=1) * hgn * _silu(g_raw)
        yhg_scr[pl.ds(r0, ch), :] = on.astype(bf16)

    def finish(r0, r1):
        mo = _dot(m_scr[r0:r1, :].astype(bf16), wout_ref[...])
        y_ref[0, r0:r1, :] = x_ref[0, r0:r1, :] + _rms(mo) * gpost_ref[...]

    quarter = half // 2
    side_work = {4: lambda: merge_hg(0, half, 0, dm // 2),
                 5: lambda: merge_hg(0, half, dm // 2, dm),
                 6: lambda: finish(0, half),
                 7: lambda: merge_hg(half, half + quarter, 0, dm)}
    chunk_factors(0)
    for c in range(n_chunks):
        if c + 1 < n_chunks:
            chunk_factors(c + 1)
        chunk_matmuls(c)
        if c < n_side:
            project(half, tile, c * hw, (c + 1) * hw)
        else:
            side_work[c]()
    merge_hg(half + quarter, tile, 0, dm)
    finish(half, tile)

    @pl.when(t_idx == n_t - 1)
    def _():
        s5r_ref[0] = cr_scr[...]
        s5i_ref[0] = ci_scr[...]
        for h in range(HG_HEADS):
            hgs_ref[0, h] = st_scr[h].T


def _const_spec(shape):
    nd = len(shape)
    return pl.BlockSpec(shape, lambda *_: (0,) * nd, pipeline_mode=pl.Buffered(1))


def _mixer_call(x, w):
    bsz, seq, dm = x.shape
    tile = min(PROMPT_TILE, seq)
    assert seq % tile == 0 and tile % HG_CHUNK == 0
    rows = tile // S5_CHUNK
    assert rows & (rows - 1) == 0 and w["apr"].shape[0] >= int(math.log2(rows))
    s5w = w["d"].shape[1]
    n_cols = s5w // LANES
    n_pairs = w["w1"].shape[0]
    sdim = w["apr"].shape[1]
    consts = [w[k] for k in ("gpre", "win", "bin", "w1", "w2", "apr", "api", "d", "wglu", "bglu",
                             "lbrow", "hgnrow", "wbs", "wbh", "wout", "gpost")]
    in_specs = [pl.BlockSpec((1, tile, dm), lambda b, t: (b, t, 0))]
    in_specs += [_const_spec(c.shape) for c in consts]
    out_shape = (jax.ShapeDtypeStruct((bsz, seq, dm), f32),
                 jax.ShapeDtypeStruct((bsz, 1, sdim), f32),
                 jax.ShapeDtypeStruct((bsz, 1, sdim), f32),
                 jax.ShapeDtypeStruct((bsz, HG_HEADS, HG_DK, HG_DK), f32))
    out_specs = (pl.BlockSpec((1, tile, dm), lambda b, t: (b, t, 0)),
                 pl.BlockSpec((1, 1, sdim), lambda b, t: (b, 0, 0)),
                 pl.BlockSpec((1, 1, sdim), lambda b, t: (b, 0, 0)),
                 pl.BlockSpec((1, HG_HEADS, HG_DK, HG_DK), lambda b, t: (b, 0, 0, 0)))
    scratch = [
        pltpu.VMEM((tile, dm), bf16),
        pltpu.VMEM((n_cols, tile, LANES), f32),
        pltpu.VMEM((n_cols, tile, LANES), f32),
        pltpu.VMEM((n_pairs, rows, 2 * LANES), bf16),
        pltpu.VMEM((n_pairs, rows, 2 * LANES), f32),
        pltpu.VMEM((rows, sdim), f32),
        pltpu.VMEM((rows, sdim), f32),
        pltpu.VMEM((1, sdim), f32),
        pltpu.VMEM((1, sdim), f32),
        pltpu.VMEM((HG_HEADS, HG_DK, HG_DK), f32),
        pltpu.VMEM((tile, HG_HEADS * 4 * HG_DK), f32),
        pltpu.VMEM((tile, HG_HEADS * HG_DK), bf16),
        pltpu.VMEM((tile, dm), f32),
        pltpu.VMEM((2, 5, HG_CHUNK, HG_HEADS * HG_DK), bf16),
        pltpu.VMEM((2, 1, HG_HEADS * HG_DK), f32),
        pltpu.VMEM((tile, dm), f32),
    ]
    return pl.pallas_call(
        _mixer_kernel,
        grid=(bsz, seq // tile),
        in_specs=in_specs,
        out_specs=out_specs,
        out_shape=out_shape,
        scratch_shapes=scratch,
        compiler_params=pltpu.CompilerParams(
            dimension_semantics=("arbitrary", "arbitrary"), vmem_limit_bytes=VMEM_LIMIT),
        name="prompt_mixer",
    )(x, *consts)


def _mlp_body(x, gpre_ref, wup_scr, wdn_scr, gpost_ref):
    h2 = (_rms(x) * gpre_ref[...]).astype(bf16)
    acc = None
    for c in range(wup_scr.shape[0]):
        a = _dot(h2, wup_scr[c])
        a = jnp.maximum(a, 0.0)
        part = _dot((a * a).astype(bf16), wdn_scr[c])
        acc = part if acc is None else acc + part
    return x + _rms(acc) * gpost_ref[...]


def _mlp_kernel(xp_ref, xs_ref, gpre_ref, wup_ref, wdn_ref, gpost_ref, yp_ref, ys_ref, wup_scr, wdn_scr):
    step = pl.program_id(0)
    n_cast = wup_scr.shape[0]
    last = pl.num_programs(0) - 1
    dm = gpre_ref.shape[1]

    @pl.when(step < n_cast)
    def _():
        wup_scr[step] = wup_ref[...].astype(bf16)
        wdn_scr[step] = wdn_ref[...].astype(bf16)

    @pl.when(jnp.logical_and(step >= n_cast, step < last))
    def _():
        sub = xp_ref.shape[0] // MLP_SUBTILES
        for r0 in range(0, xp_ref.shape[0], sub):
            yp_ref[r0:r0 + sub, :] = _mlp_body(xp_ref[r0:r0 + sub, :], gpre_ref, wup_scr, wdn_scr, gpost_ref)

    @pl.when(step == last)
    def _():
        n_seq = xs_ref.shape[0]
        steps = xs_ref.shape[1] // dm
        x = jnp.concatenate([xs_ref[:, t * dm:(t + 1) * dm] for t in range(steps)], axis=0)
        y = _mlp_body(x, gpre_ref, wup_scr, wdn_scr, gpost_ref)
        for t in range(steps):
            ys_ref[:, t * dm:(t + 1) * dm] = y[t * n_seq:(t + 1) * n_seq, :]


def _mlp_call(xp2d, xs_sm, w):
    n, dm = xp2d.shape
    tile = min(MLP_TILE, n)
    assert n % tile == 0
    n_tiles = n // tile
    wup, wdn = w["wup_f32"], w["wdn_f32"]
    dff = wup.shape[1]
    fc = MLP_FCHUNK
    assert dff % fc == 0
    n_cast = dff // fc
    blk = lambda s: jnp.minimum(s, n_cast - 1)
    row = lambda s: jnp.clip(s - n_cast, 0, n_tiles - 1)
    return pl.pallas_call(
        _mlp_kernel,
        grid=(n_cast + n_tiles + 1,),
        in_specs=[pl.BlockSpec((tile, dm), lambda s: (row(s), 0)),
                  _const_spec(xs_sm.shape),
                  _const_spec(w["g2pre"].shape),
                  pl.BlockSpec((dm, fc), lambda s: (0, blk(s))),
                  pl.BlockSpec((fc, dm), lambda s: (blk(s), 0)),
                  _const_spec(w["g2post"].shape)],
        out_specs=(pl.BlockSpec((tile, dm), lambda s: (row(s), 0)),
                   pl.BlockSpec(xs_sm.shape, lambda s: (0, 0))),
        out_shape=(jax.ShapeDtypeStruct((n, dm), f32), jax.ShapeDtypeStruct(xs_sm.shape, f32)),
        scratch_shapes=[pltpu.VMEM((n_cast, dm, fc), bf16), pltpu.VMEM((n_cast, fc, dm), bf16)],
        compiler_params=pltpu.CompilerParams(
            dimension_semantics=("arbitrary",), vmem_limit_bytes=VMEM_LIMIT),
        name="mlp",
    )(xp2d, xs_sm, w["g2pre"], wup, wdn, w["g2post"])


def _sample_pre_kernel(x_ref, h0r_ref, h0i_ref, gpre_ref, win_ref, bin_ref, bc_ref,
                       abr_ref, abi_ref, d_ref, wglu_ref, bglu_ref, lb_ref, hgn_ref, wbs_ref,
                       m5_ref, ghg_ref, gn_ref, qt_ref, cols_ref, v_ref, oin_ref,
                       s5r_ref, s5i_ref, bblk_ref, cblkt_ref):
    n_seq = h0r_ref.shape[0]
    dm = gpre_ref.shape[1]
    steps = x_ref.shape[1] // dm
    sdim = h0r_ref.shape[1]
    hw = HG_HEADS * HG_DK
    s5w = d_ref.shape[1]

    @pl.when(pl.program_id(0) == 0)
    def _():
        n_state = bc_ref.shape[2]
        row_grp = lax.broadcasted_iota(jnp.int32, (s5w, sdim), 0) // S5_GROUP
        col_grp = lax.broadcasted_iota(jnp.int32, (s5w, sdim), 1) // n_state
        own = row_grp == col_grp
        for k, ref in enumerate((bblk_ref, bblk_ref, cblkt_ref, cblkt_ref)):
            tiled = jnp.concatenate([bc_ref[k]] * (sdim // n_state), axis=1)
            c0 = (k % 2) * sdim
            ref[:, c0:c0 + sdim] = jnp.where(own, tiled, 0.0).astype(bf16)

    x = jnp.concatenate([x_ref[:, t * dm:(t + 1) * dm] for t in range(steps)], axis=0)
    hb = (_rms(x) * gpre_ref[...]).astype(bf16)

    u = _dot(hb, win_ref[:, :s5w]) + bin_ref[:, :s5w]
    bu = _dot(u.astype(bf16), bblk_ref[...])
    hr = h0r_ref[...]
    hi = h0i_ref[...]
    a_r = abr_ref[...]
    a_i = abi_ref[...]
    hs = []
    for t in range(steps):
        sl = slice(t * n_seq, (t + 1) * n_seq)
        hr, hi = (a_r * hr - a_i * hi + bu[sl, :sdim], a_r * hi + a_i * hr + bu[sl, sdim:])
        hs.append(jnp.concatenate([hr, hi], axis=1).astype(bf16))
    s5r_ref[...] = hr
    s5i_ref[...] = hi
    y = _dot_nt(jnp.concatenate(hs, axis=0), cblkt_ref[...])
    y = _gelu_tanh(y + d_ref[...] * u)
    z = _dot(y.astype(bf16), wglu_ref[...]) + bglu_ref[...]
    ys5 = (y * _sigmoid(z)).astype(bf16)
    g0 = s5w + 4 * hw
    def put_steps(ref, val):
        wcol = val.shape[1]
        for t in range(steps):
            ref[:, t * wcol:(t + 1) * wcol] = val[t * n_seq:(t + 1) * n_seq, :]

    gates = _dot(hb, win_ref[:, g0:]) + bin_ref[:, g0:]
    put_steps(m5_ref, _sigmoid(gates[:, :dm]) * _dot(ys5, wbs_ref[...]))
    put_steps(ghg_ref, _sigmoid(gates[:, dm:]))

    def proj(i):
        c0 = s5w + i * hw
        return _dot(hb, win_ref[:, c0:c0 + hw]) + bin_ref[:, c0:c0 + hw]

    q_raw = proj(0)
    f_raw = proj(1)
    v = proj(2)
    g_raw = proj(3)
    q = _silu(q_raw)
    lb = lb_ref[...]
    f = 0.5 * (1.0 + lb) + 0.5 * (1.0 - lb) * jnp.tanh(0.5 * f_raw)
    lf = jnp.log(f)
    kk = 1.0 - f
    put_steps(gn_ref, _silu(g_raw) * hgn_ref[...])
    bs = []
    acc = None
    for t in range(steps):
        sl = slice(t * n_seq, (t + 1) * n_seq)
        acc = lf[sl] if acc is None else acc + lf[sl]
        bs.append(acc)
    b_last = bs[-1]

    def put(ref, slot, val):
        for h in range(HG_HEADS):
            ref[pl.ds(h * 8 + slot, n_seq, stride=HG_HEADS * 8), :] = val[:, h * HG_DK:(h + 1) * HG_DK]

    for ref in (qt_ref, cols_ref, v_ref):
        ref[...] = jnp.zeros_like(ref)
    put(cols_ref, steps, jnp.exp(b_last))
    for t in range(steps):
        sl = slice(t * n_seq, (t + 1) * n_seq)
        put(qt_ref, t, q[sl] * jnp.exp(bs[t]))
        put(cols_ref, t, kk[sl] * jnp.exp(b_last - bs[t]))
        put(v_ref, t, v[sl])
        o_t = None
        for s in range(t + 1):
            sls = slice(s * n_seq, (s + 1) * n_seq)
            prod = q[sl] * kk[sls] * jnp.exp(bs[t] - bs[s])
            parts = []
            for h in range(HG_HEADS):
                hs = slice(h * HG_DK, (h + 1) * HG_DK)
                wgt = jnp.sum(prod[:, hs], axis=1, keepdims=True)
                parts.append(wgt * v[sls, hs])
            term = jnp.concatenate(parts, axis=1)
            o_t = term if o_t is None else o_t + term
        oin_ref[:, t * hw:(t + 1) * hw] = o_t


def _sample_pre_call(x_sm, h0r, h0i, w):
    n_seq, sdim = h0r.shape
    dm = w["gpre"].shape[1]
    steps = x_sm.shape[1] // dm
    hw = HG_HEADS * HG_DK
    nb = SAMPLE_PRE_BLOCK
    assert n_seq % nb == 0
    s5w = w["d"].shape[1]
    consts = [w[k] for k in ("gpre", "win", "bin", "bc", "abr", "abi", "d",
                             "wglu", "bglu", "lbrow", "hgnrow", "wbs")]
    slots = HG_HEADS * 8
    shapes = ((1, steps * dm),
              (1, steps * dm),
              (1, steps * hw),
              (slots, HG_DK),
              (slots, HG_DK),
              (slots, HG_DK),
              (1, steps * hw),
              (1, sdim), (1, sdim))
    seq_spec = lambda shp: pl.BlockSpec((nb * shp[0], shp[1]), lambda i: (i, 0))
    return pl.pallas_call(
        _sample_pre_kernel,
        grid=(n_seq // nb,),
        in_specs=([seq_spec((1, steps * dm)), seq_spec((1, sdim)), seq_spec((1, sdim))]
                  + [_const_spec(c.shape) for c in consts]),
        out_specs=tuple(seq_spec(shp) for shp in shapes),
        out_shape=tuple(jax.ShapeDtypeStruct((n_seq * shp[0], shp[1]), f32) for shp in shapes),
        scratch_shapes=[pltpu.VMEM((s5w, 2 * sdim), bf16),
                        pltpu.VMEM((s5w, 2 * sdim), bf16)],
        compiler_params=pltpu.CompilerParams(
            dimension_semantics=("arbitrary",), vmem_limit_bytes=VMEM_LIMIT),
        name="sample_pre",
    )(x_sm, h0r, h0i, *consts)


def _sample_state_kernel(s0_hbm, qt_ref, v_ref, cols_ref, snew_ref, oint_ref, s0_buf, sem):
    step = pl.program_id(0)
    n_steps = pl.num_programs(0)
    nb = qt_ref.shape[0]

    def state_copy(at_step, slot):
        return pltpu.make_async_copy(s0_hbm.at[pl.ds(at_step * nb, nb)], s0_buf.at[slot], sem.at[slot])

    @pl.when(step == 0)
    def _():
        for ahead in range(STATE_RING - 1):
            @pl.when(ahead < n_steps)
            def _():
                state_copy(ahead, ahead).start()

    nxt = step + STATE_RING - 1

    @pl.when(nxt < n_steps)
    def _():
        state_copy(nxt, nxt % STATE_RING).start()

    slot = step % STATE_RING
    state_copy(step, slot).wait()
    s0_ref = s0_buf.at[slot]
    nrow = cols_ref.shape[1]
    hw = HG_HEADS * HG_DK
    row_head = lax.broadcasted_iota(jnp.int32, (nrow, hw), 0) // 8
    col_head = lax.broadcasted_iota(jnp.int32, (nrow, hw), 1) // HG_DK
    diag = row_head == col_head
    pad = jnp.zeros((LANES - nrow, HG_DK), f32)
    for i in range(nb):
        xt = jnp.concatenate([cols_ref[i], pad], axis=0).T
        vbd = jnp.where(diag, jnp.concatenate([v_ref[i]] * HG_HEADS, axis=1), 0.0)
        vbd = jnp.concatenate([vbd, jnp.zeros((LANES - nrow, hw), f32)], axis=0).astype(bf16)
        ds = _dot(xt.astype(bf16), vbd)
        for h in range(HG_HEADS):
            s0 = s0_ref[i, h]
            oint_ref[i, h] = _dot(qt_ref[i, h].astype(bf16), s0.astype(bf16))
            snew_ref[i, h] = xt[:, h * 8 + 4:h * 8 + 5] * s0 + ds[:, h * HG_DK:(h + 1) * HG_DK]


def _sample_state_call(s0, qt8, v8, cols):
    n_seq = s0.shape[0]
    nb = SAMPLE_SEQ_BLOCK
    blk4 = (nb, HG_HEADS, 8, HG_DK)
    return pl.pallas_call(
        _sample_state_kernel,
        grid=(n_seq // nb,),
        in_specs=[pl.BlockSpec(memory_space=pl.ANY),
                  pl.BlockSpec(blk4, lambda i: (i, 0, 0, 0)),
                  pl.BlockSpec((nb, HG_HEADS * 8, HG_DK), lambda i: (i, 0, 0)),
                  pl.BlockSpec((nb, HG_HEADS * 8, HG_DK), lambda i: (i, 0, 0))],
        out_specs=(pl.BlockSpec((nb, HG_HEADS, HG_DK, HG_DK), lambda i: (i, 0, 0, 0)),
                   pl.BlockSpec(blk4, lambda i: (i, 0, 0, 0))),
        out_shape=(jax.ShapeDtypeStruct(s0.shape, f32),
                   jax.ShapeDtypeStruct((n_seq, HG_HEADS, 8, HG_DK), f32)),
        scratch_shapes=[pltpu.VMEM((STATE_RING, nb, HG_HEADS, HG_DK, HG_DK), f32),
                        pltpu.SemaphoreType.DMA((STATE_RING,))],
        compiler_params=pltpu.CompilerParams(
            dimension_semantics=("arbitrary",), vmem_limit_bytes=VMEM_LIMIT),
        name="sample_state",
    )(s0, qt8.reshape((n_seq,) + blk4[1:]), v8.reshape(n_seq, HG_HEADS * 8, HG_DK),
      cols.reshape(n_seq, HG_HEADS * 8, HG_DK))


def _sample_post_kernel(x_ref, oint_ref, oin_ref, gn_ref, m5_ref, ghg_ref, wbh_ref, wout_ref, gpost_ref,
                        y_ref):
    n_seq = x_ref.shape[0]
    dm = gpost_ref.shape[1]
    steps = x_ref.shape[1] // dm
    hw = HG_HEADS * HG_DK

    def steps_major(ref, wcol):
        return jnp.concatenate([ref[:, t * wcol:(t + 1) * wcol] for t in range(steps)], axis=0)

    x = steps_major(x_ref, dm)
    oint = jnp.concatenate(
        [jnp.concatenate([oint_ref[pl.ds(h * 8 + t, n_seq, stride=HG_HEADS * 8), :] for h in range(HG_HEADS)],
                         axis=1) for t in range(steps)], axis=0)
    o = oint + steps_major(oin_ref, hw)
    parts = []
    for h in range(HG_HEADS):
        hs = slice(h * HG_DK, (h + 1) * HG_DK)
        parts.append(_rms(o[:, hs]))
    yhg = (jnp.concatenate(parts, axis=1) * steps_major(gn_ref, hw)).astype(bf16)
    merged = steps_major(m5_ref, dm) + steps_major(ghg_ref, dm) * _dot(yhg, wbh_ref[...])
    mo = _dot(merged.astype(bf16), wout_ref[...])
    y = x + _rms(mo) * gpost_ref[...]
    for t in range(steps):
        y_ref[:, t * dm:(t + 1) * dm] = y[t * n_seq:(t + 1) * n_seq, :]


def _sample_post_call(x_sm, oint, oin, gn, m5, ghg, w):
    ins = [x_sm, oint, oin, gn, m5, ghg] + [w[k] for k in ("wbh", "wout", "gpost")]
    return pl.pallas_call(
        _sample_post_kernel,
        out_shape=jax.ShapeDtypeStruct(x_sm.shape, f32),
        compiler_params=pltpu.CompilerParams(vmem_limit_bytes=VMEM_LIMIT),
        name="sample_post",
    )(*ins)


def _prep_weights(l, norm_mix_pre, norm_mix_post, norm_mlp_pre, norm_mlp_post, w_in, b_in,
                  s5_a_re, s5_a_im, s5_log_dt, s5_b_re, s5_b_im, s5_c_re, s5_c_im, s5_d, s5_w_glu,
                  s5_b_glu, hg_lb_logits, hg_norm, w_br_s5, w_br_hg, w_out, w_up, w_down, n_pow):
    hp = lax.Precision.HIGHEST
    s5w = s5_d.shape[1]
    n_groups = s5w // S5_GROUP
    w = {}
    row = lambda a: a.astype(f32).reshape(1, -1)
    w["gpre"], w["gpost"] = row(norm_mix_pre[l]), row(norm_mix_post[l])
    w["g2pre"], w["g2post"] = row(norm_mlp_pre[l]), row(norm_mlp_post[l])
    w["win"], w["bin"] = w_in[l].astype(bf16), row(b_in[l])
    lb_all = jnp.cumsum(jax.nn.softmax(hg_lb_logits.astype(f32), axis=0), axis=0)
    w["lbrow"] = row(lb_all[l])
    w["hgnrow"] = row(jnp.tile(hg_norm[l].astype(f32), HG_HEADS))
    w["d"] = row(s5_d[l])
    w["wglu"], w["bglu"] = s5_w_glu[l].astype(bf16), row(s5_b_glu[l])
    w["wbs"], w["wbh"], w["wout"] = w_br_s5[l].astype(bf16), w_br_hg[l].astype(bf16), w_out[l].astype(bf16)
    w["wup_f32"], w["wdn_f32"] = w_up[l], w_down[l]

    a_re, a_im = s5_a_re[l].astype(f32), s5_a_im[l].astype(f32)
    dt = jnp.exp(s5_log_dt[l].astype(f32))[:, None]
    mag = jnp.exp(dt * a_re)
    abr, abi = mag * jnp.cos(dt * a_im), mag * jnp.sin(dt * a_im)
    den = a_re * a_re + a_im * a_im
    nr, ni = abr - 1.0, abi
    cfr, cfi = (nr * a_re + ni * a_im) / den, (ni * a_re - nr * a_im) / den
    b_re, b_im = s5_b_re[l].astype(f32), s5_b_im[l].astype(f32)
    bbr = cfr[..., None] * b_re - cfi[..., None] * b_im
    bbi = cfr[..., None] * b_im + cfi[..., None] * b_re
    c_re, c_im = s5_c_re[l].astype(f32), s5_c_im[l].astype(f32)
    w["abr"], w["abi"] = abr.reshape(1, -1), abi.reshape(1, -1)

    lam_r, lam_i = dt * a_re, dt * a_im
    grp = S5_GROUP
    n_state = a_re.shape[1]
    jj = jnp.arange(S5_CHUNK + 1, dtype=f32)[None, :, None]
    pmag = jnp.exp(jj * lam_r[:, None, :])
    pw_r, pw_i = pmag * jnp.cos(jj * lam_i[:, None, :]), pmag * jnp.sin(jj * lam_i[:, None, :])
    bt = jnp.transpose(jnp.stack([bbr, bbi]), (0, 1, 3, 2))
    bt_r, bt_i = bt[0], bt[1]
    c4r, c4i = c_re[:, None], c_im[:, None]

    def cmul(pr_, pi_, xr, xi):
        pr_, pi_ = pr_[:, :, None, :], pi_[:, :, None, :]
        return pr_ * xr - pi_ * xi, pr_ * xi + pi_ * xr

    rows128 = lambda t: t.reshape(n_groups, LANES, t.shape[-1])
    car_t, cai_t = cmul(pw_r[:, 1:], pw_i[:, 1:], c4r, c4i)
    ca = jnp.transpose(jnp.stack([rows128(car_t), -rows128(cai_t)]), (0, 1, 3, 2))
    ber, bei = cmul(pw_r[:, S5_CHUNK - 1::-1], pw_i[:, S5_CHUNK - 1::-1], bt_r[:, None], bt_i[:, None])
    ber, bei = rows128(ber), rows128(bei)
    zpad = jnp.zeros((n_groups, S5_CHUNK - 1, n_state), f32)
    lag_r = jnp.concatenate([zpad, pw_r[:, :S5_CHUNK]], axis=1)
    lag_i = jnp.concatenate([zpad, pw_i[:, :S5_CHUNK]], axis=1)
    cp_r, cp_i = cmul(lag_r, lag_i, c4r, c4i)
    n_blk = 2 * S5_CHUNK - 1
    cp_all = jnp.concatenate([cp_r, -cp_i], axis=3).reshape(n_groups, n_blk * grp, 2 * n_state)
    rpad_t = jnp.einsum("gmn,gnc->gmc", cp_all, jnp.concatenate([bbr, bbi], axis=1), precision=hp)
    toe_t = jnp.concatenate([rpad_t[:, (S5_CHUNK - 1 - s) * grp:(S5_CHUNK - 1 - s) * grp + LANES, :]
                             for s in range(S5_CHUNK)], axis=2)
    toe = jnp.transpose(toe_t, (0, 2, 1))

    def pair_diag(m):
        g, r, c = m.shape
        m = m.reshape(g // 2, 2, r, c)
        z = jnp.zeros_like(m[:, 0])
        return jnp.concatenate([jnp.concatenate([m[:, 0], z], axis=2),
                                jnp.concatenate([z, m[:, 1]], axis=2)], axis=1)

    w["w1"] = jnp.concatenate([pair_diag(toe), pair_diag(ber), pair_diag(bei)], axis=2).astype(bf16)
    w["w2"] = jnp.concatenate([pair_diag(ca[0]), pair_diag(ca[1])], axis=1).astype(bf16)
    qr, qi = pw_r[:, S5_CHUNK], pw_i[:, S5_CHUNK]
    aps_r, aps_i = [], []
    for _ in range(n_pow):
        aps_r.append(qr.reshape(-1))
        aps_i.append(qi.reshape(-1))
        qr, qi = qr * qr - qi * qi, 2.0 * qr * qi
    w["apr"], w["api"] = jnp.stack(aps_r), jnp.stack(aps_i)

    w["bc"] = jnp.stack([bt_r, bt_i, c_re, -c_im]).reshape(4, s5w, n_state)
    return w


def kernel(x_prompt, x_sample, state_s5_re, state_s5_im, state_hg, norm_mix_pre, norm_mix_post,
           norm_mlp_pre, norm_mlp_post, w_in, b_in, s5_a_re, s5_a_im, s5_log_dt, s5_b_re, s5_b_im,
           s5_c_re, s5_c_im, s5_d, s5_w_glu, s5_b_glu, hg_lb_logits, hg_norm, w_br_s5, w_br_hg,
           w_out, w_up, w_down):
    depth = w_in.shape[0]
    bsz, seq, dm = x_prompt.shape
    n_seq, steps, _ = x_sample.shape
    n_groups, n_state = s5_a_re.shape[1], s5_a_re.shape[2]
    hw = HG_HEADS * HG_DK
    assert depth == 1 and steps == 4 and n_state == S5_STATE
    n_pow = max(1, int(math.log2(min(PROMPT_TILE, seq) // S5_CHUNK)))
    xp, xs = x_prompt, x_sample
    outs = [[] for _ in range(6)]
    for l in range(depth):
        w = _prep_weights(l, norm_mix_pre, norm_mix_post, norm_mlp_pre, norm_mlp_post, w_in, b_in,
                          s5_a_re, s5_a_im, s5_log_dt, s5_b_re, s5_b_im, s5_c_re, s5_c_im, s5_d,
                          s5_w_glu, s5_b_glu, hg_lb_logits, hg_norm, w_br_s5, w_br_hg, w_out, w_up,
                          w_down, n_pow)
        x1, p_re, p_im, p_hg = _mixer_call(xp, w)
        outs[0].append(p_re.reshape(bsz, n_groups, n_state))
        outs[1].append(p_im.reshape(bsz, n_groups, n_state))
        outs[2].append(p_hg)
        x_sm = xs.reshape(n_seq, steps * dm)
        h0r = state_s5_re[l].reshape(n_seq, n_groups * n_state)
        h0i = state_s5_im[l].reshape(n_seq, n_groups * n_state)
        m5, ghg, gn, qt8, cols, v8, oin, s_re, s_im = _sample_pre_call(x_sm, h0r, h0i, w)
        s_hg, oint = _sample_state_call(state_hg[l], qt8, v8, cols)
        x1_sm = _sample_post_call(x_sm, oint.reshape(n_seq * HG_HEADS * 8, HG_DK), oin, gn, m5, ghg, w)
        yp, y_sm = _mlp_call(x1.reshape(bsz * seq, dm), x1_sm, w)
        xp = yp.reshape(bsz, seq, dm)
        xs = y_sm.reshape(n_seq, steps, dm)
        outs[3].append(s_re.reshape(n_seq, n_groups, n_state))
        outs[4].append(s_im.reshape(n_seq, n_groups, n_state))
        outs[5].append(s_hg)
    return (xp, xs) + tuple(o[0][None] for o in outs)
```

```python
import math

import jax
import jax.numpy as jnp
import numpy as np
from jax import lax
from jax.experimental import pallas as pl
from jax.experimental.pallas import tpu as pltpu

f32 = jnp.float32
bf16 = jnp.bfloat16

NORM_EPS = 1e-6
S5_GROUP = 16
S5_STATE = 64
S5_CHUNK = 8
HG_HEADS = 8
HG_DK = 128
HG_CHUNK = 64
LANES = 128
VMEM_LIMIT = 56 * 1024 * 1024
PROMPT_TILE = 512
MLP_TILE = 1024
MLP_SUBTILES = 2
MLP_FCHUNK = 512
SAMPLE_SEQ_BLOCK = 16
STATE_RING = 3
SAMPLE_PRE_BLOCK = 64


def _sigmoid(x):
    return 0.5 * (jnp.tanh(0.5 * x) + 1.0)


def _silu(x):
    h = 0.5 * x
    return h + h * jnp.tanh(h)


def _gelu_tanh(x):
    c = math.sqrt(2.0 / math.pi)
    return 0.5 * x * (1.0 + jnp.tanh(c * (x + 0.044715 * (x * x * x))))


def _rms(x):
    return x * lax.rsqrt(jnp.mean(x * x, axis=-1, keepdims=True) + NORM_EPS)


def _dot(a, b):
    return jnp.dot(a, b, preferred_element_type=f32)


def _dot_nt(a, b):
    return lax.dot_general(a, b, (((1,), (1,)), ((), ())), preferred_element_type=f32)


def _dot_tn(a, b):
    return lax.dot_general(a, b, (((0,), (0,)), ((), ())), preferred_element_type=f32)


def _block_transpose8(xs, blk):
    xs = list(xs)
    for d in (4, 2, 1):
        keep = (blk & d) == 0
        for i in range(8):
            if i & d == 0:
                a, b = xs[i], xs[i + d]
                xs[i] = jnp.where(keep, a, pltpu.roll(b, d * S5_GROUP, 1))
                xs[i + d] = jnp.where(keep, pltpu.roll(a, LANES - d * S5_GROUP, 1), b)
    return xs


def _mixer_kernel(x_ref, gpre_ref, win_ref, bin_ref, w1_ref, w2_ref, apr_ref, api_ref, d_ref,
                  wglu_ref, bglu_ref, lb_ref, hgn_ref, wbs_ref, wbh_ref, wout_ref, gpost_ref,
                  y_ref, s5r_ref, s5i_ref, hgs_ref,
                  hb_scr, u_scr, ys_scr, o_scr, yloc_scr, er_scr, ei_scr, cr_scr, ci_scr,
                  st_scr, qfig_scr, yhg_scr, m_scr, fac_scr, el_scr, ghg_scr):
    t_idx = pl.program_id(1)
    n_t = pl.num_programs(1)
    tile = x_ref.shape[1]
    rows = tile // S5_CHUNK
    n_pairs = w1_ref.shape[0]
    n_cols = u_scr.shape[0]
    hw = HG_HEADS * HG_DK
    s5w = n_cols * LANES
    g0 = s5w + 4 * hw

    @pl.when(t_idx == 0)
    def _():
        cr_scr[...] = jnp.zeros_like(cr_scr)
        ci_scr[...] = jnp.zeros_like(ci_scr)
        st_scr[...] = jnp.zeros_like(st_scr)

    x = x_ref[0]
    hb_scr[...] = (_rms(x) * gpre_ref[...]).astype(bf16)
    dm = wbs_ref.shape[1]
    half = tile // 2

    def project(r0, r1, c0, c1):
        qfig_scr[r0:r1, c0:c1] = (_dot(hb_scr[r0:r1, :], win_ref[:, s5w + c0:s5w + c1])
                                  + bin_ref[:, s5w + c0:s5w + c1])

    def gate_hg(j):
        c0, c1 = j * (dm // 4), (j + 1) * (dm // 4)
        ghg_scr[:, c0:c1] = _sigmoid(_dot(hb_scr[...], win_ref[:, g0 + dm + c0:g0 + dm + c1])
                                     + bin_ref[:, g0 + dm + c0:g0 + dm + c1])


    u = _dot(hb_scr[...], win_ref[:, :s5w]) + bin_ref[:, :s5w]
    for j in range(n_cols):
        u_scr[j] = u[:, j * LANES:(j + 1) * LANES]

    project(0, half, 0, hw)
    gate_hg(0)
    blk = lax.broadcasted_iota(jnp.int32, (rows, LANES), 1) // S5_GROUP
    for j in range(n_cols):
        xs = [u_scr[j, pl.ds(s, rows, stride=S5_CHUNK), :] for s in range(S5_CHUNK)]
        ys = _block_transpose8(xs, blk)
        for q in range(4):
            o_scr[j * 4 + q] = jnp.concatenate([ys[2 * q], ys[2 * q + 1]], axis=1).astype(bf16)

    project(0, half, hw, 2 * hw)
    gate_hg(1)
    for p in range(n_pairs):
        r1 = _dot(o_scr[p], w1_ref[p])
        yloc_scr[p] = r1[:, :2 * LANES]
        er_scr[:, p * LANES:(p + 1) * LANES] = r1[:, 2 * LANES:3 * LANES]
        ei_scr[:, p * LANES:(p + 1) * LANES] = r1[:, 3 * LANES:]

    er = er_scr[...]
    ei = ei_scr[...]
    row = lax.broadcasted_iota(jnp.int32, (rows, 1), 0)
    first = row == 0
    cr = cr_scr[...]
    ci = ci_scr[...]
    a_r = apr_ref[0:1, :]
    a_i = api_ref[0:1, :]
    project(0, half, 2 * hw, 3 * hw)
    gate_hg(2)
    er = er + jnp.where(first, a_r * cr - a_i * ci, 0.0)
    ei = ei + jnp.where(first, a_r * ci + a_i * cr, 0.0)
    for k in range(int(math.log2(rows))):
        d = 1 << k
        p_r = apr_ref[k:k + 1, :]
        p_i = api_ref[k:k + 1, :]
        s_r = pltpu.roll(er, d, 0)
        s_i = pltpu.roll(ei, d, 0)
        valid = row >= d
        er, ei = (er + jnp.where(valid, p_r * s_r - p_i * s_i, 0.0),
                  ei + jnp.where(valid, p_r * s_i + p_i * s_r, 0.0))
    hs_r = jnp.where(first, cr, pltpu.roll(er, 1, 0))
    hs_i = jnp.where(first, ci, pltpu.roll(ei, 1, 0))
    cr_scr[...] = er[rows - 1:rows, :]
    ci_scr[...] = ei[rows - 1:rows, :]

    project(0, half, 3 * hw, 4 * hw)
    gate_hg(3)
    for j in range(n_cols):
        halves = []
        for q in range(4):
            p = j * 4 + q
            hp = jnp.concatenate([hs_r[:, p * LANES:(p + 1) * LANES],
                                  hs_i[:, p * LANES:(p + 1) * LANES]], axis=1).astype(bf16)
            yp = yloc_scr[p] + _dot(hp, w2_ref[p])
            halves += [yp[:, :LANES], yp[:, LANES:]]
        zs = _block_transpose8(halves, blk)
        for s in range(S5_CHUNK):
            ys_scr[j, pl.ds(s, rows, stride=S5_CHUNK), :] = zs[s]

    y = jnp.concatenate([ys_scr[j] for j in range(n_cols)], axis=1)
    u = jnp.concatenate([u_scr[j] for j in range(n_cols)], axis=1)
    y = _gelu_tanh(y + d_ref[...] * u)
    z = _dot(y.astype(bf16), wglu_ref[...]) + bglu_ref[...]
    ys5 = (y * _sigmoid(z)).astype(bf16)
    gate_s5 = _sigmoid(_dot(hb_scr[...], win_ref[:, g0:g0 + dm]) + bin_ref[:, g0:g0 + dm])
    m_scr[...] = gate_s5 * _dot(ys5, wbs_ref[...])

    ch = HG_CHUNK
    ri = lax.broadcasted_iota(jnp.int32, (ch, ch), 0)
    ci_ = lax.broadcasted_iota(jnp.int32, (ch, ch), 1)
    causal = ci_ <= ri
    tril2 = jnp.concatenate([causal.astype(bf16)] * 2, axis=1)
    lb = lb_ref[...]
    f_mid = 0.5 * (1.0 + lb)
    f_amp = 0.5 * (1.0 - lb)
    hgn = hgn_ref[...]
    n_chunks = tile // ch
    n_side = n_chunks // 2
    assert n_side == 4 and dm % n_side == 0

    def merge_hg(r0, r1, c0, c1):
        m_scr[r0:r1, c0:c1] = (m_scr[r0:r1, c0:c1]
                               + ghg_scr[r0:r1, c0:c1] * _dot(yhg_scr[r0:r1, :], wbh_ref[:, c0:c1]))

    def chunk_factors(c):
        r0 = c * ch
        slot = c % 2
        q_raw = qfig_scr[pl.ds(r0, ch), 0:hw]
        f_raw = qfig_scr[pl.ds(r0, ch), hw:2 * hw]
        q = _silu(q_raw)
        f = f_mid + f_amp * jnp.tanh(0.5 * f_raw)
        lf = jnp.log(f)
        kk = 1.0 - f
        hi = lf.astype(bf16)
        lo = (lf - hi.astype(f32)).astype(bf16)
        b = _dot(tril2, jnp.concatenate([hi, lo], axis=0))
        mid = b[ch // 2 - 1:ch // 2, :]
        b_last = b[ch - 1:ch, :]
        qh = q * jnp.exp(b - mid)
        kh = kk * jnp.exp(mid - b)
        fac_scr[slot, 0] = qh.astype(bf16)
        fac_scr[slot, 1] = kh.astype(bf16)
        fac_scr[slot, 2] = (qh * jnp.exp(mid)).astype(bf16)
        fac_scr[slot, 3] = (kh * jnp.exp(b_last - mid)).astype(bf16)
        fac_scr[slot, 4] = qfig_scr[pl.ds(r0, ch), 2 * hw:3 * hw].astype(bf16)
        el_scr[slot] = jnp.exp(b_last)

    def chunk_matmuls(c):
        r0 = c * ch
        slot = c % 2
        heads = [slice(h * HG_DK, (h + 1) * HG_DK) for h in range(HG_HEADS)]
        scs = [jnp.where(causal, _dot_nt(fac_scr[slot, 0, :, hs], fac_scr[slot, 1, :, hs]), 0.0).astype(bf16)
               for hs in heads]
        e_last = el_scr[slot]
        outs = []
        for h, hs in enumerate(heads):
            st = st_scr[h]
            vb = fac_scr[slot, 4, :, hs]
            o = _dot(scs[h], vb) + _dot_nt(fac_scr[slot, 2, :, hs], st.astype(bf16))
            st_scr[h] = st * e_last[:, hs] + _dot_tn(vb, fac_scr[slot, 3, :, hs])
            outs.append(_rms(o))
        g_raw = qfig_scr[pl.ds(r0, ch), 3 * hw:4 * hw]
        on = jnp.concatenate(outs, axis=1) * hgn * _silu(g_raw)
        yhg_scr[pl.ds(r0, ch), :] = on.astype(bf16)

    def finish(r0, r1):
        mo = _dot(m_scr[r0:r1, :].astype(bf16), wout_ref[...])
        y_ref[0, r0:r1, :] = x_ref[0, r0:r1, :] + _rms(mo) * gpost_ref[...]

    quarter = half // 2
    side_work = {4: lambda: merge_hg(0, half, 0, dm // 2),
                 5: lambda: merge_hg(0, half, dm // 2, dm),
                 6: lambda: finish(0, half),
                 7: lambda: merge_hg(half, half + quarter, 0, dm)}
    chunk_factors(0)
    for c in range(n_chunks):
        if c + 1 < n_chunks:
            chunk_factors(c + 1)
        chunk_matmuls(c)
        if c < n_side:
            project(half, tile, c * hw, (c + 1) * hw)
        else:
            side_work[c]()
    merge_hg(half + quarter, tile, 0, dm)
    finish(half, tile)

    @pl.when(t_idx == n_t - 1)
    def _():
        s5r_ref[0] = cr_scr[...]
        s5i_ref[0] = ci_scr[...]
        for h in range(HG_HEADS):
            hgs_ref[0, h] = st_scr[h].T


def _const_spec(shape):
    nd = len(shape)
    return pl.BlockSpec(shape, lambda *_: (0,) * nd, pipeline_mode=pl.Buffered(1))


def _mixer_call(x, w):
    bsz, seq, dm = x.shape
    tile = min(PROMPT_TILE, seq)
    assert seq % tile == 0 and tile % HG_CHUNK == 0
    rows = tile // S5_CHUNK
    assert rows & (rows - 1) == 0 and w["apr"].shape[0] >= int(math.log2(rows))
    s5w = w["d"].shape[1]
    n_cols = s5w // LANES
    n_pairs = w["w1"].shape[0]
    sdim = w["apr"].shape[1]
    consts = [w[k] for k in ("gpre", "win", "bin", "w1", "w2", "apr", "api", "d", "wglu", "bglu",
                             "lbrow", "hgnrow", "wbs", "wbh", "wout", "gpost")]
    in_specs = [pl.BlockSpec((1, tile, dm), lambda b, t: (b, t, 0))]
    in_specs += [_const_spec(c.shape) for c in consts]
    out_shape = (jax.ShapeDtypeStruct((bsz, seq, dm), f32),
                 jax.ShapeDtypeStruct((bsz, 1, sdim), f32),
                 jax.ShapeDtypeStruct((bsz, 1, sdim), f32),
                 jax.ShapeDtypeStruct((bsz, HG_HEADS, HG_DK, HG_DK), f32))
    out_specs = (pl.BlockSpec((1, tile, dm), lambda b, t: (b, t, 0)),
                 pl.BlockSpec((1, 1, sdim), lambda b, t: (b, 0, 0)),
                 pl.BlockSpec((1, 1, sdim), lambda b, t: (b, 0, 0)),
                 pl.BlockSpec((1, HG_HEADS, HG_DK, HG_DK), lambda b, t: (b, 0, 0, 0)))
    scratch = [
        pltpu.VMEM((tile, dm), bf16),
        pltpu.VMEM((n_cols, tile, LANES), f32),
        pltpu.VMEM((n_cols, tile, LANES), f32),
        pltpu.VMEM((n_pairs, rows, 2 * LANES), bf16),
        pltpu.VMEM((n_pairs, rows, 2 * LANES), f32),
        pltpu.VMEM((rows, sdim), f32),
        pltpu.VMEM((rows, sdim), f32),
        pltpu.VMEM((1, sdim), f32),
        pltpu.VMEM((1, sdim), f32),
        pltpu.VMEM((HG_HEADS, HG_DK, HG_DK), f32),
        pltpu.VMEM((tile, HG_HEADS * 4 * HG_DK), f32),
        pltpu.VMEM((tile, HG_HEADS * HG_DK), bf16),
        pltpu.VMEM((tile, dm), f32),
        pltpu.VMEM((2, 5, HG_CHUNK, HG_HEADS * HG_DK), bf16),
        pltpu.VMEM((2, 1, HG_HEADS * HG_DK), f32),
        pltpu.VMEM((tile, dm), f32),
    ]
    return pl.pallas_call(
        _mixer_kernel,
        grid=(bsz, seq // tile),
        in_specs=in_specs,
        out_specs=out_specs,
        out_shape=out_shape,
        scratch_shapes=scratch,
        compiler_params=pltpu.CompilerParams(
            dimension_semantics=("arbitrary", "arbitrary"), vmem_limit_bytes=VMEM_LIMIT),
        name="prompt_mixer",
    )(x, *consts)


def _mlp_body(x, gpre_ref, wup_scr, wdn_scr, gpost_ref):
    h2 = (_rms(x) * gpre_ref[...]).astype(bf16)
    acc = None
    for c in range(wup_scr.shape[0]):
        a = _dot(h2, wup_scr[c])
        a = jnp.maximum(a, 0.0)
        part = _dot((a * a).astype(bf16), wdn_scr[c])
        acc = part if acc is None else acc + part
    return x + _rms(acc) * gpost_ref[...]


def _mlp_kernel(xp_ref, xs_ref, gpre_ref, wup_ref, wdn_ref, gpost_ref, yp_ref, ys_ref, wup_scr, wdn_scr):
    step = pl.program_id(0)
    n_cast = wup_scr.shape[0]
    last = pl.num_programs(0) - 1
    dm = gpre_ref.shape[1]

    @pl.when(step < n_cast)
    def _():
        wup_scr[step] = wup_ref[...].astype(bf16)
        wdn_scr[step] = wdn_ref[...].astype(bf16)

    @pl.when(jnp.logical_and(step >= n_cast, step < last))
    def _():
        sub = xp_ref.shape[0] // MLP_SUBTILES
        for r0 in range(0, xp_ref.shape[0], sub):
            yp_ref[r0:r0 + sub, :] = _mlp_body(xp_ref[r0:r0 + sub, :], gpre_ref, wup_scr, wdn_scr, gpost_ref)

    @pl.when(step == last)
    def _():
        n_seq = xs_ref.shape[0]
        steps = xs_ref.shape[1] // dm
        x = jnp.concatenate([xs_ref[:, t * dm:(t + 1) * dm] for t in range(steps)], axis=0)
        y = _mlp_body(x, gpre_ref, wup_scr, wdn_scr, gpost_ref)
        for t in range(steps):
            ys_ref[:, t, :] = y[t * n_seq:(t + 1) * n_seq, :]


def _mlp_call(xp2d, xs_sm, w):
    n, dm = xp2d.shape
    tile = min(MLP_TILE, n)
    assert n % tile == 0
    n_tiles = n // tile
    wup, wdn = w["wup_f32"], w["wdn_f32"]
    dff = wup.shape[1]
    fc = MLP_FCHUNK
    assert dff % fc == 0
    n_cast = dff // fc
    ys_shape = (xs_sm.shape[0], xs_sm.shape[1] // dm, dm)
    blk = lambda s: jnp.minimum(s, n_cast - 1)
    row = lambda s: jnp.clip(s - n_cast, 0, n_tiles - 1)
    return pl.pallas_call(
        _mlp_kernel,
        grid=(n_cast + n_tiles + 1,),
        in_specs=[pl.BlockSpec((tile, dm), lambda s: (row(s), 0)),
                  _const_spec(xs_sm.shape),
                  _const_spec(w["g2pre"].shape),
                  pl.BlockSpec((dm, fc), lambda s: (0, blk(s))),
                  pl.BlockSpec((fc, dm), lambda s: (blk(s), 0)),
                  _const_spec(w["g2post"].shape)],
        out_specs=(pl.BlockSpec((tile, dm), lambda s: (row(s), 0)),
                   pl.BlockSpec(ys_shape, lambda s: (0, 0, 0))),
        out_shape=(jax.ShapeDtypeStruct((n, dm), f32), jax.ShapeDtypeStruct(ys_shape, f32)),
        scratch_shapes=[pltpu.VMEM((n_cast, dm, fc), bf16), pltpu.VMEM((n_cast, fc, dm), bf16)],
        compiler_params=pltpu.CompilerParams(
            dimension_semantics=("arbitrary",), vmem_limit_bytes=VMEM_LIMIT),
        name="mlp",
    )(xp2d, xs_sm, w["g2pre"], wup, wdn, w["g2post"])


def _sample_pre_kernel(x_ref, h0r_ref, h0i_ref, gpre_ref, win_ref, bin_ref, bc_ref,
                       abr_ref, abi_ref, d_ref, wglu_ref, bglu_ref, lb_ref, hgn_ref, wbs_ref,
                       m5_ref, ghg_ref, gn_ref, qt_ref, cols_ref, v_ref, oin_ref,
                       s5r_ref, s5i_ref, bblk_ref, cblkt_ref):
    n_seq = h0r_ref.shape[0]
    dm = gpre_ref.shape[1]
    steps = x_ref.shape[1]
    sdim = h0r_ref.shape[1]
    hw = HG_HEADS * HG_DK
    s5w = d_ref.shape[1]

    @pl.when(pl.program_id(0) == 0)
    def _():
        n_state = bc_ref.shape[2]
        row_grp = lax.broadcasted_iota(jnp.int32, (s5w, sdim), 0) // S5_GROUP
        col_grp = lax.broadcasted_iota(jnp.int32, (s5w, sdim), 1) // n_state
        own = row_grp == col_grp
        for k, ref in enumerate((bblk_ref, bblk_ref, cblkt_ref, cblkt_ref)):
            tiled = jnp.concatenate([bc_ref[k]] * (sdim // n_state), axis=1)
            c0 = (k % 2) * sdim
            ref[:, c0:c0 + sdim] = jnp.where(own, tiled, 0.0).astype(bf16)

    x = jnp.concatenate([x_ref[:, t, :] for t in range(steps)], axis=0)
    hb = (_rms(x) * gpre_ref[...]).astype(bf16)

    u = _dot(hb, win_ref[:, :s5w]) + bin_ref[:, :s5w]
    bu = _dot(u.astype(bf16), bblk_ref[...])
    hr = h0r_ref[...]
    hi = h0i_ref[...]
    a_r = abr_ref[...]
    a_i = abi_ref[...]
    hs = []
    for t in range(steps):
        sl = slice(t * n_seq, (t + 1) * n_seq)
        hr, hi = (a_r * hr - a_i * hi + bu[sl, :sdim], a_r * hi + a_i * hr + bu[sl, sdim:])
        hs.append(jnp.concatenate([hr, hi], axis=1).astype(bf16))
    s5r_ref[...] = hr
    s5i_ref[...] = hi
    y = _dot_nt(jnp.concatenate(hs, axis=0), cblkt_ref[...])
    y = _gelu_tanh(y + d_ref[...] * u)
    z = _dot(y.astype(bf16), wglu_ref[...]) + bglu_ref[...]
    ys5 = (y * _sigmoid(z)).astype(bf16)
    g0 = s5w + 4 * hw
    def put_steps(ref, val):
        wcol = val.shape[1]
        for t in range(steps):
            ref[:, t * wcol:(t + 1) * wcol] = val[t * n_seq:(t + 1) * n_seq, :]

    gates = _dot(hb, win_ref[:, g0:]) + bin_ref[:, g0:]
    put_steps(m5_ref, _sigmoid(gates[:, :dm]) * _dot(ys5, wbs_ref[...]))
    put_steps(ghg_ref, _sigmoid(gates[:, dm:]))

    def proj(i):
        c0 = s5w + i * hw
        return _dot(hb, win_ref[:, c0:c0 + hw]) + bin_ref[:, c0:c0 + hw]

    q_raw = proj(0)
    f_raw = proj(1)
    v = proj(2)
    g_raw = proj(3)
    q = _silu(q_raw)
    lb = lb_ref[...]
    f = 0.5 * (1.0 + lb) + 0.5 * (1.0 - lb) * jnp.tanh(0.5 * f_raw)
    lf = jnp.log(f)
    kk = 1.0 - f
    put_steps(gn_ref, _silu(g_raw) * hgn_ref[...])
    bs = []
    acc = None
    for t in range(steps):
        sl = slice(t * n_seq, (t + 1) * n_seq)
        acc = lf[sl] if acc is None else acc + lf[sl]
        bs.append(acc)
    b_last = bs[-1]

    def put(ref, slot, val):
        for h in range(HG_HEADS):
            ref[pl.ds(h * 8 + slot, n_seq, stride=HG_HEADS * 8), :] = val[:, h * HG_DK:(h + 1) * HG_DK]

    for ref in (qt_ref, cols_ref, v_ref):
        ref[...] = jnp.zeros_like(ref)
    put(cols_ref, steps, jnp.exp(b_last))
    for t in range(steps):
        sl = slice(t * n_seq, (t + 1) * n_seq)
        put(qt_ref, t, q[sl] * jnp.exp(bs[t]))
        put(cols_ref, t, kk[sl] * jnp.exp(b_last - bs[t]))
        put(v_ref, t, v[sl])
        o_t = None
        for s in range(t + 1):
            sls = slice(s * n_seq, (s + 1) * n_seq)
            prod = q[sl] * kk[sls] * jnp.exp(bs[t] - bs[s])
            parts = []
            for h in range(HG_HEADS):
                hs = slice(h * HG_DK, (h + 1) * HG_DK)
                wgt = jnp.sum(prod[:, hs], axis=1, keepdims=True)
                parts.append(wgt * v[sls, hs])
            term = jnp.concatenate(parts, axis=1)
            o_t = term if o_t is None else o_t + term
        oin_ref[:, t * hw:(t + 1) * hw] = o_t


def _sample_pre_call(x_s, h0r, h0i, w):
    n_seq, sdim = h0r.shape
    steps, dm = x_s.shape[1], x_s.shape[2]
    hw = HG_HEADS * HG_DK
    nb = SAMPLE_PRE_BLOCK
    assert n_seq % nb == 0
    s5w = w["d"].shape[1]
    consts = [w[k] for k in ("gpre", "win", "bin", "bc", "abr", "abi", "d",
                             "wglu", "bglu", "lbrow", "hgnrow", "wbs")]
    slots = HG_HEADS * 8
    shapes = ((1, steps * dm),
              (1, steps * dm),
              (1, steps * hw),
              (slots, HG_DK),
              (slots, HG_DK),
              (slots, HG_DK),
              (1, steps * hw),
              (1, sdim), (1, sdim))
    seq_spec = lambda shp: pl.BlockSpec((nb * shp[0], shp[1]), lambda i: (i, 0))
    return pl.pallas_call(
        _sample_pre_kernel,
        grid=(n_seq // nb,),
        in_specs=([pl.BlockSpec((nb, steps, dm), lambda i: (i, 0, 0)), seq_spec((1, sdim)), seq_spec((1, sdim))]
                  + [_const_spec(c.shape) for c in consts]),
        out_specs=tuple(seq_spec(shp) for shp in shapes),
        out_shape=tuple(jax.ShapeDtypeStruct((n_seq * shp[0], shp[1]), f32) for shp in shapes),
        scratch_shapes=[pltpu.VMEM((s5w, 2 * sdim), bf16),
                        pltpu.VMEM((s5w, 2 * sdim), bf16)],
        compiler_params=pltpu.CompilerParams(
            dimension_semantics=("arbitrary",), vmem_limit_bytes=VMEM_LIMIT),
        name="sample_pre",
    )(x_s, h0r, h0i, *consts)


def _sample_state_kernel(s0_hbm, qt_ref, v_ref, cols_ref, snew_ref, oint_ref, s0_buf, sem):
    step = pl.program_id(0)
    n_steps = pl.num_programs(0)
    nb = qt_ref.shape[0]

    def state_copy(at_step, slot):
        return pltpu.make_async_copy(s0_hbm.at[pl.ds(at_step * nb, nb)], s0_buf.at[slot], sem.at[slot])

    @pl.when(step == 0)
    def _():
        for ahead in range(STATE_RING - 1):
            @pl.when(ahead < n_steps)
            def _():
                state_copy(ahead, ahead).start()

    nxt = step + STATE_RING - 1

    @pl.when(nxt < n_steps)
    def _():
        state_copy(nxt, nxt % STATE_RING).start()

    slot = step % STATE_RING
    state_copy(step, slot).wait()
    s0_ref = s0_buf.at[slot]
    nrow = cols_ref.shape[1]
    hw = HG_HEADS * HG_DK
    row_head = lax.broadcasted_iota(jnp.int32, (nrow, hw), 0) // 8
    col_head = lax.broadcasted_iota(jnp.int32, (nrow, hw), 1) // HG_DK
    diag = row_head == col_head
    pad = jnp.zeros((LANES - nrow, HG_DK), f32)
    for i in range(nb):
        xt = jnp.concatenate([cols_ref[i], pad], axis=0).T
        vbd = jnp.where(diag, jnp.concatenate([v_ref[i]] * HG_HEADS, axis=1), 0.0)
        vbd = jnp.concatenate([vbd, jnp.zeros((LANES - nrow, hw), f32)], axis=0).astype(bf16)
        ds = _dot(xt.astype(bf16), vbd)
        for h in range(HG_HEADS):
            s0 = s0_ref[i, h]
            oint_ref[i, h] = _dot(qt_ref[i, h].astype(bf16), s0.astype(bf16))
            snew_ref[i, h] = xt[:, h * 8 + 4:h * 8 + 5] * s0 + ds[:, h * HG_DK:(h + 1) * HG_DK]


def _sample_state_call(s0, qt8, v8, cols):
    n_seq = s0.shape[0]
    nb = SAMPLE_SEQ_BLOCK
    blk4 = (nb, HG_HEADS, 8, HG_DK)
    return pl.pallas_call(
        _sample_state_kernel,
        grid=(n_seq // nb,),
        in_specs=[pl.BlockSpec(memory_space=pl.ANY),
                  pl.BlockSpec(blk4, lambda i: (i, 0, 0, 0)),
                  pl.BlockSpec((nb, HG_HEADS * 8, HG_DK), lambda i: (i, 0, 0)),
                  pl.BlockSpec((nb, HG_HEADS * 8, HG_DK), lambda i: (i, 0, 0))],
        out_specs=(pl.BlockSpec((nb, HG_HEADS, HG_DK, HG_DK), lambda i: (i, 0, 0, 0)),
                   pl.BlockSpec(blk4, lambda i: (i, 0, 0, 0))),
        out_shape=(jax.ShapeDtypeStruct(s0.shape, f32),
                   jax.ShapeDtypeStruct((n_seq, HG_HEADS, 8, HG_DK), f32)),
        scratch_shapes=[pltpu.VMEM((STATE_RING, nb, HG_HEADS, HG_DK, HG_DK), f32),
                        pltpu.SemaphoreType.DMA((STATE_RING,))],
        compiler_params=pltpu.CompilerParams(
            dimension_semantics=("arbitrary",), vmem_limit_bytes=VMEM_LIMIT),
        name="sample_state",
    )(s0, qt8.reshape((n_seq,) + blk4[1:]), v8.reshape(n_seq, HG_HEADS * 8, HG_DK),
      cols.reshape(n_seq, HG_HEADS * 8, HG_DK))


def _sample_post_kernel(x_ref, oint_ref, oin_ref, gn_ref, m5_ref, ghg_ref, wbh_ref, wout_ref, gpost_ref,
                        y_ref):
    n_seq = x_ref.shape[0]
    dm = gpost_ref.shape[1]
    steps = x_ref.shape[1]
    hw = HG_HEADS * HG_DK

    def steps_major(ref, wcol):
        return jnp.concatenate([ref[:, t * wcol:(t + 1) * wcol] for t in range(steps)], axis=0)

    x = jnp.concatenate([x_ref[:, t, :] for t in range(steps)], axis=0)
    oint = jnp.concatenate(
        [jnp.concatenate([oint_ref[pl.ds(h * 8 + t, n_seq, stride=HG_HEADS * 8), :] for h in range(HG_HEADS)],
                         axis=1) for t in range(steps)], axis=0)
    o = oint + steps_major(oin_ref, hw)
    parts = []
    for h in range(HG_HEADS):
        hs = slice(h * HG_DK, (h + 1) * HG_DK)
        parts.append(_rms(o[:, hs]))
    yhg = (jnp.concatenate(parts, axis=1) * steps_major(gn_ref, hw)).astype(bf16)
    merged = steps_major(m5_ref, dm) + steps_major(ghg_ref, dm) * _dot(yhg, wbh_ref[...])
    mo = _dot(merged.astype(bf16), wout_ref[...])
    y = x + _rms(mo) * gpost_ref[...]
    for t in range(steps):
        y_ref[:, t * dm:(t + 1) * dm] = y[t * n_seq:(t + 1) * n_seq, :]


def _sample_post_call(x_s, oint, oin, gn, m5, ghg, w):
    ins = [x_s, oint, oin, gn, m5, ghg] + [w[k] for k in ("wbh", "wout", "gpost")]
    n_seq, steps, dm = x_s.shape
    return pl.pallas_call(
        _sample_post_kernel,
        out_shape=jax.ShapeDtypeStruct((n_seq, steps * dm), f32),
        compiler_params=pltpu.CompilerParams(vmem_limit_bytes=VMEM_LIMIT),
        name="sample_post",
    )(*ins)


def _prep_weights(l, norm_mix_pre, norm_mix_post, norm_mlp_pre, norm_mlp_post, w_in, b_in,
                  s5_a_re, s5_a_im, s5_log_dt, s5_b_re, s5_b_im, s5_c_re, s5_c_im, s5_d, s5_w_glu,
                  s5_b_glu, hg_lb_logits, hg_norm, w_br_s5, w_br_hg, w_out, w_up, w_down, n_pow):
    hp = lax.Precision.HIGHEST
    s5w = s5_d.shape[1]
    n_groups = s5w // S5_GROUP
    w = {}
    row = lambda a: a.astype(f32).reshape(1, -1)
    w["gpre"], w["gpost"] = row(norm_mix_pre[l]), row(norm_mix_post[l])
    w["g2pre"], w["g2post"] = row(norm_mlp_pre[l]), row(norm_mlp_post[l])
    w["win"], w["bin"] = w_in[l].astype(bf16), row(b_in[l])
    lb_all = jnp.cumsum(jax.nn.softmax(hg_lb_logits.astype(f32), axis=0), axis=0)
    w["lbrow"] = row(lb_all[l])
    w["hgnrow"] = row(jnp.tile(hg_norm[l].astype(f32), HG_HEADS))
    w["d"] = row(s5_d[l])
    w["wglu"], w["bglu"] = s5_w_glu[l].astype(bf16), row(s5_b_glu[l])
    w["wbs"], w["wbh"], w["wout"] = w_br_s5[l].astype(bf16), w_br_hg[l].astype(bf16), w_out[l].astype(bf16)
    w["wup_f32"], w["wdn_f32"] = w_up[l], w_down[l]

    a_re, a_im = s5_a_re[l].astype(f32), s5_a_im[l].astype(f32)
    dt = jnp.exp(s5_log_dt[l].astype(f32))[:, None]
    lam_r, lam_i = dt * a_re, dt * a_im
    jj = jnp.arange(S5_CHUNK + 1, dtype=f32)[None, :, None]
    pmag = jnp.exp(jj * lam_r[:, None, :])
    pw_r, pw_i = pmag * jnp.cos(jj * lam_i[:, None, :]), pmag * jnp.sin(jj * lam_i[:, None, :])
    abr, abi = pw_r[:, 1], pw_i[:, 1]
    den = a_re * a_re + a_im * a_im
    nr, ni = abr - 1.0, abi
    cfr, cfi = (nr * a_re + ni * a_im) / den, (ni * a_re - nr * a_im) / den
    b_re, b_im = s5_b_re[l].astype(f32), s5_b_im[l].astype(f32)
    bbr = cfr[..., None] * b_re - cfi[..., None] * b_im
    bbi = cfr[..., None] * b_im + cfi[..., None] * b_re
    c_re, c_im = s5_c_re[l].astype(f32), s5_c_im[l].astype(f32)
    w["abr"], w["abi"] = abr.reshape(1, -1), abi.reshape(1, -1)

    grp = S5_GROUP
    n_state = a_re.shape[1]
    bt = jnp.transpose(jnp.stack([bbr, bbi]), (0, 1, 3, 2))
    bt_r, bt_i = bt[0], bt[1]
    c4r, c4i = c_re[:, None], c_im[:, None]

    def cmul(pr_, pi_, xr, xi):
        pr_, pi_ = pr_[:, :, None, :], pi_[:, :, None, :]
        return pr_ * xr - pi_ * xi, pr_ * xi + pi_ * xr

    rows128 = lambda t: t.reshape(n_groups, LANES, t.shape[-1])
    car_t, cai_t = cmul(pw_r[:, 1:], pw_i[:, 1:], c4r, c4i)
    ca = jnp.transpose(jnp.stack([rows128(car_t), -rows128(cai_t)]), (0, 1, 3, 2))
    ber, bei = cmul(pw_r[:, S5_CHUNK - 1::-1], pw_i[:, S5_CHUNK - 1::-1], bt_r[:, None], bt_i[:, None])
    ber, bei = rows128(ber), rows128(bei)
    zpad = jnp.zeros((n_groups, S5_CHUNK - 1, n_state), f32)
    lag_r = jnp.concatenate([zpad, pw_r[:, :S5_CHUNK]], axis=1)
    lag_i = jnp.concatenate([zpad, pw_i[:, :S5_CHUNK]], axis=1)
    cp_r, cp_i = cmul(lag_r, lag_i, c4r, c4i)
    n_blk = 2 * S5_CHUNK - 1
    cp_all = jnp.concatenate([cp_r, -cp_i], axis=3).reshape(n_groups, n_blk * grp, 2 * n_state)
    rpad_t = jnp.einsum("gmn,gnc->gmc", cp_all, jnp.concatenate([bbr, bbi], axis=1), precision=hp)
    toe_t = jnp.concatenate([rpad_t[:, (S5_CHUNK - 1 - s) * grp:(S5_CHUNK - 1 - s) * grp + LANES, :]
                             for s in range(S5_CHUNK)], axis=2)
    toe = jnp.transpose(toe_t, (0, 2, 1))

    def pair_diag(m):
        g, r, c = m.shape
        m = m.reshape(g // 2, 2, r, c)
        z = jnp.zeros_like(m[:, 0])
        return jnp.concatenate([jnp.concatenate([m[:, 0], z], axis=2),
                                jnp.concatenate([z, m[:, 1]], axis=2)], axis=1)

    w["w1"] = jnp.concatenate([pair_diag(toe), pair_diag(ber), pair_diag(bei)], axis=2).astype(bf16)
    w["w2"] = jnp.concatenate([pair_diag(ca[0]), pair_diag(ca[1])], axis=1).astype(bf16)
    qr, qi = pw_r[:, S5_CHUNK], pw_i[:, S5_CHUNK]
    aps_r, aps_i = [], []
    for _ in range(n_pow):
        aps_r.append(qr.reshape(-1))
        aps_i.append(qi.reshape(-1))
        qr, qi = qr * qr - qi * qi, 2.0 * qr * qi
    w["apr"], w["api"] = jnp.stack(aps_r), jnp.stack(aps_i)

    w["bc"] = jnp.stack([bt_r, bt_i, c_re, -c_im]).reshape(4, s5w, n_state)
    return w


def kernel(x_prompt, x_sample, state_s5_re, state_s5_im, state_hg, norm_mix_pre, norm_mix_post,
           norm_mlp_pre, norm_mlp_post, w_in, b_in, s5_a_re, s5_a_im, s5_log_dt, s5_b_re, s5_b_im,
           s5_c_re, s5_c_im, s5_d, s5_w_glu, s5_b_glu, hg_lb_logits, hg_norm, w_br_s5, w_br_hg,
           w_out, w_up, w_down):
    depth = w_in.shape[0]
    bsz, seq, dm = x_prompt.shape
    n_seq, steps, _ = x_sample.shape
    n_groups, n_state = s5_a_re.shape[1], s5_a_re.shape[2]
    hw = HG_HEADS * HG_DK
    assert depth == 1 and steps == 4 and n_state == S5_STATE
    n_pow = max(1, int(math.log2(min(PROMPT_TILE, seq) // S5_CHUNK)))
    xp, xs = x_prompt, x_sample
    outs = [[] for _ in range(6)]
    for l in range(depth):
        w = _prep_weights(l, norm_mix_pre, norm_mix_post, norm_mlp_pre, norm_mlp_post, w_in, b_in,
                          s5_a_re, s5_a_im, s5_log_dt, s5_b_re, s5_b_im, s5_c_re, s5_c_im, s5_d,
                          s5_w_glu, s5_b_glu, hg_lb_logits, hg_norm, w_br_s5, w_br_hg, w_out, w_up,
                          w_down, n_pow)
        x1, p_re, p_im, p_hg = _mixer_call(xp, w)
        outs[0].append(p_re.reshape(bsz, n_groups, n_state))
        outs[1].append(p_im.reshape(bsz, n_groups, n_state))
        outs[2].append(p_hg)
        h0r = state_s5_re[l].reshape(n_seq, n_groups * n_state)
        h0i = state_s5_im[l].reshape(n_seq, n_groups * n_state)
        m5, ghg, gn, qt8, cols, v8, oin, s_re, s_im = _sample_pre_call(xs, h0r, h0i, w)
        s_hg, oint = _sample_state_call(state_hg[l], qt8, v8, cols)
        x1_sm = _sample_post_call(xs, oint.reshape(n_seq * HG_HEADS * 8, HG_DK), oin, gn, m5, ghg, w)
        yp, xs = _mlp_call(x1.reshape(bsz * seq, dm), x1_sm, w)
        xp = yp.reshape(bsz, seq, dm)
        outs[3].append(s_re.reshape(n_seq, n_groups, n_state))
        outs[4].append(s_im.reshape(n_seq, n_groups, n_state))
        outs[5].append(s_hg)
    return (xp, xs) + tuple(o[0][None] for o in outs)
```
